```python
import math
import jax
import jax.numpy as jnp
from jax import lax
import numpy as np

D_MODEL = 1024
BATCH = 8
SEQ = 4096
DEPTH = 2

N_EVEN = (DEPTH + 1) // 2
N_ODD = DEPTH // 2
EPS = 1e-6
CONV_W = 3

HG_HEADS = 4
HG_DK = 128
HG_DV = 128
HG_WIDTH = HG_HEADS * HG_DK
HG_CHUNK = 64
SC_WIDTH = D_MODEL - HG_WIDTH
SC_GROUPS = 8
AB_SIZES = (HG_WIDTH,) * 4 + (SC_WIDTH,) * 3
AB_IN = sum(AB_SIZES)

NSA_HEADS = 16
NSA_KV = 4
NSA_HPG = NSA_HEADS // NSA_KV
NSA_DH = D_MODEL // NSA_HEADS
CMP_LEN = 32
CMP_STRIDE = 16
CMP_HIDDEN = 256
SEL_BLOCK = 64
SEL_TOPN = 16
WINDOW = 512
NSA_QB = 64
NSA_SIZES = (NSA_HEADS * NSA_DH, 3 * NSA_HEADS) + (NSA_KV * NSA_DH,) * 6
NSA_IN = sum(NSA_SIZES)

REL_BUCKETS = 32
REL_MAX_DIST = 1024

D_FF = 2816

kernel_name = 'hybrid_hgrn2_shortconv_nsa_convffn_adaln'


def split_cols(a, sizes):
    return jnp.split(a, [int(v) for v in np.cumsum(sizes)[:-1]], axis=-1)


def rmsnorm(x, g):
    xf = x.astype(jnp.float32)
    y = xf * lax.rsqrt(jnp.mean(xf * xf, axis=-1, keepdims=True) + EPS)
    return (y * g.astype(jnp.float32)).astype(x.dtype)


def causal_dwconv3(u, w):
    S = u.shape[1]
    up = jnp.pad(u, ((0, 0), (CONV_W - 1, 0), (0, 0)))
    return up[:, 0:S] * w[0] + up[:, 1:S + 1] * w[1] + up[:, 2:S + 2] * w[2]


def rel_bucket(dist):
    n = jnp.maximum(dist, 0)
    exact = REL_BUCKETS // 2
    large = exact + (jnp.log(jnp.maximum(n, exact).astype(jnp.float32) / exact)
                     / math.log(REL_MAX_DIST / exact) * (REL_BUCKETS - exact)).astype(jnp.int32)
    return jnp.where(n < exact, n, jnp.minimum(large, REL_BUCKETS - 1))


def masked_softmax(logits, mask):
    logits = jnp.where(mask, logits.astype(jnp.float32), -jnp.inf)
    m = jnp.max(logits, axis=-1, keepdims=True)
    m = jnp.where(jnp.isfinite(m), m, 0.0)
    e = jnp.exp(logits - m)
    den = jnp.sum(e, axis=-1, keepdims=True)
    return e / jnp.where(den > 0, den, 1.0)


def hgrn2_chunked(q, k, v, log_f):
    f32 = jnp.float32
    q, k, v, log_f = (a.astype(f32) for a in (q, k, v, log_f))
    B, H, S, K = q.shape
    V = v.shape[-1]
    C = HG_CHUNK
    NC = S // C

    def to_chunks(a):
        return jnp.moveaxis(a.reshape(B, H, NC, C, a.shape[-1]), 2, 0)

    causal = jnp.tril(jnp.ones((C, C), bool))[:, :, None]

    def step(state, inp):
        qi, ki, vi, gi = inp
        b = jnp.cumsum(gi, axis=2)
        diff = b[:, :, :, None, :] - b[:, :, None, :, :]
        decay = jnp.exp(jnp.where(causal, diff, -jnp.inf))
        attn = jnp.einsum('bhtk,bhsk,bhtsk->bhts', qi, ki, decay)
        o = (jnp.einsum('bhts,bhsv->bhtv', attn, vi)
             + jnp.einsum('bhtk,bhkv->bhtv', qi * jnp.exp(b), state))
        b_last = b[:, :, -1]
        state = (state * jnp.exp(b_last)[..., None]
                 + jnp.einsum('bhsk,bhsv->bhkv', ki * jnp.exp(b_last[:, :, None] - b), vi))
        return state, o

    state0 = jnp.zeros((B, H, K, V), f32)
    _, o = lax.scan(step, state0, (to_chunks(q), to_chunks(k), to_chunks(v), to_chunks(log_f)))
    return jnp.moveaxis(o, 0, 2).reshape(B, H, S, V)


def mixer_hgrn2_shortconv(h, w_in, w_out, lower_bound, onorm_g, sconv_w):
    B, S, _ = h.shape
    f32 = jnp.float32
    q, f, i, g, gb, gc, sv = split_cols(h @ w_in, AB_SIZES)
    lb = lower_bound.astype(f32)
    fgate = lb + (1.0 - lb) * jax.nn.sigmoid(f.astype(f32))

    def heads(a):
        return a.reshape(B, S, HG_HEADS, -1).transpose(0, 2, 1, 3)

    o = hgrn2_chunked(heads(q), heads(1.0 - fgate), heads(i), heads(jnp.log(fgate)))
    o = rmsnorm(o, onorm_g).transpose(0, 2, 1, 3).reshape(B, S, HG_WIDTH)
    o_a = (o * jax.nn.silu(g.astype(f32))).astype(h.dtype)
    o_b = gb * causal_dwconv3(gc * sv, sconv_w)
    return jnp.concatenate([o_a, o_b.astype(h.dtype)], axis=-1) @ w_out


def mixer_nsa(h, w_in, w_out, pos_k, pos_v, w1_k, w2_k, w1_v, w2_v, rel_bias):
    B, S, _ = h.shape
    f32 = jnp.float32
    q, gl, kc, vc, ks, vs, kw, vw = split_cols(h @ w_in, NSA_SIZES)
    q = q.reshape(B, S, NSA_KV, NSA_HPG, NSA_DH).transpose(0, 2, 3, 1, 4) * (NSA_DH ** -0.5)
    gates = jax.nn.sigmoid(gl.astype(f32)).reshape(B, S, 3, NSA_KV, NSA_HPG).transpose(2, 0, 3, 4, 1)

    def kvh(a):
        return a.reshape(B, S, NSA_KV, NSA_DH).transpose(0, 2, 1, 3)

    kc, vc, ks, vs, kw, vw = (kvh(a) for a in (kc, vc, ks, vs, kw, vw))

    n_cmp = (S - CMP_LEN) // CMP_STRIDE + 1
    cmp_idx = np.arange(n_cmp)[:, None] * CMP_STRIDE + np.arange(CMP_LEN)[None, :]
    cmp_end = jnp.asarray(cmp_idx[:, -1], jnp.int32)

    def compress(a, pos, w1, w2):
        blocks = (a[:, :, cmp_idx] + pos).reshape(B, NSA_KV, n_cmp, CMP_LEN * NSA_DH)
        return jax.nn.silu(blocks @ w1) @ w2

    k_cmp = compress(kc, pos_k, w1_k, w2_k)
    v_cmp = compress(vc, pos_v, w1_v, w2_v).astype(f32)

    n_sel = S // SEL_BLOCK
    n_top = min(SEL_TOPN, n_sel)
    c_start = np.arange(n_cmp)[:, None] * CMP_STRIDE
    s_start = np.arange(n_sel)[None, :] * SEL_BLOCK
    inside = np.clip(np.minimum(c_start + CMP_LEN, s_start + SEL_BLOCK) - np.maximum(c_start, s_start), 0, None)
    cmp_to_sel = jnp.asarray(inside / CMP_LEN, f32)
    blk = jnp.arange(n_sel)

    kw_pad = jnp.pad(kw, ((0, 0), (0, 0), (WINDOW, 0), (0, 0)))
    vw_pad = jnp.pad(vw, ((0, 0), (0, 0), (WINDOW, 0), (0, 0)))
    rb = rel_bias.astype(f32)
    rb_grp = rb.reshape(REL_BUCKETS, NSA_KV, NSA_HPG)
    gather = jax.vmap(jax.vmap(lambda a, p: a[p]))

    def shared_bias(dist):
        return rb[rel_bucket(dist)].reshape(*dist.shape, NSA_KV, NSA_HPG).transpose(2, 3, 0, 1)

    def query_block(start):
        t = start + jnp.arange(NSA_QB)
        qb = lax.dynamic_slice_in_dim(q, start, NSA_QB, axis=3)
        gb = lax.dynamic_slice_in_dim(gates, start, NSA_QB, axis=4)[..., None]
        dist_c = t[:, None] - cmp_end[None, :]
        s_c = jnp.einsum('bgiqd,bgnd->bgiqn', qb, k_cmp).astype(f32) + shared_bias(dist_c)
        p_c = masked_softmax(s_c, dist_c >= 0)
        o_c = jnp.einsum('bgiqn,bgnd->bgiqd', p_c, v_cmp)
        imp = jnp.einsum('bgiqn,ns->bgqs', p_c, cmp_to_sel)
        cur = (t // SEL_BLOCK)[:, None]
        forced = (blk[None] == 0) | (blk[None] == cur) | (blk[None] == cur - 1)
        score = jnp.where(blk[None] > cur, -jnp.inf, jnp.where(forced, jnp.inf, imp))
        _, sel = lax.top_k(score, n_top)
        pos = (sel[..., None] * SEL_BLOCK + jnp.arange(SEL_BLOCK)).reshape(B, NSA_KV, NSA_QB, n_top * SEL_BLOCK)
        k_sel = gather(ks, pos)
        v_sel = gather(vs, pos).astype(f32)
        dist_s = t[:, None] - pos
        bias_s = jnp.moveaxis(rb_grp[rel_bucket(dist_s), jnp.arange(NSA_KV)[None, :, None, None]], -1, 2)
        s_s = jnp.einsum('bgiqd,bgqkd->bgiqk', qb, k_sel).astype(f32) + bias_s
        p_s = masked_softmax(s_s, (dist_s >= 0)[:, :, None])
        o_s = jnp.einsum('bgiqk,bgqkd->bgiqd', p_s, v_sel)
        kwb = lax.dynamic_slice_in_dim(kw_pad, start, WINDOW + NSA_QB, axis=2)
        vwb = lax.dynamic_slice_in_dim(vw_pad, start, WINDOW + NSA_QB, axis=2).astype(f32)
        wpos = start - WINDOW + jnp.arange(WINDOW + NSA_QB)
        dist_w = t[:, None] - wpos[None, :]
        mask_w = (dist_w >= 0) & (dist_w < WINDOW) & (wpos[None, :] >= 0)
        s_w = jnp.einsum('bgiqd,bgkd->bgiqk', qb, kwb).astype(f32) + shared_bias(dist_w)
        p_w = masked_softmax(s_w, mask_w)
        o_w = jnp.einsum('bgiqk,bgkd->bgiqd', p_w, vwb)
        return gb[0] * o_c + gb[1] * o_s + gb[2] * o_w

    starts = jnp.arange(S // NSA_QB, dtype=jnp.int32) * NSA_QB
    o = lax.map(query_block, starts)
    o = o.transpose(1, 0, 4, 2, 3, 5).reshape(B, S, NSA_HEADS * NSA_DH)
    return o.astype(h.dtype) @ w_out


def conv_ffn(h, w_up, conv_w, w_down):
    gate, up = jnp.split(h @ w_up, 2, axis=-1)
    return (jax.nn.silu(causal_dwconv3(gate, conv_w)) * up) @ w_down


def setup_inputs(seed: int = 0) -> dict:
    key = jax.random.key(seed)
    keys = iter(jax.random.split(key, 40))

    def nrm(shape, scale):
        return jax.random.normal(next(keys), shape, jnp.float32) * scale

    D = D_MODEL
    return {
        'x': nrm((BATCH, SEQ, D), 1.0),
        'c': nrm((BATCH, D), 1.0),
        'mod_w': nrm((DEPTH, D, 6 * D), 0.5 * D ** -0.5),
        'mod_b': nrm((DEPTH, 6 * D), 0.02),
        'norm_mix_g': 1.0 + nrm((DEPTH, D), 0.02),
        'norm_ffn_g': 1.0 + nrm((DEPTH, D), 0.02),
        'ab_w_in': nrm((N_EVEN, D, AB_IN), D ** -0.5),
        'ab_w_out': nrm((N_EVEN, HG_WIDTH + SC_WIDTH, D), (HG_WIDTH + SC_WIDTH) ** -0.5),
        'hgrn_lb_logits': nrm((N_EVEN + 1, HG_WIDTH), 0.1),
        'hgrn_onorm_g': 1.0 + nrm((N_EVEN, HG_DV), 0.02),
        'sconv_w': nrm((N_EVEN, CONV_W, SC_WIDTH), CONV_W ** -0.5),
        'nsa_w_in': nrm((N_ODD, D, NSA_IN), D ** -0.5),
        'nsa_w_out': nrm((N_ODD, NSA_HEADS * NSA_DH, D), (NSA_HEADS * NSA_DH) ** -0.5),
        'nsa_cmp_pos_k': nrm((N_ODD, CMP_LEN, NSA_DH), 0.02),
        'nsa_cmp_pos_v': nrm((N_ODD, CMP_LEN, NSA_DH), 0.02),
        'nsa_cmp_w1_k': nrm((N_ODD, CMP_LEN * NSA_DH, CMP_HIDDEN), (CMP_LEN * NSA_DH) ** -0.5),
        'nsa_cmp_w2_k': nrm((N_ODD, CMP_HIDDEN, NSA_DH), CMP_HIDDEN ** -0.5),
        'nsa_cmp_w1_v': nrm((N_ODD, CMP_LEN * NSA_DH, CMP_HIDDEN), (CMP_LEN * NSA_DH) ** -0.5),
        'nsa_cmp_w2_v': nrm((N_ODD, CMP_HIDDEN, NSA_DH), CMP_HIDDEN ** -0.5),
        'rel_bias': nrm((REL_BUCKETS, NSA_HEADS), 0.5),
        'ffn_w_up': nrm((DEPTH, D, 2 * D_FF), D ** -0.5),
        'ffn_conv_w': nrm((DEPTH, CONV_W, D_FF), CONV_W ** -0.5),
        'ffn_w_down': nrm((DEPTH, D_FF, D), D_FF ** -0.5),
        'final_norm_g': 1.0 + nrm((D,), 0.02),
    }


def reference(x, c, mod_w, mod_b, norm_mix_g, norm_ffn_g, ab_w_in, ab_w_out, hgrn_lb_logits,
              hgrn_onorm_g, sconv_w, nsa_w_in, nsa_w_out, nsa_cmp_pos_k, nsa_cmp_pos_v,
              nsa_cmp_w1_k, nsa_cmp_w2_k, nsa_cmp_w1_v, nsa_cmp_w2_v, rel_bias,
              ffn_w_up, ffn_conv_w, ffn_w_down, final_norm_g):
    lower = jnp.cumsum(jax.nn.softmax(hgrn_lb_logits.astype(jnp.float32), axis=0), axis=0)
    c_act = jax.nn.silu(c)
    for l in range(DEPTH):
        mod = c_act @ mod_w[l] + mod_b[l]
        sh1, sc1, g1, sh2, sc2, g2 = jnp.split(mod[:, None, :], 6, axis=-1)
        hm = rmsnorm(x, norm_mix_g[l]) * (1.0 + sc1) + sh1
        j = l // 2
        if l % 2 == 0:
            y = mixer_hgrn2_shortconv(hm, ab_w_in[j], ab_w_out[j], lower[j], hgrn_onorm_g[j], sconv_w[j])
        else:
            y = mixer_nsa(hm, nsa_w_in[j], nsa_w_out[j], nsa_cmp_pos_k[j], nsa_cmp_pos_v[j],
                          nsa_cmp_w1_k[j], nsa_cmp_w2_k[j], nsa_cmp_w1_v[j], nsa_cmp_w2_v[j], rel_bias)
        x = x + g1 * y
        hf = rmsnorm(x, norm_ffn_g[l]) * (1.0 + sc2) + sh2
        x = x + g2 * conv_ffn(hf, ffn_w_up[l], ffn_conv_w[l], ffn_w_down[l])
    return rmsnorm(x, final_norm_g)
```

```python
import functools
import math

import jax
import jax.numpy as jnp
import numpy as np
from jax import lax
from jax.experimental import pallas as pl
from jax.experimental.pallas import tpu as pltpu

F32 = jnp.float32
BF16 = jnp.bfloat16

EPS = 1e-6
D_MODEL = 1024
D_FF = 2816

HG_HEADS = 4
HG_DK = 128
HG_WIDTH = HG_HEADS * HG_DK
SC_WIDTH = D_MODEL - HG_WIDTH
AB_IN = 4 * HG_WIDTH + 3 * SC_WIDTH
HG_CHUNK = 64
HG_LEVELS = 6

NSA_HEADS = 16
NSA_KV = 4
NSA_HPG = NSA_HEADS // NSA_KV
NSA_DH = D_MODEL // NSA_HEADS
CMP_LEN = 32
CMP_STRIDE = 16
CMP_HIDDEN = 256
SEL_BLOCK = 64
SEL_TOPN = 16
WINDOW = 512
REL_BUCKETS = 32
REL_MAX_DIST = 1024

TOK_TILE = 256
NEAR_TILES = 5
WIN_TILES = WINDOW // TOK_TILE + 1
NEG = -1e30
VMEM_LIMIT = 56 * 1024 * 1024
HALO = 8

_NT = (((1,), (1,)), ((), ()))
_TN = (((0,), (0,)), ((), ()))


def _sigmoid(x):
    return 1.0 / (1.0 + jnp.exp(-x))


def _silu(x):
    return x * _sigmoid(x)


def _norm_mod(x, g, sc, sh):
    ms = jnp.mean(x * x, axis=-1, keepdims=True)
    return (x * lax.rsqrt(ms + EPS) * g) * (1.0 + sc) + sh


def _const_spec(shape):
    n = len(shape)
    return pl.BlockSpec(shape, lambda *_: (0,) * n, pipeline_mode=pl.Buffered(1))


def _params(sem):
    return pltpu.CompilerParams(dimension_semantics=sem, vmem_limit_bytes=VMEM_LIMIT)


def _mod_kernel(c_ref, w_ref, b_ref, o_ref):
    c = c_ref[...]
    o_ref[0] = jnp.dot(_silu(c), w_ref[0], preferred_element_type=F32,
                       precision=lax.Precision.HIGHEST) + b_ref[0]


def _modulation(c, mod_w, mod_b):
    depth, d, n = mod_w.shape
    bsz = c.shape[0]
    tn = 1024
    return pl.pallas_call(
        _mod_kernel,
        grid=(depth, n // tn),
        in_specs=[pl.BlockSpec((bsz, d), lambda l, j: (0, 0)),
                  pl.BlockSpec((1, d, tn), lambda l, j: (l, 0, j)),
                  pl.BlockSpec((1, 1, tn), lambda l, j: (l, 0, j))],
        out_specs=pl.BlockSpec((1, bsz, tn), lambda l, j: (l, 0, j)),
        out_shape=jax.ShapeDtypeStruct((depth, bsz, n), F32),
        compiler_params=_params(("arbitrary", "arbitrary")),
        name="adaln_mod",
    )(c, mod_w, mod_b.reshape(depth, 1, n))


def _hgrn_decay_matrix():
    t = np.arange(HG_CHUNK)
    tril = (t[None, :] <= t[:, None]).astype(np.float32)
    mats = [tril]
    for lvl in range(1, HG_LEVELS + 1):
        m = 1 << (lvl - 1)
        mid = (t // (2 * m)) * (2 * m) + m - 1
        mats.append(tril - tril[mid])
    return np.concatenate(mats, axis=0)


def _mixer0_kernel(x_ref, sh_ref, sc_ref, gt_ref, ng_ref, win_ref, wout_ref, lb_ref, og_ref, cw_ref,
                   pm_ref, o_ref, proj_ref, cat_ref, st_ref, ubuf_ref):
    tile = x_ref.shape[1]
    c64 = HG_CHUNK

    @pl.when(pl.program_id(1) == 0)
    def _():
        st_ref[...] = jnp.zeros_like(st_ref)
        ubuf_ref[0:HALO, :] = jnp.zeros((HALO, SC_WIDTH), F32)

    x = x_ref[0]
    hm = _norm_mod(x, ng_ref[...], sc_ref[0], sh_ref[0]).astype(BF16)
    proj_ref[...] = jnp.dot(hm, win_ref[...], preferred_element_type=F32)

    row = lax.broadcasted_iota(jnp.int32, (c64, HG_DK), 0)
    r64 = lax.broadcasted_iota(jnp.int32, (c64, c64), 0)
    c64i = lax.broadcasted_iota(jnp.int32, (c64, c64), 1)
    second = [None] + [(row & (1 << (lvl - 1))) != 0 for lvl in range(1, HG_LEVELS + 1)]
    same = [None] + [(r64 >> lvl) == (c64i >> lvl) for lvl in range(1, HG_LEVELS + 1)]
    eye = r64 == c64i
    pm = pm_ref[...]
    og = og_ref[...]

    def chunk(c, carry):
        r0 = pl.multiple_of(c * c64, c64)
        rows = pl.ds(r0, c64)
        for h in range(HG_HEADS):
            lanes = lambda j: pl.ds(j * HG_WIDTH + h * HG_DK, HG_DK)
            q = proj_ref[rows, lanes(0)]
            f = proj_ref[rows, lanes(1)]
            v = proj_ref[rows, lanes(2)]
            g = proj_ref[rows, lanes(3)]
            lb = lb_ref[:, h * HG_DK:(h + 1) * HG_DK]
            fg = lb + (1.0 - lb) * _sigmoid(f)
            k = 1.0 - fg
            lg = jnp.log(fg)
            lg_hi = lg.astype(BF16)
            lg_lo = (lg - lg_hi.astype(F32)).astype(BF16)
            dall = jnp.dot(pm, jnp.concatenate([lg_hi, lg_lo], axis=1), preferred_element_type=F32)
            dall = dall[:, :HG_DK] + dall[:, HG_DK:]
            b = dall[0:c64]
            a = jnp.zeros((c64, c64), F32)
            for lvl in range(1, HG_LEVELS + 1):
                e = jnp.exp(-jnp.abs(dall[lvl * c64:(lvl + 1) * c64]))
                qt = jnp.where(second[lvl], q * e, 0.0).astype(BF16)
                kt = jnp.where(second[lvl], 0.0, k * e).astype(BF16)
                al = lax.dot_general(qt, kt, _NT, preferred_element_type=F32)
                a = a + (al if lvl == HG_LEVELS else jnp.where(same[lvl], al, 0.0))
            a = jnp.where(eye, jnp.sum(q * k, axis=-1, keepdims=True), a)
            vb = v.astype(BF16)
            st = st_ref[h]
            o = jnp.dot(a.astype(BF16), vb, preferred_element_type=F32)
            o = o + lax.dot_general((q * jnp.exp(b)).astype(BF16), st.astype(BF16), _NT,
                                    preferred_element_type=F32)
            b_last = b[c64 - 1:c64, :]
            khat = (k * jnp.exp(b_last - b)).astype(BF16)
            st_ref[h] = st * jnp.exp(b_last) + lax.dot_general(vb, khat, _TN, preferred_element_type=F32)
            on = o * lax.rsqrt(jnp.mean(o * o, axis=-1, keepdims=True) + EPS) * og
            cat_ref[rows, h * HG_DK:(h + 1) * HG_DK] = (on * _silu(g)).astype(BF16)
        return carry

    lax.fori_loop(0, tile // c64, chunk, 0)

    off = 4 * HG_WIDTH
    u = proj_ref[:, off + SC_WIDTH:off + 2 * SC_WIDTH] * proj_ref[:, off + 2 * SC_WIDTH:off + 3 * SC_WIDTH]
    ubuf_ref[HALO:HALO + tile, :] = u
    cw = cw_ref[...]
    conv = (ubuf_ref[HALO - 2:HALO - 2 + tile, :] * cw[0:1] + ubuf_ref[HALO - 1:HALO - 1 + tile, :] * cw[1:2]
            + u * cw[2:3])
    ubuf_ref[0:HALO, :] = u[tile - HALO:tile, :]
    cat_ref[:, HG_WIDTH:] = (proj_ref[:, off:off + SC_WIDTH] * conv).astype(BF16)

    y = jnp.dot(cat_ref[...], wout_ref[...], preferred_element_type=F32)
    o_ref[0] = x + gt_ref[0] * y


def _mixer0(x, sh, sc, gt, norm_g, w_in, w_out, lower, onorm_g, sconv_w):
    bsz, seq, d = x.shape
    t = TOK_TILE
    pm = jnp.asarray(_hgrn_decay_matrix(), BF16)
    vec = pl.BlockSpec((1, 1, d), lambda b, i: (b, 0, 0))
    return pl.pallas_call(
        _mixer0_kernel,
        grid=(bsz, seq // t),
        in_specs=[pl.BlockSpec((1, t, d), lambda b, i: (b, i, 0)), vec, vec, vec,
                  _const_spec((1, d)), _const_spec((d, AB_IN)), _const_spec((d, d)),
                  _const_spec((1, HG_WIDTH)), _const_spec((1, HG_DK)), _const_spec((3, SC_WIDTH)),
                  _const_spec(pm.shape)],
        out_specs=pl.BlockSpec((1, t, d), lambda b, i: (b, i, 0)),
        out_shape=jax.ShapeDtypeStruct(x.shape, F32),
        scratch_shapes=[pltpu.VMEM((t, AB_IN), F32), pltpu.VMEM((t, d), BF16),
                        pltpu.VMEM((HG_HEADS, HG_DK, HG_DK), F32), pltpu.VMEM((t + HALO, SC_WIDTH), F32)],
        compiler_params=_params(("arbitrary", "arbitrary")),
        name="mixer_hgrn_sconv",
    )(x, sh, sc, gt, norm_g.reshape(1, d), w_in.astype(BF16), w_out.astype(BF16),
      lower.reshape(1, HG_WIDTH), onorm_g.reshape(1, HG_DK), sconv_w, pm)


FF_CHUNK = 1408


def _ffn_kernel(*refs, has_attn, final):
    if has_attn:
        (x_ref, ot_ref, wo_ref, g1_ref, sh_ref, sc_ref, g2_ref, ng_ref, wup_ref, cw_ref, wdn_ref, fg_ref,
         o_ref, gbuf_ref, carry_ref) = refs
    else:
        (x_ref, sh_ref, sc_ref, g2_ref, ng_ref, wup_ref, cw_ref, wdn_ref, fg_ref,
         o_ref, gbuf_ref, carry_ref) = refs
    tile = x_ref.shape[1]

    @pl.when(pl.program_id(1) == 0)
    def _():
        carry_ref[...] = jnp.zeros_like(carry_ref)

    x = x_ref[0]
    if has_attn:
        x = x + g1_ref[0] * lax.dot_general(ot_ref[0, 0], wo_ref[...], _TN, preferred_element_type=F32)
    hf = _norm_mod(x, ng_ref[...], sc_ref[0], sh_ref[0]).astype(BF16)
    acc = jnp.zeros((tile, D_MODEL), F32)
    for c0 in range(0, D_FF, FF_CHUNK):
        gate = jnp.dot(hf, wup_ref[:, c0:c0 + FF_CHUNK], preferred_element_type=F32)
        up = jnp.dot(hf, wup_ref[:, D_FF + c0:D_FF + c0 + FF_CHUNK], preferred_element_type=F32)
        gbuf_ref[0:HALO, :] = carry_ref[:, c0:c0 + FF_CHUNK]
        gbuf_ref[HALO:HALO + tile, :] = gate
        carry_ref[:, c0:c0 + FF_CHUNK] = gate[tile - HALO:tile, :]
        cw = cw_ref[:, c0:c0 + FF_CHUNK]
        conv = (gbuf_ref[HALO - 2:HALO - 2 + tile, :] * cw[0:1] + gbuf_ref[HALO - 1:HALO - 1 + tile, :] * cw[1:2]
                + gate * cw[2:3])
        act = (_silu(conv) * up).astype(BF16)
        acc = acc + jnp.dot(act, wdn_ref[c0:c0 + FF_CHUNK, :], preferred_element_type=F32)
    out = x + g2_ref[0] * acc
    if final:
        out = out * lax.rsqrt(jnp.mean(out * out, axis=-1, keepdims=True) + EPS) * fg_ref[...]
    o_ref[0] = out


def _conv_ffn(x, sh, sc, g2, norm_g, w_up, conv_w, w_down, final_g, attn=None, final=False):
    bsz, seq, d = x.shape
    t = TOK_TILE
    vec = pl.BlockSpec((1, 1, d), lambda b, i: (b, 0, 0))
    args, specs = [x], [pl.BlockSpec((1, t, d), lambda b, i: (b, i, 0))]
    if attn is not None:
        o_t, w_o, g1 = attn
        args += [o_t, w_o.astype(BF16), g1]
        specs += [pl.BlockSpec((1, 1, d, t), lambda b, i: (b, i, 0, 0)), _const_spec((d, d)), vec]
    args += [sh, sc, g2, norm_g.reshape(1, d), w_up.astype(BF16), conv_w, w_down.astype(BF16),
             final_g.reshape(1, d)]
    specs += [vec, vec, vec, _const_spec((1, d)), _const_spec((d, 2 * D_FF)), _const_spec((3, D_FF)),
              _const_spec((D_FF, d)), _const_spec((1, d))]
    return pl.pallas_call(
        functools.partial(_ffn_kernel, has_attn=attn is not None, final=final),
        grid=(bsz, seq // t),
        in_specs=specs,
        out_specs=pl.BlockSpec((1, t, d), lambda b, i: (b, i, 0)),
        out_shape=jax.ShapeDtypeStruct(x.shape, F32),
        scratch_shapes=[pltpu.VMEM((t + HALO, FF_CHUNK), F32), pltpu.VMEM((HALO, D_FF), F32)],
        compiler_params=_params(("arbitrary", "arbitrary")),
        name="conv_ffn_attnproj" if attn is not None else "conv_ffn",
    )(*args)


KV_ROWS = NSA_KV * NSA_DH


def _nsa_proj_kernel(x_ref, sh_ref, sc_ref, ng_ref, wa_ref, wq_ref, wg_ref, wv_ref,
                     kv_ref, qt_ref, gt_ref, vst_ref, vwt_ref):
    hm = _norm_mod(x_ref[0], ng_ref[...], sc_ref[0], sh_ref[0]).astype(BF16)
    kv_ref[0] = jnp.dot(hm, wa_ref[...], preferred_element_type=F32).astype(BF16)
    qt = lax.dot_general(wq_ref[...], hm, _NT, preferred_element_type=F32)
    qt_ref[0, 0] = (qt * (NSA_DH ** -0.5)).astype(BF16)
    gt_ref[0] = _sigmoid(lax.dot_general(wg_ref[...], hm, _NT, preferred_element_type=F32))
    vt = lax.dot_general(wv_ref[...], hm, _NT, preferred_element_type=F32).astype(BF16)
    vst_ref[0, 0] = vt[:KV_ROWS]
    vwt_ref[0, 0] = vt[KV_ROWS:]


def _nsa_proj(x, sh, sc, norm_g, w_in):
    bsz, seq, d = x.shape
    t = TOK_TILE
    nt = seq // t
    nq, ng = NSA_HEADS * NSA_DH, 3 * NSA_HEADS
    o = nq + ng
    w = w_in.astype(BF16)
    col = lambda j: w[:, o + j * KV_ROWS:o + (j + 1) * KV_ROWS]
    w_a = jnp.concatenate([col(0), col(1), col(2), col(4)], axis=1)
    w_q = w[:, :nq].T
    w_g = w[:, nq:o].T
    w_v = jnp.concatenate([col(3), col(5)], axis=1).T
    vec = pl.BlockSpec((1, 1, d), lambda b, i: (b, 0, 0))
    tiled = lambda r: pl.BlockSpec((1, 1, r, t), lambda b, i: (b, i, 0, 0))
    return pl.pallas_call(
        _nsa_proj_kernel,
        grid=(bsz, nt),
        in_specs=[pl.BlockSpec((1, t, d), lambda b, i: (b, i, 0)), vec, vec, _const_spec((1, d)),
                  _const_spec((d, 4 * KV_ROWS)), _const_spec((nq, d)), _const_spec((ng, d)),
                  _const_spec((2 * KV_ROWS, d))],
        out_specs=[pl.BlockSpec((1, t, 4 * KV_ROWS), lambda b, i: (b, i, 0)), tiled(nq),
                   pl.BlockSpec((1, ng, t), lambda b, i: (b, 0, i)), tiled(KV_ROWS), tiled(KV_ROWS)],
        out_shape=[jax.ShapeDtypeStruct((bsz, seq, 4 * KV_ROWS), BF16),
                   jax.ShapeDtypeStruct((bsz, nt, nq, t), BF16),
                   jax.ShapeDtypeStruct((bsz, ng, seq), F32),
                   jax.ShapeDtypeStruct((bsz, nt, KV_ROWS, t), BF16),
                   jax.ShapeDtypeStruct((bsz, nt, KV_ROWS, t), BF16)],
        compiler_params=_params(("arbitrary", "arbitrary")),
        name="nsa_in_proj",
    )(x, sh, sc, norm_g.reshape(1, d), w_a, w_q, w_g, w_v)


def _compress_kernel(hk_ref, hv_ref, pk_ref, pv_ref, w1k_ref, w2k_ref, w1v_ref, w2vt_ref, kc_ref, vct_ref):
    half = CMP_STRIDE * NSA_DH

    def hidden(h_ref, pos_ref, w1_ref):
        h = h_ref[0, 0]
        n = h.shape[0]
        top = jnp.dot(h, w1_ref[:half, :], preferred_element_type=F32)
        bot = jnp.dot(h, w1_ref[half:, :], preferred_element_type=F32)
        pos = jnp.dot(jnp.broadcast_to(pos_ref[...], (8, 2 * half)).astype(BF16), w1_ref[...],
                      preferred_element_type=F32)[0:1]
        return _silu(top + pltpu.roll(bot, n - 1, 0) + pos).astype(BF16)

    kc_ref[0, 0] = jnp.dot(hidden(hk_ref, pk_ref, w1k_ref), w2k_ref[...],
                           preferred_element_type=F32).astype(BF16)
    vct_ref[0, 0] = lax.dot_general(w2vt_ref[...], hidden(hv_ref, pv_ref, w1v_ref), _NT,
                                    preferred_element_type=F32).astype(BF16)


def _compress(hk, hv, pos_k, pos_v, w1_k, w2_k, w1_v, w2_v):
    bsz, g, n, width = hk.shape
    blk = pl.BlockSpec((1, 1, n, width), lambda b, j: (b, j, 0, 0))
    return pl.pallas_call(
        _compress_kernel,
        grid=(bsz, g),
        in_specs=[blk, blk, _const_spec((1, width * 2)), _const_spec((1, width * 2)),
                  _const_spec((2 * width, CMP_HIDDEN)), _const_spec((CMP_HIDDEN, NSA_DH)),
                  _const_spec((2 * width, CMP_HIDDEN)), _const_spec((NSA_DH, CMP_HIDDEN))],
        out_specs=[pl.BlockSpec((1, 1, n, NSA_DH), lambda b, j: (b, j, 0, 0)),
                   pl.BlockSpec((1, 1, NSA_DH, n), lambda b, j: (b, j, 0, 0))],
        out_shape=[jax.ShapeDtypeStruct((bsz, g, n, NSA_DH), BF16),
                   jax.ShapeDtypeStruct((bsz, g, NSA_DH, n), BF16)],
        compiler_params=_params(("arbitrary", "arbitrary")),
        name="nsa_compress",
    )(hk, hv, pos_k.reshape(1, -1), pos_v.reshape(1, -1), w1_k.astype(BF16), w2_k.astype(BF16),
      w1_v.astype(BF16), w2_v.T.astype(BF16))


def _bucket_upper_bounds():
    n = np.arange(4 * REL_MAX_DIST, dtype=np.float64)
    exact = REL_BUCKETS // 2
    large = exact + (np.log(np.maximum(n, exact) / exact) / math.log(REL_MAX_DIST / exact)
                     * (REL_BUCKETS - exact)).astype(np.int64)
    bucket = np.where(n < exact, n.astype(np.int64), np.minimum(large, REL_BUCKETS - 1))
    return [int(np.max(np.nonzero(bucket <= j)[0])) for j in range(REL_BUCKETS - 1)]


def _bias_of_dist(dist, rb_ref, heads, uppers):
    vals = [jnp.full(dist.shape, rb_ref[REL_BUCKETS - 1, h], F32) for h in heads]
    for j in range(REL_BUCKETS - 2, -1, -1):
        m = dist <= uppers[j]
        vals = [jnp.where(m, rb_ref[j, h], v) for h, v in zip(heads, vals)]
    return vals


def _bias_near_kernel(rb_ref, o_ref, *, uppers):
    delta = pl.program_id(0)
    t = o_ref.shape[2]
    heads = list(range(NSA_HEADS))

    def strip(i, carry):
        r0 = pl.multiple_of(i * 8, 8)
        key = r0 + lax.broadcasted_iota(jnp.int32, (8, t), 0)
        tok = lax.broadcasted_iota(jnp.int32, (8, t), 1)
        vals = _bias_of_dist(delta * t + tok - key, rb_ref, heads, uppers)
        for h in heads:
            o_ref[h, 0, pl.ds(r0, 8), :] = vals[h]
        return carry

    lax.fori_loop(0, t // 8, strip, 0)


def _bias_cmp_kernel(rb_ref, o_ref, *, uppers):
    tb = pl.program_id(0)
    n, t = o_ref.shape[1], o_ref.shape[2]
    heads = list(range(NSA_HEADS))

    def strip(i, carry):
        r0 = pl.multiple_of(i * 8, 8)
        blk = r0 + lax.broadcasted_iota(jnp.int32, (8, t), 0)
        tok = tb * t + lax.broadcasted_iota(jnp.int32, (8, t), 1)
        vals = _bias_of_dist(tok - (blk * CMP_STRIDE + CMP_LEN - 1), rb_ref, heads, uppers)
        for h in heads:
            o_ref[h, pl.ds(r0, 8), :] = vals[h]
        return carry

    lax.fori_loop(0, n // 8, strip, 0)


def _bias_tables(rel_bias, seq):
    t = TOK_TILE
    n_cmp = seq // CMP_STRIDE
    uppers = _bucket_upper_bounds()
    smem = pl.BlockSpec(memory_space=pltpu.SMEM)
    near = pl.pallas_call(
        functools.partial(_bias_near_kernel, uppers=uppers),
        grid=(NEAR_TILES,),
        in_specs=[smem],
        out_specs=pl.BlockSpec((NSA_HEADS, 1, t, t), lambda d: (0, d, 0, 0)),
        out_shape=jax.ShapeDtypeStruct((NSA_HEADS, NEAR_TILES, t, t), F32),
        compiler_params=_params(("arbitrary",)),
        name="relbias_near",
    )(rel_bias)
    cmp_b = pl.pallas_call(
        functools.partial(_bias_cmp_kernel, uppers=uppers),
        grid=(seq // t,),
        in_specs=[smem],
        out_specs=pl.BlockSpec((NSA_HEADS, n_cmp, t), lambda i: (0, 0, i)),
        out_shape=jax.ShapeDtypeStruct((NSA_HEADS, n_cmp, seq), F32),
        compiler_params=_params(("arbitrary",)),
        name="relbias_cmp",
    )(rel_bias)
    return near, cmp_b


def _cmp_to_sel_t(seq):
    n_cmp, n_sel = seq // CMP_STRIDE, seq // SEL_BLOCK
    c_start = np.arange(n_cmp)[:, None] * CMP_STRIDE
    s_start = np.arange(n_sel)[None, :] * SEL_BLOCK
    inside = np.clip(np.minimum(c_start + CMP_LEN, s_start + SEL_BLOCK) - np.maximum(c_start, s_start), 0, None)
    return (inside / CMP_LEN).T.astype(np.float32)


def _cmp_topk_kernel(qt_ref, kc_ref, vct_ref, bias_ref, c2s_ref, oc_ref, sel_ref):
    tb = pl.program_id(1)
    n, t = bias_ref.shape[1], bias_ref.shape[2]
    n_sel = sel_ref.shape[3]
    tok = tb * t + lax.broadcasted_iota(jnp.int32, (n, t), 1)
    cmp_end = lax.broadcasted_iota(jnp.int32, (n, t), 0) * CMP_STRIDE + (CMP_LEN - 1)
    visible = cmp_end <= tok
    kc = kc_ref[0, 0]
    vct = vct_ref[0, 0]
    psum = jnp.zeros((n, t), F32)
    for i in range(NSA_HPG):
        rows = slice(i * NSA_DH, (i + 1) * NSA_DH)
        s = jnp.dot(kc, qt_ref[0, 0, rows, :], preferred_element_type=F32) + bias_ref[i]
        s = jnp.where(visible, s, NEG)
        m = jnp.max(s, axis=0, keepdims=True)
        e = jnp.where(visible, jnp.exp(s - m), 0.0)
        den = jnp.sum(e, axis=0, keepdims=True)
        p = e / jnp.where(den > 0.0, den, 1.0)
        oc_ref[0, 0, rows, :] = jnp.dot(vct, p.astype(BF16), preferred_element_type=F32).astype(BF16)
        psum = psum + p
    imp = jnp.dot(c2s_ref[...], psum.astype(BF16), preferred_element_type=F32)

    blk = lax.broadcasted_iota(jnp.int32, (n_sel, t), 0)
    cur = (tb * t + lax.broadcasted_iota(jnp.int32, (n_sel, t), 1)) // SEL_BLOCK
    forced = (blk == 0) | (blk == cur) | (blk == cur - 1)
    score = jnp.where(blk > cur, -jnp.inf, jnp.where(forced, jnp.inf, imp))
    groups = n_sel // 8
    sub = lax.broadcasted_iota(jnp.int32, (8, t), 0)
    part = [score[8 * v:8 * v + 8] for v in range(groups)]
    cnt = [jnp.zeros((8, t), F32) for _ in range(groups)]
    for sp in range(n_sel):
        r = jnp.broadcast_to(score[sp:sp + 1], (8, t))
        for v in range(groups):
            if sp < 8 * v:
                beats = r >= part[v]
            elif sp >= 8 * v + 8:
                beats = r > part[v]
            else:
                beats = (r > part[v]) | ((r == part[v]) & (sub > sp - 8 * v))
            cnt[v] = cnt[v] + jnp.where(beats, 1.0, 0.0)
    for v in range(groups):
        sel_ref[0, 0, 0, 8 * v:8 * v + 8, :] = jnp.where(cnt[v] < min(SEL_TOPN, n_sel), 1.0, 0.0)


def _cmp_topk(q_t, k_cmp, v_cmp_t, bias_cmp, seq):
    bsz, nt, nq, t = q_t.shape
    n_cmp, n_sel = seq // CMP_STRIDE, seq // SEL_BLOCK
    grp = NSA_HPG * NSA_DH
    c2s = jnp.asarray(_cmp_to_sel_t(seq), BF16)
    return pl.pallas_call(
        _cmp_topk_kernel,
        grid=(NSA_KV, nt, bsz),
        in_specs=[pl.BlockSpec((1, 1, grp, t), lambda g, i, b: (b, i, g, 0)),
                  pl.BlockSpec((1, 1, n_cmp, NSA_DH), lambda g, i, b: (b, g, 0, 0)),
                  pl.BlockSpec((1, 1, NSA_DH, n_cmp), lambda g, i, b: (b, g, 0, 0)),
                  pl.BlockSpec((NSA_HPG, n_cmp, t), lambda g, i, b: (g, 0, i)),
                  _const_spec((n_sel, n_cmp))],
        out_specs=[pl.BlockSpec((1, 1, grp, t), lambda g, i, b: (b, i, g, 0)),
                   pl.BlockSpec((1, 1, 1, n_sel, t), lambda g, i, b: (b, g, i, 0, 0))],
        out_shape=[jax.ShapeDtypeStruct((bsz, nt, nq, t), BF16),
                   jax.ShapeDtypeStruct((bsz, NSA_KV, nt, n_sel, t), F32)],
        compiler_params=_params(("arbitrary", "arbitrary", "arbitrary")),
        name="nsa_cmp_topk",
    )(q_t, k_cmp, v_cmp_t, bias_cmp, c2s)


def _sel_win_kernel(rb_ref, qt_ref, ks_ref, vst_ref, kw_ref, vwt_ref, sel_ref, near_ref, gt_ref, oc_ref, o_ref):
    h = pl.program_id(0)
    tb = pl.program_id(2)
    t = qt_ref.shape[3]
    per_tile = t // SEL_BLOCK
    qt = qt_ref[0, 0]
    key = lax.broadcasted_iota(jnp.int32, (t, t), 0)
    tok = lax.broadcasted_iota(jnp.int32, (t, t), 1)
    lag = tok - key

    def attend(carry, k_ref, vt_ref, kt, bias, mask):
        m, l, acc = carry
        s = jnp.dot(k_ref[0, 0, pl.ds(pl.multiple_of(kt * t, t), t), :], qt, preferred_element_type=F32) + bias
        s = jnp.where(mask, s, NEG)
        m_new = jnp.maximum(m, jnp.max(s, axis=0, keepdims=True))
        alpha = jnp.exp(m - m_new)
        p = jnp.exp(s - m_new)
        l = alpha * l + jnp.sum(p, axis=0, keepdims=True)
        acc = alpha * acc + jnp.dot(vt_ref[0, kt], p.astype(BF16), preferred_element_type=F32)
        return m_new, l, acc

    def sel_mask(kt, valid=True):
        rows = [jnp.broadcast_to(sel_ref[0, 0, 0, pl.ds(kt * per_tile + j, 1), :], (SEL_BLOCK, t))
                for j in range(per_tile)]
        return jnp.concatenate(rows, axis=0) > jnp.where(valid, 0.5, 2.0)

    init = (jnp.full((1, t), NEG, F32), jnp.zeros((1, t), F32), jnp.zeros((NSA_DH, t), F32))

    far_bias = rb_ref[REL_BUCKETS - 1, h]
    carry = lax.fori_loop(0, jnp.maximum(tb - (NEAR_TILES - 1), 0),
                          lambda kt, c: attend(c, ks_ref, vst_ref, kt, far_bias, sel_mask(kt)), init)
    for delta in range(NEAR_TILES - 1, -1, -1):
        kt = jnp.maximum(tb - delta, 0)
        mask = sel_mask(kt, tb >= delta)
        if delta == 0:
            mask = mask & (lag >= 0)
        carry = attend(carry, ks_ref, vst_ref, kt, near_ref[0, delta], mask)
    o_s = carry[2] / carry[1]

    carry = init
    for delta in range(WIN_TILES - 1, -1, -1):
        kt = jnp.maximum(tb - delta, 0)
        dist = lag + delta * t
        mask = (dist >= 0) & (dist < jnp.where(tb >= delta, WINDOW, 0))
        carry = attend(carry, kw_ref, vwt_ref, kt, near_ref[0, delta], mask)
    o_w = carry[2] / carry[1]

    gate = lambda br: gt_ref[0, pl.ds(br * NSA_HEADS + h, 1), :]
    o = gate(0) * oc_ref[0, 0].astype(F32) + gate(1) * o_s + gate(2) * o_w
    o_ref[0, 0] = o.astype(BF16)


def _sel_win(rel_bias, q_t, ks, vs_t, kw, vw_t, sel, near, gates_t, oc_t):
    bsz, nt, nq, t = q_t.shape
    seq = nt * t
    n_sel = seq // SEL_BLOCK
    head = pl.BlockSpec((1, 1, NSA_DH, t), lambda h, b, i: (b, i, h, 0))
    keys = pl.BlockSpec((1, 1, seq, NSA_DH), lambda h, b, i: (b, h // NSA_HPG, 0, 0))
    vals = pl.BlockSpec((1, nt, NSA_DH, t), lambda h, b, i: (b, 0, h // NSA_HPG, 0))
    return pl.pallas_call(
        _sel_win_kernel,
        grid=(NSA_HEADS, bsz, nt),
        in_specs=[pl.BlockSpec(memory_space=pltpu.SMEM), head, keys, vals, keys, vals,
                  pl.BlockSpec((1, 1, 1, n_sel, t), lambda h, b, i: (b, h // NSA_HPG, i, 0, 0)),
                  pl.BlockSpec((1, NEAR_TILES, t, t), lambda h, b, i: (h, 0, 0, 0)),
                  pl.BlockSpec((1, 3 * NSA_HEADS, t), lambda h, b, i: (b, 0, i)),
                  head],
        out_specs=head,
        out_shape=jax.ShapeDtypeStruct((bsz, nt, nq, t), BF16),
        compiler_params=_params(("arbitrary", "arbitrary", "arbitrary")),
        name="nsa_sel_win",
    )(rel_bias, q_t, ks, vs_t, kw, vw_t, sel, near, gates_t, oc_t)


def kernel(x, c, mod_w, mod_b, norm_mix_g, norm_ffn_g, ab_w_in, ab_w_out, hgrn_lb_logits, hgrn_onorm_g, sconv_w, nsa_w_in, nsa_w_out, nsa_cmp_pos_k, nsa_cmp_pos_v, nsa_cmp_w1_k, nsa_cmp_w2_k, nsa_cmp_w1_v, nsa_cmp_w2_v, rel_bias, ffn_w_up, ffn_conv_w, ffn_w_down, final_norm_g):
    bsz, seq, d = x.shape
    assert d == D_MODEL and seq % TOK_TILE == 0
    lower = jnp.cumsum(jax.nn.softmax(hgrn_lb_logits.astype(F32), axis=0), axis=0)
    mod = _modulation(c, mod_w, mod_b)
    parts = lambda l: [mod[l, :, j * d:(j + 1) * d].reshape(bsz, 1, d) for j in range(6)]

    sh1, sc1, g1, sh2, sc2, g2 = parts(0)
    x = _mixer0(x, sh1, sc1, g1, norm_mix_g[0], ab_w_in[0], ab_w_out[0], lower[0], hgrn_onorm_g[0], sconv_w[0])
    x = _conv_ffn(x, sh2, sc2, g2, norm_ffn_g[0], ffn_w_up[0], ffn_conv_w[0], ffn_w_down[0], final_norm_g)

    sh1, sc1, g1, sh2, sc2, g2 = parts(1)
    kv, q_t, gates_t, vs_t, vw_t = _nsa_proj(x, sh1, sc1, norm_mix_g[1], nsa_w_in[0])
    kvg = kv.reshape(bsz, seq, 4, NSA_KV, NSA_DH).transpose(2, 0, 3, 1, 4)
    to_blocks = lambda a: a.reshape(bsz, NSA_KV, seq // CMP_STRIDE, CMP_STRIDE * NSA_DH)
    k_cmp, v_cmp_t = _compress(to_blocks(kvg[0]), to_blocks(kvg[1]), nsa_cmp_pos_k[0], nsa_cmp_pos_v[0],
                               nsa_cmp_w1_k[0], nsa_cmp_w2_k[0], nsa_cmp_w1_v[0], nsa_cmp_w2_v[0])
    near, bias_cmp = _bias_tables(rel_bias.astype(F32), seq)
    oc_t, sel = _cmp_topk(q_t, k_cmp, v_cmp_t, bias_cmp, seq)
    o_t = _sel_win(rel_bias.astype(F32), q_t, kvg[2], vs_t, kvg[3], vw_t, sel, near, gates_t, oc_t)
    return _conv_ffn(x, sh2, sc2, g2, norm_ffn_g[1], ffn_w_up[1], ffn_conv_w[1], ffn_w_down[1], final_norm_g,
                     attn=(o_t, nsa_w_out[0], g1), final=True)
```

```python
import functools
import math

import jax
import jax.numpy as jnp
import numpy as np
from jax import lax
from jax.experimental import pallas as pl
from jax.experimental.pallas import tpu as pltpu

F32 = jnp.float32
BF16 = jnp.bfloat16

EPS = 1e-6
D_MODEL = 1024
D_FF = 2816

HG_HEADS = 4
HG_DK = 128
HG_WIDTH = HG_HEADS * HG_DK
SC_WIDTH = D_MODEL - HG_WIDTH
AB_IN = 4 * HG_WIDTH + 3 * SC_WIDTH
HG_CHUNK = 64
HG_LEVELS = 6

NSA_HEADS = 16
NSA_KV = 4
NSA_HPG = NSA_HEADS // NSA_KV
NSA_DH = D_MODEL // NSA_HEADS
CMP_LEN = 32
CMP_STRIDE = 16
CMP_HIDDEN = 256
SEL_BLOCK = 64
SEL_TOPN = 16
WINDOW = 512
REL_BUCKETS = 32
REL_MAX_DIST = 1024

TOK_TILE = 256
NEAR_TILES = 5
SEL_TABLES = NEAR_TILES + 1
WIN_TILES = WINDOW // TOK_TILE + 1
NEG = -1e30
VMEM_LIMIT = 56 * 1024 * 1024
HALO = 8

_NT = (((1,), (1,)), ((), ()))
_TN = (((0,), (0,)), ((), ()))


def _sigmoid(x):
    return 1.0 / (1.0 + jnp.exp(-x))


def _silu(x):
    return x * _sigmoid(x)


def _norm_mod(x, g, sc, sh):
    ms = jnp.mean(x * x, axis=-1, keepdims=True)
    return (x * lax.rsqrt(ms + EPS) * g) * (1.0 + sc) + sh


def _const_spec(shape):
    n = len(shape)
    return pl.BlockSpec(shape, lambda *_: (0,) * n, pipeline_mode=pl.Buffered(1))


def _params(sem):
    return pltpu.CompilerParams(dimension_semantics=sem, vmem_limit_bytes=VMEM_LIMIT)


def _mod_kernel(c_ref, w_ref, b_ref, o_ref):
    c = c_ref[...]
    o_ref[0] = jnp.dot(_silu(c), w_ref[0], preferred_element_type=F32,
                       precision=lax.Precision.HIGHEST) + b_ref[0]


def _modulation(c, mod_w, mod_b):
    depth, d, n = mod_w.shape
    bsz = c.shape[0]
    tn = 1024
    return pl.pallas_call(
        _mod_kernel,
        grid=(depth, n // tn),
        in_specs=[pl.BlockSpec((bsz, d), lambda l, j: (0, 0)),
                  pl.BlockSpec((1, d, tn), lambda l, j: (l, 0, j)),
                  pl.BlockSpec((1, 1, tn), lambda l, j: (l, 0, j))],
        out_specs=pl.BlockSpec((1, bsz, tn), lambda l, j: (l, 0, j)),
        out_shape=jax.ShapeDtypeStruct((depth, bsz, n), F32),
        compiler_params=_params(("arbitrary", "arbitrary")),
        name="adaln_mod",
    )(c, mod_w, mod_b.reshape(depth, 1, n))


def _hgrn_decay_matrix():
    t = np.arange(HG_CHUNK)
    tril = (t[None, :] <= t[:, None]).astype(np.float32)
    mats = [tril]
    for lvl in range(1, HG_LEVELS + 1):
        m = 1 << (lvl - 1)
        mid = (t // (2 * m)) * (2 * m) + m - 1
        mats.append(tril - tril[mid])
    return np.concatenate(mats, axis=0)


def _mixer0_kernel(x_ref, sh_ref, sc_ref, gt_ref, ng_ref, win_ref, wout_ref, lb_ref, og_ref, cw_ref,
                   pm_ref, o_ref, proj_ref, cat_ref, st_ref, ubuf_ref):
    tile = x_ref.shape[1]
    c64 = HG_CHUNK

    @pl.when(pl.program_id(1) == 0)
    def _():
        st_ref[...] = jnp.zeros_like(st_ref)
        ubuf_ref[0:HALO, :] = jnp.zeros((HALO, SC_WIDTH), F32)

    x = x_ref[0]
    hm = _norm_mod(x, ng_ref[...], sc_ref[0], sh_ref[0]).astype(BF16)
    proj_ref[...] = jnp.dot(hm, win_ref[...], preferred_element_type=F32)

    row = lax.broadcasted_iota(jnp.int32, (c64, HG_DK), 0)
    r64 = lax.broadcasted_iota(jnp.int32, (c64, c64), 0)
    c64i = lax.broadcasted_iota(jnp.int32, (c64, c64), 1)
    second = [None] + [(row & (1 << (lvl - 1))) != 0 for lvl in range(1, HG_LEVELS + 1)]
    same = [None] + [(r64 >> lvl) == (c64i >> lvl) for lvl in range(1, HG_LEVELS + 1)]
    eye = r64 == c64i
    pm = pm_ref[...]
    og = og_ref[...]

    def chunk(c, carry):
        r0 = pl.multiple_of(c * c64, c64)
        rows = pl.ds(r0, c64)
        for h in range(HG_HEADS):
            lanes = lambda j: pl.ds(j * HG_WIDTH + h * HG_DK, HG_DK)
            q = proj_ref[rows, lanes(0)]
            f = proj_ref[rows, lanes(1)]
            v = proj_ref[rows, lanes(2)]
            g = proj_ref[rows, lanes(3)]
            lb = lb_ref[:, h * HG_DK:(h + 1) * HG_DK]
            fg = lb + (1.0 - lb) * _sigmoid(f)
            k = 1.0 - fg
            lg = jnp.log(fg)
            lg_hi = lg.astype(BF16)
            lg_lo = (lg - lg_hi.astype(F32)).astype(BF16)
            dall = jnp.dot(pm, jnp.concatenate([lg_hi, lg_lo], axis=1), preferred_element_type=F32)
            dall = dall[:, :HG_DK] + dall[:, HG_DK:]
            b = dall[0:c64]
            a = jnp.zeros((c64, c64), F32)
            for lvl in range(1, HG_LEVELS + 1):
                e = jnp.exp(-jnp.abs(dall[lvl * c64:(lvl + 1) * c64]))
                qt = jnp.where(second[lvl], q * e, 0.0).astype(BF16)
                kt = jnp.where(second[lvl], 0.0, k * e).astype(BF16)
                al = lax.dot_general(qt, kt, _NT, preferred_element_type=F32)
                a = a + (al if lvl == HG_LEVELS else jnp.where(same[lvl], al, 0.0))
            a = jnp.where(eye, jnp.sum(q * k, axis=-1, keepdims=True), a)
            vb = v.astype(BF16)
            st = st_ref[h]
            o = jnp.dot(a.astype(BF16), vb, preferred_element_type=F32)
            o = o + lax.dot_general((q * jnp.exp(b)).astype(BF16), st.astype(BF16), _NT,
                                    preferred_element_type=F32)
            b_last = b[c64 - 1:c64, :]
            khat = (k * jnp.exp(b_last - b)).astype(BF16)
            st_ref[h] = st * jnp.exp(b_last) + lax.dot_general(vb, khat, _TN, preferred_element_type=F32)
            on = o * lax.rsqrt(jnp.mean(o * o, axis=-1, keepdims=True) + EPS) * og
            cat_ref[rows, h * HG_DK:(h + 1) * HG_DK] = (on * _silu(g)).astype(BF16)
        return carry

    lax.fori_loop(0, tile // c64, chunk, 0)

    off = 4 * HG_WIDTH
    u = proj_ref[:, off + SC_WIDTH:off + 2 * SC_WIDTH] * proj_ref[:, off + 2 * SC_WIDTH:off + 3 * SC_WIDTH]
    ubuf_ref[HALO:HALO + tile, :] = u
    cw = cw_ref[...]
    conv = (ubuf_ref[HALO - 2:HALO - 2 + tile, :] * cw[0:1] + ubuf_ref[HALO - 1:HALO - 1 + tile, :] * cw[1:2]
            + u * cw[2:3])
    ubuf_ref[0:HALO, :] = u[tile - HALO:tile, :]
    cat_ref[:, HG_WIDTH:] = (proj_ref[:, off:off + SC_WIDTH] * conv).astype(BF16)

    y = jnp.dot(cat_ref[...], wout_ref[...], preferred_element_type=F32)
    o_ref[0] = x + gt_ref[0] * y


def _mixer0(x, sh, sc, gt, norm_g, w_in, w_out, lower, onorm_g, sconv_w):
    bsz, seq, d = x.shape
    t = TOK_TILE
    pm = jnp.asarray(_hgrn_decay_matrix(), BF16)
    vec = pl.BlockSpec((1, 1, d), lambda b, i: (b, 0, 0))
    return pl.pallas_call(
        _mixer0_kernel,
        grid=(bsz, seq // t),
        in_specs=[pl.BlockSpec((1, t, d), lambda b, i: (b, i, 0)), vec, vec, vec,
                  _const_spec((1, d)), _const_spec((d, AB_IN)), _const_spec((d, d)),
                  _const_spec((1, HG_WIDTH)), _const_spec((1, HG_DK)), _const_spec((3, SC_WIDTH)),
                  _const_spec(pm.shape)],
        out_specs=pl.BlockSpec((1, t, d), lambda b, i: (b, i, 0)),
        out_shape=jax.ShapeDtypeStruct(x.shape, F32),
        scratch_shapes=[pltpu.VMEM((t, AB_IN), F32), pltpu.VMEM((t, d), BF16),
                        pltpu.VMEM((HG_HEADS, HG_DK, HG_DK), F32), pltpu.VMEM((t + HALO, SC_WIDTH), F32)],
        compiler_params=_params(("arbitrary", "arbitrary")),
        name="mixer_hgrn_sconv",
    )(x, sh, sc, gt, norm_g.reshape(1, d), w_in.astype(BF16), w_out.astype(BF16),
      lower.reshape(1, HG_WIDTH), onorm_g.reshape(1, HG_DK), sconv_w, pm)


FF_CHUNK = 1408


def _ffn_kernel(*refs, has_attn, final):
    if has_attn:
        (x_ref, ot_ref, wo_ref, g1_ref, sh_ref, sc_ref, g2_ref, ng_ref, wup_ref, cw_ref, wdn_ref, fg_ref,
         o_ref, gbuf_ref, carry_ref) = refs
    else:
        (x_ref, sh_ref, sc_ref, g2_ref, ng_ref, wup_ref, cw_ref, wdn_ref, fg_ref,
         o_ref, gbuf_ref, carry_ref) = refs
    tile = x_ref.shape[1]

    @pl.when(pl.program_id(1) == 0)
    def _():
        carry_ref[...] = jnp.zeros_like(carry_ref)

    x = x_ref[0]
    if has_attn:
        x = x + g1_ref[0] * lax.dot_general(ot_ref[0, 0], wo_ref[...], _TN, preferred_element_type=F32)
    hf = _norm_mod(x, ng_ref[...], sc_ref[0], sh_ref[0]).astype(BF16)
    acc = jnp.zeros((tile, D_MODEL), F32)
    for c0 in range(0, D_FF, FF_CHUNK):
        gate = jnp.dot(hf, wup_ref[:, c0:c0 + FF_CHUNK], preferred_element_type=F32)
        up = jnp.dot(hf, wup_ref[:, D_FF + c0:D_FF + c0 + FF_CHUNK], preferred_element_type=F32)
        gbuf_ref[0:HALO, :] = carry_ref[:, c0:c0 + FF_CHUNK]
        gbuf_ref[HALO:HALO + tile, :] = gate
        carry_ref[:, c0:c0 + FF_CHUNK] = gate[tile - HALO:tile, :]
        cw = cw_ref[:, c0:c0 + FF_CHUNK]
        conv = (gbuf_ref[HALO - 2:HALO - 2 + tile, :] * cw[0:1] + gbuf_ref[HALO - 1:HALO - 1 + tile, :] * cw[1:2]
                + gate * cw[2:3])
        act = (_silu(conv) * up).astype(BF16)
        acc = acc + jnp.dot(act, wdn_ref[c0:c0 + FF_CHUNK, :], preferred_element_type=F32)
    out = x + g2_ref[0] * acc
    if final:
        out = out * lax.rsqrt(jnp.mean(out * out, axis=-1, keepdims=True) + EPS) * fg_ref[...]
    o_ref[0] = out


def _conv_ffn(x, sh, sc, g2, norm_g, w_up, conv_w, w_down, final_g, attn=None, final=False):
    bsz, seq, d = x.shape
    t = TOK_TILE
    vec = pl.BlockSpec((1, 1, d), lambda b, i: (b, 0, 0))
    args, specs = [x], [pl.BlockSpec((1, t, d), lambda b, i: (b, i, 0))]
    if attn is not None:
        o_t, w_o, g1 = attn
        args += [o_t, w_o.astype(BF16), g1]
        specs += [pl.BlockSpec((1, 1, d, t), lambda b, i: (b, i, 0, 0)), _const_spec((d, d)), vec]
    args += [sh, sc, g2, norm_g.reshape(1, d), w_up.astype(BF16), conv_w, w_down.astype(BF16),
             final_g.reshape(1, d)]
    specs += [vec, vec, vec, _const_spec((1, d)), _const_spec((d, 2 * D_FF)), _const_spec((3, D_FF)),
              _const_spec((D_FF, d)), _const_spec((1, d))]
    return pl.pallas_call(
        functools.partial(_ffn_kernel, has_attn=attn is not None, final=final),
        grid=(bsz, seq // t),
        in_specs=specs,
        out_specs=pl.BlockSpec((1, t, d), lambda b, i: (b, i, 0)),
        out_shape=jax.ShapeDtypeStruct(x.shape, F32),
        scratch_shapes=[pltpu.VMEM((t + HALO, FF_CHUNK), F32), pltpu.VMEM((HALO, D_FF), F32)],
        compiler_params=_params(("arbitrary", "arbitrary")),
        name="conv_ffn_attnproj" if attn is not None else "conv_ffn",
    )(*args)


KV_ROWS = NSA_KV * NSA_DH


def _nsa_proj_kernel(x_ref, sh_ref, sc_ref, ng_ref, wa_ref, wq_ref, wg_ref, wv_ref,
                     kv_ref, qt_ref, gt_ref, vst_ref, vwt_ref):
    hm = _norm_mod(x_ref[0], ng_ref[...], sc_ref[0], sh_ref[0]).astype(BF16)
    kv_ref[0] = jnp.dot(hm, wa_ref[...], preferred_element_type=F32).astype(BF16)
    qt = lax.dot_general(wq_ref[...], hm, _NT, preferred_element_type=F32)
    qt_ref[0, 0] = (qt * (NSA_DH ** -0.5)).astype(BF16)
    gt_ref[0] = _sigmoid(lax.dot_general(wg_ref[...], hm, _NT, preferred_element_type=F32))
    vt = lax.dot_general(wv_ref[...], hm, _NT, preferred_element_type=F32).astype(BF16)
    vst_ref[0, 0] = vt[:KV_ROWS]
    vwt_ref[0, 0] = vt[KV_ROWS:]


def _nsa_proj(x, sh, sc, norm_g, w_in):
    bsz, seq, d = x.shape
    t = TOK_TILE
    nt = seq // t
    nq, ng = NSA_HEADS * NSA_DH, 3 * NSA_HEADS
    o = nq + ng
    w = w_in.astype(BF16)
    col = lambda j: w[:, o + j * KV_ROWS:o + (j + 1) * KV_ROWS]
    w_a = jnp.concatenate([col(0), col(1), col(2), col(4)], axis=1)
    w_q = w[:, :nq].T
    w_g = w[:, nq:o].T
    w_v = jnp.concatenate([col(3), col(5)], axis=1).T
    vec = pl.BlockSpec((1, 1, d), lambda b, i: (b, 0, 0))
    tiled = lambda r: pl.BlockSpec((1, 1, r, t), lambda b, i: (b, i, 0, 0))
    return pl.pallas_call(
        _nsa_proj_kernel,
        grid=(bsz, nt),
        in_specs=[pl.BlockSpec((1, t, d), lambda b, i: (b, i, 0)), vec, vec, _const_spec((1, d)),
                  _const_spec((d, 4 * KV_ROWS)), _const_spec((nq, d)), _const_spec((ng, d)),
                  _const_spec((2 * KV_ROWS, d))],
        out_specs=[pl.BlockSpec((1, t, 4 * KV_ROWS), lambda b, i: (b, i, 0)), tiled(nq),
                   pl.BlockSpec((1, ng, t), lambda b, i: (b, 0, i)), tiled(KV_ROWS), tiled(KV_ROWS)],
        out_shape=[jax.ShapeDtypeStruct((bsz, seq, 4 * KV_ROWS), BF16),
                   jax.ShapeDtypeStruct((bsz, nt, nq, t), BF16),
                   jax.ShapeDtypeStruct((bsz, ng, seq), F32),
                   jax.ShapeDtypeStruct((bsz, nt, KV_ROWS, t), BF16),
                   jax.ShapeDtypeStruct((bsz, nt, KV_ROWS, t), BF16)],
        compiler_params=_params(("arbitrary", "arbitrary")),
        name="nsa_in_proj",
    )(x, sh, sc, norm_g.reshape(1, d), w_a, w_q, w_g, w_v)


def _compress_kernel(hk_ref, hv_ref, pk_ref, pv_ref, w1k_ref, w2k_ref, w1v_ref, w2vt_ref, kc_ref, vct_ref):
    half = CMP_STRIDE * NSA_DH

    def hidden(h_ref, pos_ref, w1_ref):
        h = h_ref[0, 0]
        n = h.shape[0]
        top = jnp.dot(h, w1_ref[:half, :], preferred_element_type=F32)
        bot = jnp.dot(h, w1_ref[half:, :], preferred_element_type=F32)
        pos = jnp.dot(jnp.broadcast_to(pos_ref[...], (8, 2 * half)).astype(BF16), w1_ref[...],
                      preferred_element_type=F32)[0:1]
        return _silu(top + pltpu.roll(bot, n - 1, 0) + pos).astype(BF16)

    kc_ref[0, 0] = jnp.dot(hidden(hk_ref, pk_ref, w1k_ref), w2k_ref[...],
                           preferred_element_type=F32).astype(BF16)
    vct_ref[0, 0] = lax.dot_general(w2vt_ref[...], hidden(hv_ref, pv_ref, w1v_ref), _NT,
                                    preferred_element_type=F32).astype(BF16)


def _compress(hk, hv, pos_k, pos_v, w1_k, w2_k, w1_v, w2_v):
    bsz, g, n, width = hk.shape
    blk = pl.BlockSpec((1, 1, n, width), lambda b, j: (b, j, 0, 0))
    return pl.pallas_call(
        _compress_kernel,
        grid=(bsz, g),
        in_specs=[blk, blk, _const_spec((1, width * 2)), _const_spec((1, width * 2)),
                  _const_spec((2 * width, CMP_HIDDEN)), _const_spec((CMP_HIDDEN, NSA_DH)),
                  _const_spec((2 * width, CMP_HIDDEN)), _const_spec((NSA_DH, CMP_HIDDEN))],
        out_specs=[pl.BlockSpec((1, 1, n, NSA_DH), lambda b, j: (b, j, 0, 0)),
                   pl.BlockSpec((1, 1, NSA_DH, n), lambda b, j: (b, j, 0, 0))],
        out_shape=[jax.ShapeDtypeStruct((bsz, g, n, NSA_DH), BF16),
                   jax.ShapeDtypeStruct((bsz, g, NSA_DH, n), BF16)],
        compiler_params=_params(("arbitrary", "arbitrary")),
        name="nsa_compress",
    )(hk, hv, pos_k.reshape(1, -1), pos_v.reshape(1, -1), w1_k.astype(BF16), w2_k.astype(BF16),
      w1_v.astype(BF16), w2_v.T.astype(BF16))


def _bucket_upper_bounds():
    n = np.arange(4 * REL_MAX_DIST, dtype=np.float64)
    exact = REL_BUCKETS // 2
    large = exact + (np.log(np.maximum(n, exact) / exact) / math.log(REL_MAX_DIST / exact)
                     * (REL_BUCKETS - exact)).astype(np.int64)
    bucket = np.where(n < exact, n.astype(np.int64), np.minimum(large, REL_BUCKETS - 1))
    return [int(np.max(np.nonzero(bucket <= j)[0])) for j in range(REL_BUCKETS - 1)]


def _bias_of_dist(dist, rb_ref, heads, uppers):
    vals = [jnp.full(dist.shape, rb_ref[REL_BUCKETS - 1, h], F32) for h in heads]
    for j in range(REL_BUCKETS - 2, -1, -1):
        m = dist <= uppers[j]
        vals = [jnp.where(m, rb_ref[j, h], v) for h, v in zip(heads, vals)]
    return vals


def _bias_near_kernel(rb_ref, o_ref, *, uppers):
    d = pl.program_id(0)
    t = o_ref.shape[2]
    heads = list(range(NSA_HEADS))
    is_win = d >= SEL_TABLES
    delta = jnp.where(is_win, d - SEL_TABLES, d)
    limit = jnp.where(is_win, WINDOW, (SEL_TABLES + 1) * t)

    def strip(i, carry):
        r0 = pl.multiple_of(i * 8, 8)
        key = r0 + lax.broadcasted_iota(jnp.int32, (8, t), 0)
        tok = lax.broadcasted_iota(jnp.int32, (8, t), 1)
        dist = delta * t + tok - key
        valid = (dist >= 0) & (dist < limit)
        vals = _bias_of_dist(dist, rb_ref, heads, uppers)
        for h in heads:
            o_ref[h, 0, pl.ds(r0, 8), :] = jnp.where(valid, vals[h], NEG)
        return carry

    lax.fori_loop(0, t // 8, strip, 0)


def _bias_cmp_kernel(rb_ref, o_ref, *, uppers):
    tb = pl.program_id(0)
    n, t = o_ref.shape[1], o_ref.shape[2]
    heads = list(range(NSA_HEADS))

    def strip(i, carry):
        r0 = pl.multiple_of(i * 8, 8)
        blk = r0 + lax.broadcasted_iota(jnp.int32, (8, t), 0)
        tok = tb * t + lax.broadcasted_iota(jnp.int32, (8, t), 1)
        vals = _bias_of_dist(tok - (blk * CMP_STRIDE + CMP_LEN - 1), rb_ref, heads, uppers)
        for h in heads:
            o_ref[h, pl.ds(r0, 8), :] = vals[h]
        return carry

    lax.fori_loop(0, n // 8, strip, 0)


def _bias_tables(rel_bias, seq):
    t = TOK_TILE
    n_cmp = seq // CMP_STRIDE
    uppers = _bucket_upper_bounds()
    smem = pl.BlockSpec(memory_space=pltpu.SMEM)
    near = pl.pallas_call(
        functools.partial(_bias_near_kernel, uppers=uppers),
        grid=(SEL_TABLES + WIN_TILES,),
        in_specs=[smem],
        out_specs=pl.BlockSpec((NSA_HEADS, 1, t, t), lambda d: (0, d, 0, 0)),
        out_shape=jax.ShapeDtypeStruct((NSA_HEADS, SEL_TABLES + WIN_TILES, t, t), F32),
        compiler_params=_params(("arbitrary",)),
        name="relbias_near",
    )(rel_bias)
    cmp_b = pl.pallas_call(
        functools.partial(_bias_cmp_kernel, uppers=uppers),
        grid=(seq // t,),
        in_specs=[smem],
        out_specs=pl.BlockSpec((NSA_HEADS, n_cmp, t), lambda i: (0, 0, i)),
        out_shape=jax.ShapeDtypeStruct((NSA_HEADS, n_cmp, seq), F32),
        compiler_params=_params(("arbitrary",)),
        name="relbias_cmp",
    )(rel_bias)
    return near, cmp_b


def _cmp_to_sel_t(seq):
    n_cmp, n_sel = seq // CMP_STRIDE, seq // SEL_BLOCK
    c_start = np.arange(n_cmp)[:, None] * CMP_STRIDE
    s_start = np.arange(n_sel)[None, :] * SEL_BLOCK
    inside = np.clip(np.minimum(c_start + CMP_LEN, s_start + SEL_BLOCK) - np.maximum(c_start, s_start), 0, None)
    return (inside / CMP_LEN).T.astype(np.float32)


def _cmp_topk_kernel(qt_ref, kc_ref, vct_ref, bias_ref, c2s_ref, oc_ref, sel_ref):
    tb = pl.program_id(1)
    n, t = bias_ref.shape[1], bias_ref.shape[2]
    n_sel = sel_ref.shape[3]
    tok = tb * t + lax.broadcasted_iota(jnp.int32, (n, t), 1)
    cmp_end = lax.broadcasted_iota(jnp.int32, (n, t), 0) * CMP_STRIDE + (CMP_LEN - 1)
    visible = cmp_end <= tok
    kc = kc_ref[0, 0]
    vct = vct_ref[0, 0]
    psum = jnp.zeros((n, t), F32)
    for i in range(NSA_HPG):
        rows = slice(i * NSA_DH, (i + 1) * NSA_DH)
        s = jnp.dot(kc, qt_ref[0, 0, rows, :], preferred_element_type=F32) + bias_ref[i]
        s = jnp.where(visible, s, NEG)
        m = jnp.max(s, axis=0, keepdims=True)
        e = jnp.where(visible, jnp.exp(s - m), 0.0)
        den = jnp.sum(e, axis=0, keepdims=True)
        p = e / jnp.where(den > 0.0, den, 1.0)
        oc_ref[0, 0, rows, :] = jnp.dot(vct, p.astype(BF16), preferred_element_type=F32).astype(BF16)
        psum = psum + p
    imp = jnp.dot(c2s_ref[...], psum.astype(BF16), preferred_element_type=F32)

    blk = lax.broadcasted_iota(jnp.int32, (n_sel, t), 0)
    cur = (tb * t + lax.broadcasted_iota(jnp.int32, (n_sel, t), 1)) // SEL_BLOCK
    forced = (blk == 0) | (blk == cur) | (blk == cur - 1)
    score = jnp.where(blk > cur, -jnp.inf, jnp.where(forced, jnp.inf, imp))
    groups = n_sel // 8
    sub = lax.broadcasted_iota(jnp.int32, (8, t), 0)
    part = [score[8 * v:8 * v + 8] for v in range(groups)]
    cnt = [jnp.zeros((8, t), F32) for _ in range(groups)]
    for sp in range(n_sel):
        r = jnp.broadcast_to(score[sp:sp + 1], (8, t))
        for v in range(groups):
            if sp < 8 * v:
                beats = r >= part[v]
            elif sp >= 8 * v + 8:
                beats = r > part[v]
            else:
                beats = (r > part[v]) | ((r == part[v]) & (sub > sp - 8 * v))
            cnt[v] = cnt[v] + jnp.where(beats, 1.0, 0.0)
    for v in range(groups):
        sel_ref[0, 0, 0, 8 * v:8 * v + 8, :] = jnp.where(cnt[v] < min(SEL_TOPN, n_sel), 0.0, NEG)


def _cmp_topk(q_t, k_cmp, v_cmp_t, bias_cmp, seq):
    bsz, nt, nq, t = q_t.shape
    n_cmp, n_sel = seq // CMP_STRIDE, seq // SEL_BLOCK
    grp = NSA_HPG * NSA_DH
    c2s = jnp.asarray(_cmp_to_sel_t(seq), BF16)
    return pl.pallas_call(
        _cmp_topk_kernel,
        grid=(NSA_KV, nt, bsz),
        in_specs=[pl.BlockSpec((1, 1, grp, t), lambda g, i, b: (b, i, g, 0)),
                  pl.BlockSpec((1, 1, n_cmp, NSA_DH), lambda g, i, b: (b, g, 0, 0)),
                  pl.BlockSpec((1, 1, NSA_DH, n_cmp), lambda g, i, b: (b, g, 0, 0)),
                  pl.BlockSpec((NSA_HPG, n_cmp, t), lambda g, i, b: (g, 0, i)),
                  _const_spec((n_sel, n_cmp))],
        out_specs=[pl.BlockSpec((1, 1, grp, t), lambda g, i, b: (b, i, g, 0)),
                   pl.BlockSpec((1, 1, 1, n_sel, t), lambda g, i, b: (b, g, i, 0, 0))],
        out_shape=[jax.ShapeDtypeStruct((bsz, nt, nq, t), BF16),
                   jax.ShapeDtypeStruct((bsz, NSA_KV, nt, n_sel, t), F32)],
        compiler_params=_params(("arbitrary", "arbitrary", "arbitrary")),
        name="nsa_cmp_topk",
    )(q_t, k_cmp, v_cmp_t, bias_cmp, c2s)


def _fold(x, op):
    return op(x.reshape(x.shape[0] // 8, 8, x.shape[1]), axis=0)


def _sel_win_kernel(qt_ref, ks_ref, vst_ref, kw_ref, vwt_ref, sel_ref, tab_ref, gt_ref, oc_ref, o_ref,
                    s_scr, w_scr):
    h = pl.program_id(0)
    tb = pl.program_id(2)
    t = qt_ref.shape[3]
    per_tile = t // SEL_BLOCK
    qt = qt_ref[0, 0]
    n_pairs = (tb + 1) // 2

    def tile_rows(kt):
        return pl.ds(pl.multiple_of(kt * t, t), t)

    def sel_scores(kt):
        table = tab_ref[0, jnp.minimum(tb - kt, SEL_TABLES - 1)]
        block = jnp.concatenate(
            [jnp.broadcast_to(sel_ref[0, 0, 0, pl.ds(kt * per_tile + j, 1), :], (SEL_BLOCK, t))
             for j in range(per_tile)], axis=0)
        return jnp.dot(ks_ref[0, 0, tile_rows(kt), :], qt, preferred_element_type=F32) + table + block

    def pair(j):
        return 2 * j, jnp.minimum(2 * j + 1, tb - 1)

    w_valid = [tb >= delta for delta in range(WIN_TILES)]
    w_max = jnp.full((8, t), NEG, F32)
    for delta in range(WIN_TILES):
        kt = jnp.maximum(tb - delta, 0)
        s = jnp.dot(kw_ref[0, 0, tile_rows(kt), :], qt, preferred_element_type=F32) + tab_ref[0, SEL_TABLES + delta]
        w_scr[delta] = s
        w_max = jnp.maximum(w_max, jnp.where(w_valid[delta], _fold(s, jnp.max), NEG))
    s = sel_scores(tb)
    s_scr[tb] = s
    s_max = _fold(s, jnp.max)

    def pass1(j, mx):
        a, b = pair(j)
        sa, sb = sel_scores(a), sel_scores(b)
        s_scr[a] = sa
        s_scr[b] = sb
        return jnp.maximum(mx, jnp.maximum(_fold(sa, jnp.max), _fold(sb, jnp.max)))

    s_max = jnp.max(lax.fori_loop(0, n_pairs, pass1, s_max), axis=0, keepdims=True)
    w_max = jnp.max(w_max, axis=0, keepdims=True)

    def weigh(scr, idx, vt_ref, kt, mx, keep):
        p = jnp.exp(scr[idx] - jnp.where(keep, mx, -NEG))
        return _fold(p, jnp.sum), jnp.dot(vt_ref[0, kt], p.astype(BF16), preferred_element_type=F32)

    def pass2(j, carry):
        a, b = pair(j)
        la, oa = weigh(s_scr, a, vst_ref, a, s_max, True)
        lb, ob = weigh(s_scr, b, vst_ref, b, s_max, 2 * j + 1 < tb)
        return carry[0] + (la + lb), carry[1] + (oa + ob)

    l_s, o_s = weigh(s_scr, tb, vst_ref, tb, s_max, True)
    l_s, o_s = lax.fori_loop(0, n_pairs, pass2, (l_s, o_s))
    l_w, o_w = jnp.zeros((8, t), F32), jnp.zeros((NSA_DH, t), F32)
    for delta in range(WIN_TILES):
        lw, ow = weigh(w_scr, delta, vwt_ref, jnp.maximum(tb - delta, 0), w_max, w_valid[delta])
        l_w, o_w = l_w + lw, o_w + ow
    o_s = o_s / jnp.sum(l_s, axis=0, keepdims=True)
    o_w = o_w / jnp.sum(l_w, axis=0, keepdims=True)

    gate = lambda br: gt_ref[0, pl.ds(br * NSA_HEADS + h, 1), :]
    o = gate(0) * oc_ref[0, 0].astype(F32) + gate(1) * o_s + gate(2) * o_w
    o_ref[0, 0] = o.astype(BF16)


def _sel_win(q_t, ks, vs_t, kw, vw_t, sel, tables, gates_t, oc_t):
    bsz, nt, nq, t = q_t.shape
    seq = nt * t
    n_sel = seq // SEL_BLOCK
    head = pl.BlockSpec((1, 1, NSA_DH, t), lambda h, b, i: (b, i, h, 0))
    keys = pl.BlockSpec((1, 1, seq, NSA_DH), lambda h, b, i: (b, h // NSA_HPG, 0, 0))
    vals = pl.BlockSpec((1, nt, NSA_DH, t), lambda h, b, i: (b, 0, h // NSA_HPG, 0))
    return pl.pallas_call(
        _sel_win_kernel,
        grid=(NSA_HEADS, bsz, nt),
        in_specs=[head, keys, vals, keys, vals,
                  pl.BlockSpec((1, 1, 1, n_sel, t), lambda h, b, i: (b, h // NSA_HPG, i, 0, 0)),
                  pl.BlockSpec((1, SEL_TABLES + WIN_TILES, t, t), lambda h, b, i: (h, 0, 0, 0)),
                  pl.BlockSpec((1, 3 * NSA_HEADS, t), lambda h, b, i: (b, 0, i)),
                  head],
        out_specs=head,
        out_shape=jax.ShapeDtypeStruct((bsz, nt, nq, t), BF16),
        scratch_shapes=[pltpu.VMEM((nt, t, t), F32), pltpu.VMEM((WIN_TILES, t, t), F32)],
        compiler_params=_params(("arbitrary", "arbitrary", "arbitrary")),
        name="nsa_sel_win",
    )(q_t, ks, vs_t, kw, vw_t, sel, tables, gates_t, oc_t)


def kernel(x, c, mod_w, mod_b, norm_mix_g, norm_ffn_g, ab_w_in, ab_w_out, hgrn_lb_logits, hgrn_onorm_g, sconv_w, nsa_w_in, nsa_w_out, nsa_cmp_pos_k, nsa_cmp_pos_v, nsa_cmp_w1_k, nsa_cmp_w2_k, nsa_cmp_w1_v, nsa_cmp_w2_v, rel_bias, ffn_w_up, ffn_conv_w, ffn_w_down, final_norm_g):
    bsz, seq, d = x.shape
    assert d == D_MODEL and seq % TOK_TILE == 0
    lower = jnp.cumsum(jax.nn.softmax(hgrn_lb_logits.astype(F32), axis=0), axis=0)
    mod = _modulation(c, mod_w, mod_b)
    parts = lambda l: [mod[l, :, j * d:(j + 1) * d].reshape(bsz, 1, d) for j in range(6)]

    sh1, sc1, g1, sh2, sc2, g2 = parts(0)
    x = _mixer0(x, sh1, sc1, g1, norm_mix_g[0], ab_w_in[0], ab_w_out[0], lower[0], hgrn_onorm_g[0], sconv_w[0])
    x = _conv_ffn(x, sh2, sc2, g2, norm_ffn_g[0], ffn_w_up[0], ffn_conv_w[0], ffn_w_down[0], final_norm_g)

    sh1, sc1, g1, sh2, sc2, g2 = parts(1)
    kv, q_t, gates_t, vs_t, vw_t = _nsa_proj(x, sh1, sc1, norm_mix_g[1], nsa_w_in[0])
    kvg = kv.reshape(bsz, seq, 4, NSA_KV, NSA_DH).transpose(2, 0, 3, 1, 4)
    to_blocks = lambda a: a.reshape(bsz, NSA_KV, seq // CMP_STRIDE, CMP_STRIDE * NSA_DH)
    k_cmp, v_cmp_t = _compress(to_blocks(kvg[0]), to_blocks(kvg[1]), nsa_cmp_pos_k[0], nsa_cmp_pos_v[0],
                               nsa_cmp_w1_k[0], nsa_cmp_w2_k[0], nsa_cmp_w1_v[0], nsa_cmp_w2_v[0])
    near, bias_cmp = _bias_tables(rel_bias.astype(F32), seq)
    oc_t, sel = _cmp_topk(q_t, k_cmp, v_cmp_t, bias_cmp, seq)
    o_t = _sel_win(q_t, kvg[2], vs_t, kvg[3], vw_t, sel, near, gates_t, oc_t)
    return _conv_ffn(x, sh2, sc2, g2, norm_ffn_g[1], ffn_w_up[1], ffn_conv_w[1], ffn_w_down[1], final_norm_g,
                     attn=(o_t, nsa_w_out[0], g1), final=True)
```

```python
import functools
import math

import jax
import jax.numpy as jnp
import numpy as np
from jax import lax
from jax.experimental import pallas as pl
from jax.experimental.pallas import tpu as pltpu

F32 = jnp.float32
BF16 = jnp.bfloat16

EPS = 1e-6
D_MODEL = 1024
D_FF = 2816

HG_HEADS = 4
HG_DK = 128
HG_WIDTH = HG_HEADS * HG_DK
SC_WIDTH = D_MODEL - HG_WIDTH
AB_IN = 4 * HG_WIDTH + 3 * SC_WIDTH
HG_CHUNK = 64
HG_LEVELS = 6

NSA_HEADS = 16
NSA_KV = 4
NSA_HPG = NSA_HEADS // NSA_KV
NSA_DH = D_MODEL // NSA_HEADS
CMP_LEN = 32
CMP_STRIDE = 16
CMP_HIDDEN = 256
SEL_BLOCK = 64
SEL_TOPN = 16
WINDOW = 512
REL_BUCKETS = 32
REL_MAX_DIST = 1024

TOK_TILE = 256
NEAR_TILES = 5
SEL_TABLES = NEAR_TILES + 1
WIN_TILES = WINDOW // TOK_TILE + 1
NEG = -1e30
LOG2E = math.log2(math.e)
KEY_LANES = 128
SEL_ROWS = 16
VMEM_LIMIT = 56 * 1024 * 1024
HALO = 8

_NT = (((1,), (1,)), ((), ()))
_TN = (((0,), (0,)), ((), ()))


def _sigmoid(x):
    return 1.0 / (1.0 + jnp.exp(-x))


def _silu(x):
    return x * _sigmoid(x)


def _norm_mod(x, g, sc, sh):
    ms = jnp.mean(x * x, axis=-1, keepdims=True)
    return (x * lax.rsqrt(ms + EPS) * g) * (1.0 + sc) + sh


def _const_spec(shape):
    n = len(shape)
    return pl.BlockSpec(shape, lambda *_: (0,) * n, pipeline_mode=pl.Buffered(1))


def _params(sem):
    return pltpu.CompilerParams(dimension_semantics=sem, vmem_limit_bytes=VMEM_LIMIT)


def _mod_kernel(c_ref, w_ref, b_ref, o_ref):
    c = c_ref[...]
    o_ref[0] = jnp.dot(_silu(c), w_ref[0], preferred_element_type=F32,
                       precision=lax.Precision.HIGHEST) + b_ref[0]


def _modulation(c, mod_w, mod_b):
    depth, d, n = mod_w.shape
    bsz = c.shape[0]
    tn = 1024
    return pl.pallas_call(
        _mod_kernel,
        grid=(depth, n // tn),
        in_specs=[pl.BlockSpec((bsz, d), lambda l, j: (0, 0)),
                  pl.BlockSpec((1, d, tn), lambda l, j: (l, 0, j)),
                  pl.BlockSpec((1, 1, tn), lambda l, j: (l, 0, j))],
        out_specs=pl.BlockSpec((1, bsz, tn), lambda l, j: (l, 0, j)),
        out_shape=jax.ShapeDtypeStruct((depth, bsz, n), F32),
        compiler_params=_params(("arbitrary", "arbitrary")),
        name="adaln_mod",
    )(c, mod_w, mod_b.reshape(depth, 1, n))


def _hgrn_decay_matrix():
    t = np.arange(HG_CHUNK)
    tril = (t[None, :] <= t[:, None]).astype(np.float32)
    mats = [tril]
    for lvl in range(1, HG_LEVELS + 1):
        m = 1 << (lvl - 1)
        mid = (t // (2 * m)) * (2 * m) + m - 1
        mats.append(tril - tril[mid])
    return np.concatenate(mats, axis=0)


def _mixer0_kernel(x_ref, sh_ref, sc_ref, gt_ref, ng_ref, win_ref, wout_ref, lb_ref, og_ref, cw_ref,
                   pm_ref, o_ref, proj_ref, cat_ref, st_ref, ubuf_ref):
    tile = x_ref.shape[1]
    c64 = HG_CHUNK

    @pl.when(pl.program_id(1) == 0)
    def _():
        st_ref[...] = jnp.zeros_like(st_ref)
        ubuf_ref[0:HALO, :] = jnp.zeros((HALO, SC_WIDTH), F32)

    x = x_ref[0]
    hm = _norm_mod(x, ng_ref[...], sc_ref[0], sh_ref[0]).astype(BF16)
    proj_ref[...] = jnp.dot(hm, win_ref[...], preferred_element_type=F32)

    row = lax.broadcasted_iota(jnp.int32, (c64, HG_DK), 0)
    r64 = lax.broadcasted_iota(jnp.int32, (c64, c64), 0)
    c64i = lax.broadcasted_iota(jnp.int32, (c64, c64), 1)
    second = [None] + [(row & (1 << (lvl - 1))) != 0 for lvl in range(1, HG_LEVELS + 1)]
    same = [None] + [(r64 >> lvl) == (c64i >> lvl) for lvl in range(1, HG_LEVELS + 1)]
    eye = r64 == c64i
    pm = pm_ref[...]
    og = og_ref[...]

    def chunk(c, carry):
        r0 = pl.multiple_of(c * c64, c64)
        rows = pl.ds(r0, c64)
        for h in range(HG_HEADS):
            lanes = lambda j: pl.ds(j * HG_WIDTH + h * HG_DK, HG_DK)
            q = proj_ref[rows, lanes(0)]
            f = proj_ref[rows, lanes(1)]
            v = proj_ref[rows, lanes(2)]
            g = proj_ref[rows, lanes(3)]
            lb = lb_ref[:, h * HG_DK:(h + 1) * HG_DK]
            fg = lb + (1.0 - lb) * _sigmoid(f)
            k = 1.0 - fg
            lg = jnp.log(fg)
            lg_hi = lg.astype(BF16)
            lg_lo = (lg - lg_hi.astype(F32)).astype(BF16)
            dall = jnp.dot(pm, jnp.concatenate([lg_hi, lg_lo], axis=1), preferred_element_type=F32)
            dall = dall[:, :HG_DK] + dall[:, HG_DK:]
            b = dall[0:c64]
            a = jnp.zeros((c64, c64), F32)
            for lvl in range(1, HG_LEVELS + 1):
                e = jnp.exp(-jnp.abs(dall[lvl * c64:(lvl + 1) * c64]))
                qt = jnp.where(second[lvl], q * e, 0.0).astype(BF16)
                kt = jnp.where(second[lvl], 0.0, k * e).astype(BF16)
                al = lax.dot_general(qt, kt, _NT, preferred_element_type=F32)
                a = a + (al if lvl == HG_LEVELS else jnp.where(same[lvl], al, 0.0))
            a = jnp.where(eye, jnp.sum(q * k, axis=-1, keepdims=True), a)
            vb = v.astype(BF16)
            st = st_ref[h]
            o = jnp.dot(a.astype(BF16), vb, preferred_element_type=F32)
            o = o + lax.dot_general((q * jnp.exp(b)).astype(BF16), st.astype(BF16), _NT,
                                    preferred_element_type=F32)
            b_last = b[c64 - 1:c64, :]
            khat = (k * jnp.exp(b_last - b)).astype(BF16)
            st_ref[h] = st * jnp.exp(b_last) + lax.dot_general(vb, khat, _TN, preferred_element_type=F32)
            on = o * lax.rsqrt(jnp.mean(o * o, axis=-1, keepdims=True) + EPS) * og
            cat_ref[rows, h * HG_DK:(h + 1) * HG_DK] = (on * _silu(g)).astype(BF16)
        return carry

    lax.fori_loop(0, tile // c64, chunk, 0)

    off = 4 * HG_WIDTH
    u = proj_ref[:, off + SC_WIDTH:off + 2 * SC_WIDTH] * proj_ref[:, off + 2 * SC_WIDTH:off + 3 * SC_WIDTH]
    ubuf_ref[HALO:HALO + tile, :] = u
    cw = cw_ref[...]
    conv = (ubuf_ref[HALO - 2:HALO - 2 + tile, :] * cw[0:1] + ubuf_ref[HALO - 1:HALO - 1 + tile, :] * cw[1:2]
            + u * cw[2:3])
    ubuf_ref[0:HALO, :] = u[tile - HALO:tile, :]
    cat_ref[:, HG_WIDTH:] = (proj_ref[:, off:off + SC_WIDTH] * conv).astype(BF16)

    y = jnp.dot(cat_ref[...], wout_ref[...], preferred_element_type=F32)
    o_ref[0] = x + gt_ref[0] * y


def _mixer0(x, sh, sc, gt, norm_g, w_in, w_out, lower, onorm_g, sconv_w):
    bsz, seq, d = x.shape
    t = TOK_TILE
    pm = jnp.asarray(_hgrn_decay_matrix(), BF16)
    vec = pl.BlockSpec((1, 1, d), lambda b, i: (b, 0, 0))
    return pl.pallas_call(
        _mixer0_kernel,
        grid=(bsz, seq // t),
        in_specs=[pl.BlockSpec((1, t, d), lambda b, i: (b, i, 0)), vec, vec, vec,
                  _const_spec((1, d)), _const_spec((d, AB_IN)), _const_spec((d, d)),
                  _const_spec((1, HG_WIDTH)), _const_spec((1, HG_DK)), _const_spec((3, SC_WIDTH)),
                  _const_spec(pm.shape)],
        out_specs=pl.BlockSpec((1, t, d), lambda b, i: (b, i, 0)),
        out_shape=jax.ShapeDtypeStruct(x.shape, F32),
        scratch_shapes=[pltpu.VMEM((t, AB_IN), F32), pltpu.VMEM((t, d), BF16),
                        pltpu.VMEM((HG_HEADS, HG_DK, HG_DK), F32), pltpu.VMEM((t + HALO, SC_WIDTH), F32)],
        compiler_params=_params(("arbitrary", "arbitrary")),
        name="mixer_hgrn_sconv",
    )(x, sh, sc, gt, norm_g.reshape(1, d), w_in.astype(BF16), w_out.astype(BF16),
      lower.reshape(1, HG_WIDTH), onorm_g.reshape(1, HG_DK), sconv_w, pm)


FF_CHUNK = 1408


def _ffn_kernel(*refs, has_attn, final):
    if has_attn:
        (x_ref, olo_ref, ohi_ref, wo_ref, g1_ref, sh_ref, sc_ref, g2_ref, ng_ref, wup_ref, cw_ref, wdn_ref, fg_ref,
         o_ref, gbuf_ref, carry_ref) = refs
    else:
        (x_ref, sh_ref, sc_ref, g2_ref, ng_ref, wup_ref, cw_ref, wdn_ref, fg_ref,
         o_ref, gbuf_ref, carry_ref) = refs
    tile = x_ref.shape[1]

    @pl.when(pl.program_id(1) == 0)
    def _():
        carry_ref[...] = jnp.zeros_like(carry_ref)

    x = x_ref[0]
    if has_attn:
        o_t = jnp.where(pl.program_id(1) < pl.num_programs(1) // 2, olo_ref[0, 0], ohi_ref[0, 0])
        x = x + g1_ref[0] * lax.dot_general(o_t, wo_ref[...], _TN, preferred_element_type=F32)
    hf = _norm_mod(x, ng_ref[...], sc_ref[0], sh_ref[0]).astype(BF16)
    acc = jnp.zeros((tile, D_MODEL), F32)
    for c0 in range(0, D_FF, FF_CHUNK):
        gate = jnp.dot(hf, wup_ref[:, c0:c0 + FF_CHUNK], preferred_element_type=F32)
        up = jnp.dot(hf, wup_ref[:, D_FF + c0:D_FF + c0 + FF_CHUNK], preferred_element_type=F32)
        gbuf_ref[0:HALO, :] = carry_ref[:, c0:c0 + FF_CHUNK]
        gbuf_ref[HALO:HALO + tile, :] = gate
        carry_ref[:, c0:c0 + FF_CHUNK] = gate[tile - HALO:tile, :]
        cw = cw_ref[:, c0:c0 + FF_CHUNK]
        conv = (gbuf_ref[HALO - 2:HALO - 2 + tile, :] * cw[0:1] + gbuf_ref[HALO - 1:HALO - 1 + tile, :] * cw[1:2]
                + gate * cw[2:3])
        act = (_silu(conv) * up).astype(BF16)
        acc = acc + jnp.dot(act, wdn_ref[c0:c0 + FF_CHUNK, :], preferred_element_type=F32)
    out = x + g2_ref[0] * acc
    if final:
        out = out * lax.rsqrt(jnp.mean(out * out, axis=-1, keepdims=True) + EPS) * fg_ref[...]
    o_ref[0] = out


def _conv_ffn(x, sh, sc, g2, norm_g, w_up, conv_w, w_down, final_g, attn=None, final=False):
    bsz, seq, d = x.shape
    t = TOK_TILE
    vec = pl.BlockSpec((1, 1, d), lambda b, i: (b, 0, 0))
    args, specs = [x], [pl.BlockSpec((1, t, d), lambda b, i: (b, i, 0))]
    if attn is not None:
        o_lo, o_hi, w_o, g1 = attn
        top = o_lo.shape[1] - 1
        last = seq // t - 1
        args += [o_lo, o_hi, w_o.astype(BF16), g1]
        specs += [pl.BlockSpec((1, 1, d, t), lambda b, i: (b, jnp.minimum(i, top), 0, 0)),
                  pl.BlockSpec((1, 1, d, t), lambda b, i: (b, jnp.minimum(last - i, top), 0, 0)),
                  _const_spec((d, d)), vec]
    args += [sh, sc, g2, norm_g.reshape(1, d), w_up.astype(BF16), conv_w, w_down.astype(BF16),
             final_g.reshape(1, d)]
    specs += [vec, vec, vec, _const_spec((1, d)), _const_spec((d, 2 * D_FF)), _const_spec((3, D_FF)),
              _const_spec((D_FF, d)), _const_spec((1, d))]
    return pl.pallas_call(
        functools.partial(_ffn_kernel, has_attn=attn is not None, final=final),
        grid=(bsz, seq // t),
        in_specs=specs,
        out_specs=pl.BlockSpec((1, t, d), lambda b, i: (b, i, 0)),
        out_shape=jax.ShapeDtypeStruct(x.shape, F32),
        scratch_shapes=[pltpu.VMEM((t + HALO, FF_CHUNK), F32), pltpu.VMEM((HALO, D_FF), F32)],
        compiler_params=_params(("arbitrary", "arbitrary")),
        name="conv_ffn_attnproj" if attn is not None else "conv_ffn",
    )(*args)


KV_ROWS = NSA_KV * NSA_DH


def _nsa_proj_kernel(x_ref, sh_ref, sc_ref, ng_ref, wa_ref, wq_ref, wg_ref, wv_ref,
                     kv_ref, qt_ref, gt_ref, vst_ref, vwt_ref):
    hm = _norm_mod(x_ref[0], ng_ref[...], sc_ref[0], sh_ref[0]).astype(BF16)
    kv_ref[0] = jnp.dot(hm, wa_ref[...], preferred_element_type=F32).astype(BF16)
    qt = lax.dot_general(wq_ref[...], hm, _NT, preferred_element_type=F32)
    qt_ref[0, 0] = (qt * (NSA_DH ** -0.5 * LOG2E)).astype(BF16)
    gt_ref[0] = _sigmoid(lax.dot_general(wg_ref[...], hm, _NT, preferred_element_type=F32))
    vt = lax.dot_general(wv_ref[...], hm, _NT, preferred_element_type=F32).astype(BF16)
    vst_ref[0, 0] = vt[:KV_ROWS]
    vwt_ref[0, 0] = vt[KV_ROWS:]


def _nsa_proj(x, sh, sc, norm_g, w_in):
    bsz, seq, d = x.shape
    t = TOK_TILE
    nt = seq // t
    nq, ng = NSA_HEADS * NSA_DH, 3 * NSA_HEADS
    o = nq + ng
    w = w_in.astype(BF16)
    col = lambda j: w[:, o + j * KV_ROWS:o + (j + 1) * KV_ROWS]
    w_a = jnp.concatenate([col(0), col(1), col(2), col(4)], axis=1)
    w_q = w[:, :nq].T
    w_g = w[:, nq:o].T
    w_v = jnp.concatenate([col(3), col(5)], axis=1).T
    vec = pl.BlockSpec((1, 1, d), lambda b, i: (b, 0, 0))
    tiled = lambda r: pl.BlockSpec((1, 1, r, t), lambda b, i: (b, i, 0, 0))
    return pl.pallas_call(
        _nsa_proj_kernel,
        grid=(bsz, nt),
        in_specs=[pl.BlockSpec((1, t, d), lambda b, i: (b, i, 0)), vec, vec, _const_spec((1, d)),
                  _const_spec((d, 4 * KV_ROWS)), _const_spec((nq, d)), _const_spec((ng, d)),
                  _const_spec((2 * KV_ROWS, d))],
        out_specs=[pl.BlockSpec((1, t, 4 * KV_ROWS), lambda b, i: (b, i, 0)), tiled(nq),
                   pl.BlockSpec((1, ng, t), lambda b, i: (b, 0, i)), tiled(KV_ROWS), tiled(KV_ROWS)],
        out_shape=[jax.ShapeDtypeStruct((bsz, seq, 4 * KV_ROWS), BF16),
                   jax.ShapeDtypeStruct((bsz, nt, nq, t), BF16),
                   jax.ShapeDtypeStruct((bsz, ng, seq), F32),
                   jax.ShapeDtypeStruct((bsz, nt, KV_ROWS, t), BF16),
                   jax.ShapeDtypeStruct((bsz, nt, KV_ROWS, t), BF16)],
        compiler_params=_params(("arbitrary", "arbitrary")),
        name="nsa_in_proj",
    )(x, sh, sc, norm_g.reshape(1, d), w_a, w_q, w_g, w_v)


def _compress_kernel(hk_ref, hv_ref, pk_ref, pv_ref, w1k_ref, w2k_ref, w1v_ref, w2vt_ref, kc_ref, vct_ref):
    half = CMP_STRIDE * NSA_DH

    def hidden(h_ref, pos_ref, w1_ref):
        h = h_ref[0, 0]
        n = h.shape[0]
        top = jnp.dot(h, w1_ref[:half, :], preferred_element_type=F32)
        bot = jnp.dot(h, w1_ref[half:, :], preferred_element_type=F32)
        pos = jnp.dot(jnp.broadcast_to(pos_ref[...], (8, 2 * half)).astype(BF16), w1_ref[...],
                      preferred_element_type=F32)[0:1]
        return _silu(top + pltpu.roll(bot, n - 1, 0) + pos).astype(BF16)

    kc_ref[0, 0] = jnp.dot(hidden(hk_ref, pk_ref, w1k_ref), w2k_ref[...],
                           preferred_element_type=F32).astype(BF16)
    vct_ref[0, 0] = lax.dot_general(w2vt_ref[...], hidden(hv_ref, pv_ref, w1v_ref), _NT,
                                    preferred_element_type=F32).astype(BF16)


def _compress(hk, hv, pos_k, pos_v, w1_k, w2_k, w1_v, w2_v):
    bsz, g, n, width = hk.shape
    blk = pl.BlockSpec((1, 1, n, width), lambda b, j: (b, j, 0, 0))
    return pl.pallas_call(
        _compress_kernel,
        grid=(bsz, g),
        in_specs=[blk, blk, _const_spec((1, width * 2)), _const_spec((1, width * 2)),
                  _const_spec((2 * width, CMP_HIDDEN)), _const_spec((CMP_HIDDEN, NSA_DH)),
                  _const_spec((2 * width, CMP_HIDDEN)), _const_spec((NSA_DH, CMP_HIDDEN))],
        out_specs=[pl.BlockSpec((1, 1, n, NSA_DH), lambda b, j: (b, j, 0, 0)),
                   pl.BlockSpec((1, 1, NSA_DH, n), lambda b, j: (b, j, 0, 0))],
        out_shape=[jax.ShapeDtypeStruct((bsz, g, n, NSA_DH), BF16),
                   jax.ShapeDtypeStruct((bsz, g, NSA_DH, n), BF16)],
        compiler_params=_params(("arbitrary", "arbitrary")),
        name="nsa_compress",
    )(hk, hv, pos_k.reshape(1, -1), pos_v.reshape(1, -1), w1_k.astype(BF16), w2_k.astype(BF16),
      w1_v.astype(BF16), w2_v.T.astype(BF16))


def _bucket_upper_bounds():
    n = np.arange(4 * REL_MAX_DIST, dtype=np.float64)
    exact = REL_BUCKETS // 2
    large = exact + (np.log(np.maximum(n, exact) / exact) / math.log(REL_MAX_DIST / exact)
                     * (REL_BUCKETS - exact)).astype(np.int64)
    bucket = np.where(n < exact, n.astype(np.int64), np.minimum(large, REL_BUCKETS - 1))
    return [int(np.max(np.nonzero(bucket <= j)[0])) for j in range(REL_BUCKETS - 1)]


def _bias_of_dist(dist, rb_ref, heads, uppers):
    vals = [jnp.full(dist.shape, rb_ref[REL_BUCKETS - 1, h], F32) for h in heads]
    for j in range(REL_BUCKETS - 2, -1, -1):
        m = dist <= uppers[j]
        vals = [jnp.where(m, rb_ref[j, h], v) for h, v in zip(heads, vals)]
    return vals


def _bias_near_kernel(rb_ref, o_ref, *, uppers):
    d = pl.program_id(0)
    t = o_ref.shape[2]
    heads = list(range(NSA_HEADS))
    is_win = d >= SEL_TABLES
    delta = jnp.where(is_win, d - SEL_TABLES, d)
    limit = jnp.where(is_win, WINDOW, (SEL_TABLES + 1) * t)

    def strip(i, carry):
        r0 = pl.multiple_of(i * 8, 8)
        key = r0 + lax.broadcasted_iota(jnp.int32, (8, t), 0)
        tok = lax.broadcasted_iota(jnp.int32, (8, t), 1)
        dist = delta * t + tok - key
        valid = (dist >= 0) & (dist < limit)
        vals = _bias_of_dist(dist, rb_ref, heads, uppers)
        for h in heads:
            o_ref[h, 0, pl.ds(r0, 8), :] = jnp.where(valid, vals[h], NEG)
        return carry

    lax.fori_loop(0, t // 8, strip, 0)


def _bias_cmp_kernel(rb_ref, o_ref, *, uppers):
    tb = pl.program_id(0)
    n, t = o_ref.shape[1], o_ref.shape[2]
    heads = list(range(NSA_HEADS))

    def strip(i, carry):
        r0 = pl.multiple_of(i * 8, 8)
        blk = r0 + lax.broadcasted_iota(jnp.int32, (8, t), 0)
        tok = tb * t + lax.broadcasted_iota(jnp.int32, (8, t), 1)
        vals = _bias_of_dist(tok - (blk * CMP_STRIDE + CMP_LEN - 1), rb_ref, heads, uppers)
        for h in heads:
            o_ref[h, pl.ds(r0, 8), :] = vals[h]
        return carry

    lax.fori_loop(0, n // 8, strip, 0)


def _bias_tables(rel_bias, seq):
    t = TOK_TILE
    n_cmp = seq // CMP_STRIDE
    uppers = _bucket_upper_bounds()
    smem = pl.BlockSpec(memory_space=pltpu.SMEM)
    near = pl.pallas_call(
        functools.partial(_bias_near_kernel, uppers=uppers),
        grid=(SEL_TABLES + WIN_TILES,),
        in_specs=[smem],
        out_specs=pl.BlockSpec((NSA_HEADS, 1, t, t), lambda d: (0, d, 0, 0)),
        out_shape=jax.ShapeDtypeStruct((NSA_HEADS, SEL_TABLES + WIN_TILES, t, t), F32),
        compiler_params=_params(("arbitrary",)),
        name="relbias_near",
    )(rel_bias)
    cmp_b = pl.pallas_call(
        functools.partial(_bias_cmp_kernel, uppers=uppers),
        grid=(seq // t,),
        in_specs=[smem],
        out_specs=pl.BlockSpec((NSA_HEADS, n_cmp, t), lambda i: (0, 0, i)),
        out_shape=jax.ShapeDtypeStruct((NSA_HEADS, n_cmp, seq), F32),
        compiler_params=_params(("arbitrary",)),
        name="relbias_cmp",
    )(rel_bias)
    return near, cmp_b


def _cmp_to_sel_t(seq):
    n_cmp, n_sel = seq // CMP_STRIDE, seq // SEL_BLOCK
    c_start = np.arange(n_cmp)[:, None] * CMP_STRIDE
    s_start = np.arange(n_sel)[None, :] * SEL_BLOCK
    inside = np.clip(np.minimum(c_start + CMP_LEN, s_start + SEL_BLOCK) - np.maximum(c_start, s_start), 0, None)
    return (inside / CMP_LEN).T.astype(np.float32)


def _cmp_topk_kernel(qt_ref, kc_ref, vct_ref, bias_ref, c2s_ref, oc_ref, sel_ref):
    tb = pl.program_id(1)
    n, t = bias_ref.shape[1], bias_ref.shape[2]
    n_sel = c2s_ref.shape[0]
    tok = tb * t + lax.broadcasted_iota(jnp.int32, (n, t), 1)
    cmp_end = lax.broadcasted_iota(jnp.int32, (n, t), 0) * CMP_STRIDE + (CMP_LEN - 1)
    visible = cmp_end <= tok
    kc = kc_ref[0, 0]
    vct = vct_ref[0, 0]
    psum = jnp.zeros((n, t), F32)
    for i in range(NSA_HPG):
        rows = slice(i * NSA_DH, (i + 1) * NSA_DH)
        s = jnp.dot(kc, qt_ref[0, 0, rows, :], preferred_element_type=F32) + bias_ref[i]
        s = jnp.where(visible, s, NEG)
        m = jnp.max(s, axis=0, keepdims=True)
        e = jnp.where(visible, jnp.exp2(s - m), 0.0)
        den = jnp.sum(e, axis=0, keepdims=True)
        p = e / jnp.where(den > 0.0, den, 1.0)
        oc_ref[0, 0, rows, :] = jnp.dot(vct, p.astype(BF16), preferred_element_type=F32).astype(BF16)
        psum = psum + p
    imp = jnp.dot(c2s_ref[...], psum.astype(BF16), preferred_element_type=F32)

    blk = lax.broadcasted_iota(jnp.int32, (n_sel, t), 0)
    cur = (tb * t + lax.broadcasted_iota(jnp.int32, (n_sel, t), 1)) // SEL_BLOCK
    forced = (blk == 0) | (blk == cur) | (blk == cur - 1)
    score = jnp.where(blk > cur, -jnp.inf, jnp.where(forced, jnp.inf, imp))
    groups = n_sel // 8
    sub = lax.broadcasted_iota(jnp.int32, (8, t), 0)
    part = [score[8 * v:8 * v + 8] for v in range(groups)]
    cnt = [jnp.zeros((8, t), F32) for _ in range(groups)]
    for sp in range(n_sel):
        r = jnp.broadcast_to(score[sp:sp + 1], (8, t))
        for v in range(groups):
            if sp < 8 * v:
                beats = r >= part[v]
            elif sp >= 8 * v + 8:
                beats = r > part[v]
            else:
                beats = (r > part[v]) | ((r == part[v]) & (sub > sp - 8 * v))
            cnt[v] = cnt[v] + jnp.where(beats, 1.0, 0.0)
    per_tile = t // SEL_BLOCK
    pad = jnp.zeros((SEL_ROWS - per_tile, t), F32)
    for v in range(groups):
        mask = jnp.where(cnt[v] < min(SEL_TOPN, n_sel), 0.0, NEG)
        for r0 in range(0, 8, per_tile):
            kt = (8 * v + r0) // per_tile
            sel_ref[0, 0, 0, kt] = jnp.concatenate([mask[r0:r0 + per_tile], pad], axis=0).astype(BF16)


def _cmp_topk(q_t, k_cmp, v_cmp_t, bias_cmp, seq):
    bsz, nt, nq, t = q_t.shape
    n_cmp, n_sel = seq // CMP_STRIDE, seq // SEL_BLOCK
    grp = NSA_HPG * NSA_DH
    c2s = jnp.asarray(_cmp_to_sel_t(seq), BF16)
    return pl.pallas_call(
        _cmp_topk_kernel,
        grid=(NSA_KV, nt, bsz),
        in_specs=[pl.BlockSpec((1, 1, grp, t), lambda g, i, b: (b, i, g, 0)),
                  pl.BlockSpec((1, 1, n_cmp, NSA_DH), lambda g, i, b: (b, g, 0, 0)),
                  pl.BlockSpec((1, 1, NSA_DH, n_cmp), lambda g, i, b: (b, g, 0, 0)),
                  pl.BlockSpec((NSA_HPG, n_cmp, t), lambda g, i, b: (g, 0, i)),
                  _const_spec((n_sel, n_cmp))],
        out_specs=[pl.BlockSpec((1, 1, grp, t), lambda g, i, b: (b, i, g, 0)),
                   pl.BlockSpec((1, 1, 1, nt, SEL_ROWS, t), lambda g, i, b: (b, g, i, 0, 0, 0))],
        out_shape=[jax.ShapeDtypeStruct((bsz, nt, nq, t), BF16),
                   jax.ShapeDtypeStruct((bsz, NSA_KV, nt, nt, SEL_ROWS, t), BF16)],
        compiler_params=_params(("arbitrary", "arbitrary", "arbitrary")),
        name="nsa_cmp_topk",
    )(q_t, k_cmp, v_cmp_t, bias_cmp, c2s)


def _fold(x, op):
    return op(x.reshape(x.shape[0] // 8, 8, x.shape[1]), axis=0)


def _sel_win_kernel(qa_ref, qb_ref, ks_ref, vst_ref, kw_ref, vwt_ref, sela_ref, selb_ref, tab_ref, ga_ref, gb_ref,
                    oca_ref, ocb_ref, olo_ref, ohi_ref, s_scr):
    h = pl.program_id(0)
    i = pl.program_id(2)
    nt, t = vst_ref.shape[1], vst_ref.shape[3]
    half = nt // 2
    tb = (i, nt - 1 - i)
    q = (qa_ref[0, 0], qb_ref[0, 0])
    sel = (sela_ref, selb_ref)

    def tile_rows(kt):
        return pl.ds(pl.multiple_of(kt * t, t), t)

    def pick(on_a, xa, xb):
        if isinstance(on_a, bool):
            return xa if on_a else xb
        return jnp.where(on_a, xa, xb)

    work = []
    for p in range(nt + 1):
        on_a = True if p == 0 else (False if p >= half else p <= i)
        kt = pick(on_a, p, jnp.maximum(p - i - 1, 0))
        work.append((on_a, kt, True, pick(on_a, tb[0], tb[1]) - kt))
    for side in range(2):
        for delta in range(WIN_TILES):
            keep = True if (side == 1 and half >= WIN_TILES) else tb[side] >= delta
            work.append((side == 0, jnp.maximum(tb[side] - delta, 0), keep, delta))
    n_selpos = nt + 1

    neg8 = jnp.full((8, t), NEG, F32)
    mx = {(0, 0): neg8, (0, 1): neg8, (1, 0): neg8, (1, 1): neg8}
    q_pad = jnp.zeros((KEY_LANES - NSA_DH - SEL_ROWS, t), BF16)
    for p, (on_a, kt, keep, delta) in enumerate(work):
        win = p >= n_selpos
        qp = pick(on_a, q[0], q[1])
        if win:
            s = jnp.dot(kw_ref[0, 0, tile_rows(kt), :], qp, preferred_element_type=F32)
            s = s + tab_ref[0, SEL_TABLES + delta]
        else:
            rows = pick(on_a, sel[0][0, 0, 0, kt], sel[1][0, 0, 0, kt])
            q_aug = jnp.concatenate([qp, rows, q_pad], axis=0)
            s = jnp.dot(ks_ref[0, 0, tile_rows(kt), :], q_aug, preferred_element_type=F32)
            s = s + tab_ref[0, jnp.minimum(delta, SEL_TABLES - 1)]
        s_scr[p] = s
        f = _fold(s, jnp.max)
        if not isinstance(keep, bool):
            f = jnp.where(keep, f, NEG)
        br = int(win)
        if isinstance(on_a, bool):
            mx[(br, int(not on_a))] = jnp.maximum(mx[(br, int(not on_a))], f)
        else:
            mx[(br, 0)] = jnp.maximum(mx[(br, 0)], jnp.where(on_a, f, NEG))
            mx[(br, 1)] = jnp.maximum(mx[(br, 1)], jnp.where(on_a, NEG, f))
    mx = {k: jnp.max(v, axis=0, keepdims=True) for k, v in mx.items()}

    ones = (lax.broadcasted_iota(jnp.int32, (SEL_ROWS, t), 0) == 0).astype(BF16)
    zo = jnp.zeros((NSA_DH + SEL_ROWS, t), F32)
    acc = {k: zo for k in mx}
    for p, (on_a, kt, keep, delta) in enumerate(work):
        win = p >= n_selpos
        br = int(win)
        m = pick(on_a, mx[(br, 0)], mx[(br, 1)])
        if not isinstance(keep, bool):
            m = jnp.where(keep, m, -NEG)
        pr = jnp.exp2(s_scr[p] - m).astype(BF16)
        vt_ref = vwt_ref if win else vst_ref
        ov = jnp.dot(jnp.concatenate([vt_ref[0, kt], ones], axis=0), pr, preferred_element_type=F32)
        if isinstance(on_a, bool):
            k = (br, int(not on_a))
            acc[k] = acc[k] + ov
        else:
            acc[(br, 0)] = acc[(br, 0)] + jnp.where(on_a, ov, 0.0)
            acc[(br, 1)] = acc[(br, 1)] + jnp.where(on_a, 0.0, ov)

    for side, (g_ref, oc_ref, o_ref) in enumerate(((ga_ref, oca_ref, olo_ref), (gb_ref, ocb_ref, ohi_ref))):
        gate = lambda br: g_ref[0, pl.ds(br * NSA_HEADS + h, 1), :]
        o_s = acc[(0, side)][:NSA_DH] / acc[(0, side)][NSA_DH:NSA_DH + 1]
        o_w = acc[(1, side)][:NSA_DH] / acc[(1, side)][NSA_DH:NSA_DH + 1]
        o_ref[0, 0] = (gate(0) * oc_ref[0, 0].astype(F32) + gate(1) * o_s + gate(2) * o_w).astype(BF16)


def _sel_win(q_t, ks, vs_t, kw, vw_t, sel, tables, gates_t, oc_t):
    bsz, nt, nq, t = q_t.shape
    seq = nt * t
    last = nt - 1
    lane = np.arange(KEY_LANES - NSA_DH)[None, :]
    onehot = ((np.arange(seq)[:, None] % t) // SEL_BLOCK == lane).astype(np.float32)
    ks = jnp.concatenate([ks, jnp.broadcast_to(jnp.asarray(onehot, BF16), ks.shape[:2] + onehot.shape)], axis=-1)
    head_a = pl.BlockSpec((1, 1, NSA_DH, t), lambda h, b, i: (b, i, h, 0))
    head_b = pl.BlockSpec((1, 1, NSA_DH, t), lambda h, b, i: (b, last - i, h, 0))
    keys = pl.BlockSpec((1, 1, seq, NSA_DH), lambda h, b, i: (b, h // NSA_HPG, 0, 0))
    keys_sel = pl.BlockSpec((1, 1, seq, KEY_LANES), lambda h, b, i: (b, h // NSA_HPG, 0, 0))
    vals = pl.BlockSpec((1, nt, NSA_DH, t), lambda h, b, i: (b, 0, h // NSA_HPG, 0))
    sel_a = pl.BlockSpec((1, 1, 1, nt, SEL_ROWS, t), lambda h, b, i: (b, h // NSA_HPG, i, 0, 0, 0))
    sel_b = pl.BlockSpec((1, 1, 1, nt, SEL_ROWS, t), lambda h, b, i: (b, h // NSA_HPG, last - i, 0, 0, 0))
    gate_a = pl.BlockSpec((1, 3 * NSA_HEADS, t), lambda h, b, i: (b, 0, i))
    gate_b = pl.BlockSpec((1, 3 * NSA_HEADS, t), lambda h, b, i: (b, 0, last - i))
    out = jax.ShapeDtypeStruct((bsz, nt // 2, nq, t), BF16)
    return pl.pallas_call(
        _sel_win_kernel,
        grid=(NSA_HEADS, bsz, nt // 2),
        in_specs=[head_a, head_b, keys_sel, vals, keys, vals, sel_a, sel_b,
                  pl.BlockSpec((1, SEL_TABLES + WIN_TILES, t, t), lambda h, b, i: (h, 0, 0, 0)),
                  gate_a, gate_b, head_a, head_b],
        out_specs=[head_a, head_a],
        out_shape=[out, out],
        scratch_shapes=[pltpu.VMEM((nt + 1 + 2 * WIN_TILES, t, t), F32)],
        compiler_params=_params(("arbitrary", "arbitrary", "arbitrary")),
        name="nsa_sel_win",
    )(q_t, q_t, ks, vs_t, kw, vw_t, sel, sel, tables, gates_t, gates_t, oc_t, oc_t)


def kernel(x, c, mod_w, mod_b, norm_mix_g, norm_ffn_g, ab_w_in, ab_w_out, hgrn_lb_logits, hgrn_onorm_g, sconv_w, nsa_w_in, nsa_w_out, nsa_cmp_pos_k, nsa_cmp_pos_v, nsa_cmp_w1_k, nsa_cmp_w2_k, nsa_cmp_w1_v, nsa_cmp_w2_v, rel_bias, ffn_w_up, ffn_conv_w, ffn_w_down, final_norm_g):
    bsz, seq, d = x.shape
    assert d == D_MODEL and seq % TOK_TILE == 0
    lower = jnp.cumsum(jax.nn.softmax(hgrn_lb_logits.astype(F32), axis=0), axis=0)
    mod = _modulation(c, mod_w, mod_b)
    parts = lambda l: [mod[l, :, j * d:(j + 1) * d].reshape(bsz, 1, d) for j in range(6)]

    sh1, sc1, g1, sh2, sc2, g2 = parts(0)
    x = _mixer0(x, sh1, sc1, g1, norm_mix_g[0], ab_w_in[0], ab_w_out[0], lower[0], hgrn_onorm_g[0], sconv_w[0])
    x = _conv_ffn(x, sh2, sc2, g2, norm_ffn_g[0], ffn_w_up[0], ffn_conv_w[0], ffn_w_down[0], final_norm_g)

    sh1, sc1, g1, sh2, sc2, g2 = parts(1)
    kv, q_t, gates_t, vs_t, vw_t = _nsa_proj(x, sh1, sc1, norm_mix_g[1], nsa_w_in[0])
    kvg = kv.reshape(bsz, seq, 4, NSA_KV, NSA_DH).transpose(2, 0, 3, 1, 4)
    to_blocks = lambda a: a.reshape(bsz, NSA_KV, seq // CMP_STRIDE, CMP_STRIDE * NSA_DH)
    k_cmp, v_cmp_t = _compress(to_blocks(kvg[0]), to_blocks(kvg[1]), nsa_cmp_pos_k[0], nsa_cmp_pos_v[0],
                               nsa_cmp_w1_k[0], nsa_cmp_w2_k[0], nsa_cmp_w1_v[0], nsa_cmp_w2_v[0])
    near, bias_cmp = _bias_tables(rel_bias.astype(F32) * LOG2E, seq)
    oc_t, sel = _cmp_topk(q_t, k_cmp, v_cmp_t, bias_cmp, seq)
    o_lo, o_hi = _sel_win(q_t, kvg[2], vs_t, kvg[3], vw_t, sel, near, gates_t, oc_t)
    return _conv_ffn(x, sh2, sc2, g2, norm_ffn_g[1], ffn_w_up[1], ffn_conv_w[1], ffn_w_down[1], final_norm_g,
                     attn=(o_lo, o_hi, nsa_w_out[0], g1), final=True)
```

```python
import functools
import math

import jax
import jax.numpy as jnp
import numpy as np
from jax import lax
from jax.experimental import pallas as pl
from jax.experimental.pallas import tpu as pltpu

F32 = jnp.float32
BF16 = jnp.bfloat16

EPS = 1e-6
D_MODEL = 1024
D_FF = 2816

HG_HEADS = 4
HG_DK = 128
HG_WIDTH = HG_HEADS * HG_DK
SC_WIDTH = D_MODEL - HG_WIDTH
AB_IN = 4 * HG_WIDTH + 3 * SC_WIDTH
HG_CHUNK = 64
HG_LEVELS = 6

NSA_HEADS = 16
NSA_KV = 4
NSA_HPG = NSA_HEADS // NSA_KV
NSA_DH = D_MODEL // NSA_HEADS
CMP_LEN = 32
CMP_STRIDE = 16
CMP_HIDDEN = 256
SEL_BLOCK = 64
SEL_TOPN = 16
WINDOW = 512
REL_BUCKETS = 32
REL_MAX_DIST = 1024

TOK_TILE = 256
NEAR_TILES = 5
SEL_TABLES = NEAR_TILES + 1
WIN_TILES = WINDOW // TOK_TILE + 1
NEG = -1e30
LOG2E = math.log2(math.e)
KEY_LANES = 128
SEL_ROWS = 16
VMEM_LIMIT = 56 * 1024 * 1024
HALO = 8

_NT = (((1,), (1,)), ((), ()))
_TN = (((0,), (0,)), ((), ()))


def _sigmoid(x):
    return 1.0 / (1.0 + jnp.exp(-x))


def _silu(x):
    return x * _sigmoid(x)


def _norm_mod(x, g, sc, sh):
    ms = jnp.mean(x * x, axis=-1, keepdims=True)
    return (x * lax.rsqrt(ms + EPS) * g) * (1.0 + sc) + sh


def _fold(x, op):
    return op(x.reshape(x.shape[0] // 8, 8, x.shape[1]), axis=0)


def _const_spec(shape):
    n = len(shape)
    return pl.BlockSpec(shape, lambda *_: (0,) * n, pipeline_mode=pl.Buffered(1))


def _params(sem):
    return pltpu.CompilerParams(dimension_semantics=sem, vmem_limit_bytes=VMEM_LIMIT)


def _mod_kernel(c_ref, w_ref, b_ref, o_ref):
    c = c_ref[...]
    o_ref[0] = jnp.dot(_silu(c), w_ref[0], preferred_element_type=F32,
                       precision=lax.Precision.HIGHEST) + b_ref[0]


def _modulation(c, mod_w, mod_b):
    depth, d, n = mod_w.shape
    bsz = c.shape[0]
    tn = 1024
    return pl.pallas_call(
        _mod_kernel,
        grid=(depth, n // tn),
        in_specs=[pl.BlockSpec((bsz, d), lambda l, j: (0, 0)),
                  pl.BlockSpec((1, d, tn), lambda l, j: (l, 0, j)),
                  pl.BlockSpec((1, 1, tn), lambda l, j: (l, 0, j))],
        out_specs=pl.BlockSpec((1, bsz, tn), lambda l, j: (l, 0, j)),
        out_shape=jax.ShapeDtypeStruct((depth, bsz, n), F32),
        compiler_params=_params(("arbitrary", "arbitrary")),
        name="adaln_mod",
    )(c, mod_w, mod_b.reshape(depth, 1, n))


def _hgrn_decay_matrix():
    t = np.arange(HG_CHUNK)
    tril = (t[None, :] <= t[:, None]).astype(np.float32)
    mats = [tril]
    for lvl in range(1, HG_LEVELS + 1):
        m = 1 << (lvl - 1)
        mid = (t // (2 * m)) * (2 * m) + m - 1
        mats.append(tril - tril[mid])
    return np.concatenate(mats, axis=0)


def _mixer0_kernel(x_ref, sh_ref, sc_ref, gt_ref, ng_ref, win_ref, wout_ref, lb_ref, og_ref, cw_ref,
                   pm_ref, o_ref, proj_ref, cat_ref, st_ref, ubuf_ref):
    tile = x_ref.shape[1]
    c64 = HG_CHUNK

    @pl.when(pl.program_id(1) == 0)
    def _():
        st_ref[...] = jnp.zeros_like(st_ref)
        ubuf_ref[0:HALO, :] = jnp.zeros((HALO, SC_WIDTH), F32)

    x = x_ref[0]
    hm = _norm_mod(x, ng_ref[...], sc_ref[0], sh_ref[0]).astype(BF16)
    proj_ref[...] = jnp.dot(hm, win_ref[...], preferred_element_type=F32)

    row = lax.broadcasted_iota(jnp.int32, (c64, HG_DK), 0)
    r64 = lax.broadcasted_iota(jnp.int32, (c64, c64), 0)
    c64i = lax.broadcasted_iota(jnp.int32, (c64, c64), 1)
    second = [None] + [(row & (1 << (lvl - 1))) != 0 for lvl in range(1, HG_LEVELS + 1)]
    same = [None] + [(r64 >> lvl) == (c64i >> lvl) for lvl in range(1, HG_LEVELS + 1)]
    eye = r64 == c64i
    pm = pm_ref[...]
    og = og_ref[...]

    def chunk(c):
        rows = pl.ds(c * c64, c64)
        for h in range(HG_HEADS):
            lanes = lambda j: pl.ds(j * HG_WIDTH + h * HG_DK, HG_DK)
            q = proj_ref[rows, lanes(0)]
            f = proj_ref[rows, lanes(1)]
            v = proj_ref[rows, lanes(2)]
            g = proj_ref[rows, lanes(3)]
            lb = lb_ref[:, h * HG_DK:(h + 1) * HG_DK]
            fg = lb + (1.0 - lb) * _sigmoid(f)
            k = 1.0 - fg
            lg = jnp.log(fg)
            lg_hi = lg.astype(BF16)
            lg_lo = (lg - lg_hi.astype(F32)).astype(BF16)
            dall = jnp.dot(pm, jnp.concatenate([lg_hi, lg_lo], axis=1), preferred_element_type=F32)
            dall = dall[:, :HG_DK] + dall[:, HG_DK:]
            b = dall[0:c64]
            a = jnp.zeros((c64, c64), F32)
            for lvl in range(1, HG_LEVELS + 1):
                e = jnp.exp(-jnp.abs(dall[lvl * c64:(lvl + 1) * c64]))
                qt = jnp.where(second[lvl], q * e, 0.0).astype(BF16)
                kt = jnp.where(second[lvl], 0.0, k * e).astype(BF16)
                al = lax.dot_general(qt, kt, _NT, preferred_element_type=F32)
                a = a + (al if lvl == HG_LEVELS else jnp.where(same[lvl], al, 0.0))
            a = jnp.where(eye, jnp.sum(q * k, axis=-1, keepdims=True), a)
            vb = v.astype(BF16)
            st = st_ref[h]
            o = jnp.dot(a.astype(BF16), vb, preferred_element_type=F32)
            o = o + lax.dot_general((q * jnp.exp(b)).astype(BF16), st.astype(BF16), _NT,
                                    preferred_element_type=F32)
            b_last = b[c64 - 1:c64, :]
            khat = (k * jnp.exp(b_last - b)).astype(BF16)
            st_ref[h] = st * jnp.exp(b_last) + lax.dot_general(vb, khat, _TN, preferred_element_type=F32)
            on = o * lax.rsqrt(jnp.mean(o * o, axis=-1, keepdims=True) + EPS) * og
            cat_ref[rows, h * HG_DK:(h + 1) * HG_DK] = (on * _silu(g)).astype(BF16)

    for c in range(tile // c64):
        chunk(c)

    off = 4 * HG_WIDTH
    u = proj_ref[:, off + SC_WIDTH:off + 2 * SC_WIDTH] * proj_ref[:, off + 2 * SC_WIDTH:off + 3 * SC_WIDTH]
    ubuf_ref[HALO:HALO + tile, :] = u
    cw = cw_ref[...]
    conv = (ubuf_ref[HALO - 2:HALO - 2 + tile, :] * cw[0:1] + ubuf_ref[HALO - 1:HALO - 1 + tile, :] * cw[1:2]
            + u * cw[2:3])
    ubuf_ref[0:HALO, :] = u[tile - HALO:tile, :]
    cat_ref[:, HG_WIDTH:] = (proj_ref[:, off:off + SC_WIDTH] * conv).astype(BF16)

    y = jnp.dot(cat_ref[...], wout_ref[...], preferred_element_type=F32)
    o_ref[0] = x + gt_ref[0] * y


def _mixer0(x, sh, sc, gt, norm_g, w_in, w_out, lower, onorm_g, sconv_w):
    bsz, seq, d = x.shape
    t = TOK_TILE
    pm = jnp.asarray(_hgrn_decay_matrix(), BF16)
    vec = pl.BlockSpec((1, 1, d), lambda b, i: (b, 0, 0))
    return pl.pallas_call(
        _mixer0_kernel,
        grid=(bsz, seq // t),
        in_specs=[pl.BlockSpec((1, t, d), lambda b, i: (b, i, 0)), vec, vec, vec,
                  _const_spec((1, d)), _const_spec((d, AB_IN)), _const_spec((d, d)),
                  _const_spec((1, HG_WIDTH)), _const_spec((1, HG_DK)), _const_spec((3, SC_WIDTH)),
                  _const_spec(pm.shape)],
        out_specs=pl.BlockSpec((1, t, d), lambda b, i: (b, i, 0)),
        out_shape=jax.ShapeDtypeStruct(x.shape, F32),
        scratch_shapes=[pltpu.VMEM((t, AB_IN), F32), pltpu.VMEM((t, d), BF16),
                        pltpu.VMEM((HG_HEADS, HG_DK, HG_DK), F32), pltpu.VMEM((t + HALO, SC_WIDTH), F32)],
        compiler_params=_params(("arbitrary", "arbitrary")),
        name="mixer_hgrn_sconv",
    )(x, sh, sc, gt, norm_g.reshape(1, d), w_in.astype(BF16), w_out.astype(BF16),
      lower.reshape(1, HG_WIDTH), onorm_g.reshape(1, HG_DK), sconv_w, pm)


FF_CHUNK = 1408


def _ffn_kernel(*refs, has_attn, final):
    if has_attn:
        (x_ref, olo_ref, ohi_ref, wo_ref, g1_ref, sh_ref, sc_ref, g2_ref, ng_ref, wup_ref, cw_ref, wdn_ref, fg_ref,
         o_ref, gbuf_ref, carry_ref) = refs
    else:
        (x_ref, sh_ref, sc_ref, g2_ref, ng_ref, wup_ref, cw_ref, wdn_ref, fg_ref,
         o_ref, gbuf_ref, carry_ref) = refs
    tile = x_ref.shape[1]

    @pl.when(pl.program_id(1) == 0)
    def _():
        carry_ref[...] = jnp.zeros_like(carry_ref)

    x = x_ref[0]
    if has_attn:
        o_t = jnp.where(pl.program_id(1) < pl.num_programs(1) // 2, olo_ref[0, 0], ohi_ref[0, 0])
        x = x + g1_ref[0] * lax.dot_general(o_t, wo_ref[...], _TN, preferred_element_type=F32)
    hf = _norm_mod(x, ng_ref[...], sc_ref[0], sh_ref[0]).astype(BF16)
    acc = jnp.zeros((tile, D_MODEL), F32)
    for c0 in range(0, D_FF, FF_CHUNK):
        gate = jnp.dot(hf, wup_ref[:, c0:c0 + FF_CHUNK], preferred_element_type=F32)
        up = jnp.dot(hf, wup_ref[:, D_FF + c0:D_FF + c0 + FF_CHUNK], preferred_element_type=F32)
        gbuf_ref[0:HALO, :] = carry_ref[:, c0:c0 + FF_CHUNK]
        gbuf_ref[HALO:HALO + tile, :] = gate
        carry_ref[:, c0:c0 + FF_CHUNK] = gate[tile - HALO:tile, :]
        cw = cw_ref[:, c0:c0 + FF_CHUNK]
        conv = (gbuf_ref[HALO - 2:HALO - 2 + tile, :] * cw[0:1] + gbuf_ref[HALO - 1:HALO - 1 + tile, :] * cw[1:2]
                + gate * cw[2:3])
        act = (_silu(conv) * up).astype(BF16)
        acc = acc + jnp.dot(act, wdn_ref[c0:c0 + FF_CHUNK, :], preferred_element_type=F32)
    out = x + g2_ref[0] * acc
    if final:
        out = out * lax.rsqrt(jnp.mean(out * out, axis=-1, keepdims=True) + EPS) * fg_ref[...]
    o_ref[0] = out


def _conv_ffn(x, sh, sc, g2, norm_g, w_up, conv_w, w_down, final_g, attn=None, final=False):
    bsz, seq, d = x.shape
    t = TOK_TILE
    vec = pl.BlockSpec((1, 1, d), lambda b, i: (b, 0, 0))
    args, specs = [x], [pl.BlockSpec((1, t, d), lambda b, i: (b, i, 0))]
    if attn is not None:
        o_lo, o_hi, w_o, g1 = attn
        top = o_lo.shape[1] - 1
        last = seq // t - 1
        args += [o_lo, o_hi, w_o.astype(BF16), g1]
        specs += [pl.BlockSpec((1, 1, d, t), lambda b, i: (b, jnp.minimum(i, top), 0, 0)),
                  pl.BlockSpec((1, 1, d, t), lambda b, i: (b, jnp.minimum(last - i, top), 0, 0)),
                  _const_spec((d, d)), vec]
    args += [sh, sc, g2, norm_g.reshape(1, d), w_up.astype(BF16), conv_w, w_down.astype(BF16),
             final_g.reshape(1, d)]
    specs += [vec, vec, vec, _const_spec((1, d)), _const_spec((d, 2 * D_FF)), _const_spec((3, D_FF)),
              _const_spec((D_FF, d)), _const_spec((1, d))]
    return pl.pallas_call(
        functools.partial(_ffn_kernel, has_attn=attn is not None, final=final),
        grid=(bsz, seq // t),
        in_specs=specs,
        out_specs=pl.BlockSpec((1, t, d), lambda b, i: (b, i, 0)),
        out_shape=jax.ShapeDtypeStruct(x.shape, F32),
        scratch_shapes=[pltpu.VMEM((t + HALO, FF_CHUNK), F32), pltpu.VMEM((HALO, D_FF), F32)],
        compiler_params=_params(("arbitrary", "arbitrary")),
        name="conv_ffn_attnproj" if attn is not None else "conv_ffn",
    )(*args)


KV_ROWS = NSA_KV * NSA_DH


def _nsa_proj_kernel(x_ref, sh_ref, sc_ref, ng_ref, wa_ref, wq_ref, wg_ref, wv_ref, hot_ref,
                     kc_ref, vc_ref, ks_ref, kw_ref, qt_ref, gt_ref, vst_ref, vwt_ref):
    hm = _norm_mod(x_ref[0], ng_ref[...], sc_ref[0], sh_ref[0]).astype(BF16)
    kv = jnp.dot(hm, wa_ref[...], preferred_element_type=F32).astype(BF16)
    for g in range(NSA_KV):
        part = lambda j: kv[:, j * KV_ROWS + g * NSA_DH:j * KV_ROWS + (g + 1) * NSA_DH]
        kc_ref[0, g] = part(0)
        vc_ref[0, g] = part(1)
        ks_ref[0, g] = jnp.concatenate([part(2), hot_ref[...]], axis=1)
        kw_ref[0, g] = part(3)
    qt = lax.dot_general(wq_ref[...], hm, _NT, preferred_element_type=F32)
    qt_ref[0, 0] = (qt * (NSA_DH ** -0.5 * LOG2E)).astype(BF16)
    gt_ref[0] = _sigmoid(lax.dot_general(wg_ref[...], hm, _NT, preferred_element_type=F32))
    vt = lax.dot_general(wv_ref[...], hm, _NT, preferred_element_type=F32).astype(BF16)
    vst_ref[0, 0] = vt[:KV_ROWS]
    vwt_ref[0, 0] = vt[KV_ROWS:]


def _nsa_proj(x, sh, sc, norm_g, w_in):
    bsz, seq, d = x.shape
    t = TOK_TILE
    nt = seq // t
    nq, ng = NSA_HEADS * NSA_DH, 3 * NSA_HEADS
    o = nq + ng
    w = w_in.astype(BF16)
    col = lambda j: w[:, o + j * KV_ROWS:o + (j + 1) * KV_ROWS]
    w_a = jnp.concatenate([col(0), col(1), col(2), col(4)], axis=1)
    w_q = w[:, :nq].T
    w_g = w[:, nq:o].T
    w_v = jnp.concatenate([col(3), col(5)], axis=1).T
    hot = (np.arange(t)[:, None] // SEL_BLOCK == np.arange(KEY_LANES - NSA_DH)[None, :]).astype(np.float32)
    vec = pl.BlockSpec((1, 1, d), lambda b, i: (b, 0, 0))
    tiled = lambda r: pl.BlockSpec((1, 1, r, t), lambda b, i: (b, i, 0, 0))
    grouped = lambda n: pl.BlockSpec((1, NSA_KV, t, n), lambda b, i: (b, 0, i, 0))
    rows = lambda n: jax.ShapeDtypeStruct((bsz, NSA_KV, seq, n), BF16)
    return pl.pallas_call(
        _nsa_proj_kernel,
        grid=(bsz, nt),
        in_specs=[pl.BlockSpec((1, t, d), lambda b, i: (b, i, 0)), vec, vec, _const_spec((1, d)),
                  _const_spec((d, 4 * KV_ROWS)), _const_spec((nq, d)), _const_spec((ng, d)),
                  _const_spec((2 * KV_ROWS, d)), _const_spec(hot.shape)],
        out_specs=[grouped(NSA_DH), grouped(NSA_DH), grouped(KEY_LANES), grouped(NSA_DH), tiled(nq),
                   pl.BlockSpec((1, ng, t), lambda b, i: (b, 0, i)), tiled(KV_ROWS), tiled(KV_ROWS)],
        out_shape=[rows(NSA_DH), rows(NSA_DH), rows(KEY_LANES), rows(NSA_DH),
                   jax.ShapeDtypeStruct((bsz, nt, nq, t), BF16),
                   jax.ShapeDtypeStruct((bsz, ng, seq), F32),
                   jax.ShapeDtypeStruct((bsz, nt, KV_ROWS, t), BF16),
                   jax.ShapeDtypeStruct((bsz, nt, KV_ROWS, t), BF16)],
        compiler_params=_params(("arbitrary", "arbitrary")),
        name="nsa_in_proj",
    )(x, sh, sc, norm_g.reshape(1, d), w_a, w_q, w_g, w_v, jnp.asarray(hot, BF16))


def _compress_kernel(hk_ref, hv_ref, pk_ref, pv_ref, w1k_ref, w2k_ref, w1v_ref, w2vt_ref, kc_ref, vct_ref):
    half = CMP_STRIDE * NSA_DH

    def hidden(h_ref, pos_ref, w1_ref):
        h = h_ref[0, 0]
        n = h.shape[0]
        top = jnp.dot(h, w1_ref[:half, :], preferred_element_type=F32)
        bot = jnp.dot(h, w1_ref[half:, :], preferred_element_type=F32)
        pos = jnp.dot(jnp.broadcast_to(pos_ref[...], (8, 2 * half)).astype(BF16), w1_ref[...],
                      preferred_element_type=F32)[0:1]
        return _silu(top + pltpu.roll(bot, n - 1, 0) + pos).astype(BF16)

    kc_ref[0, 0] = jnp.dot(hidden(hk_ref, pk_ref, w1k_ref), w2k_ref[...],
                           preferred_element_type=F32).astype(BF16)
    vct_ref[0, 0] = lax.dot_general(w2vt_ref[...], hidden(hv_ref, pv_ref, w1v_ref), _NT,
                                    preferred_element_type=F32).astype(BF16)


def _compress(hk, hv, pos_k, pos_v, w1_k, w2_k, w1_v, w2_v):
    bsz, g, n, width = hk.shape
    blk = pl.BlockSpec((1, 1, n, width), lambda b, j: (b, j, 0, 0))
    return pl.pallas_call(
        _compress_kernel,
        grid=(bsz, g),
        in_specs=[blk, blk, _const_spec((1, width * 2)), _const_spec((1, width * 2)),
                  _const_spec((2 * width, CMP_HIDDEN)), _const_spec((CMP_HIDDEN, NSA_DH)),
                  _const_spec((2 * width, CMP_HIDDEN)), _const_spec((NSA_DH, CMP_HIDDEN))],
        out_specs=[pl.BlockSpec((1, 1, n, NSA_DH), lambda b, j: (b, j, 0, 0)),
                   pl.BlockSpec((1, 1, NSA_DH, n), lambda b, j: (b, j, 0, 0))],
        out_shape=[jax.ShapeDtypeStruct((bsz, g, n, NSA_DH), BF16),
                   jax.ShapeDtypeStruct((bsz, g, NSA_DH, n), BF16)],
        compiler_params=_params(("arbitrary", "arbitrary")),
        name="nsa_compress",
    )(hk, hv, pos_k.reshape(1, -1), pos_v.reshape(1, -1), w1_k.astype(BF16), w2_k.astype(BF16),
      w1_v.astype(BF16), w2_v.T.astype(BF16))


def _bucket_upper_bounds():
    n = np.arange(4 * REL_MAX_DIST, dtype=np.float64)
    exact = REL_BUCKETS // 2
    large = exact + (np.log(np.maximum(n, exact) / exact) / math.log(REL_MAX_DIST / exact)
                     * (REL_BUCKETS - exact)).astype(np.int64)
    bucket = np.where(n < exact, n.astype(np.int64), np.minimum(large, REL_BUCKETS - 1))
    return [int(np.max(np.nonzero(bucket <= j)[0])) for j in range(REL_BUCKETS - 1)]


def _bias_of_dist(dist, rb_ref, heads, uppers):
    vals = [jnp.full(dist.shape, rb_ref[REL_BUCKETS - 1, h], F32) for h in heads]
    for j in range(REL_BUCKETS - 2, -1, -1):
        m = dist <= uppers[j]
        vals = [jnp.where(m, rb_ref[j, h], v) for h, v in zip(heads, vals)]
    return vals


def _bias_near_kernel(rb_ref, o_ref, *, uppers):
    d = pl.program_id(0)
    t = o_ref.shape[2]
    heads = list(range(NSA_HEADS))
    is_win = d >= SEL_TABLES
    delta = jnp.where(is_win, d - SEL_TABLES, d)
    limit = jnp.where(is_win, WINDOW, (SEL_TABLES + 1) * t)

    def strip(i, carry):
        r0 = pl.multiple_of(i * 8, 8)
        key = r0 + lax.broadcasted_iota(jnp.int32, (8, t), 0)
        tok = lax.broadcasted_iota(jnp.int32, (8, t), 1)
        dist = delta * t + tok - key
        valid = (dist >= 0) & (dist < limit)
        vals = _bias_of_dist(dist, rb_ref, heads, uppers)
        for h in heads:
            o_ref[h, 0, pl.ds(r0, 8), :] = jnp.where(valid, vals[h], NEG)
        return carry

    lax.fori_loop(0, t // 8, strip, 0)


def _bias_cmp_kernel(rb_ref, o_ref, *, uppers):
    tb = pl.program_id(0)
    n, t = o_ref.shape[1], o_ref.shape[2]
    heads = list(range(NSA_HEADS))

    def strip(i, carry):
        r0 = pl.multiple_of(i * 8, 8)
        blk = r0 + lax.broadcasted_iota(jnp.int32, (8, t), 0)
        tok = tb * t + lax.broadcasted_iota(jnp.int32, (8, t), 1)
        dist = tok - (blk * CMP_STRIDE + CMP_LEN - 1)
        vals = _bias_of_dist(dist, rb_ref, heads, uppers)
        for h in heads:
            o_ref[h, pl.ds(r0, 8), :] = jnp.where(dist >= 0, vals[h], NEG)
        return carry

    lax.fori_loop(0, n // 8, strip, 0)


def _bias_tables(rel_bias, seq):
    t = TOK_TILE
    n_cmp = seq // CMP_STRIDE
    uppers = _bucket_upper_bounds()
    smem = pl.BlockSpec(memory_space=pltpu.SMEM)
    near = pl.pallas_call(
        functools.partial(_bias_near_kernel, uppers=uppers),
        grid=(SEL_TABLES + WIN_TILES,),
        in_specs=[smem],
        out_specs=pl.BlockSpec((NSA_HEADS, 1, t, t), lambda d: (0, d, 0, 0)),
        out_shape=jax.ShapeDtypeStruct((NSA_HEADS, SEL_TABLES + WIN_TILES, t, t), F32),
        compiler_params=_params(("arbitrary",)),
        name="relbias_near",
    )(rel_bias)
    cmp_b = pl.pallas_call(
        functools.partial(_bias_cmp_kernel, uppers=uppers),
        grid=(seq // t,),
        in_specs=[smem],
        out_specs=pl.BlockSpec((NSA_HEADS, n_cmp, t), lambda i: (0, 0, i)),
        out_shape=jax.ShapeDtypeStruct((NSA_HEADS, n_cmp, seq), F32),
        compiler_params=_params(("arbitrary",)),
        name="relbias_cmp",
    )(rel_bias)
    return near, cmp_b


def _cmp_to_sel_t(seq):
    n_cmp, n_sel = seq // CMP_STRIDE, seq // SEL_BLOCK
    c_start = np.arange(n_cmp)[:, None] * CMP_STRIDE
    s_start = np.arange(n_sel)[None, :] * SEL_BLOCK
    inside = np.clip(np.minimum(c_start + CMP_LEN, s_start + SEL_BLOCK) - np.maximum(c_start, s_start), 0, None)
    return (inside / CMP_LEN).T.astype(np.float32)


def _cmp_topk_kernel(qt_ref, kc_ref, vct_ref, bias_ref, c2s_ref, oc_ref, sel_ref):
    tb = pl.program_id(1)
    n, t = bias_ref.shape[1], bias_ref.shape[2]
    n_sel = c2s_ref.shape[0]
    tok = tb * t + lax.broadcasted_iota(jnp.int32, (1, t), 1)
    any_visible = tok >= CMP_LEN - 1
    kc = kc_ref[0, 0]
    ones = (lax.broadcasted_iota(jnp.int32, (SEL_ROWS, n), 0) == 0).astype(BF16)
    lhs = jnp.concatenate([vct_ref[0, 0], ones, c2s_ref[...]], axis=0)
    imp = jnp.zeros((n_sel, t), F32)
    for i in range(NSA_HPG):
        rows = slice(i * NSA_DH, (i + 1) * NSA_DH)
        s = jnp.dot(kc, qt_ref[0, 0, rows, :], preferred_element_type=F32) + bias_ref[i]
        m = jnp.max(_fold(s, jnp.max), axis=0, keepdims=True)
        e = jnp.exp2(s - m).astype(BF16)
        r = jnp.dot(lhs, e, preferred_element_type=F32)
        scale = jnp.where(any_visible, 1.0 / r[NSA_DH:NSA_DH + 1], 0.0)
        oc_ref[0, 0, rows, :] = (r[:NSA_DH] * scale).astype(BF16)
        imp = imp + r[NSA_DH + SEL_ROWS:] * scale

    blk = lax.broadcasted_iota(jnp.int32, (n_sel, t), 0)
    cur = (tb * t + lax.broadcasted_iota(jnp.int32, (n_sel, t), 1)) // SEL_BLOCK
    forced = (blk == 0) | (blk == cur) | (blk == cur - 1)
    score = jnp.where(blk > cur, -jnp.inf, jnp.where(forced, jnp.inf, imp))
    groups = n_sel // 8
    sub = lax.broadcasted_iota(jnp.int32, (8, t), 0)
    part = [score[8 * v:8 * v + 8] for v in range(groups)]
    cnt = [jnp.zeros((8, t), F32) for _ in range(groups)]
    for sp in range(n_sel):
        r = jnp.broadcast_to(score[sp:sp + 1], (8, t))
        for v in range(groups):
            if sp < 8 * v:
                beats = r >= part[v]
            elif sp >= 8 * v + 8:
                beats = r > part[v]
            else:
                beats = (r > part[v]) | ((r == part[v]) & (sub > sp - 8 * v))
            cnt[v] = cnt[v] + jnp.where(beats, 1.0, 0.0)
    per_tile = t // SEL_BLOCK
    pad = jnp.zeros((SEL_ROWS - per_tile, t), F32)
    for v in range(groups):
        mask = jnp.where(cnt[v] < min(SEL_TOPN, n_sel), 0.0, NEG)
        for r0 in range(0, 8, per_tile):
            kt = (8 * v + r0) // per_tile
            sel_ref[0, 0, 0, kt] = jnp.concatenate([mask[r0:r0 + per_tile], pad], axis=0).astype(BF16)


def _cmp_topk(q_t, k_cmp, v_cmp_t, bias_cmp, seq):
    bsz, nt, nq, t = q_t.shape
    n_cmp, n_sel = seq // CMP_STRIDE, seq // SEL_BLOCK
    grp = NSA_HPG * NSA_DH
    c2s = jnp.asarray(_cmp_to_sel_t(seq), BF16)
    return pl.pallas_call(
        _cmp_topk_kernel,
        grid=(NSA_KV, nt, bsz),
        in_specs=[pl.BlockSpec((1, 1, grp, t), lambda g, i, b: (b, i, g, 0)),
                  pl.BlockSpec((1, 1, n_cmp, NSA_DH), lambda g, i, b: (b, g, 0, 0)),
                  pl.BlockSpec((1, 1, NSA_DH, n_cmp), lambda g, i, b: (b, g, 0, 0)),
                  pl.BlockSpec((NSA_HPG, n_cmp, t), lambda g, i, b: (g, 0, i)),
                  _const_spec((n_sel, n_cmp))],
        out_specs=[pl.BlockSpec((1, 1, grp, t), lambda g, i, b: (b, i, g, 0)),
                   pl.BlockSpec((1, 1, 1, nt, SEL_ROWS, t), lambda g, i, b: (b, g, i, 0, 0, 0))],
        out_shape=[jax.ShapeDtypeStruct((bsz, nt, nq, t), BF16),
                   jax.ShapeDtypeStruct((bsz, NSA_KV, nt, nt, SEL_ROWS, t), BF16)],
        compiler_params=_params(("arbitrary", "arbitrary", "arbitrary")),
        name="nsa_cmp_topk",
    )(q_t, k_cmp, v_cmp_t, bias_cmp, c2s)


def _sel_win_kernel(qa_ref, qb_ref, ks_ref, vst_ref, kw_ref, vwt_ref, sela_ref, selb_ref, tab_ref, ga_ref, gb_ref,
                    oca_ref, ocb_ref, olo_ref, ohi_ref, s_scr):
    h = pl.program_id(0)
    i = pl.program_id(2)
    nt, t = vst_ref.shape[1], vst_ref.shape[3]
    half = nt // 2
    tb = (i, nt - 1 - i)
    q = (qa_ref[0, 0], qb_ref[0, 0])
    sel = (sela_ref, selb_ref)

    def tile_rows(kt):
        return pl.ds(pl.multiple_of(kt * t, t), t)

    def pick(on_a, xa, xb):
        if isinstance(on_a, bool):
            return xa if on_a else xb
        return jnp.where(on_a, xa, xb)

    work = []
    for p in range(nt + 1):
        on_a = True if p == 0 else (False if p >= half else p <= i)
        kt = pick(on_a, p, jnp.maximum(p - i - 1, 0))
        work.append((on_a, kt, True, pick(on_a, tb[0], tb[1]) - kt))
    for side in range(2):
        for delta in range(WIN_TILES):
            keep = True if (side == 1 and half >= WIN_TILES) else tb[side] >= delta
            work.append((side == 0, jnp.maximum(tb[side] - delta, 0), keep, delta))
    n_selpos = nt + 1

    neg8 = jnp.full((8, t), NEG, F32)
    mx = {(0, 0): neg8, (0, 1): neg8, (1, 0): neg8, (1, 1): neg8}
    q_pad = jnp.zeros((KEY_LANES - NSA_DH - SEL_ROWS, t), BF16)
    for p, (on_a, kt, keep, delta) in enumerate(work):
        win = p >= n_selpos
        qp = pick(on_a, q[0], q[1])
        if win:
            s = jnp.dot(kw_ref[0, 0, tile_rows(kt), :], qp, preferred_element_type=F32)
            s = s + tab_ref[0, SEL_TABLES + delta]
        else:
            rows = pick(on_a, sel[0][0, 0, 0, kt], sel[1][0, 0, 0, kt])
            q_aug = jnp.concatenate([qp, rows, q_pad], axis=0)
            s = jnp.dot(ks_ref[0, 0, tile_rows(kt), :], q_aug, preferred_element_type=F32)
            s = s + tab_ref[0, jnp.minimum(delta, SEL_TABLES - 1)]
        s_scr[p] = s
        f = _fold(s, jnp.max)
        if not isinstance(keep, bool):
            f = jnp.where(keep, f, NEG)
        br = int(win)
        if isinstance(on_a, bool):
            mx[(br, int(not on_a))] = jnp.maximum(mx[(br, int(not on_a))], f)
        else:
            mx[(br, 0)] = jnp.maximum(mx[(br, 0)], jnp.where(on_a, f, NEG))
            mx[(br, 1)] = jnp.maximum(mx[(br, 1)], jnp.where(on_a, NEG, f))
    mx = {k: jnp.max(v, axis=0, keepdims=True) for k, v in mx.items()}

    ones = (lax.broadcasted_iota(jnp.int32, (SEL_ROWS, t), 0) == 0).astype(BF16)
    zo = jnp.zeros((NSA_DH + SEL_ROWS, t), F32)
    acc = {k: zo for k in mx}
    for p, (on_a, kt, keep, delta) in enumerate(work):
        win = p >= n_selpos
        br = int(win)
        m = pick(on_a, mx[(br, 0)], mx[(br, 1)])
        if not isinstance(keep, bool):
            m = jnp.where(keep, m, -NEG)
        pr = jnp.exp2(s_scr[p] - m).astype(BF16)
        vt_ref = vwt_ref if win else vst_ref
        ov = jnp.dot(jnp.concatenate([vt_ref[0, kt], ones], axis=0), pr, preferred_element_type=F32)
        if isinstance(on_a, bool):
            k = (br, int(not on_a))
            acc[k] = acc[k] + ov
        else:
            acc[(br, 0)] = acc[(br, 0)] + jnp.where(on_a, ov, 0.0)
            acc[(br, 1)] = acc[(br, 1)] + jnp.where(on_a, 0.0, ov)

    for side, (g_ref, oc_ref, o_ref) in enumerate(((ga_ref, oca_ref, olo_ref), (gb_ref, ocb_ref, ohi_ref))):
        gate = lambda br: g_ref[0, pl.ds(br * NSA_HEADS + h, 1), :]
        o_s = acc[(0, side)][:NSA_DH] / acc[(0, side)][NSA_DH:NSA_DH + 1]
        o_w = acc[(1, side)][:NSA_DH] / acc[(1, side)][NSA_DH:NSA_DH + 1]
        o_ref[0, 0] = (gate(0) * oc_ref[0, 0].astype(F32) + gate(1) * o_s + gate(2) * o_w).astype(BF16)


def _sel_win(q_t, ks, vs_t, kw, vw_t, sel, tables, gates_t, oc_t):
    bsz, nt, nq, t = q_t.shape
    seq = nt * t
    last = nt - 1
    head_a = pl.BlockSpec((1, 1, NSA_DH, t), lambda h, b, i: (b, i, h, 0))
    head_b = pl.BlockSpec((1, 1, NSA_DH, t), lambda h, b, i: (b, last - i, h, 0))
    keys = pl.BlockSpec((1, 1, seq, NSA_DH), lambda h, b, i: (b, h // NSA_HPG, 0, 0))
    keys_sel = pl.BlockSpec((1, 1, seq, KEY_LANES), lambda h, b, i: (b, h // NSA_HPG, 0, 0))
    vals = pl.BlockSpec((1, nt, NSA_DH, t), lambda h, b, i: (b, 0, h // NSA_HPG, 0))
    sel_a = pl.BlockSpec((1, 1, 1, nt, SEL_ROWS, t), lambda h, b, i: (b, h // NSA_HPG, i, 0, 0, 0))
    sel_b = pl.BlockSpec((1, 1, 1, nt, SEL_ROWS, t), lambda h, b, i: (b, h // NSA_HPG, last - i, 0, 0, 0))
    gate_a = pl.BlockSpec((1, 3 * NSA_HEADS, t), lambda h, b, i: (b, 0, i))
    gate_b = pl.BlockSpec((1, 3 * NSA_HEADS, t), lambda h, b, i: (b, 0, last - i))
    out = jax.ShapeDtypeStruct((bsz, nt // 2, nq, t), BF16)
    return pl.pallas_call(
        _sel_win_kernel,
        grid=(NSA_HEADS, bsz, nt // 2),
        in_specs=[head_a, head_b, keys_sel, vals, keys, vals, sel_a, sel_b,
                  pl.BlockSpec((1, SEL_TABLES + WIN_TILES, t, t), lambda h, b, i: (h, 0, 0, 0)),
                  gate_a, gate_b, head_a, head_b],
        out_specs=[head_a, head_a],
        out_shape=[out, out],
        scratch_shapes=[pltpu.VMEM((nt + 1 + 2 * WIN_TILES, t, t), F32)],
        compiler_params=_params(("arbitrary", "arbitrary", "arbitrary")),
        name="nsa_sel_win",
    )(q_t, q_t, ks, vs_t, kw, vw_t, sel, sel, tables, gates_t, gates_t, oc_t, oc_t)


def kernel(x, c, mod_w, mod_b, norm_mix_g, norm_ffn_g, ab_w_in, ab_w_out, hgrn_lb_logits, hgrn_onorm_g, sconv_w, nsa_w_in, nsa_w_out, nsa_cmp_pos_k, nsa_cmp_pos_v, nsa_cmp_w1_k, nsa_cmp_w2_k, nsa_cmp_w1_v, nsa_cmp_w2_v, rel_bias, ffn_w_up, ffn_conv_w, ffn_w_down, final_norm_g):
    bsz, seq, d = x.shape
    assert d == D_MODEL and seq % TOK_TILE == 0
    lower = jnp.cumsum(jax.nn.softmax(hgrn_lb_logits.astype(F32), axis=0), axis=0)
    mod = _modulation(c, mod_w, mod_b)
    parts = lambda l: [mod[l, :, j * d:(j + 1) * d].reshape(bsz, 1, d) for j in range(6)]

    sh1, sc1, g1, sh2, sc2, g2 = parts(0)
    x = _mixer0(x, sh1, sc1, g1, norm_mix_g[0], ab_w_in[0], ab_w_out[0], lower[0], hgrn_onorm_g[0], sconv_w[0])
    x = _conv_ffn(x, sh2, sc2, g2, norm_ffn_g[0], ffn_w_up[0], ffn_conv_w[0], ffn_w_down[0], final_norm_g)

    sh1, sc1, g1, sh2, sc2, g2 = parts(1)
    kc, vc, ks, kw, q_t, gates_t, vs_t, vw_t = _nsa_proj(x, sh1, sc1, norm_mix_g[1], nsa_w_in[0])
    to_blocks = lambda a: a.reshape(bsz, NSA_KV, seq // CMP_STRIDE, CMP_STRIDE * NSA_DH)
    k_cmp, v_cmp_t = _compress(to_blocks(kc), to_blocks(vc), nsa_cmp_pos_k[0], nsa_cmp_pos_v[0],
                               nsa_cmp_w1_k[0], nsa_cmp_w2_k[0], nsa_cmp_w1_v[0], nsa_cmp_w2_v[0])
    near, bias_cmp = _bias_tables(rel_bias.astype(F32) * LOG2E, seq)
    oc_t, sel = _cmp_topk(q_t, k_cmp, v_cmp_t, bias_cmp, seq)
    o_lo, o_hi = _sel_win(q_t, ks, vs_t, kw, vw_t, sel, near, gates_t, oc_t)
    return _conv_ffn(x, sh2, sc2, g2, norm_ffn_g[1], ffn_w_up[1], ffn_conv_w[1], ffn_w_down[1], final_norm_g,
                     attn=(o_lo, o_hi, nsa_w_out[0], g1), final=True)
```

```python
import functools
import math

import jax
import jax.numpy as jnp
import numpy as np
from jax import lax
from jax.experimental import pallas as pl
from jax.experimental.pallas import tpu as pltpu

F32 = jnp.float32
BF16 = jnp.bfloat16

EPS = 1e-6
D_MODEL = 1024
D_FF = 2816

HG_HEADS = 4
HG_DK = 128
HG_WIDTH = HG_HEADS * HG_DK
SC_WIDTH = D_MODEL - HG_WIDTH
AB_IN = 4 * HG_WIDTH + 3 * SC_WIDTH
HG_CHUNK = 64
HG_LEVELS = 6

NSA_HEADS = 16
NSA_KV = 4
NSA_HPG = NSA_HEADS // NSA_KV
NSA_DH = D_MODEL // NSA_HEADS
CMP_LEN = 32
CMP_STRIDE = 16
CMP_HIDDEN = 256
SEL_BLOCK = 64
SEL_TOPN = 16
WINDOW = 512
REL_BUCKETS = 32
REL_MAX_DIST = 1024

TOK_TILE = 256
NEAR_TILES = 5
SEL_TABLES = NEAR_TILES + 1
WIN_TILES = WINDOW // TOK_TILE + 1
NEG = -1e30
LOG2E = math.log2(math.e)
KEY_LANES = 128
SEL_ROWS = 16
MXU_SKEW = 4
VMEM_LIMIT = 56 * 1024 * 1024
HALO = 8

_NT = (((1,), (1,)), ((), ()))
_TN = (((0,), (0,)), ((), ()))


def _sigmoid(x):
    return 1.0 / (1.0 + jnp.exp(-x))


def _silu(x):
    return x * _sigmoid(x)


def _norm_mod(x, g, sc, sh):
    ms = jnp.mean(x * x, axis=-1, keepdims=True)
    return (x * lax.rsqrt(ms + EPS) * g) * (1.0 + sc) + sh


def _fold(x, op):
    return op(x.reshape(x.shape[0] // 8, 8, x.shape[1]), axis=0)


def _const_spec(shape):
    n = len(shape)
    return pl.BlockSpec(shape, lambda *_: (0,) * n, pipeline_mode=pl.Buffered(1))


def _params(sem):
    return pltpu.CompilerParams(dimension_semantics=sem, vmem_limit_bytes=VMEM_LIMIT)


def _mod_kernel(c_ref, w_ref, b_ref, o_ref):
    c = c_ref[...]
    o_ref[0] = jnp.dot(_silu(c), w_ref[0], preferred_element_type=F32,
                       precision=lax.Precision.HIGHEST) + b_ref[0]


def _modulation(c, mod_w, mod_b):
    depth, d, n = mod_w.shape
    bsz = c.shape[0]
    tn = 1024
    return pl.pallas_call(
        _mod_kernel,
        grid=(depth, n // tn),
        in_specs=[pl.BlockSpec((bsz, d), lambda l, j: (0, 0)),
                  pl.BlockSpec((1, d, tn), lambda l, j: (l, 0, j)),
                  pl.BlockSpec((1, 1, tn), lambda l, j: (l, 0, j))],
        out_specs=pl.BlockSpec((1, bsz, tn), lambda l, j: (l, 0, j)),
        out_shape=jax.ShapeDtypeStruct((depth, bsz, n), F32),
        compiler_params=_params(("arbitrary", "arbitrary")),
        name="adaln_mod",
    )(c, mod_w, mod_b.reshape(depth, 1, n))


def _hgrn_decay_matrix():
    t = np.arange(HG_CHUNK)
    tril = (t[None, :] <= t[:, None]).astype(np.float32)
    mats = [tril]
    for lvl in range(1, HG_LEVELS + 1):
        m = 1 << (lvl - 1)
        mid = (t // (2 * m)) * (2 * m) + m - 1
        mats.append(tril - tril[mid])
    return np.concatenate(mats, axis=0)


def _mixer0_kernel(x_ref, sh_ref, sc_ref, gt_ref, ng_ref, win_ref, wout_ref, lb_ref, og_ref, cw_ref,
                   pm_ref, o_ref, proj_ref, cat_ref, st_ref, ubuf_ref, d_scr):
    tile = x_ref.shape[1]
    c64 = HG_CHUNK

    @pl.when(pl.program_id(1) == 0)
    def _():
        st_ref[...] = jnp.zeros_like(st_ref)
        ubuf_ref[0:HALO, :] = jnp.zeros((HALO, SC_WIDTH), F32)

    x = x_ref[0]
    hm = _norm_mod(x, ng_ref[...], sc_ref[0], sh_ref[0]).astype(BF16)
    proj_ref[...] = jnp.dot(hm, win_ref[...], preferred_element_type=F32)

    row = lax.broadcasted_iota(jnp.int32, (c64, HG_DK), 0)
    r64 = lax.broadcasted_iota(jnp.int32, (c64, c64), 0)
    c64i = lax.broadcasted_iota(jnp.int32, (c64, c64), 1)
    second = [None] + [(row & (1 << (lvl - 1))) != 0 for lvl in range(1, HG_LEVELS + 1)]
    same = [None] + [(r64 >> lvl) == (c64i >> lvl) for lvl in range(1, HG_LEVELS + 1)]
    eye = r64 == c64i
    pm = pm_ref[...]
    og = og_ref[...]

    units = [(c, h) for c in range(tile // c64) for h in range(HG_HEADS)]

    def blk(c, h, j):
        return pl.ds(c * c64, c64), pl.ds(j * HG_WIDTH + h * HG_DK, HG_DK)

    for u, (c, h) in enumerate(units):
        lb = lb_ref[:, h * HG_DK:(h + 1) * HG_DK]
        fg = lb + (1.0 - lb) * _sigmoid(proj_ref[blk(c, h, 1)])
        proj_ref[blk(c, h, 1)] = 1.0 - fg
        lg = jnp.log(fg)
        lg_hi = lg.astype(BF16)
        lg_lo = (lg - lg_hi.astype(F32)).astype(BF16)
        dall = jnp.dot(pm, jnp.concatenate([lg_hi, lg_lo], axis=1), preferred_element_type=F32)
        d_scr[u] = dall[:, :HG_DK] + dall[:, HG_DK:]

    intra, q_in, k_out, decay = [], [], [], []
    for u, (c, h) in enumerate(units):
        q = proj_ref[blk(c, h, 0)]
        k = proj_ref[blk(c, h, 1)]
        a = jnp.zeros((c64, c64), F32)
        for lvl in range(1, HG_LEVELS + 1):
            e = jnp.exp(-jnp.abs(d_scr[u, lvl * c64:(lvl + 1) * c64, :]))
            qt = jnp.where(second[lvl], q * e, 0.0).astype(BF16)
            kt = jnp.where(second[lvl], 0.0, k * e).astype(BF16)
            al = lax.dot_general(qt, kt, _NT, preferred_element_type=F32)
            a = a + (al if lvl == HG_LEVELS else jnp.where(same[lvl], al, 0.0))
        intra.append(jnp.where(eye, jnp.sum(q * k, axis=-1, keepdims=True), a).astype(BF16))
        b = d_scr[u, 0:c64, :]
        b_last = d_scr[u, c64 - 1:c64, :]
        q_in.append((q * jnp.exp(b)).astype(BF16))
        k_out.append((k * jnp.exp(b_last - b)).astype(BF16))
        decay.append(jnp.exp(b_last))

    state = [st_ref[h] for h in range(HG_HEADS)]
    for u, (c, h) in enumerate(units):
        vb = proj_ref[blk(c, h, 2)].astype(BF16)
        o = jnp.dot(intra[u], vb, preferred_element_type=F32)
        o = o + lax.dot_general(q_in[u], state[h].astype(BF16), _NT, preferred_element_type=F32)
        state[h] = state[h] * decay[u] + lax.dot_general(vb, k_out[u], _TN, preferred_element_type=F32)
        on = o * lax.rsqrt(jnp.mean(o * o, axis=-1, keepdims=True) + EPS) * og
        rows, _ = blk(c, h, 0)
        cat_ref[rows, h * HG_DK:(h + 1) * HG_DK] = (on * _silu(proj_ref[blk(c, h, 3)])).astype(BF16)
    for h in range(HG_HEADS):
        st_ref[h] = state[h]

    off = 4 * HG_WIDTH
    u = proj_ref[:, off + SC_WIDTH:off + 2 * SC_WIDTH] * proj_ref[:, off + 2 * SC_WIDTH:off + 3 * SC_WIDTH]
    ubuf_ref[HALO:HALO + tile, :] = u
    cw = cw_ref[...]
    conv = (ubuf_ref[HALO - 2:HALO - 2 + tile, :] * cw[0:1] + ubuf_ref[HALO - 1:HALO - 1 + tile, :] * cw[1:2]
            + u * cw[2:3])
    ubuf_ref[0:HALO, :] = u[tile - HALO:tile, :]
    cat_ref[:, HG_WIDTH:] = (proj_ref[:, off:off + SC_WIDTH] * conv).astype(BF16)

    y = jnp.dot(cat_ref[...], wout_ref[...], preferred_element_type=F32)
    o_ref[0] = x + gt_ref[0] * y


def _mixer0(x, sh, sc, gt, norm_g, w_in, w_out, lower, onorm_g, sconv_w):
    bsz, seq, d = x.shape
    t = TOK_TILE
    pm = jnp.asarray(_hgrn_decay_matrix(), BF16)
    vec = pl.BlockSpec((1, 1, d), lambda b, i: (b, 0, 0))
    return pl.pallas_call(
        _mixer0_kernel,
        grid=(bsz, seq // t),
        in_specs=[pl.BlockSpec((1, t, d), lambda b, i: (b, i, 0)), vec, vec, vec,
                  _const_spec((1, d)), _const_spec((d, AB_IN)), _const_spec((d, d)),
                  _const_spec((1, HG_WIDTH)), _const_spec((1, HG_DK)), _const_spec((3, SC_WIDTH)),
                  _const_spec(pm.shape)],
        out_specs=pl.BlockSpec((1, t, d), lambda b, i: (b, i, 0)),
        out_shape=jax.ShapeDtypeStruct(x.shape, F32),
        scratch_shapes=[pltpu.VMEM((t, AB_IN), F32), pltpu.VMEM((t, d), BF16),
                        pltpu.VMEM((HG_HEADS, HG_DK, HG_DK), F32), pltpu.VMEM((t + HALO, SC_WIDTH), F32),
                        pltpu.VMEM((t // HG_CHUNK * HG_HEADS, pm.shape[0], HG_DK), F32)],
        compiler_params=_params(("arbitrary", "arbitrary")),
        name="mixer_hgrn_sconv",
    )(x, sh, sc, gt, norm_g.reshape(1, d), w_in.astype(BF16), w_out.astype(BF16),
      lower.reshape(1, HG_WIDTH), onorm_g.reshape(1, HG_DK), sconv_w, pm)


FF_CHUNK = 1408


def _ffn_kernel(*refs, has_attn, final):
    if has_attn:
        (x_ref, olo_ref, ohi_ref, wo_ref, g1_ref, sh_ref, sc_ref, g2_ref, ng_ref, wup_ref, cw_ref, wdn_ref, fg_ref,
         o_ref, gbuf_ref, carry_ref) = refs
    else:
        (x_ref, sh_ref, sc_ref, g2_ref, ng_ref, wup_ref, cw_ref, wdn_ref, fg_ref,
         o_ref, gbuf_ref, carry_ref) = refs
    tile = x_ref.shape[1]

    @pl.when(pl.program_id(1) == 0)
    def _():
        carry_ref[...] = jnp.zeros_like(carry_ref)

    x = x_ref[0]
    if has_attn:
        o_t = jnp.where(pl.program_id(1) < pl.num_programs(1) // 2, olo_ref[0, 0], ohi_ref[0, 0])
        x = x + g1_ref[0] * lax.dot_general(o_t, wo_ref[...], _TN, preferred_element_type=F32)
    hf = _norm_mod(x, ng_ref[...], sc_ref[0], sh_ref[0]).astype(BF16)
    gbuf_ref[0:HALO, :] = carry_ref[...]
    gates, ups = [], []
    for c0 in range(0, D_FF, FF_CHUNK):
        gate = jnp.dot(hf, wup_ref[:, c0:c0 + FF_CHUNK], preferred_element_type=F32)
        ups.append(jnp.dot(hf, wup_ref[:, D_FF + c0:D_FF + c0 + FF_CHUNK], preferred_element_type=F32))
        gbuf_ref[HALO:HALO + tile, c0:c0 + FF_CHUNK] = gate
        carry_ref[:, c0:c0 + FF_CHUNK] = gate[tile - HALO:tile, :]
        gates.append(gate)
    acc = jnp.zeros((tile, D_MODEL), F32)
    for j, c0 in enumerate(range(0, D_FF, FF_CHUNK)):
        cols = slice(c0, c0 + FF_CHUNK)
        cw = cw_ref[:, cols]
        conv = (gbuf_ref[HALO - 2:HALO - 2 + tile, cols] * cw[0:1] + gbuf_ref[HALO - 1:HALO - 1 + tile, cols] * cw[1:2]
                + gates[j] * cw[2:3])
        act = (_silu(conv) * ups[j]).astype(BF16)
        acc = acc + jnp.dot(act, wdn_ref[cols, :], preferred_element_type=F32)
    out = x + g2_ref[0] * acc
    if final:
        out = out * lax.rsqrt(jnp.mean(out * out, axis=-1, keepdims=True) + EPS) * fg_ref[...]
    o_ref[0] = out


def _conv_ffn(x, sh, sc, g2, norm_g, w_up, conv_w, w_down, final_g, attn=None, final=False):
    bsz, seq, d = x.shape
    t = TOK_TILE
    vec = pl.BlockSpec((1, 1, d), lambda b, i: (b, 0, 0))
    args, specs = [x], [pl.BlockSpec((1, t, d), lambda b, i: (b, i, 0))]
    if attn is not None:
        o_lo, o_hi, w_o, g1 = attn
        top = o_lo.shape[1] - 1
        last = seq // t - 1
        args += [o_lo, o_hi, w_o.astype(BF16), g1]
        specs += [pl.BlockSpec((1, 1, d, t), lambda b, i: (b, jnp.minimum(i, top), 0, 0)),
                  pl.BlockSpec((1, 1, d, t), lambda b, i: (b, jnp.minimum(last - i, top), 0, 0)),
                  _const_spec((d, d)), vec]
    args += [sh, sc, g2, norm_g.reshape(1, d), w_up.astype(BF16), conv_w, w_down.astype(BF16),
             final_g.reshape(1, d)]
    specs += [vec, vec, vec, _const_spec((1, d)), _const_spec((d, 2 * D_FF)), _const_spec((3, D_FF)),
              _const_spec((D_FF, d)), _const_spec((1, d))]
    return pl.pallas_call(
        functools.partial(_ffn_kernel, has_attn=attn is not None, final=final),
        grid=(bsz, seq // t),
        in_specs=specs,
        out_specs=pl.BlockSpec((1, t, d), lambda b, i: (b, i, 0)),
        out_shape=jax.ShapeDtypeStruct(x.shape, F32),
        scratch_shapes=[pltpu.VMEM((t + HALO, D_FF), F32), pltpu.VMEM((HALO, D_FF), F32)],
        compiler_params=_params(("arbitrary", "arbitrary")),
        name="conv_ffn_attnproj" if attn is not None else "conv_ffn",
    )(*args)


KV_ROWS = NSA_KV * NSA_DH


def _nsa_proj_kernel(x_ref, sh_ref, sc_ref, ng_ref, wa_ref, wq_ref, wg_ref, wv_ref, hot_ref,
                     kc_ref, vc_ref, ks_ref, kw_ref, qt_ref, gt_ref, vst_ref, vwt_ref):
    hm = _norm_mod(x_ref[0], ng_ref[...], sc_ref[0], sh_ref[0]).astype(BF16)
    kv = jnp.dot(hm, wa_ref[...], preferred_element_type=F32).astype(BF16)
    for g in range(NSA_KV):
        part = lambda j: kv[:, j * KV_ROWS + g * NSA_DH:j * KV_ROWS + (g + 1) * NSA_DH]
        kc_ref[0, g] = part(0)
        vc_ref[0, g] = part(1)
        ks_ref[0, g] = jnp.concatenate([part(2), hot_ref[...]], axis=1)
        kw_ref[0, g] = part(3)
    qt = lax.dot_general(wq_ref[...], hm, _NT, preferred_element_type=F32)
    qt_ref[0, 0] = (qt * (NSA_DH ** -0.5 * LOG2E)).astype(BF16)
    gt_ref[0] = _sigmoid(lax.dot_general(wg_ref[...], hm, _NT, preferred_element_type=F32))
    vt = lax.dot_general(wv_ref[...], hm, _NT, preferred_element_type=F32).astype(BF16)
    vst_ref[0, 0] = vt[:KV_ROWS]
    vwt_ref[0, 0] = vt[KV_ROWS:]


def _nsa_proj(x, sh, sc, norm_g, w_in):
    bsz, seq, d = x.shape
    t = TOK_TILE
    nt = seq // t
    nq, ng = NSA_HEADS * NSA_DH, 3 * NSA_HEADS
    o = nq + ng
    w = w_in.astype(BF16)
    col = lambda j: w[:, o + j * KV_ROWS:o + (j + 1) * KV_ROWS]
    w_a = jnp.concatenate([col(0), col(1), col(2), col(4)], axis=1)
    w_q = w[:, :nq].T
    w_g = w[:, nq:o].T
    w_v = jnp.concatenate([col(3), col(5)], axis=1).T
    hot = (np.arange(t)[:, None] // SEL_BLOCK == np.arange(KEY_LANES - NSA_DH)[None, :]).astype(np.float32)
    vec = pl.BlockSpec((1, 1, d), lambda b, i: (b, 0, 0))
    tiled = lambda r: pl.BlockSpec((1, 1, r, t), lambda b, i: (b, i, 0, 0))
    grouped = lambda n: pl.BlockSpec((1, NSA_KV, t, n), lambda b, i: (b, 0, i, 0))
    rows = lambda n: jax.ShapeDtypeStruct((bsz, NSA_KV, seq, n), BF16)
    return pl.pallas_call(
        _nsa_proj_kernel,
        grid=(bsz, nt),
        in_specs=[pl.BlockSpec((1, t, d), lambda b, i: (b, i, 0)), vec, vec, _const_spec((1, d)),
                  _const_spec((d, 4 * KV_ROWS)), _const_spec((nq, d)), _const_spec((ng, d)),
                  _const_spec((2 * KV_ROWS, d)), _const_spec(hot.shape)],
        out_specs=[grouped(NSA_DH), grouped(NSA_DH), grouped(KEY_LANES), grouped(NSA_DH), tiled(nq),
                   pl.BlockSpec((1, ng, t), lambda b, i: (b, 0, i)), tiled(KV_ROWS), tiled(KV_ROWS)],
        out_shape=[rows(NSA_DH), rows(NSA_DH), rows(KEY_LANES), rows(NSA_DH),
                   jax.ShapeDtypeStruct((bsz, nt, nq, t), BF16),
                   jax.ShapeDtypeStruct((bsz, ng, seq), F32),
                   jax.ShapeDtypeStruct((bsz, nt, KV_ROWS, t), BF16),
                   jax.ShapeDtypeStruct((bsz, nt, KV_ROWS, t), BF16)],
        compiler_params=_params(("arbitrary", "arbitrary")),
        name="nsa_in_proj",
    )(x, sh, sc, norm_g.reshape(1, d), w_a, w_q, w_g, w_v, jnp.asarray(hot, BF16))


def _compress_kernel(hk_ref, hv_ref, pk_ref, pv_ref, w1k_ref, w2k_ref, w1v_ref, w2vt_ref, kc_ref, vct_ref):
    half = CMP_STRIDE * NSA_DH

    def hidden(h_ref, pos_ref, w1_ref):
        h = h_ref[0, 0]
        n = h.shape[0]
        top = jnp.dot(h, w1_ref[:half, :], preferred_element_type=F32)
        bot = jnp.dot(h, w1_ref[half:, :], preferred_element_type=F32)
        pos = jnp.dot(jnp.broadcast_to(pos_ref[...], (8, 2 * half)).astype(BF16), w1_ref[...],
                      preferred_element_type=F32)[0:1]
        return _silu(top + pltpu.roll(bot, n - 1, 0) + pos).astype(BF16)

    kc_ref[0, 0] = jnp.dot(hidden(hk_ref, pk_ref, w1k_ref), w2k_ref[...],
                           preferred_element_type=F32).astype(BF16)
    vct_ref[0, 0] = lax.dot_general(w2vt_ref[...], hidden(hv_ref, pv_ref, w1v_ref), _NT,
                                    preferred_element_type=F32).astype(BF16)


def _compress(hk, hv, pos_k, pos_v, w1_k, w2_k, w1_v, w2_v):
    bsz, g, n, width = hk.shape
    blk = pl.BlockSpec((1, 1, n, width), lambda b, j: (b, j, 0, 0))
    return pl.pallas_call(
        _compress_kernel,
        grid=(bsz, g),
        in_specs=[blk, blk, _const_spec((1, width * 2)), _const_spec((1, width * 2)),
                  _const_spec((2 * width, CMP_HIDDEN)), _const_spec((CMP_HIDDEN, NSA_DH)),
                  _const_spec((2 * width, CMP_HIDDEN)), _const_spec((NSA_DH, CMP_HIDDEN))],
        out_specs=[pl.BlockSpec((1, 1, n, NSA_DH), lambda b, j: (b, j, 0, 0)),
                   pl.BlockSpec((1, 1, NSA_DH, n), lambda b, j: (b, j, 0, 0))],
        out_shape=[jax.ShapeDtypeStruct((bsz, g, n, NSA_DH), BF16),
                   jax.ShapeDtypeStruct((bsz, g, NSA_DH, n), BF16)],
        compiler_params=_params(("arbitrary", "arbitrary")),
        name="nsa_compress",
    )(hk, hv, pos_k.reshape(1, -1), pos_v.reshape(1, -1), w1_k.astype(BF16), w2_k.astype(BF16),
      w1_v.astype(BF16), w2_v.T.astype(BF16))


def _bucket_upper_bounds():
    n = np.arange(4 * REL_MAX_DIST, dtype=np.float64)
    exact = REL_BUCKETS // 2
    large = exact + (np.log(np.maximum(n, exact) / exact) / math.log(REL_MAX_DIST / exact)
                     * (REL_BUCKETS - exact)).astype(np.int64)
    bucket = np.where(n < exact, n.astype(np.int64), np.minimum(large, REL_BUCKETS - 1))
    return [int(np.max(np.nonzero(bucket <= j)[0])) for j in range(REL_BUCKETS - 1)]


def _bias_of_dist(dist, rb_ref, heads, uppers):
    vals = [jnp.full(dist.shape, rb_ref[REL_BUCKETS - 1, h], F32) for h in heads]
    for j in range(REL_BUCKETS - 2, -1, -1):
        m = dist <= uppers[j]
        vals = [jnp.where(m, rb_ref[j, h], v) for h, v in zip(heads, vals)]
    return vals


def _bias_near_kernel(rb_ref, o_ref, *, uppers):
    d = pl.program_id(0)
    t = o_ref.shape[2]
    heads = list(range(NSA_HEADS))
    is_win = d >= SEL_TABLES
    delta = jnp.where(is_win, d - SEL_TABLES, d)
    limit = jnp.where(is_win, WINDOW, (SEL_TABLES + 1) * t)

    def strip(i, carry):
        r0 = pl.multiple_of(i * 8, 8)
        key = r0 + lax.broadcasted_iota(jnp.int32, (8, t), 0)
        tok = lax.broadcasted_iota(jnp.int32, (8, t), 1)
        dist = delta * t + tok - key
        valid = (dist >= 0) & (dist < limit)
        vals = _bias_of_dist(dist, rb_ref, heads, uppers)
        for h in heads:
            o_ref[h, 0, pl.ds(r0, 8), :] = jnp.where(valid, vals[h], NEG)
        return carry

    lax.fori_loop(0, t // 8, strip, 0)


def _bias_cmp_kernel(rb_ref, o_ref, *, uppers):
    tb = pl.program_id(0)
    n, t = o_ref.shape[1], o_ref.shape[2]
    heads = list(range(NSA_HEADS))

    def strip(i, carry):
        r0 = pl.multiple_of(i * 8, 8)
        blk = r0 + lax.broadcasted_iota(jnp.int32, (8, t), 0)
        tok = tb * t + lax.broadcasted_iota(jnp.int32, (8, t), 1)
        dist = tok - (blk * CMP_STRIDE + CMP_LEN - 1)
        vals = _bias_of_dist(dist, rb_ref, heads, uppers)
        for h in heads:
            o_ref[h, pl.ds(r0, 8), :] = jnp.where(dist >= 0, vals[h], NEG)
        return carry

    lax.fori_loop(0, n // 8, strip, 0)


def _bias_tables(rel_bias, seq):
    t = TOK_TILE
    n_cmp = seq // CMP_STRIDE
    uppers = _bucket_upper_bounds()
    smem = pl.BlockSpec(memory_space=pltpu.SMEM)
    near = pl.pallas_call(
        functools.partial(_bias_near_kernel, uppers=uppers),
        grid=(SEL_TABLES + WIN_TILES,),
        in_specs=[smem],
        out_specs=pl.BlockSpec((NSA_HEADS, 1, t, t), lambda d: (0, d, 0, 0)),
        out_shape=jax.ShapeDtypeStruct((NSA_HEADS, SEL_TABLES + WIN_TILES, t, t), F32),
        compiler_params=_params(("arbitrary",)),
        name="relbias_near",
    )(rel_bias)
    cmp_b = pl.pallas_call(
        functools.partial(_bias_cmp_kernel, uppers=uppers),
        grid=(seq // t,),
        in_specs=[smem],
        out_specs=pl.BlockSpec((NSA_HEADS, n_cmp, t), lambda i: (0, 0, i)),
        out_shape=jax.ShapeDtypeStruct((NSA_HEADS, n_cmp, seq), F32),
        compiler_params=_params(("arbitrary",)),
        name="relbias_cmp",
    )(rel_bias)
    return near, cmp_b


def _cmp_to_sel_t(seq):
    n_cmp, n_sel = seq // CMP_STRIDE, seq // SEL_BLOCK
    c_start = np.arange(n_cmp)[:, None] * CMP_STRIDE
    s_start = np.arange(n_sel)[None, :] * SEL_BLOCK
    inside = np.clip(np.minimum(c_start + CMP_LEN, s_start + SEL_BLOCK) - np.maximum(c_start, s_start), 0, None)
    return (inside / CMP_LEN).T.astype(np.float32)


def _cmp_topk_kernel(qt_ref, kc_ref, vct_ref, bias_ref, c2s_ref, oc_ref, sel_ref):
    tb = pl.program_id(1)
    n, t = bias_ref.shape[1], bias_ref.shape[2]
    n_sel = c2s_ref.shape[0]
    tok = tb * t + lax.broadcasted_iota(jnp.int32, (1, t), 1)
    any_visible = tok >= CMP_LEN - 1
    kc = kc_ref[0, 0]
    ones = (lax.broadcasted_iota(jnp.int32, (SEL_ROWS, n), 0) == 0).astype(BF16)
    lhs = jnp.concatenate([vct_ref[0, 0], ones, c2s_ref[...]], axis=0)
    imp = jnp.zeros((n_sel, t), F32)
    weights = []
    for i in range(NSA_HPG):
        rows = slice(i * NSA_DH, (i + 1) * NSA_DH)
        s = jnp.dot(kc, qt_ref[0, 0, rows, :], preferred_element_type=F32) + bias_ref[i]
        m = jnp.max(_fold(s, jnp.max), axis=0, keepdims=True)
        weights.append(jnp.exp2(s - m).astype(BF16))
    for i in range(NSA_HPG):
        rows = slice(i * NSA_DH, (i + 1) * NSA_DH)
        r = jnp.dot(lhs, weights[i], preferred_element_type=F32)
        scale = jnp.where(any_visible, 1.0 / r[NSA_DH:NSA_DH + 1], 0.0)
        oc_ref[0, 0, rows, :] = (r[:NSA_DH] * scale).astype(BF16)
        imp = imp + r[NSA_DH + SEL_ROWS:] * scale

    blk = lax.broadcasted_iota(jnp.int32, (n_sel, t), 0)
    cur = (tb * t + lax.broadcasted_iota(jnp.int32, (n_sel, t), 1)) // SEL_BLOCK
    forced = (blk == 0) | (blk == cur) | (blk == cur - 1)
    score = jnp.where(blk > cur, -jnp.inf, jnp.where(forced, jnp.inf, imp))
    groups = n_sel // 8
    sub = lax.broadcasted_iota(jnp.int32, (8, t), 0)
    part = [score[8 * v:8 * v + 8] for v in range(groups)]
    cnt = [jnp.zeros((8, t), F32) for _ in range(groups)]
    for sp in range(n_sel):
        r = jnp.broadcast_to(score[sp:sp + 1], (8, t))
        for v in range(groups):
            if sp < 8 * v:
                beats = r >= part[v]
            elif sp >= 8 * v + 8:
                beats = r > part[v]
            else:
                beats = (r > part[v]) | ((r == part[v]) & (sub > sp - 8 * v))
            cnt[v] = cnt[v] + jnp.where(beats, 1.0, 0.0)
    per_tile = t // SEL_BLOCK
    pad = jnp.zeros((SEL_ROWS - per_tile, t), F32)
    for v in range(groups):
        mask = jnp.where(cnt[v] < min(SEL_TOPN, n_sel), 0.0, NEG)
        for r0 in range(0, 8, per_tile):
            kt = (8 * v + r0) // per_tile
            sel_ref[0, 0, 0, kt] = jnp.concatenate([mask[r0:r0 + per_tile], pad], axis=0).astype(BF16)


def _cmp_topk(q_t, k_cmp, v_cmp_t, bias_cmp, seq):
    bsz, nt, nq, t = q_t.shape
    n_cmp, n_sel = seq // CMP_STRIDE, seq // SEL_BLOCK
    grp = NSA_HPG * NSA_DH
    c2s = jnp.asarray(_cmp_to_sel_t(seq), BF16)
    return pl.pallas_call(
        _cmp_topk_kernel,
        grid=(NSA_KV, nt, bsz),
        in_specs=[pl.BlockSpec((1, 1, grp, t), lambda g, i, b: (b, i, g, 0)),
                  pl.BlockSpec((1, 1, n_cmp, NSA_DH), lambda g, i, b: (b, g, 0, 0)),
                  pl.BlockSpec((1, 1, NSA_DH, n_cmp), lambda g, i, b: (b, g, 0, 0)),
                  pl.BlockSpec((NSA_HPG, n_cmp, t), lambda g, i, b: (g, 0, i)),
                  _const_spec((n_sel, n_cmp))],
        out_specs=[pl.BlockSpec((1, 1, grp, t), lambda g, i, b: (b, i, g, 0)),
                   pl.BlockSpec((1, 1, 1, nt, SEL_ROWS, t), lambda g, i, b: (b, g, i, 0, 0, 0))],
        out_shape=[jax.ShapeDtypeStruct((bsz, nt, nq, t), BF16),
                   jax.ShapeDtypeStruct((bsz, NSA_KV, nt, nt, SEL_ROWS, t), BF16)],
        compiler_params=_params(("arbitrary", "arbitrary", "arbitrary")),
        name="nsa_cmp_topk",
    )(q_t, k_cmp, v_cmp_t, bias_cmp, c2s)


def _sel_win_kernel(qa_ref, qb_ref, ks_ref, vst_ref, kw_ref, vwt_ref, sela_ref, selb_ref, tab_ref, ga_ref, gb_ref,
                    oca_ref, ocb_ref, olo_ref, ohi_ref, s_scr):
    h = pl.program_id(0)
    i = pl.program_id(2)
    nt, t = vst_ref.shape[1], vst_ref.shape[3]
    half = nt // 2
    tb = (i, nt - 1 - i)
    q = (qa_ref[0, 0], qb_ref[0, 0])
    sel = (sela_ref, selb_ref)

    def tile_rows(kt):
        return pl.ds(pl.multiple_of(kt * t, t), t)

    def pick(on_a, xa, xb):
        if isinstance(on_a, bool):
            return xa if on_a else xb
        return jnp.where(on_a, xa, xb)

    work = []
    for p in range(nt + 1):
        on_a = True if p == 0 else (False if p >= half else p <= i)
        kt = pick(on_a, p, jnp.maximum(p - i - 1, 0))
        work.append((on_a, kt, True, pick(on_a, tb[0], tb[1]) - kt))
    for side in range(2):
        for delta in range(WIN_TILES):
            keep = True if (side == 1 and half >= WIN_TILES) else tb[side] >= delta
            work.append((side == 0, jnp.maximum(tb[side] - delta, 0), keep, delta))
    n_selpos = nt + 1

    ones = (lax.broadcasted_iota(jnp.int32, (SEL_ROWS, t), 0) == 0).astype(BF16)
    q_pad = jnp.zeros((KEY_LANES - NSA_DH - SEL_ROWS, t), BF16)
    run_max = {k: jnp.full((1, t), NEG, F32) for k in ((0, 0), (0, 1), (1, 0), (1, 1))}
    acc = {k: jnp.zeros((NSA_DH + SEL_ROWS, t), F32) for k in run_max}

    def merge(k, m_tile, part):
        m_new = jnp.maximum(run_max[k], m_tile)
        acc[k] = acc[k] * jnp.exp2(run_max[k] - m_new) + part * jnp.exp2(m_tile - m_new)
        run_max[k] = m_new

    def scores(p):
        on_a, kt, keep, delta = work[p]
        qp = pick(on_a, q[0], q[1])
        if p >= n_selpos:
            s = jnp.dot(kw_ref[0, 0, tile_rows(kt), :], qp, preferred_element_type=F32)
            s = s + tab_ref[0, SEL_TABLES + delta]
        else:
            rows = pick(on_a, sel[0][0, 0, 0, kt], sel[1][0, 0, 0, kt])
            q_aug = jnp.concatenate([qp, rows, q_pad], axis=0)
            s = jnp.dot(ks_ref[0, 0, tile_rows(kt), :], q_aug, preferred_element_type=F32)
            s = s + tab_ref[0, jnp.minimum(delta, SEL_TABLES - 1)]
        s_scr[p] = s
        m_tile = jnp.max(_fold(s, jnp.max), axis=0, keepdims=True)
        return m_tile, jnp.exp2(s_scr[p] - m_tile).astype(BF16)

    def values(p, m_tile, pr):
        on_a, kt, keep, delta = work[p]
        br = int(p >= n_selpos)
        vt_ref = vwt_ref if br else vst_ref
        part = jnp.dot(jnp.concatenate([vt_ref[0, kt], ones], axis=0), pr, preferred_element_type=F32)
        if not isinstance(keep, bool):
            m_tile = jnp.where(keep, m_tile, NEG)
        if isinstance(on_a, bool):
            merge((br, int(not on_a)), m_tile, part)
        else:
            merge((br, 0), jnp.where(on_a, m_tile, NEG), part)
            merge((br, 1), jnp.where(on_a, NEG, m_tile), part)

    pending = {}
    for step in range(len(work) + MXU_SKEW):
        if step < len(work):
            pending[step] = scores(step)
        if step >= MXU_SKEW:
            values(step - MXU_SKEW, *pending.pop(step - MXU_SKEW))

    for side, (g_ref, oc_ref, o_ref) in enumerate(((ga_ref, oca_ref, olo_ref), (gb_ref, ocb_ref, ohi_ref))):
        gate = lambda br: g_ref[0, pl.ds(br * NSA_HEADS + h, 1), :]
        o_s = acc[(0, side)][:NSA_DH] / acc[(0, side)][NSA_DH:NSA_DH + 1]
        o_w = acc[(1, side)][:NSA_DH] / acc[(1, side)][NSA_DH:NSA_DH + 1]
        o_ref[0, 0] = (gate(0) * oc_ref[0, 0].astype(F32) + gate(1) * o_s + gate(2) * o_w).astype(BF16)


def _sel_win(q_t, ks, vs_t, kw, vw_t, sel, tables, gates_t, oc_t):
    bsz, nt, nq, t = q_t.shape
    seq = nt * t
    last = nt - 1
    head_a = pl.BlockSpec((1, 1, NSA_DH, t), lambda h, b, i: (b, i, h, 0))
    head_b = pl.BlockSpec((1, 1, NSA_DH, t), lambda h, b, i: (b, last - i, h, 0))
    keys = pl.BlockSpec((1, 1, seq, NSA_DH), lambda h, b, i: (b, h // NSA_HPG, 0, 0))
    keys_sel = pl.BlockSpec((1, 1, seq, KEY_LANES), lambda h, b, i: (b, h // NSA_HPG, 0, 0))
    vals = pl.BlockSpec((1, nt, NSA_DH, t), lambda h, b, i: (b, 0, h // NSA_HPG, 0))
    sel_a = pl.BlockSpec((1, 1, 1, nt, SEL_ROWS, t), lambda h, b, i: (b, h // NSA_HPG, i, 0, 0, 0))
    sel_b = pl.BlockSpec((1, 1, 1, nt, SEL_ROWS, t), lambda h, b, i: (b, h // NSA_HPG, last - i, 0, 0, 0))
    gate_a = pl.BlockSpec((1, 3 * NSA_HEADS, t), lambda h, b, i: (b, 0, i))
    gate_b = pl.BlockSpec((1, 3 * NSA_HEADS, t), lambda h, b, i: (b, 0, last - i))
    out = jax.ShapeDtypeStruct((bsz, nt // 2, nq, t), BF16)
    return pl.pallas_call(
        _sel_win_kernel,
        grid=(NSA_HEADS, bsz, nt // 2),
        in_specs=[head_a, head_b, keys_sel, vals, keys, vals, sel_a, sel_b,
                  pl.BlockSpec((1, SEL_TABLES + WIN_TILES, t, t), lambda h, b, i: (h, 0, 0, 0)),
                  gate_a, gate_b, head_a, head_b],
        out_specs=[head_a, head_a],
        out_shape=[out, out],
        scratch_shapes=[pltpu.VMEM((nt + 1 + 2 * WIN_TILES, t, t), F32)],
        compiler_params=_params(("arbitrary", "arbitrary", "arbitrary")),
        name="nsa_sel_win",
    )(q_t, q_t, ks, vs_t, kw, vw_t, sel, sel, tables, gates_t, gates_t, oc_t, oc_t)


def kernel(x, c, mod_w, mod_b, norm_mix_g, norm_ffn_g, ab_w_in, ab_w_out, hgrn_lb_logits, hgrn_onorm_g, sconv_w, nsa_w_in, nsa_w_out, nsa_cmp_pos_k, nsa_cmp_pos_v, nsa_cmp_w1_k, nsa_cmp_w2_k, nsa_cmp_w1_v, nsa_cmp_w2_v, rel_bias, ffn_w_up, ffn_conv_w, ffn_w_down, final_norm_g):
    bsz, seq, d = x.shape
    assert d == D_MODEL and seq % TOK_TILE == 0
    lower = jnp.cumsum(jax.nn.softmax(hgrn_lb_logits.astype(F32), axis=0), axis=0)
    mod = _modulation(c, mod_w, mod_b)
    parts = lambda l: [mod[l, :, j * d:(j + 1) * d].reshape(bsz, 1, d) for j in range(6)]

    sh1, sc1, g1, sh2, sc2, g2 = parts(0)
    x = _mixer0(x, sh1, sc1, g1, norm_mix_g[0], ab_w_in[0], ab_w_out[0], lower[0], hgrn_onorm_g[0], sconv_w[0])
    x = _conv_ffn(x, sh2, sc2, g2, norm_ffn_g[0], ffn_w_up[0], ffn_conv_w[0], ffn_w_down[0], final_norm_g)

    sh1, sc1, g1, sh2, sc2, g2 = parts(1)
    kc, vc, ks, kw, q_t, gates_t, vs_t, vw_t = _nsa_proj(x, sh1, sc1, norm_mix_g[1], nsa_w_in[0])
    to_blocks = lambda a: a.reshape(bsz, NSA_KV, seq // CMP_STRIDE, CMP_STRIDE * NSA_DH)
    k_cmp, v_cmp_t = _compress(to_blocks(kc), to_blocks(vc), nsa_cmp_pos_k[0], nsa_cmp_pos_v[0],
                               nsa_cmp_w1_k[0], nsa_cmp_w2_k[0], nsa_cmp_w1_v[0], nsa_cmp_w2_v[0])
    near, bias_cmp = _bias_tables(rel_bias.astype(F32) * LOG2E, seq)
    oc_t, sel = _cmp_topk(q_t, k_cmp, v_cmp_t, bias_cmp, seq)
    o_lo, o_hi = _sel_win(q_t, ks, vs_t, kw, vw_t, sel, near, gates_t, oc_t)
    return _conv_ffn(x, sh2, sc2, g2, norm_ffn_g[1], ffn_w_up[1], ffn_conv_w[1], ffn_w_down[1], final_norm_g,
                     attn=(o_lo, o_hi, nsa_w_out[0], g1), final=True)
```

```python
import functools
import math

import jax
import jax.numpy as jnp
import numpy as np
from jax import lax
from jax.experimental import pallas as pl
from jax.experimental.pallas import tpu as pltpu

F32 = jnp.float32
BF16 = jnp.bfloat16

EPS = 1e-6
D_MODEL = 1024
D_FF = 2816

HG_HEADS = 4
HG_DK = 128
HG_WIDTH = HG_HEADS * HG_DK
SC_WIDTH = D_MODEL - HG_WIDTH
AB_IN = 4 * HG_WIDTH + 3 * SC_WIDTH
HG_CHUNK = 64
HG_LEVELS = 6

NSA_HEADS = 16
NSA_KV = 4
NSA_HPG = NSA_HEADS // NSA_KV
NSA_DH = D_MODEL // NSA_HEADS
CMP_LEN = 32
CMP_STRIDE = 16
CMP_HIDDEN = 256
SEL_BLOCK = 64
SEL_TOPN = 16
WINDOW = 512
REL_BUCKETS = 32
REL_MAX_DIST = 1024

TOK_TILE = 256
NEAR_TILES = 5
SEL_TABLES = NEAR_TILES + 1
WIN_TILES = WINDOW // TOK_TILE + 1
NEG = -1e30
LOG2E = math.log2(math.e)
KEY_LANES = 128
SEL_ROWS = 16
SW_HEADS = 2
MXU_SKEW = 5
VMEM_LIMIT = 56 * 1024 * 1024
HALO = 8

_NT = (((1,), (1,)), ((), ()))
_TN = (((0,), (0,)), ((), ()))


def _sigmoid(x):
    return 1.0 / (1.0 + jnp.exp(-x))


def _silu(x):
    return x * _sigmoid(x)


def _norm_mod(x, g, sc, sh):
    ms = jnp.mean(x * x, axis=-1, keepdims=True)
    return (x * lax.rsqrt(ms + EPS) * g) * (1.0 + sc) + sh


def _fold(x, op):
    return op(x.reshape(x.shape[0] // 8, 8, x.shape[1]), axis=0)


def _const_spec(shape):
    n = len(shape)
    return pl.BlockSpec(shape, lambda *_: (0,) * n, pipeline_mode=pl.Buffered(1))


def _params(sem):
    return pltpu.CompilerParams(dimension_semantics=sem, vmem_limit_bytes=VMEM_LIMIT)


def _mod_kernel(c_ref, w_ref, b_ref, o_ref):
    c = c_ref[...]
    o_ref[0] = jnp.dot(_silu(c), w_ref[0], preferred_element_type=F32,
                       precision=lax.Precision.HIGHEST) + b_ref[0]


def _modulation(c, mod_w, mod_b):
    depth, d, n = mod_w.shape
    bsz = c.shape[0]
    tn = 1024
    return pl.pallas_call(
        _mod_kernel,
        grid=(depth, n // tn),
        in_specs=[pl.BlockSpec((bsz, d), lambda l, j: (0, 0)),
                  pl.BlockSpec((1, d, tn), lambda l, j: (l, 0, j)),
                  pl.BlockSpec((1, 1, tn), lambda l, j: (l, 0, j))],
        out_specs=pl.BlockSpec((1, bsz, tn), lambda l, j: (l, 0, j)),
        out_shape=jax.ShapeDtypeStruct((depth, bsz, n), F32),
        compiler_params=_params(("arbitrary", "arbitrary")),
        name="adaln_mod",
    )(c, mod_w, mod_b.reshape(depth, 1, n))


def _hgrn_decay_matrix():
    t = np.arange(HG_CHUNK)
    tril = (t[None, :] <= t[:, None]).astype(np.float32)
    mats = [tril]
    for lvl in range(1, HG_LEVELS + 1):
        m = 1 << (lvl - 1)
        mid = (t // (2 * m)) * (2 * m) + m - 1
        mats.append(tril - tril[mid])
    return np.concatenate(mats, axis=0)


def _mixer0_kernel(x_ref, sh_ref, sc_ref, gt_ref, ng_ref, win_ref, wout_ref, lb_ref, og_ref, cw_ref,
                   pm_ref, o_ref, proj_ref, cat_ref, st_ref, ubuf_ref, d_scr):
    tile = x_ref.shape[1]
    c64 = HG_CHUNK

    @pl.when(pl.program_id(1) == 0)
    def _():
        st_ref[...] = jnp.zeros_like(st_ref)
        ubuf_ref[0:HALO, :] = jnp.zeros((HALO, SC_WIDTH), F32)

    x = x_ref[0]
    hm = _norm_mod(x, ng_ref[...], sc_ref[0], sh_ref[0]).astype(BF16)
    proj_ref[...] = jnp.dot(hm, win_ref[...], preferred_element_type=F32)

    row = lax.broadcasted_iota(jnp.int32, (c64, HG_DK), 0)
    r64 = lax.broadcasted_iota(jnp.int32, (c64, c64), 0)
    c64i = lax.broadcasted_iota(jnp.int32, (c64, c64), 1)
    second = [None] + [(row & (1 << (lvl - 1))) != 0 for lvl in range(1, HG_LEVELS + 1)]
    same = [None] + [(r64 >> lvl) == (c64i >> lvl) for lvl in range(1, HG_LEVELS + 1)]
    eye = r64 == c64i
    pm = pm_ref[...]
    og = og_ref[...]

    units = [(c, h) for c in range(tile // c64) for h in range(HG_HEADS)]

    def blk(c, h, j):
        return pl.ds(c * c64, c64), pl.ds(j * HG_WIDTH + h * HG_DK, HG_DK)

    for u, (c, h) in enumerate(units):
        lb = lb_ref[:, h * HG_DK:(h + 1) * HG_DK]
        fg = lb + (1.0 - lb) * _sigmoid(proj_ref[blk(c, h, 1)])
        proj_ref[blk(c, h, 1)] = 1.0 - fg
        lg = jnp.log(fg)
        lg_hi = lg.astype(BF16)
        lg_lo = (lg - lg_hi.astype(F32)).astype(BF16)
        dall = jnp.dot(pm, jnp.concatenate([lg_hi, lg_lo], axis=1), preferred_element_type=F32)
        d_scr[u] = dall[:, :HG_DK] + dall[:, HG_DK:]

    intra, q_in, k_out, decay = [], [], [], []
    for u, (c, h) in enumerate(units):
        q = proj_ref[blk(c, h, 0)]
        k = proj_ref[blk(c, h, 1)]
        a = jnp.zeros((c64, c64), F32)
        for lvl in range(1, HG_LEVELS + 1):
            e = jnp.exp(-jnp.abs(d_scr[u, lvl * c64:(lvl + 1) * c64, :]))
            qt = jnp.where(second[lvl], q * e, 0.0).astype(BF16)
            kt = jnp.where(second[lvl], 0.0, k * e).astype(BF16)
            al = lax.dot_general(qt, kt, _NT, preferred_element_type=F32)
            a = a + (al if lvl == HG_LEVELS else jnp.where(same[lvl], al, 0.0))
        intra.append(jnp.where(eye, jnp.sum(q * k, axis=-1, keepdims=True), a).astype(BF16))
        b = d_scr[u, 0:c64, :]
        b_last = d_scr[u, c64 - 1:c64, :]
        q_in.append((q * jnp.exp(b)).astype(BF16))
        k_out.append((k * jnp.exp(b_last - b)).astype(BF16))
        decay.append(jnp.exp(b_last))

    state = [st_ref[h] for h in range(HG_HEADS)]
    for u, (c, h) in enumerate(units):
        vb = proj_ref[blk(c, h, 2)].astype(BF16)
        o = jnp.dot(intra[u], vb, preferred_element_type=F32)
        o = o + lax.dot_general(q_in[u], state[h].astype(BF16), _NT, preferred_element_type=F32)
        state[h] = state[h] * decay[u] + lax.dot_general(vb, k_out[u], _TN, preferred_element_type=F32)
        on = o * lax.rsqrt(jnp.mean(o * o, axis=-1, keepdims=True) + EPS) * og
        rows, _ = blk(c, h, 0)
        cat_ref[rows, h * HG_DK:(h + 1) * HG_DK] = (on * _silu(proj_ref[blk(c, h, 3)])).astype(BF16)
    for h in range(HG_HEADS):
        st_ref[h] = state[h]

    off = 4 * HG_WIDTH
    u = proj_ref[:, off + SC_WIDTH:off + 2 * SC_WIDTH] * proj_ref[:, off + 2 * SC_WIDTH:off + 3 * SC_WIDTH]
    ubuf_ref[HALO:HALO + tile, :] = u
    cw = cw_ref[...]
    conv = (ubuf_ref[HALO - 2:HALO - 2 + tile, :] * cw[0:1] + ubuf_ref[HALO - 1:HALO - 1 + tile, :] * cw[1:2]
            + u * cw[2:3])
    ubuf_ref[0:HALO, :] = u[tile - HALO:tile, :]
    cat_ref[:, HG_WIDTH:] = (proj_ref[:, off:off + SC_WIDTH] * conv).astype(BF16)

    y = jnp.dot(cat_ref[...], wout_ref[...], preferred_element_type=F32)
    o_ref[0] = x + gt_ref[0] * y


def _mixer0(x, sh, sc, gt, norm_g, w_in, w_out, lower, onorm_g, sconv_w):
    bsz, seq, d = x.shape
    t = TOK_TILE
    pm = jnp.asarray(_hgrn_decay_matrix(), BF16)
    vec = pl.BlockSpec((1, 1, d), lambda b, i: (b, 0, 0))
    return pl.pallas_call(
        _mixer0_kernel,
        grid=(bsz, seq // t),
        in_specs=[pl.BlockSpec((1, t, d), lambda b, i: (b, i, 0)), vec, vec, vec,
                  _const_spec((1, d)), _const_spec((d, AB_IN)), _const_spec((d, d)),
                  _const_spec((1, HG_WIDTH)), _const_spec((1, HG_DK)), _const_spec((3, SC_WIDTH)),
                  _const_spec(pm.shape)],
        out_specs=pl.BlockSpec((1, t, d), lambda b, i: (b, i, 0)),
        out_shape=jax.ShapeDtypeStruct(x.shape, F32),
        scratch_shapes=[pltpu.VMEM((t, AB_IN), F32), pltpu.VMEM((t, d), BF16),
                        pltpu.VMEM((HG_HEADS, HG_DK, HG_DK), F32), pltpu.VMEM((t + HALO, SC_WIDTH), F32),
                        pltpu.VMEM((t // HG_CHUNK * HG_HEADS, pm.shape[0], HG_DK), F32)],
        compiler_params=_params(("arbitrary", "arbitrary")),
        name="mixer_hgrn_sconv",
    )(x, sh, sc, gt, norm_g.reshape(1, d), w_in.astype(BF16), w_out.astype(BF16),
      lower.reshape(1, HG_WIDTH), onorm_g.reshape(1, HG_DK), sconv_w, pm)


FF_CHUNK = 1408


def _ffn_kernel(*refs, has_attn, final):
    if has_attn:
        (x_ref, olo_ref, ohi_ref, wo_ref, g1_ref, sh_ref, sc_ref, g2_ref, ng_ref, wup_ref, cw_ref, wdn_ref, fg_ref,
         o_ref, gbuf_ref, carry_ref) = refs
    else:
        (x_ref, sh_ref, sc_ref, g2_ref, ng_ref, wup_ref, cw_ref, wdn_ref, fg_ref,
         o_ref, gbuf_ref, carry_ref) = refs
    tile = x_ref.shape[1]

    @pl.when(pl.program_id(1) == 0)
    def _():
        carry_ref[...] = jnp.zeros_like(carry_ref)

    x = x_ref[0]
    if has_attn:
        o_t = jnp.where(pl.program_id(1) < pl.num_programs(1) // 2, olo_ref[0, 0], ohi_ref[0, 0])
        x = x + g1_ref[0] * lax.dot_general(o_t, wo_ref[...], _TN, preferred_element_type=F32)
    hf = _norm_mod(x, ng_ref[...], sc_ref[0], sh_ref[0]).astype(BF16)
    gbuf_ref[0:HALO, :] = carry_ref[...]
    gates, ups = [], []
    for c0 in range(0, D_FF, FF_CHUNK):
        gate = jnp.dot(hf, wup_ref[:, c0:c0 + FF_CHUNK], preferred_element_type=F32)
        ups.append(jnp.dot(hf, wup_ref[:, D_FF + c0:D_FF + c0 + FF_CHUNK], preferred_element_type=F32))
        gbuf_ref[HALO:HALO + tile, c0:c0 + FF_CHUNK] = gate
        carry_ref[:, c0:c0 + FF_CHUNK] = gate[tile - HALO:tile, :]
        gates.append(gate)
    acc = jnp.zeros((tile, D_MODEL), F32)
    for j, c0 in enumerate(range(0, D_FF, FF_CHUNK)):
        cols = slice(c0, c0 + FF_CHUNK)
        cw = cw_ref[:, cols]
        conv = (gbuf_ref[HALO - 2:HALO - 2 + tile, cols] * cw[0:1] + gbuf_ref[HALO - 1:HALO - 1 + tile, cols] * cw[1:2]
                + gates[j] * cw[2:3])
        act = (_silu(conv) * ups[j]).astype(BF16)
        acc = acc + jnp.dot(act, wdn_ref[cols, :], preferred_element_type=F32)
    out = x + g2_ref[0] * acc
    if final:
        out = out * lax.rsqrt(jnp.mean(out * out, axis=-1, keepdims=True) + EPS) * fg_ref[...]
    o_ref[0] = out


def _conv_ffn(x, sh, sc, g2, norm_g, w_up, conv_w, w_down, final_g, attn=None, final=False):
    bsz, seq, d = x.shape
    t = TOK_TILE
    vec = pl.BlockSpec((1, 1, d), lambda b, i: (b, 0, 0))
    args, specs = [x], [pl.BlockSpec((1, t, d), lambda b, i: (b, i, 0))]
    if attn is not None:
        o_lo, o_hi, w_o, g1 = attn
        top = o_lo.shape[1] - 1
        last = seq // t - 1
        args += [o_lo, o_hi, w_o.astype(BF16), g1]
        specs += [pl.BlockSpec((1, 1, d, t), lambda b, i: (b, jnp.minimum(i, top), 0, 0)),
                  pl.BlockSpec((1, 1, d, t), lambda b, i: (b, jnp.minimum(last - i, top), 0, 0)),
                  _const_spec((d, d)), vec]
    args += [sh, sc, g2, norm_g.reshape(1, d), w_up.astype(BF16), conv_w, w_down.astype(BF16),
             final_g.reshape(1, d)]
    specs += [vec, vec, vec, _const_spec((1, d)), _const_spec((d, 2 * D_FF)), _const_spec((3, D_FF)),
              _const_spec((D_FF, d)), _const_spec((1, d))]
    return pl.pallas_call(
        functools.partial(_ffn_kernel, has_attn=attn is not None, final=final),
        grid=(bsz, seq // t),
        in_specs=specs,
        out_specs=pl.BlockSpec((1, t, d), lambda b, i: (b, i, 0)),
        out_shape=jax.ShapeDtypeStruct(x.shape, F32),
        scratch_shapes=[pltpu.VMEM((t + HALO, D_FF), F32), pltpu.VMEM((HALO, D_FF), F32)],
        compiler_params=_params(("arbitrary", "arbitrary")),
        name="conv_ffn_attnproj" if attn is not None else "conv_ffn",
    )(*args)


KV_ROWS = NSA_KV * NSA_DH


def _nsa_proj_kernel(x_ref, sh_ref, sc_ref, ng_ref, wa_ref, wq_ref, wg_ref, wv_ref, hot_ref,
                     kc_ref, vc_ref, ks_ref, kw_ref, qt_ref, gt_ref, vst_ref, vwt_ref):
    hm = _norm_mod(x_ref[0], ng_ref[...], sc_ref[0], sh_ref[0]).astype(BF16)
    kv = jnp.dot(hm, wa_ref[...], preferred_element_type=F32).astype(BF16)
    for g in range(NSA_KV):
        part = lambda j: kv[:, j * KV_ROWS + g * NSA_DH:j * KV_ROWS + (g + 1) * NSA_DH]
        kc_ref[0, g] = part(0)
        vc_ref[0, g] = part(1)
        ks_ref[0, g] = jnp.concatenate([part(2), hot_ref[...]], axis=1)
        kw_ref[0, g] = part(3)
    qt = lax.dot_general(wq_ref[...], hm, _NT, preferred_element_type=F32)
    qt_ref[0, 0] = (qt * (NSA_DH ** -0.5 * LOG2E)).astype(BF16)
    gt_ref[0] = _sigmoid(lax.dot_general(wg_ref[...], hm, _NT, preferred_element_type=F32))
    vt = lax.dot_general(wv_ref[...], hm, _NT, preferred_element_type=F32).astype(BF16)
    vst_ref[0, 0] = vt[:KV_ROWS]
    vwt_ref[0, 0] = vt[KV_ROWS:]


def _nsa_proj(x, sh, sc, norm_g, w_in):
    bsz, seq, d = x.shape
    t = TOK_TILE
    nt = seq // t
    nq, ng = NSA_HEADS * NSA_DH, 3 * NSA_HEADS
    o = nq + ng
    w = w_in.astype(BF16)
    col = lambda j: w[:, o + j * KV_ROWS:o + (j + 1) * KV_ROWS]
    w_a = jnp.concatenate([col(0), col(1), col(2), col(4)], axis=1)
    w_q = w[:, :nq].T
    w_g = w[:, nq:o].T
    w_v = jnp.concatenate([col(3), col(5)], axis=1).T
    hot = (np.arange(t)[:, None] // SEL_BLOCK == np.arange(KEY_LANES - NSA_DH)[None, :]).astype(np.float32)
    vec = pl.BlockSpec((1, 1, d), lambda b, i: (b, 0, 0))
    tiled = lambda r: pl.BlockSpec((1, 1, r, t), lambda b, i: (b, i, 0, 0))
    grouped = lambda n: pl.BlockSpec((1, NSA_KV, t, n), lambda b, i: (b, 0, i, 0))
    rows = lambda n: jax.ShapeDtypeStruct((bsz, NSA_KV, seq, n), BF16)
    return pl.pallas_call(
        _nsa_proj_kernel,
        grid=(bsz, nt),
        in_specs=[pl.BlockSpec((1, t, d), lambda b, i: (b, i, 0)), vec, vec, _const_spec((1, d)),
                  _const_spec((d, 4 * KV_ROWS)), _const_spec((nq, d)), _const_spec((ng, d)),
                  _const_spec((2 * KV_ROWS, d)), _const_spec(hot.shape)],
        out_specs=[grouped(NSA_DH), grouped(NSA_DH), grouped(KEY_LANES), grouped(NSA_DH), tiled(nq),
                   pl.BlockSpec((1, ng, t), lambda b, i: (b, 0, i)), tiled(KV_ROWS), tiled(KV_ROWS)],
        out_shape=[rows(NSA_DH), rows(NSA_DH), rows(KEY_LANES), rows(NSA_DH),
                   jax.ShapeDtypeStruct((bsz, nt, nq, t), BF16),
                   jax.ShapeDtypeStruct((bsz, ng, seq), F32),
                   jax.ShapeDtypeStruct((bsz, nt, KV_ROWS, t), BF16),
                   jax.ShapeDtypeStruct((bsz, nt, KV_ROWS, t), BF16)],
        compiler_params=_params(("arbitrary", "arbitrary")),
        name="nsa_in_proj",
    )(x, sh, sc, norm_g.reshape(1, d), w_a, w_q, w_g, w_v, jnp.asarray(hot, BF16))


def _compress_kernel(hk_ref, hv_ref, pk_ref, pv_ref, w1k_ref, w2k_ref, w1v_ref, w2vt_ref, kc_ref, vct_ref):
    half = CMP_STRIDE * NSA_DH

    def hidden(h_ref, pos_ref, w1_ref):
        h = h_ref[0, 0]
        n = h.shape[0]
        top = jnp.dot(h, w1_ref[:half, :], preferred_element_type=F32)
        bot = jnp.dot(h, w1_ref[half:, :], preferred_element_type=F32)
        pos = jnp.dot(jnp.broadcast_to(pos_ref[...], (8, 2 * half)).astype(BF16), w1_ref[...],
                      preferred_element_type=F32)[0:1]
        return _silu(top + pltpu.roll(bot, n - 1, 0) + pos).astype(BF16)

    kc_ref[0, 0] = jnp.dot(hidden(hk_ref, pk_ref, w1k_ref), w2k_ref[...],
                           preferred_element_type=F32).astype(BF16)
    vct_ref[0, 0] = lax.dot_general(w2vt_ref[...], hidden(hv_ref, pv_ref, w1v_ref), _NT,
                                    preferred_element_type=F32).astype(BF16)


def _compress(hk, hv, pos_k, pos_v, w1_k, w2_k, w1_v, w2_v):
    bsz, g, n, width = hk.shape
    blk = pl.BlockSpec((1, 1, n, width), lambda b, j: (b, j, 0, 0))
    return pl.pallas_call(
        _compress_kernel,
        grid=(bsz, g),
        in_specs=[blk, blk, _const_spec((1, width * 2)), _const_spec((1, width * 2)),
                  _const_spec((2 * width, CMP_HIDDEN)), _const_spec((CMP_HIDDEN, NSA_DH)),
                  _const_spec((2 * width, CMP_HIDDEN)), _const_spec((NSA_DH, CMP_HIDDEN))],
        out_specs=[pl.BlockSpec((1, 1, n, NSA_DH), lambda b, j: (b, j, 0, 0)),
                   pl.BlockSpec((1, 1, NSA_DH, n), lambda b, j: (b, j, 0, 0))],
        out_shape=[jax.ShapeDtypeStruct((bsz, g, n, NSA_DH), BF16),
                   jax.ShapeDtypeStruct((bsz, g, NSA_DH, n), BF16)],
        compiler_params=_params(("arbitrary", "arbitrary")),
        name="nsa_compress",
    )(hk, hv, pos_k.reshape(1, -1), pos_v.reshape(1, -1), w1_k.astype(BF16), w2_k.astype(BF16),
      w1_v.astype(BF16), w2_v.T.astype(BF16))


def _bucket_upper_bounds():
    n = np.arange(4 * REL_MAX_DIST, dtype=np.float64)
    exact = REL_BUCKETS // 2
    large = exact + (np.log(np.maximum(n, exact) / exact) / math.log(REL_MAX_DIST / exact)
                     * (REL_BUCKETS - exact)).astype(np.int64)
    bucket = np.where(n < exact, n.astype(np.int64), np.minimum(large, REL_BUCKETS - 1))
    return [int(np.max(np.nonzero(bucket <= j)[0])) for j in range(REL_BUCKETS - 1)]


def _bias_of_dist(dist, rb_ref, heads, uppers):
    vals = [jnp.full(dist.shape, rb_ref[REL_BUCKETS - 1, h], F32) for h in heads]
    for j in range(REL_BUCKETS - 2, -1, -1):
        m = dist <= uppers[j]
        vals = [jnp.where(m, rb_ref[j, h], v) for h, v in zip(heads, vals)]
    return vals


def _bias_strip(dist_min, dist_max, dist, rb_ref, heads, uppers, store):
    is_const = (dist_min > uppers[-1]) | (dist_max < 0)

    @pl.when(is_const)
    def _():
        store([jnp.full(dist.shape, rb_ref[REL_BUCKETS - 1, h], F32) for h in heads])

    @pl.when(jnp.logical_not(is_const))
    def _():
        store(_bias_of_dist(dist, rb_ref, heads, uppers))


def _bias_near_kernel(rb_ref, o_ref, *, uppers):
    d = pl.program_id(0)
    t = o_ref.shape[2]
    heads = list(range(NSA_HEADS))
    is_win = d >= SEL_TABLES
    delta = jnp.where(is_win, d - SEL_TABLES, d)
    limit = jnp.where(is_win, WINDOW, (SEL_TABLES + 1) * t)

    def strip(i, carry):
        r0 = pl.multiple_of(i * 8, 8)
        key = r0 + lax.broadcasted_iota(jnp.int32, (8, t), 0)
        tok = lax.broadcasted_iota(jnp.int32, (8, t), 1)
        dist = delta * t + tok - key
        valid = (dist >= 0) & (dist < limit)

        def store(vals):
            for h in heads:
                o_ref[h, 0, pl.ds(r0, 8), :] = jnp.where(valid, vals[h], NEG)

        _bias_strip(delta * t - (r0 + 7), delta * t + (t - 1) - r0, dist, rb_ref, heads, uppers, store)
        return carry

    lax.fori_loop(0, t // 8, strip, 0)


def _bias_cmp_kernel(rb_ref, o_ref, *, uppers):
    tb = pl.program_id(0)
    n, t = o_ref.shape[1], o_ref.shape[2]
    heads = list(range(NSA_HEADS))

    def strip(i, carry):
        r0 = pl.multiple_of(i * 8, 8)
        blk = r0 + lax.broadcasted_iota(jnp.int32, (8, t), 0)
        tok = tb * t + lax.broadcasted_iota(jnp.int32, (8, t), 1)
        dist = tok - (blk * CMP_STRIDE + CMP_LEN - 1)

        def store(vals):
            for h in heads:
                o_ref[h, pl.ds(r0, 8), :] = jnp.where(dist >= 0, vals[h], NEG)

        last = CMP_LEN - 1
        _bias_strip(tb * t - ((r0 + 7) * CMP_STRIDE + last), tb * t + (t - 1) - (r0 * CMP_STRIDE + last), dist,
                    rb_ref, heads, uppers, store)
        return carry

    lax.fori_loop(0, n // 8, strip, 0)


def _bias_tables(rel_bias, seq):
    t = TOK_TILE
    n_cmp = seq // CMP_STRIDE
    uppers = _bucket_upper_bounds()
    smem = pl.BlockSpec(memory_space=pltpu.SMEM)
    near = pl.pallas_call(
        functools.partial(_bias_near_kernel, uppers=uppers),
        grid=(SEL_TABLES + WIN_TILES,),
        in_specs=[smem],
        out_specs=pl.BlockSpec((NSA_HEADS, 1, t, t), lambda d: (0, d, 0, 0)),
        out_shape=jax.ShapeDtypeStruct((NSA_HEADS, SEL_TABLES + WIN_TILES, t, t), F32),
        compiler_params=_params(("arbitrary",)),
        name="relbias_near",
    )(rel_bias)
    cmp_b = pl.pallas_call(
        functools.partial(_bias_cmp_kernel, uppers=uppers),
        grid=(seq // t,),
        in_specs=[smem],
        out_specs=pl.BlockSpec((NSA_HEADS, n_cmp, t), lambda i: (0, 0, i)),
        out_shape=jax.ShapeDtypeStruct((NSA_HEADS, n_cmp, seq), F32),
        compiler_params=_params(("arbitrary",)),
        name="relbias_cmp",
    )(rel_bias)
    return near, cmp_b


def _cmp_to_sel_t(seq):
    n_cmp, n_sel = seq // CMP_STRIDE, seq // SEL_BLOCK
    c_start = np.arange(n_cmp)[:, None] * CMP_STRIDE
    s_start = np.arange(n_sel)[None, :] * SEL_BLOCK
    inside = np.clip(np.minimum(c_start + CMP_LEN, s_start + SEL_BLOCK) - np.maximum(c_start, s_start), 0, None)
    return (inside / CMP_LEN).T.astype(np.float32)


def _cmp_topk_kernel(qt_ref, kc_ref, vct_ref, bias_ref, c2s_ref, oc_ref, sel_ref, score_scr, cnt_scr):
    tb = pl.program_id(1)
    n, t = bias_ref.shape[1], bias_ref.shape[2]
    n_sel = c2s_ref.shape[0]
    tok = tb * t + lax.broadcasted_iota(jnp.int32, (1, t), 1)
    any_visible = tok >= CMP_LEN - 1
    kc = kc_ref[0, 0]
    ones = (lax.broadcasted_iota(jnp.int32, (SEL_ROWS, n), 0) == 0).astype(BF16)
    lhs = jnp.concatenate([vct_ref[0, 0], ones, c2s_ref[...]], axis=0)
    imp = jnp.zeros((n_sel, t), F32)
    weights = []
    for i in range(NSA_HPG):
        rows = slice(i * NSA_DH, (i + 1) * NSA_DH)
        s = jnp.dot(kc, qt_ref[0, 0, rows, :], preferred_element_type=F32) + bias_ref[i]
        m = jnp.max(_fold(s, jnp.max), axis=0, keepdims=True)
        weights.append(jnp.exp2(s - m).astype(BF16))
    for i in range(NSA_HPG):
        rows = slice(i * NSA_DH, (i + 1) * NSA_DH)
        r = jnp.dot(lhs, weights[i], preferred_element_type=F32)
        scale = jnp.where(any_visible, 1.0 / r[NSA_DH:NSA_DH + 1], 0.0)
        oc_ref[0, 0, rows, :] = (r[:NSA_DH] * scale).astype(BF16)
        imp = imp + r[NSA_DH + SEL_ROWS:] * scale

    blk = lax.broadcasted_iota(jnp.int32, (n_sel, t), 0)
    cur = (tb * t + lax.broadcasted_iota(jnp.int32, (n_sel, t), 1)) // SEL_BLOCK
    forced = (blk == 0) | (blk == cur) | (blk == cur - 1)
    score = jnp.where(blk > cur, -jnp.inf, jnp.where(forced, jnp.inf, imp))
    groups = n_sel // 8
    per_tile = t // SEL_BLOCK
    sub = lax.broadcasted_iota(jnp.int32, (8, t), 0)
    score_scr[...] = score
    cnt_scr[...] = jnp.zeros_like(cnt_scr)
    for j in range(groups):
        @pl.when(8 * j < (tb + 1) * per_tile)
        def _():
            src = [jnp.broadcast_to(score_scr[sp:sp + 1, :], (8, t)) for sp in range(8 * j, 8 * j + 8)]
            for v in range(groups):
                part = score_scr[8 * v:8 * v + 8, :]
                cnt = cnt_scr[8 * v:8 * v + 8, :]
                for sp, r in zip(range(8 * j, 8 * j + 8), src):
                    if j < v:
                        beats = r >= part
                    elif j > v:
                        beats = r > part
                    else:
                        beats = (r > part) | ((r == part) & (sub > sp - 8 * v))
                    cnt = cnt + jnp.where(beats, 1.0, 0.0)
                cnt_scr[8 * v:8 * v + 8, :] = cnt
    pad = jnp.zeros((SEL_ROWS - per_tile, t), F32)
    for v in range(groups):
        mask = jnp.where(cnt_scr[8 * v:8 * v + 8, :] < min(SEL_TOPN, n_sel), 0.0, NEG)
        for r0 in range(0, 8, per_tile):
            kt = (8 * v + r0) // per_tile
            sel_ref[0, 0, 0, kt] = jnp.concatenate([mask[r0:r0 + per_tile], pad], axis=0).astype(BF16)


def _cmp_topk(q_t, k_cmp, v_cmp_t, bias_cmp, seq):
    bsz, nt, nq, t = q_t.shape
    n_cmp, n_sel = seq // CMP_STRIDE, seq // SEL_BLOCK
    grp = NSA_HPG * NSA_DH
    c2s = jnp.asarray(_cmp_to_sel_t(seq), BF16)
    return pl.pallas_call(
        _cmp_topk_kernel,
        grid=(NSA_KV, nt, bsz),
        in_specs=[pl.BlockSpec((1, 1, grp, t), lambda g, i, b: (b, i, g, 0)),
                  pl.BlockSpec((1, 1, n_cmp, NSA_DH), lambda g, i, b: (b, g, 0, 0)),
                  pl.BlockSpec((1, 1, NSA_DH, n_cmp), lambda g, i, b: (b, g, 0, 0)),
                  pl.BlockSpec((NSA_HPG, n_cmp, t), lambda g, i, b: (g, 0, i)),
                  _const_spec((n_sel, n_cmp))],
        out_specs=[pl.BlockSpec((1, 1, grp, t), lambda g, i, b: (b, i, g, 0)),
                   pl.BlockSpec((1, 1, 1, nt, SEL_ROWS, t), lambda g, i, b: (b, g, i, 0, 0, 0))],
        out_shape=[jax.ShapeDtypeStruct((bsz, nt, nq, t), BF16),
                   jax.ShapeDtypeStruct((bsz, NSA_KV, nt, nt, SEL_ROWS, t), BF16)],
        scratch_shapes=[pltpu.VMEM((n_sel, t), F32), pltpu.VMEM((n_sel, t), F32)],
        compiler_params=_params(("arbitrary", "arbitrary", "arbitrary")),
        name="nsa_cmp_topk",
    )(q_t, k_cmp, v_cmp_t, bias_cmp, c2s)


def _sel_win_kernel(qa_ref, qb_ref, ks_ref, vst_ref, kw_ref, vwt_ref, sela_ref, selb_ref, tab_ref, ga_ref, gb_ref,
                    oca_ref, ocb_ref, olo_ref, ohi_ref):
    h0 = pl.program_id(0) * SW_HEADS
    i = pl.program_id(2)
    nt, t = vst_ref.shape[1], vst_ref.shape[3]
    half = nt // 2
    tb = (i, nt - 1 - i)
    head_rows = lambda hh: slice(hh * NSA_DH, (hh + 1) * NSA_DH)
    q = [(qa_ref[0, 0, head_rows(hh), :], qb_ref[0, 0, head_rows(hh), :]) for hh in range(SW_HEADS)]
    sel = (sela_ref, selb_ref)

    def tile_rows(kt):
        return pl.ds(pl.multiple_of(kt * t, t), t)

    def pick(on_a, xa, xb):
        if isinstance(on_a, bool):
            return xa if on_a else xb
        return jnp.where(on_a, xa, xb)

    work = []
    for p in range(nt + 1):
        on_a = True if p == 0 else (False if p >= half else p <= i)
        kt = pick(on_a, p, jnp.maximum(p - i - 1, 0))
        work.append((on_a, kt, True, pick(on_a, tb[0], tb[1]) - kt))
    for side in range(2):
        for delta in range(WIN_TILES):
            keep = True if (side == 1 and half >= WIN_TILES) else tb[side] >= delta
            work.append((side == 0, jnp.maximum(tb[side] - delta, 0), keep, delta))
    n_selpos = nt + 1

    ones = (lax.broadcasted_iota(jnp.int32, (SEL_ROWS, t), 0) == 0).astype(BF16)
    q_pad = jnp.zeros((KEY_LANES - NSA_DH - SEL_ROWS, t), BF16)
    keys3 = [(br, side, hh) for br in range(2) for side in range(2) for hh in range(SW_HEADS)]
    run_max = {k: jnp.full((1, t), NEG, F32) for k in keys3}
    acc = {k: jnp.zeros((NSA_DH + SEL_ROWS, t), F32) for k in keys3}

    def merge(k, m_tile, part):
        m_new = jnp.maximum(run_max[k], m_tile)
        acc[k] = acc[k] * jnp.exp2(run_max[k] - m_new) + part * jnp.exp2(m_tile - m_new)
        run_max[k] = m_new

    def scores(p, hh):
        on_a, kt, keep, delta = work[p]
        qp = pick(on_a, q[hh][0], q[hh][1])
        if p >= n_selpos:
            s = jnp.dot(kw_ref[0, 0, tile_rows(kt), :], qp, preferred_element_type=F32)
            s = s + tab_ref[hh, SEL_TABLES + delta]
        else:
            rows = pick(on_a, sel[0][0, 0, 0, kt], sel[1][0, 0, 0, kt])
            q_aug = jnp.concatenate([qp, rows, q_pad], axis=0)
            s = jnp.dot(ks_ref[0, 0, tile_rows(kt), :], q_aug, preferred_element_type=F32)
            s = s + tab_ref[hh, jnp.minimum(delta, SEL_TABLES - 1)]
        m_tile = jnp.max(_fold(s, jnp.max), axis=0, keepdims=True)
        return m_tile, jnp.exp2(s - m_tile).astype(BF16)

    def values(p, hh, m_tile, pr):
        on_a, kt, keep, delta = work[p]
        br = int(p >= n_selpos)
        vt_ref = vwt_ref if br else vst_ref
        part = jnp.dot(jnp.concatenate([vt_ref[0, kt], ones], axis=0), pr, preferred_element_type=F32)
        if not isinstance(keep, bool):
            m_tile = jnp.where(keep, m_tile, NEG)
        if isinstance(on_a, bool):
            merge((br, int(not on_a), hh), m_tile, part)
        else:
            merge((br, 0, hh), jnp.where(on_a, m_tile, NEG), part)
            merge((br, 1, hh), jnp.where(on_a, NEG, m_tile), part)

    items = [(p, hh) for p in range(len(work)) for hh in range(SW_HEADS)]
    pending = {}
    for step in range(len(items) + MXU_SKEW):
        if step < len(items):
            pending[step] = scores(*items[step])
        if step >= MXU_SKEW:
            values(*items[step - MXU_SKEW], *pending.pop(step - MXU_SKEW))

    for side, (g_ref, oc_ref, o_ref) in enumerate(((ga_ref, oca_ref, olo_ref), (gb_ref, ocb_ref, ohi_ref))):
        for hh in range(SW_HEADS):
            gate = lambda br: g_ref[0, pl.ds(br * NSA_HEADS + h0 + hh, 1), :]
            o_s = acc[(0, side, hh)][:NSA_DH] / acc[(0, side, hh)][NSA_DH:NSA_DH + 1]
            o_w = acc[(1, side, hh)][:NSA_DH] / acc[(1, side, hh)][NSA_DH:NSA_DH + 1]
            o_c = oc_ref[0, 0, head_rows(hh), :].astype(F32)
            o_ref[0, 0, head_rows(hh), :] = (gate(0) * o_c + gate(1) * o_s + gate(2) * o_w).astype(BF16)


def _sel_win(q_t, ks, vs_t, kw, vw_t, sel, tables, gates_t, oc_t):
    bsz, nt, nq, t = q_t.shape
    seq = nt * t
    last = nt - 1
    per_group = NSA_HPG // SW_HEADS
    head_a = pl.BlockSpec((1, 1, SW_HEADS * NSA_DH, t), lambda h, b, i: (b, i, h, 0))
    head_b = pl.BlockSpec((1, 1, SW_HEADS * NSA_DH, t), lambda h, b, i: (b, last - i, h, 0))
    keys = pl.BlockSpec((1, 1, seq, NSA_DH), lambda h, b, i: (b, h // per_group, 0, 0))
    keys_sel = pl.BlockSpec((1, 1, seq, KEY_LANES), lambda h, b, i: (b, h // per_group, 0, 0))
    vals = pl.BlockSpec((1, nt, NSA_DH, t), lambda h, b, i: (b, 0, h // per_group, 0))
    sel_a = pl.BlockSpec((1, 1, 1, nt, SEL_ROWS, t), lambda h, b, i: (b, h // per_group, i, 0, 0, 0))
    sel_b = pl.BlockSpec((1, 1, 1, nt, SEL_ROWS, t), lambda h, b, i: (b, h // per_group, last - i, 0, 0, 0))
    gate_a = pl.BlockSpec((1, 3 * NSA_HEADS, t), lambda h, b, i: (b, 0, i))
    gate_b = pl.BlockSpec((1, 3 * NSA_HEADS, t), lambda h, b, i: (b, 0, last - i))
    out = jax.ShapeDtypeStruct((bsz, nt // 2, nq, t), BF16)
    return pl.pallas_call(
        _sel_win_kernel,
        grid=(NSA_HEADS // SW_HEADS, bsz, nt // 2),
        in_specs=[head_a, head_b, keys_sel, vals, keys, vals, sel_a, sel_b,
                  pl.BlockSpec((SW_HEADS, SEL_TABLES + WIN_TILES, t, t), lambda h, b, i: (h, 0, 0, 0)),
                  gate_a, gate_b, head_a, head_b],
        out_specs=[head_a, head_a],
        out_shape=[out, out],
        compiler_params=_params(("arbitrary", "arbitrary", "arbitrary")),
        name="nsa_sel_win",
    )(q_t, q_t, ks, vs_t, kw, vw_t, sel, sel, tables, gates_t, gates_t, oc_t, oc_t)


def kernel(x, c, mod_w, mod_b, norm_mix_g, norm_ffn_g, ab_w_in, ab_w_out, hgrn_lb_logits, hgrn_onorm_g, sconv_w, nsa_w_in, nsa_w_out, nsa_cmp_pos_k, nsa_cmp_pos_v, nsa_cmp_w1_k, nsa_cmp_w2_k, nsa_cmp_w1_v, nsa_cmp_w2_v, rel_bias, ffn_w_up, ffn_conv_w, ffn_w_down, final_norm_g):
    bsz, seq, d = x.shape
    assert d == D_MODEL and seq % TOK_TILE == 0
    lower = jnp.cumsum(jax.nn.softmax(hgrn_lb_logits.astype(F32), axis=0), axis=0)
    mod = _modulation(c, mod_w, mod_b)
    parts = lambda l: [mod[l, :, j * d:(j + 1) * d].reshape(bsz, 1, d) for j in range(6)]

    sh1, sc1, g1, sh2, sc2, g2 = parts(0)
    x = _mixer0(x, sh1, sc1, g1, norm_mix_g[0], ab_w_in[0], ab_w_out[0], lower[0], hgrn_onorm_g[0], sconv_w[0])
    x = _conv_ffn(x, sh2, sc2, g2, norm_ffn_g[0], ffn_w_up[0], ffn_conv_w[0], ffn_w_down[0], final_norm_g)

    sh1, sc1, g1, sh2, sc2, g2 = parts(1)
    kc, vc, ks, kw, q_t, gates_t, vs_t, vw_t = _nsa_proj(x, sh1, sc1, norm_mix_g[1], nsa_w_in[0])
    to_blocks = lambda a: a.reshape(bsz, NSA_KV, seq // CMP_STRIDE, CMP_STRIDE * NSA_DH)
    k_cmp, v_cmp_t = _compress(to_blocks(kc), to_blocks(vc), nsa_cmp_pos_k[0], nsa_cmp_pos_v[0],
                               nsa_cmp_w1_k[0], nsa_cmp_w2_k[0], nsa_cmp_w1_v[0], nsa_cmp_w2_v[0])
    near, bias_cmp = _bias_tables(rel_bias.astype(F32) * LOG2E, seq)
    oc_t, sel = _cmp_topk(q_t, k_cmp, v_cmp_t, bias_cmp, seq)
    o_lo, o_hi = _sel_win(q_t, ks, vs_t, kw, vw_t, sel, near, gates_t, oc_t)
    return _conv_ffn(x, sh2, sc2, g2, norm_ffn_g[1], ffn_w_up[1], ffn_conv_w[1], ffn_w_down[1], final_norm_g,
                     attn=(o_lo, o_hi, nsa_w_out[0], g1), final=True)
```

```python
import functools
import math

import jax
import jax.numpy as jnp
import numpy as np
from jax import lax
from jax.experimental import pallas as pl
from jax.experimental.pallas import tpu as pltpu

F32 = jnp.float32
BF16 = jnp.bfloat16

EPS = 1e-6
D_MODEL = 1024
D_FF = 2816

HG_HEADS = 4
HG_DK = 128
HG_WIDTH = HG_HEADS * HG_DK
SC_WIDTH = D_MODEL - HG_WIDTH
AB_IN = 4 * HG_WIDTH + 3 * SC_WIDTH
HG_CHUNK = 64
HG_LEVELS = 6

NSA_HEADS = 16
NSA_KV = 4
NSA_HPG = NSA_HEADS // NSA_KV
NSA_DH = D_MODEL // NSA_HEADS
CMP_LEN = 32
CMP_STRIDE = 16
CMP_HIDDEN = 256
SEL_BLOCK = 64
SEL_TOPN = 16
WINDOW = 512
REL_BUCKETS = 32
REL_MAX_DIST = 1024

TOK_TILE = 256
NEAR_TILES = 5
SEL_TABLES = NEAR_TILES + 1
WIN_TILES = WINDOW // TOK_TILE + 1
NEG = -1e30
LOG2E = math.log2(math.e)
KEY_LANES = 128
SEL_ROWS = 16
SW_HEADS = 2
MXU_SKEW = 5
VMEM_LIMIT = 56 * 1024 * 1024
HALO = 8

_NT = (((1,), (1,)), ((), ()))
_TN = (((0,), (0,)), ((), ()))


def _sigmoid(x):
    return 1.0 / (1.0 + jnp.exp(-x))


def _silu(x):
    return x * _sigmoid(x)


def _norm_mod(x, g, sc, sh):
    ms = jnp.mean(x * x, axis=-1, keepdims=True)
    return (x * lax.rsqrt(ms + EPS) * g) * (1.0 + sc) + sh


def _fold(x, op):
    return op(x.reshape(x.shape[0] // 8, 8, x.shape[1]), axis=0)


def _const_spec(shape):
    n = len(shape)
    return pl.BlockSpec(shape, lambda *_: (0,) * n, pipeline_mode=pl.Buffered(1))


def _params(sem):
    return pltpu.CompilerParams(dimension_semantics=sem, vmem_limit_bytes=VMEM_LIMIT)


def _mod_kernel(c_ref, w_ref, b_ref, o_ref):
    c = c_ref[...]
    o_ref[0] = jnp.dot(_silu(c), w_ref[0], preferred_element_type=F32,
                       precision=lax.Precision.HIGHEST) + b_ref[0]


def _modulation(c, mod_w, mod_b):
    depth, d, n = mod_w.shape
    bsz = c.shape[0]
    tn = 1024
    return pl.pallas_call(
        _mod_kernel,
        grid=(depth, n // tn),
        in_specs=[pl.BlockSpec((bsz, d), lambda l, j: (0, 0)),
                  pl.BlockSpec((1, d, tn), lambda l, j: (l, 0, j)),
                  pl.BlockSpec((1, 1, tn), lambda l, j: (l, 0, j))],
        out_specs=pl.BlockSpec((1, bsz, tn), lambda l, j: (l, 0, j)),
        out_shape=jax.ShapeDtypeStruct((depth, bsz, n), F32),
        compiler_params=_params(("arbitrary", "arbitrary")),
        name="adaln_mod",
    )(c, mod_w, mod_b.reshape(depth, 1, n))


def _hgrn_decay_matrix():
    t = np.arange(HG_CHUNK)
    tril = (t[None, :] <= t[:, None]).astype(np.float32)
    mats = [tril]
    for lvl in range(1, HG_LEVELS + 1):
        m = 1 << (lvl - 1)
        mid = (t // (2 * m)) * (2 * m) + m - 1
        mats.append(tril - tril[mid])
    return np.concatenate(mats, axis=0)


def _mixer0_kernel(x_ref, sh_ref, sc_ref, gt_ref, ng_ref, win_ref, wout_ref, lb_ref, og_ref, cw_ref,
                   pm_ref, o_ref, proj_ref, cat_ref, st_ref, ubuf_ref, d_scr):
    tile = x_ref.shape[1]
    c64 = HG_CHUNK

    @pl.when(pl.program_id(1) == 0)
    def _():
        st_ref[...] = jnp.zeros_like(st_ref)
        ubuf_ref[0:HALO, :] = jnp.zeros((HALO, SC_WIDTH), F32)

    x = x_ref[0]
    hm = _norm_mod(x, ng_ref[...], sc_ref[0], sh_ref[0]).astype(BF16)
    proj_ref[...] = jnp.dot(hm, win_ref[...], preferred_element_type=F32)

    row = lax.broadcasted_iota(jnp.int32, (c64, HG_DK), 0)
    r64 = lax.broadcasted_iota(jnp.int32, (c64, c64), 0)
    c64i = lax.broadcasted_iota(jnp.int32, (c64, c64), 1)
    second = [None] + [(row & (1 << (lvl - 1))) != 0 for lvl in range(1, HG_LEVELS + 1)]
    same = [None] + [(r64 >> lvl) == (c64i >> lvl) for lvl in range(1, HG_LEVELS + 1)]
    eye = r64 == c64i
    pm = pm_ref[...]
    og = og_ref[...]

    units = [(c, h) for c in range(tile // c64) for h in range(HG_HEADS)]

    def blk(c, h, j):
        return pl.ds(c * c64, c64), pl.ds(j * HG_WIDTH + h * HG_DK, HG_DK)

    for u, (c, h) in enumerate(units):
        lb = lb_ref[:, h * HG_DK:(h + 1) * HG_DK]
        fg = lb + (1.0 - lb) * _sigmoid(proj_ref[blk(c, h, 1)])
        proj_ref[blk(c, h, 1)] = 1.0 - fg
        lg = jnp.log(fg)
        lg_hi = lg.astype(BF16)
        lg_lo = (lg - lg_hi.astype(F32)).astype(BF16)
        dall = jnp.dot(pm, jnp.concatenate([lg_hi, lg_lo], axis=1), preferred_element_type=F32)
        d_scr[u] = dall[:, :HG_DK] + dall[:, HG_DK:]

    intra, q_in, k_out, decay = [], [], [], []
    for u, (c, h) in enumerate(units):
        q = proj_ref[blk(c, h, 0)]
        k = proj_ref[blk(c, h, 1)]
        a = jnp.zeros((c64, c64), F32)
        for lvl in range(1, HG_LEVELS + 1):
            e = jnp.exp(-jnp.abs(d_scr[u, lvl * c64:(lvl + 1) * c64, :]))
            qt = jnp.where(second[lvl], q * e, 0.0).astype(BF16)
            kt = jnp.where(second[lvl], 0.0, k * e).astype(BF16)
            al = lax.dot_general(qt, kt, _NT, preferred_element_type=F32)
            a = a + (al if lvl == HG_LEVELS else jnp.where(same[lvl], al, 0.0))
        intra.append(jnp.where(eye, jnp.sum(q * k, axis=-1, keepdims=True), a).astype(BF16))
        b = d_scr[u, 0:c64, :]
        b_last = d_scr[u, c64 - 1:c64, :]
        q_in.append((q * jnp.exp(b)).astype(BF16))
        k_out.append((k * jnp.exp(b_last - b)).astype(BF16))
        decay.append(jnp.exp(b_last))

    state = [st_ref[h] for h in range(HG_HEADS)]
    for u, (c, h) in enumerate(units):
        vb = proj_ref[blk(c, h, 2)].astype(BF16)
        o = jnp.dot(intra[u], vb, preferred_element_type=F32)
        o = o + lax.dot_general(q_in[u], state[h].astype(BF16), _NT, preferred_element_type=F32)
        state[h] = state[h] * decay[u] + lax.dot_general(vb, k_out[u], _TN, preferred_element_type=F32)
        on = o * lax.rsqrt(jnp.mean(o * o, axis=-1, keepdims=True) + EPS) * og
        rows, _ = blk(c, h, 0)
        cat_ref[rows, h * HG_DK:(h + 1) * HG_DK] = (on * _silu(proj_ref[blk(c, h, 3)])).astype(BF16)
    for h in range(HG_HEADS):
        st_ref[h] = state[h]

    off = 4 * HG_WIDTH
    u = proj_ref[:, off + SC_WIDTH:off + 2 * SC_WIDTH] * proj_ref[:, off + 2 * SC_WIDTH:off + 3 * SC_WIDTH]
    ubuf_ref[HALO:HALO + tile, :] = u
    cw = cw_ref[...]
    conv = (ubuf_ref[HALO - 2:HALO - 2 + tile, :] * cw[0:1] + ubuf_ref[HALO - 1:HALO - 1 + tile, :] * cw[1:2]
            + u * cw[2:3])
    ubuf_ref[0:HALO, :] = u[tile - HALO:tile, :]
    cat_ref[:, HG_WIDTH:] = (proj_ref[:, off:off + SC_WIDTH] * conv).astype(BF16)

    y = jnp.dot(cat_ref[...], wout_ref[...], preferred_element_type=F32)
    o_ref[0] = x + gt_ref[0] * y


def _mixer0(x, sh, sc, gt, norm_g, w_in, w_out, lower, onorm_g, sconv_w):
    bsz, seq, d = x.shape
    t = TOK_TILE
    pm = jnp.asarray(_hgrn_decay_matrix(), BF16)
    vec = pl.BlockSpec((1, 1, d), lambda b, i: (b, 0, 0))
    return pl.pallas_call(
        _mixer0_kernel,
        grid=(bsz, seq // t),
        in_specs=[pl.BlockSpec((1, t, d), lambda b, i: (b, i, 0)), vec, vec, vec,
                  _const_spec((1, d)), _const_spec((d, AB_IN)), _const_spec((d, d)),
                  _const_spec((1, HG_WIDTH)), _const_spec((1, HG_DK)), _const_spec((3, SC_WIDTH)),
                  _const_spec(pm.shape)],
        out_specs=pl.BlockSpec((1, t, d), lambda b, i: (b, i, 0)),
        out_shape=jax.ShapeDtypeStruct(x.shape, F32),
        scratch_shapes=[pltpu.VMEM((t, AB_IN), F32), pltpu.VMEM((t, d), BF16),
                        pltpu.VMEM((HG_HEADS, HG_DK, HG_DK), F32), pltpu.VMEM((t + HALO, SC_WIDTH), F32),
                        pltpu.VMEM((t // HG_CHUNK * HG_HEADS, pm.shape[0], HG_DK), F32)],
        compiler_params=_params(("arbitrary", "arbitrary")),
        name="mixer_hgrn_sconv",
    )(x, sh, sc, gt, norm_g.reshape(1, d), w_in.astype(BF16), w_out.astype(BF16),
      lower.reshape(1, HG_WIDTH), onorm_g.reshape(1, HG_DK), sconv_w, pm)


MXU_DIM = 256
FF_CHUNKS = ((0, 5 * MXU_DIM), (5 * MXU_DIM, D_FF))


def _ffn_kernel(*refs, has_attn, final):
    if has_attn:
        (x_ref, olo_ref, ohi_ref, wo_ref, g1_ref, sh_ref, sc_ref, g2_ref, ng_ref, wup_ref, cw_ref, wdn_ref, fg_ref,
         o_ref, gbuf_ref, carry_ref) = refs
    else:
        (x_ref, sh_ref, sc_ref, g2_ref, ng_ref, wup_ref, cw_ref, wdn_ref, fg_ref,
         o_ref, gbuf_ref, carry_ref) = refs
    tile = x_ref.shape[1]

    @pl.when(pl.program_id(1) == 0)
    def _():
        carry_ref[...] = jnp.zeros_like(carry_ref)

    x = x_ref[0]
    if has_attn:
        o_t = jnp.where(pl.program_id(1) < pl.num_programs(1) // 2, olo_ref[0, 0], ohi_ref[0, 0])
        x = x + g1_ref[0] * lax.dot_general(o_t, wo_ref[...], _TN, preferred_element_type=F32)
    hf = _norm_mod(x, ng_ref[...], sc_ref[0], sh_ref[0]).astype(BF16)
    gbuf_ref[0:HALO, :] = carry_ref[...]
    gates, ups = [], []
    for c0, c1 in FF_CHUNKS:
        gate = jnp.dot(hf, wup_ref[:, c0:c1], preferred_element_type=F32)
        ups.append(jnp.dot(hf, wup_ref[:, D_FF + c0:D_FF + c1], preferred_element_type=F32))
        gbuf_ref[HALO:HALO + tile, c0:c1] = gate
        carry_ref[:, c0:c1] = gate[tile - HALO:tile, :]
        gates.append(gate)
    acc = jnp.zeros((tile, D_MODEL), F32)
    for j, (c0, c1) in enumerate(FF_CHUNKS):
        cols = slice(c0, c1)
        cw = cw_ref[:, cols]
        conv = (gbuf_ref[HALO - 2:HALO - 2 + tile, cols] * cw[0:1] + gbuf_ref[HALO - 1:HALO - 1 + tile, cols] * cw[1:2]
                + gates[j] * cw[2:3])
        act = (_silu(conv) * ups[j]).astype(BF16)
        acc = acc + jnp.dot(act, wdn_ref[cols, :], preferred_element_type=F32)
    out = x + g2_ref[0] * acc
    if final:
        out = out * lax.rsqrt(jnp.mean(out * out, axis=-1, keepdims=True) + EPS) * fg_ref[...]
    o_ref[0] = out


def _conv_ffn(x, sh, sc, g2, norm_g, w_up, conv_w, w_down, final_g, attn=None, final=False):
    bsz, seq, d = x.shape
    t = TOK_TILE
    vec = pl.BlockSpec((1, 1, d), lambda b, i: (b, 0, 0))
    args, specs = [x], [pl.BlockSpec((1, t, d), lambda b, i: (b, i, 0))]
    if attn is not None:
        o_lo, o_hi, w_o, g1 = attn
        top = o_lo.shape[1] - 1
        last = seq // t - 1
        args += [o_lo, o_hi, w_o.astype(BF16), g1]
        specs += [pl.BlockSpec((1, 1, d, t), lambda b, i: (b, jnp.minimum(i, top), 0, 0)),
                  pl.BlockSpec((1, 1, d, t), lambda b, i: (b, jnp.minimum(last - i, top), 0, 0)),
                  _const_spec((d, d)), vec]
    args += [sh, sc, g2, norm_g.reshape(1, d), w_up.astype(BF16), conv_w, w_down.astype(BF16),
             final_g.reshape(1, d)]
    specs += [vec, vec, vec, _const_spec((1, d)), _const_spec((d, 2 * D_FF)), _const_spec((3, D_FF)),
              _const_spec((D_FF, d)), _const_spec((1, d))]
    return pl.pallas_call(
        functools.partial(_ffn_kernel, has_attn=attn is not None, final=final),
        grid=(bsz, seq // t),
        in_specs=specs,
        out_specs=pl.BlockSpec((1, t, d), lambda b, i: (b, i, 0)),
        out_shape=jax.ShapeDtypeStruct(x.shape, F32),
        scratch_shapes=[pltpu.VMEM((t + HALO, D_FF), F32), pltpu.VMEM((HALO, D_FF), F32)],
        compiler_params=_params(("arbitrary", "arbitrary")),
        name="conv_ffn_attnproj" if attn is not None else "conv_ffn",
    )(*args)


KV_ROWS = NSA_KV * NSA_DH


def _nsa_proj_kernel(x_ref, sh_ref, sc_ref, ng_ref, wa_ref, wq_ref, wg_ref, wv_ref, hot_ref,
                     kc_ref, vc_ref, ks_ref, kw_ref, qt_ref, gt_ref, vst_ref, vwt_ref):
    hm = _norm_mod(x_ref[0], ng_ref[...], sc_ref[0], sh_ref[0]).astype(BF16)
    kv = jnp.dot(hm, wa_ref[...], preferred_element_type=F32).astype(BF16)
    for g in range(NSA_KV):
        part = lambda j: kv[:, j * KV_ROWS + g * NSA_DH:j * KV_ROWS + (g + 1) * NSA_DH]
        kc_ref[0, g] = part(0)
        vc_ref[0, g] = part(1)
        ks_ref[0, g] = jnp.concatenate([part(2), hot_ref[...]], axis=1)
        kw_ref[0, g] = part(3)
    qt = lax.dot_general(wq_ref[...], hm, _NT, preferred_element_type=F32)
    qt_ref[0, 0] = (qt * (NSA_DH ** -0.5 * LOG2E)).astype(BF16)
    gt_ref[0] = _sigmoid(lax.dot_general(wg_ref[...], hm, _NT, preferred_element_type=F32))
    vt = lax.dot_general(wv_ref[...], hm, _NT, preferred_element_type=F32).astype(BF16)
    vst_ref[0, 0] = vt[:KV_ROWS]
    vwt_ref[0, 0] = vt[KV_ROWS:]


def _nsa_proj(x, sh, sc, norm_g, w_in):
    bsz, seq, d = x.shape
    t = TOK_TILE
    nt = seq // t
    nq, ng = NSA_HEADS * NSA_DH, 3 * NSA_HEADS
    o = nq + ng
    w = w_in.astype(BF16)
    col = lambda j: w[:, o + j * KV_ROWS:o + (j + 1) * KV_ROWS]
    w_a = jnp.concatenate([col(0), col(1), col(2), col(4)], axis=1)
    w_q = w[:, :nq].T
    w_g = w[:, nq:o].T
    w_v = jnp.concatenate([col(3), col(5)], axis=1).T
    hot = (np.arange(t)[:, None] // SEL_BLOCK == np.arange(KEY_LANES - NSA_DH)[None, :]).astype(np.float32)
    vec = pl.BlockSpec((1, 1, d), lambda b, i: (b, 0, 0))
    tiled = lambda r: pl.BlockSpec((1, 1, r, t), lambda b, i: (b, i, 0, 0))
    grouped = lambda n: pl.BlockSpec((1, NSA_KV, t, n), lambda b, i: (b, 0, i, 0))
    rows = lambda n: jax.ShapeDtypeStruct((bsz, NSA_KV, seq, n), BF16)
    return pl.pallas_call(
        _nsa_proj_kernel,
        grid=(bsz, nt),
        in_specs=[pl.BlockSpec((1, t, d), lambda b, i: (b, i, 0)), vec, vec, _const_spec((1, d)),
                  _const_spec((d, 4 * KV_ROWS)), _const_spec((nq, d)), _const_spec((ng, d)),
                  _const_spec((2 * KV_ROWS, d)), _const_spec(hot.shape)],
        out_specs=[grouped(NSA_DH), grouped(NSA_DH), grouped(KEY_LANES), grouped(NSA_DH), tiled(nq),
                   pl.BlockSpec((1, ng, t), lambda b, i: (b, 0, i)), tiled(KV_ROWS), tiled(KV_ROWS)],
        out_shape=[rows(NSA_DH), rows(NSA_DH), rows(KEY_LANES), rows(NSA_DH),
                   jax.ShapeDtypeStruct((bsz, nt, nq, t), BF16),
                   jax.ShapeDtypeStruct((bsz, ng, seq), F32),
                   jax.ShapeDtypeStruct((bsz, nt, KV_ROWS, t), BF16),
                   jax.ShapeDtypeStruct((bsz, nt, KV_ROWS, t), BF16)],
        compiler_params=_params(("arbitrary", "arbitrary")),
        name="nsa_in_proj",
    )(x, sh, sc, norm_g.reshape(1, d), w_a, w_q, w_g, w_v, jnp.asarray(hot, BF16))


def _compress_kernel(hk_ref, hv_ref, pk_ref, pv_ref, w1k_ref, w2k_ref, w1v_ref, w2vt_ref, kc_ref, vct_ref):
    half = CMP_STRIDE * NSA_DH

    def hidden(h_ref, pos_ref, w1_ref):
        h = h_ref[0, 0]
        n = h.shape[0]
        top = jnp.dot(h, w1_ref[:half, :], preferred_element_type=F32)
        bot = jnp.dot(h, w1_ref[half:, :], preferred_element_type=F32)
        pos = jnp.dot(jnp.broadcast_to(pos_ref[...], (8, 2 * half)).astype(BF16), w1_ref[...],
                      preferred_element_type=F32)[0:1]
        return _silu(top + pltpu.roll(bot, n - 1, 0) + pos).astype(BF16)

    kc_ref[0, 0] = jnp.dot(hidden(hk_ref, pk_ref, w1k_ref), w2k_ref[...],
                           preferred_element_type=F32).astype(BF16)
    vct_ref[0, 0] = lax.dot_general(w2vt_ref[...], hidden(hv_ref, pv_ref, w1v_ref), _NT,
                                    preferred_element_type=F32).astype(BF16)


def _compress(hk, hv, pos_k, pos_v, w1_k, w2_k, w1_v, w2_v):
    bsz, g, n, width = hk.shape
    blk = pl.BlockSpec((1, 1, n, width), lambda b, j: (b, j, 0, 0))
    return pl.pallas_call(
        _compress_kernel,
        grid=(bsz, g),
        in_specs=[blk, blk, _const_spec((1, width * 2)), _const_spec((1, width * 2)),
                  _const_spec((2 * width, CMP_HIDDEN)), _const_spec((CMP_HIDDEN, NSA_DH)),
                  _const_spec((2 * width, CMP_HIDDEN)), _const_spec((NSA_DH, CMP_HIDDEN))],
        out_specs=[pl.BlockSpec((1, 1, n, NSA_DH), lambda b, j: (b, j, 0, 0)),
                   pl.BlockSpec((1, 1, NSA_DH, n), lambda b, j: (b, j, 0, 0))],
        out_shape=[jax.ShapeDtypeStruct((bsz, g, n, NSA_DH), BF16),
                   jax.ShapeDtypeStruct((bsz, g, NSA_DH, n), BF16)],
        compiler_params=_params(("arbitrary", "arbitrary")),
        name="nsa_compress",
    )(hk, hv, pos_k.reshape(1, -1), pos_v.reshape(1, -1), w1_k.astype(BF16), w2_k.astype(BF16),
      w1_v.astype(BF16), w2_v.T.astype(BF16))


def _bucket_upper_bounds():
    n = np.arange(4 * REL_MAX_DIST, dtype=np.float64)
    exact = REL_BUCKETS // 2
    large = exact + (np.log(np.maximum(n, exact) / exact) / math.log(REL_MAX_DIST / exact)
                     * (REL_BUCKETS - exact)).astype(np.int64)
    bucket = np.where(n < exact, n.astype(np.int64), np.minimum(large, REL_BUCKETS - 1))
    return [int(np.max(np.nonzero(bucket <= j)[0])) for j in range(REL_BUCKETS - 1)]


def _bias_of_dist(dist, rb_ref, heads, uppers):
    vals = [jnp.full(dist.shape, rb_ref[REL_BUCKETS - 1, h], F32) for h in heads]
    for j in range(REL_BUCKETS - 2, -1, -1):
        m = dist <= uppers[j]
        vals = [jnp.where(m, rb_ref[j, h], v) for h, v in zip(heads, vals)]
    return vals


def _bias_strip(dist_min, dist_max, dist, rb_ref, heads, uppers, store, limit=None):
    is_const = (dist_min > uppers[-1]) | (dist_max < 0)
    if limit is not None:
        is_const = is_const | (dist_min >= limit)

    @pl.when(is_const)
    def _():
        store([jnp.full(dist.shape, rb_ref[REL_BUCKETS - 1, h], F32) for h in heads])

    @pl.when(jnp.logical_not(is_const))
    def _():
        store(_bias_of_dist(dist, rb_ref, heads, uppers))


def _bias_near_kernel(rb_ref, o_ref, *, uppers):
    d = pl.program_id(0)
    t = o_ref.shape[2]
    heads = list(range(NSA_HEADS))
    is_win = d >= SEL_TABLES
    delta = jnp.where(is_win, d - SEL_TABLES, d)
    limit = jnp.where(is_win, WINDOW, (SEL_TABLES + 1) * t)

    def strip(i, carry):
        r0 = pl.multiple_of(i * 8, 8)
        key = r0 + lax.broadcasted_iota(jnp.int32, (8, t), 0)
        tok = lax.broadcasted_iota(jnp.int32, (8, t), 1)
        dist = delta * t + tok - key
        valid = (dist >= 0) & (dist < limit)

        def store(vals):
            for h in heads:
                o_ref[h, 0, pl.ds(r0, 8), :] = jnp.where(valid, vals[h], NEG)

        _bias_strip(delta * t - (r0 + 7), delta * t + (t - 1) - r0, dist, rb_ref, heads, uppers, store, limit)
        return carry

    lax.fori_loop(0, t // 8, strip, 0)


def _bias_cmp_kernel(rb_ref, o_ref, *, uppers):
    tb = pl.program_id(0)
    n, t = o_ref.shape[1], o_ref.shape[2]
    heads = list(range(NSA_HEADS))

    def strip(i, carry):
        r0 = pl.multiple_of(i * 8, 8)
        blk = r0 + lax.broadcasted_iota(jnp.int32, (8, t), 0)
        tok = tb * t + lax.broadcasted_iota(jnp.int32, (8, t), 1)
        dist = tok - (blk * CMP_STRIDE + CMP_LEN - 1)

        def store(vals):
            for h in heads:
                o_ref[h, pl.ds(r0, 8), :] = jnp.where(dist >= 0, vals[h], NEG)

        last = CMP_LEN - 1
        _bias_strip(tb * t - ((r0 + 7) * CMP_STRIDE + last), tb * t + (t - 1) - (r0 * CMP_STRIDE + last), dist,
                    rb_ref, heads, uppers, store)
        return carry

    lax.fori_loop(0, n // 8, strip, 0)


def _bias_tables(rel_bias, seq):
    t = TOK_TILE
    n_cmp = seq // CMP_STRIDE
    uppers = _bucket_upper_bounds()
    smem = pl.BlockSpec(memory_space=pltpu.SMEM)
    near = pl.pallas_call(
        functools.partial(_bias_near_kernel, uppers=uppers),
        grid=(SEL_TABLES + WIN_TILES,),
        in_specs=[smem],
        out_specs=pl.BlockSpec((NSA_HEADS, 1, t, t), lambda d: (0, d, 0, 0)),
        out_shape=jax.ShapeDtypeStruct((NSA_HEADS, SEL_TABLES + WIN_TILES, t, t), F32),
        compiler_params=_params(("arbitrary",)),
        name="relbias_near",
    )(rel_bias)
    cmp_b = pl.pallas_call(
        functools.partial(_bias_cmp_kernel, uppers=uppers),
        grid=(seq // t,),
        in_specs=[smem],
        out_specs=pl.BlockSpec((NSA_HEADS, n_cmp, t), lambda i: (0, 0, i)),
        out_shape=jax.ShapeDtypeStruct((NSA_HEADS, n_cmp, seq), F32),
        compiler_params=_params(("arbitrary",)),
        name="relbias_cmp",
    )(rel_bias)
    return near, cmp_b


def _cmp_to_sel_t(seq):
    n_cmp, n_sel = seq // CMP_STRIDE, seq // SEL_BLOCK
    c_start = np.arange(n_cmp)[:, None] * CMP_STRIDE
    s_start = np.arange(n_sel)[None, :] * SEL_BLOCK
    inside = np.clip(np.minimum(c_start + CMP_LEN, s_start + SEL_BLOCK) - np.maximum(c_start, s_start), 0, None)
    return (inside / CMP_LEN).T.astype(np.float32)


def _cmp_topk_kernel(qt_ref, kc_ref, vct_ref, bias_ref, c2s_ref, oc_ref, sel_ref, score_scr, cnt_scr):
    tb = pl.program_id(1)
    n, t = bias_ref.shape[1], bias_ref.shape[2]
    n_sel = c2s_ref.shape[0]
    tok = tb * t + lax.broadcasted_iota(jnp.int32, (1, t), 1)
    any_visible = tok >= CMP_LEN - 1
    kc = kc_ref[0, 0]
    ones = (lax.broadcasted_iota(jnp.int32, (SEL_ROWS, n), 0) == 0).astype(BF16)
    lhs = jnp.concatenate([vct_ref[0, 0], ones, c2s_ref[...]], axis=0)
    imp = jnp.zeros((n_sel, t), F32)
    weights = []
    for i in range(NSA_HPG):
        rows = slice(i * NSA_DH, (i + 1) * NSA_DH)
        s = jnp.dot(kc, qt_ref[0, 0, rows, :], preferred_element_type=F32) + bias_ref[i]
        m = jnp.max(_fold(s, jnp.max), axis=0, keepdims=True)
        weights.append(jnp.exp2(s - m).astype(BF16))
    for i in range(NSA_HPG):
        rows = slice(i * NSA_DH, (i + 1) * NSA_DH)
        r = jnp.dot(lhs, weights[i], preferred_element_type=F32)
        scale = jnp.where(any_visible, 1.0 / r[NSA_DH:NSA_DH + 1], 0.0)
        oc_ref[0, 0, rows, :] = (r[:NSA_DH] * scale).astype(BF16)
        imp = imp + r[NSA_DH + SEL_ROWS:] * scale

    blk = lax.broadcasted_iota(jnp.int32, (n_sel, t), 0)
    cur = (tb * t + lax.broadcasted_iota(jnp.int32, (n_sel, t), 1)) // SEL_BLOCK
    forced = (blk == 0) | (blk == cur) | (blk == cur - 1)
    score = jnp.where(blk > cur, -jnp.inf, jnp.where(forced, jnp.inf, imp))
    groups = n_sel // 8
    per_tile = t // SEL_BLOCK
    sub = lax.broadcasted_iota(jnp.int32, (8, t), 0)
    score_scr[...] = score
    cnt_scr[...] = jnp.zeros_like(cnt_scr)
    for j in range(groups):
        @pl.when(8 * j < (tb + 1) * per_tile)
        def _():
            src = [jnp.broadcast_to(score_scr[sp:sp + 1, :], (8, t)) for sp in range(8 * j, 8 * j + 8)]
            for v in range(groups):
                part = score_scr[8 * v:8 * v + 8, :]
                cnt = cnt_scr[8 * v:8 * v + 8, :]
                for sp, r in zip(range(8 * j, 8 * j + 8), src):
                    if j < v:
                        beats = r >= part
                    elif j > v:
                        beats = r > part
                    else:
                        beats = (r > part) | ((r == part) & (sub > sp - 8 * v))
                    cnt = cnt + jnp.where(beats, 1.0, 0.0)
                cnt_scr[8 * v:8 * v + 8, :] = cnt
    pad = jnp.zeros((SEL_ROWS - per_tile, t), F32)
    for v in range(groups):
        mask = jnp.where(cnt_scr[8 * v:8 * v + 8, :] < min(SEL_TOPN, n_sel), 0.0, NEG)
        for r0 in range(0, 8, per_tile):
            kt = (8 * v + r0) // per_tile
            sel_ref[0, 0, 0, kt] = jnp.concatenate([mask[r0:r0 + per_tile], pad], axis=0).astype(BF16)


def _cmp_topk(q_t, k_cmp, v_cmp_t, bias_cmp, seq):
    bsz, nt, nq, t = q_t.shape
    n_cmp, n_sel = seq // CMP_STRIDE, seq // SEL_BLOCK
    grp = NSA_HPG * NSA_DH
    c2s = jnp.asarray(_cmp_to_sel_t(seq), BF16)
    return pl.pallas_call(
        _cmp_topk_kernel,
        grid=(NSA_KV, nt, bsz),
        in_specs=[pl.BlockSpec((1, 1, grp, t), lambda g, i, b: (b, i, g, 0)),
                  pl.BlockSpec((1, 1, n_cmp, NSA_DH), lambda g, i, b: (b, g, 0, 0)),
                  pl.BlockSpec((1, 1, NSA_DH, n_cmp), lambda g, i, b: (b, g, 0, 0)),
                  pl.BlockSpec((NSA_HPG, n_cmp, t), lambda g, i, b: (g, 0, i)),
                  _const_spec((n_sel, n_cmp))],
        out_specs=[pl.BlockSpec((1, 1, grp, t), lambda g, i, b: (b, i, g, 0)),
                   pl.BlockSpec((1, 1, 1, nt, SEL_ROWS, t), lambda g, i, b: (b, g, i, 0, 0, 0))],
        out_shape=[jax.ShapeDtypeStruct((bsz, nt, nq, t), BF16),
                   jax.ShapeDtypeStruct((bsz, NSA_KV, nt, nt, SEL_ROWS, t), BF16)],
        scratch_shapes=[pltpu.VMEM((n_sel, t), F32), pltpu.VMEM((n_sel, t), F32)],
        compiler_params=_params(("arbitrary", "arbitrary", "arbitrary")),
        name="nsa_cmp_topk",
    )(q_t, k_cmp, v_cmp_t, bias_cmp, c2s)


def _sel_win_kernel(qa_ref, qb_ref, ks_ref, vst_ref, kw_ref, vwt_ref, sela_ref, selb_ref, tab_ref, ga_ref, gb_ref,
                    oca_ref, ocb_ref, olo_ref, ohi_ref):
    h0 = pl.program_id(0) * SW_HEADS
    i = pl.program_id(2)
    nt, t = vst_ref.shape[1], vst_ref.shape[3]
    half = nt // 2
    tb = (i, nt - 1 - i)
    head_rows = lambda hh: slice(hh * NSA_DH, (hh + 1) * NSA_DH)
    q = [(qa_ref[0, 0, head_rows(hh), :], qb_ref[0, 0, head_rows(hh), :]) for hh in range(SW_HEADS)]
    sel = (sela_ref, selb_ref)

    def tile_rows(kt):
        return pl.ds(pl.multiple_of(kt * t, t), t)

    def pick(on_a, xa, xb):
        if isinstance(on_a, bool):
            return xa if on_a else xb
        return jnp.where(on_a, xa, xb)

    work = []
    for p in range(nt + 1):
        on_a = True if p == 0 else (False if p >= half else p <= i)
        kt = pick(on_a, p, jnp.maximum(p - i - 1, 0))
        work.append((on_a, kt, True, pick(on_a, tb[0], tb[1]) - kt))
    for side in range(2):
        for delta in range(WIN_TILES):
            keep = True if (side == 1 and half >= WIN_TILES) else tb[side] >= delta
            work.append((side == 0, jnp.maximum(tb[side] - delta, 0), keep, delta))
    n_selpos = nt + 1

    ones = (lax.broadcasted_iota(jnp.int32, (SEL_ROWS, t), 0) == 0).astype(BF16)
    q_pad = jnp.zeros((KEY_LANES - NSA_DH - SEL_ROWS, t), BF16)
    keys3 = [(br, side, hh) for br in range(2) for side in range(2) for hh in range(SW_HEADS)]
    run_max = {k: jnp.full((1, t), NEG, F32) for k in keys3}
    acc = {k: jnp.zeros((NSA_DH + SEL_ROWS, t), F32) for k in keys3}

    def merge(k, m_tile, part):
        m_new = jnp.maximum(run_max[k], m_tile)
        acc[k] = acc[k] * jnp.exp2(run_max[k] - m_new) + part * jnp.exp2(m_tile - m_new)
        run_max[k] = m_new

    def scores(p, hh):
        on_a, kt, keep, delta = work[p]
        qp = pick(on_a, q[hh][0], q[hh][1])
        if p >= n_selpos:
            s = jnp.dot(kw_ref[0, 0, tile_rows(kt), :], qp, preferred_element_type=F32)
            s = s + tab_ref[hh, SEL_TABLES + delta]
        else:
            rows = pick(on_a, sel[0][0, 0, 0, kt], sel[1][0, 0, 0, kt])
            q_aug = jnp.concatenate([qp, rows, q_pad], axis=0)
            s = jnp.dot(ks_ref[0, 0, tile_rows(kt), :], q_aug, preferred_element_type=F32)
            s = s + tab_ref[hh, jnp.minimum(delta, SEL_TABLES - 1)]
        m_tile = jnp.max(_fold(s, jnp.max), axis=0, keepdims=True)
        return m_tile, jnp.exp2(s - m_tile).astype(BF16)

    def values(p, hh, m_tile, pr):
        on_a, kt, keep, delta = work[p]
        br = int(p >= n_selpos)
        vt_ref = vwt_ref if br else vst_ref
        part = jnp.dot(jnp.concatenate([vt_ref[0, kt], ones], axis=0), pr, preferred_element_type=F32)
        if not isinstance(keep, bool):
            m_tile = jnp.where(keep, m_tile, NEG)
        if isinstance(on_a, bool):
            merge((br, int(not on_a), hh), m_tile, part)
        else:
            merge((br, 0, hh), jnp.where(on_a, m_tile, NEG), part)
            merge((br, 1, hh), jnp.where(on_a, NEG, m_tile), part)

    items = [(p, hh) for p in range(len(work)) for hh in range(SW_HEADS)]
    pending = {}
    for step in range(len(items) + MXU_SKEW):
        if step < len(items):
            pending[step] = scores(*items[step])
        if step >= MXU_SKEW:
            values(*items[step - MXU_SKEW], *pending.pop(step - MXU_SKEW))

    for side, (g_ref, oc_ref, o_ref) in enumerate(((ga_ref, oca_ref, olo_ref), (gb_ref, ocb_ref, ohi_ref))):
        for hh in range(SW_HEADS):
            gate = lambda br: g_ref[0, pl.ds(br * NSA_HEADS + h0 + hh, 1), :]
            o_s = acc[(0, side, hh)][:NSA_DH] / acc[(0, side, hh)][NSA_DH:NSA_DH + 1]
            o_w = acc[(1, side, hh)][:NSA_DH] / acc[(1, side, hh)][NSA_DH:NSA_DH + 1]
            o_c = oc_ref[0, 0, head_rows(hh), :].astype(F32)
            o_ref[0, 0, head_rows(hh), :] = (gate(0) * o_c + gate(1) * o_s + gate(2) * o_w).astype(BF16)


def _sel_win(q_t, ks, vs_t, kw, vw_t, sel, tables, gates_t, oc_t):
    bsz, nt, nq, t = q_t.shape
    seq = nt * t
    last = nt - 1
    per_group = NSA_HPG // SW_HEADS
    head_a = pl.BlockSpec((1, 1, SW_HEADS * NSA_DH, t), lambda h, b, i: (b, i, h, 0))
    head_b = pl.BlockSpec((1, 1, SW_HEADS * NSA_DH, t), lambda h, b, i: (b, last - i, h, 0))
    keys = pl.BlockSpec((1, 1, seq, NSA_DH), lambda h, b, i: (b, h // per_group, 0, 0))
    keys_sel = pl.BlockSpec((1, 1, seq, KEY_LANES), lambda h, b, i: (b, h // per_group, 0, 0))
    vals = pl.BlockSpec((1, nt, NSA_DH, t), lambda h, b, i: (b, 0, h // per_group, 0))
    sel_a = pl.BlockSpec((1, 1, 1, nt, SEL_ROWS, t), lambda h, b, i: (b, h // per_group, i, 0, 0, 0))
    sel_b = pl.BlockSpec((1, 1, 1, nt, SEL_ROWS, t), lambda h, b, i: (b, h // per_group, last - i, 0, 0, 0))
    gate_a = pl.BlockSpec((1, 3 * NSA_HEADS, t), lambda h, b, i: (b, 0, i))
    gate_b = pl.BlockSpec((1, 3 * NSA_HEADS, t), lambda h, b, i: (b, 0, last - i))
    out = jax.ShapeDtypeStruct((bsz, nt // 2, nq, t), BF16)
    return pl.pallas_call(
        _sel_win_kernel,
        grid=(NSA_HEADS // SW_HEADS, bsz, nt // 2),
        in_specs=[head_a, head_b, keys_sel, vals, keys, vals, sel_a, sel_b,
                  pl.BlockSpec((SW_HEADS, SEL_TABLES + WIN_TILES, t, t), lambda h, b, i: (h, 0, 0, 0)),
                  gate_a, gate_b, head_a, head_b],
        out_specs=[head_a, head_a],
        out_shape=[out, out],
        compiler_params=_params(("arbitrary", "arbitrary", "arbitrary")),
        name="nsa_sel_win",
    )(q_t, q_t, ks, vs_t, kw, vw_t, sel, sel, tables, gates_t, gates_t, oc_t, oc_t)


def kernel(x, c, mod_w, mod_b, norm_mix_g, norm_ffn_g, ab_w_in, ab_w_out, hgrn_lb_logits, hgrn_onorm_g, sconv_w, nsa_w_in, nsa_w_out, nsa_cmp_pos_k, nsa_cmp_pos_v, nsa_cmp_w1_k, nsa_cmp_w2_k, nsa_cmp_w1_v, nsa_cmp_w2_v, rel_bias, ffn_w_up, ffn_conv_w, ffn_w_down, final_norm_g):
    bsz, seq, d = x.shape
    assert d == D_MODEL and seq % TOK_TILE == 0
    lower = jnp.cumsum(jax.nn.softmax(hgrn_lb_logits.astype(F32), axis=0), axis=0)
    mod = _modulation(c, mod_w, mod_b)
    parts = lambda l: [mod[l, :, j * d:(j + 1) * d].reshape(bsz, 1, d) for j in range(6)]

    sh1, sc1, g1, sh2, sc2, g2 = parts(0)
    x = _mixer0(x, sh1, sc1, g1, norm_mix_g[0], ab_w_in[0], ab_w_out[0], lower[0], hgrn_onorm_g[0], sconv_w[0])
    x = _conv_ffn(x, sh2, sc2, g2, norm_ffn_g[0], ffn_w_up[0], ffn_conv_w[0], ffn_w_down[0], final_norm_g)

    sh1, sc1, g1, sh2, sc2, g2 = parts(1)
    kc, vc, ks, kw, q_t, gates_t, vs_t, vw_t = _nsa_proj(x, sh1, sc1, norm_mix_g[1], nsa_w_in[0])
    to_blocks = lambda a: a.reshape(bsz, NSA_KV, seq // CMP_STRIDE, CMP_STRIDE * NSA_DH)
    k_cmp, v_cmp_t = _compress(to_blocks(kc), to_blocks(vc), nsa_cmp_pos_k[0], nsa_cmp_pos_v[0],
                               nsa_cmp_w1_k[0], nsa_cmp_w2_k[0], nsa_cmp_w1_v[0], nsa_cmp_w2_v[0])
    near, bias_cmp = _bias_tables(rel_bias.astype(F32) * LOG2E, seq)
    oc_t, sel = _cmp_topk(q_t, k_cmp, v_cmp_t, bias_cmp, seq)
    o_lo, o_hi = _sel_win(q_t, ks, vs_t, kw, vw_t, sel, near, gates_t, oc_t)
    return _conv_ffn(x, sh2, sc2, g2, norm_ffn_g[1], ffn_w_up[1], ffn_conv_w[1], ffn_w_down[1], final_norm_g,
                     attn=(o_lo, o_hi, nsa_w_out[0], g1), final=True)
```

```python
import functools
import math

import jax
import jax.numpy as jnp
import numpy as np
from jax import lax
from jax.experimental import pallas as pl
from jax.experimental.pallas import tpu as pltpu

F32 = jnp.float32
BF16 = jnp.bfloat16

EPS = 1e-6
D_MODEL = 1024
D_FF = 2816

HG_HEADS = 4
HG_DK = 128
HG_WIDTH = HG_HEADS * HG_DK
SC_WIDTH = D_MODEL - HG_WIDTH
AB_IN = 4 * HG_WIDTH + 3 * SC_WIDTH
HG_CHUNK = 64
HG_LEVELS = 6

NSA_HEADS = 16
NSA_KV = 4
NSA_HPG = NSA_HEADS // NSA_KV
NSA_DH = D_MODEL // NSA_HEADS
CMP_LEN = 32
CMP_STRIDE = 16
CMP_HIDDEN = 256
SEL_BLOCK = 64
SEL_TOPN = 16
WINDOW = 512
REL_BUCKETS = 32
REL_MAX_DIST = 1024

TOK_TILE = 256
NEAR_TILES = 5
SEL_TABLES = NEAR_TILES + 1
WIN_TILES = WINDOW // TOK_TILE + 1
NEG = -1e30
LOG2E = math.log2(math.e)
KEY_LANES = 128
SEL_ROWS = 16
CMP_BATCH = 2
SW_HEADS = 4
MXU_SKEW = 5
VMEM_LIMIT = 56 * 1024 * 1024
HALO = 8

_NT = (((1,), (1,)), ((), ()))
_TN = (((0,), (0,)), ((), ()))


def _sigmoid(x):
    return 1.0 / (1.0 + jnp.exp(-x))


def _silu(x):
    return x * _sigmoid(x)


def _norm_mod(x, g, sc, sh):
    ms = jnp.mean(x * x, axis=-1, keepdims=True)
    return (x * lax.rsqrt(ms + EPS) * g) * (1.0 + sc) + sh


def _fold(x, op):
    return op(x.reshape(x.shape[0] // 8, 8, x.shape[1]), axis=0)


def _const_spec(shape):
    n = len(shape)
    return pl.BlockSpec(shape, lambda *_: (0,) * n, pipeline_mode=pl.Buffered(1))


def _params(sem):
    return pltpu.CompilerParams(dimension_semantics=sem, vmem_limit_bytes=VMEM_LIMIT)


def _mod_kernel(c_ref, w_ref, b_ref, o_ref):
    c = c_ref[...]
    o_ref[0] = jnp.dot(_silu(c), w_ref[0], preferred_element_type=F32,
                       precision=lax.Precision.HIGHEST) + b_ref[0]


def _modulation(c, mod_w, mod_b):
    depth, d, n = mod_w.shape
    bsz = c.shape[0]
    tn = 1024
    return pl.pallas_call(
        _mod_kernel,
        grid=(depth, n // tn),
        in_specs=[pl.BlockSpec((bsz, d), lambda l, j: (0, 0)),
                  pl.BlockSpec((1, d, tn), lambda l, j: (l, 0, j)),
                  pl.BlockSpec((1, 1, tn), lambda l, j: (l, 0, j))],
        out_specs=pl.BlockSpec((1, bsz, tn), lambda l, j: (l, 0, j)),
        out_shape=jax.ShapeDtypeStruct((depth, bsz, n), F32),
        compiler_params=_params(("arbitrary", "arbitrary")),
        name="adaln_mod",
    )(c, mod_w, mod_b.reshape(depth, 1, n))


def _hgrn_decay_matrix():
    t = np.arange(HG_CHUNK)
    tril = (t[None, :] <= t[:, None]).astype(np.float32)
    mats = [tril]
    for lvl in range(1, HG_LEVELS + 1):
        m = 1 << (lvl - 1)
        mid = (t // (2 * m)) * (2 * m) + m - 1
        mats.append(tril - tril[mid])
    return np.concatenate(mats, axis=0)


def _mixer0_kernel(x_ref, sh_ref, sc_ref, gt_ref, ng_ref, win_ref, wout_ref, lb_ref, og_ref, cw_ref,
                   pm_ref, o_ref, proj_ref, cat_ref, st_ref, ubuf_ref, d_scr):
    tile = x_ref.shape[1]
    c64 = HG_CHUNK

    @pl.when(pl.program_id(1) == 0)
    def _():
        st_ref[...] = jnp.zeros_like(st_ref)
        ubuf_ref[0:HALO, :] = jnp.zeros((HALO, SC_WIDTH), F32)

    x = x_ref[0]
    hm = _norm_mod(x, ng_ref[...], sc_ref[0], sh_ref[0]).astype(BF16)
    proj_ref[...] = jnp.dot(hm, win_ref[...], preferred_element_type=F32)

    row = lax.broadcasted_iota(jnp.int32, (c64, HG_DK), 0)
    r64 = lax.broadcasted_iota(jnp.int32, (c64, c64), 0)
    c64i = lax.broadcasted_iota(jnp.int32, (c64, c64), 1)
    second = [None] + [(row & (1 << (lvl - 1))) != 0 for lvl in range(1, HG_LEVELS + 1)]
    same = [None] + [(r64 >> lvl) == (c64i >> lvl) for lvl in range(1, HG_LEVELS + 1)]
    eye = r64 == c64i
    pm = pm_ref[...]
    og = og_ref[...]

    units = [(c, h) for c in range(tile // c64) for h in range(HG_HEADS)]

    def blk(c, h, j):
        return pl.ds(c * c64, c64), pl.ds(j * HG_WIDTH + h * HG_DK, HG_DK)

    for u, (c, h) in enumerate(units):
        lb = lb_ref[:, h * HG_DK:(h + 1) * HG_DK]
        fg = lb + (1.0 - lb) * _sigmoid(proj_ref[blk(c, h, 1)])
        proj_ref[blk(c, h, 1)] = 1.0 - fg
        lg = jnp.log(fg)
        lg_hi = lg.astype(BF16)
        lg_lo = (lg - lg_hi.astype(F32)).astype(BF16)
        dall = jnp.dot(pm, jnp.concatenate([lg_hi, lg_lo], axis=1), preferred_element_type=F32)
        d_scr[u] = dall[:, :HG_DK] + dall[:, HG_DK:]

    intra, q_in, k_out, decay = [], [], [], []
    for u, (c, h) in enumerate(units):
        q = proj_ref[blk(c, h, 0)]
        k = proj_ref[blk(c, h, 1)]
        a = jnp.zeros((c64, c64), F32)
        for lvl in range(1, HG_LEVELS + 1):
            e = jnp.exp(-jnp.abs(d_scr[u, lvl * c64:(lvl + 1) * c64, :]))
            qt = jnp.where(second[lvl], q * e, 0.0).astype(BF16)
            kt = jnp.where(second[lvl], 0.0, k * e).astype(BF16)
            al = lax.dot_general(qt, kt, _NT, preferred_element_type=F32)
            a = a + (al if lvl == HG_LEVELS else jnp.where(same[lvl], al, 0.0))
        intra.append(jnp.where(eye, jnp.sum(q * k, axis=-1, keepdims=True), a).astype(BF16))
        b = d_scr[u, 0:c64, :]
        b_last = d_scr[u, c64 - 1:c64, :]
        q_in.append((q * jnp.exp(b)).astype(BF16))
        k_out.append((k * jnp.exp(b_last - b)).astype(BF16))
        decay.append(jnp.exp(b_last))

    state = [st_ref[h] for h in range(HG_HEADS)]
    for u, (c, h) in enumerate(units):
        vb = proj_ref[blk(c, h, 2)].astype(BF16)
        o = jnp.dot(intra[u], vb, preferred_element_type=F32)
        o = o + lax.dot_general(q_in[u], state[h].astype(BF16), _NT, preferred_element_type=F32)
        state[h] = state[h] * decay[u] + lax.dot_general(vb, k_out[u], _TN, preferred_element_type=F32)
        on = o * lax.rsqrt(jnp.mean(o * o, axis=-1, keepdims=True) + EPS) * og
        rows, _ = blk(c, h, 0)
        cat_ref[rows, h * HG_DK:(h + 1) * HG_DK] = (on * _silu(proj_ref[blk(c, h, 3)])).astype(BF16)
    for h in range(HG_HEADS):
        st_ref[h] = state[h]

    off = 4 * HG_WIDTH
    u = proj_ref[:, off + SC_WIDTH:off + 2 * SC_WIDTH] * proj_ref[:, off + 2 * SC_WIDTH:off + 3 * SC_WIDTH]
    ubuf_ref[HALO:HALO + tile, :] = u
    cw = cw_ref[...]
    conv = (ubuf_ref[HALO - 2:HALO - 2 + tile, :] * cw[0:1] + ubuf_ref[HALO - 1:HALO - 1 + tile, :] * cw[1:2]
            + u * cw[2:3])
    ubuf_ref[0:HALO, :] = u[tile - HALO:tile, :]
    cat_ref[:, HG_WIDTH:] = (proj_ref[:, off:off + SC_WIDTH] * conv).astype(BF16)

    y = jnp.dot(cat_ref[...], wout_ref[...], preferred_element_type=F32)
    o_ref[0] = x + gt_ref[0] * y


def _mixer0(x, sh, sc, gt, norm_g, w_in, w_out, lower, onorm_g, sconv_w):
    bsz, seq, d = x.shape
    t = TOK_TILE
    pm = jnp.asarray(_hgrn_decay_matrix(), BF16)
    vec = pl.BlockSpec((1, 1, d), lambda b, i: (b, 0, 0))
    return pl.pallas_call(
        _mixer0_kernel,
        grid=(bsz, seq // t),
        in_specs=[pl.BlockSpec((1, t, d), lambda b, i: (b, i, 0)), vec, vec, vec,
                  _const_spec((1, d)), _const_spec((d, AB_IN)), _const_spec((d, d)),
                  _const_spec((1, HG_WIDTH)), _const_spec((1, HG_DK)), _const_spec((3, SC_WIDTH)),
                  _const_spec(pm.shape)],
        out_specs=pl.BlockSpec((1, t, d), lambda b, i: (b, i, 0)),
        out_shape=jax.ShapeDtypeStruct(x.shape, F32),
        scratch_shapes=[pltpu.VMEM((t, AB_IN), F32), pltpu.VMEM((t, d), BF16),
                        pltpu.VMEM((HG_HEADS, HG_DK, HG_DK), F32), pltpu.VMEM((t + HALO, SC_WIDTH), F32),
                        pltpu.VMEM((t // HG_CHUNK * HG_HEADS, pm.shape[0], HG_DK), F32)],
        compiler_params=_params(("arbitrary", "arbitrary")),
        name="mixer_hgrn_sconv",
    )(x, sh, sc, gt, norm_g.reshape(1, d), w_in.astype(BF16), w_out.astype(BF16),
      lower.reshape(1, HG_WIDTH), onorm_g.reshape(1, HG_DK), sconv_w, pm)


MXU_DIM = 256
FF_CHUNKS = ((0, 5 * MXU_DIM), (5 * MXU_DIM, D_FF))


def _ffn_kernel(*refs, has_attn, final):
    if has_attn:
        (x_ref, olo_ref, ohi_ref, wo_ref, g1_ref, sh_ref, sc_ref, g2_ref, ng_ref, wup_ref, cw_ref, wdn_ref, fg_ref,
         o_ref, gbuf_ref, carry_ref) = refs
    else:
        (x_ref, sh_ref, sc_ref, g2_ref, ng_ref, wup_ref, cw_ref, wdn_ref, fg_ref,
         o_ref, gbuf_ref, carry_ref) = refs
    tile = x_ref.shape[1]

    @pl.when(pl.program_id(1) == 0)
    def _():
        carry_ref[...] = jnp.zeros_like(carry_ref)

    x = x_ref[0]
    if has_attn:
        o_t = jnp.where(pl.program_id(1) < pl.num_programs(1) // 2, olo_ref[0, 0], ohi_ref[0, 0])
        x = x + g1_ref[0] * lax.dot_general(o_t, wo_ref[...], _TN, preferred_element_type=F32)
    hf = _norm_mod(x, ng_ref[...], sc_ref[0], sh_ref[0]).astype(BF16)
    gbuf_ref[0:HALO, :] = carry_ref[...]
    gates, ups = [], []
    for c0, c1 in FF_CHUNKS:
        gate = jnp.dot(hf, wup_ref[:, c0:c1], preferred_element_type=F32)
        ups.append(jnp.dot(hf, wup_ref[:, D_FF + c0:D_FF + c1], preferred_element_type=F32))
        gbuf_ref[HALO:HALO + tile, c0:c1] = gate
        carry_ref[:, c0:c1] = gate[tile - HALO:tile, :]
        gates.append(gate)
    acc = jnp.zeros((tile, D_MODEL), F32)
    for j, (c0, c1) in enumerate(FF_CHUNKS):
        cols = slice(c0, c1)
        cw = cw_ref[:, cols]
        conv = (gbuf_ref[HALO - 2:HALO - 2 + tile, cols] * cw[0:1] + gbuf_ref[HALO - 1:HALO - 1 + tile, cols] * cw[1:2]
                + gates[j] * cw[2:3])
        act = (_silu(conv) * ups[j]).astype(BF16)
        acc = acc + jnp.dot(act, wdn_ref[cols, :], preferred_element_type=F32)
    out = x + g2_ref[0] * acc
    if final:
        out = out * lax.rsqrt(jnp.mean(out * out, axis=-1, keepdims=True) + EPS) * fg_ref[...]
    o_ref[0] = out


def _conv_ffn(x, sh, sc, g2, norm_g, w_up, conv_w, w_down, final_g, attn=None, final=False):
    bsz, seq, d = x.shape
    t = TOK_TILE
    vec = pl.BlockSpec((1, 1, d), lambda b, i: (b, 0, 0))
    args, specs = [x], [pl.BlockSpec((1, t, d), lambda b, i: (b, i, 0))]
    if attn is not None:
        o_lo, o_hi, w_o, g1 = attn
        top = o_lo.shape[1] - 1
        last = seq // t - 1
        args += [o_lo, o_hi, w_o.astype(BF16), g1]
        specs += [pl.BlockSpec((1, 1, d, t), lambda b, i: (b, jnp.minimum(i, top), 0, 0)),
                  pl.BlockSpec((1, 1, d, t), lambda b, i: (b, jnp.minimum(last - i, top), 0, 0)),
                  _const_spec((d, d)), vec]
    args += [sh, sc, g2, norm_g.reshape(1, d), w_up.astype(BF16), conv_w, w_down.astype(BF16),
             final_g.reshape(1, d)]
    specs += [vec, vec, vec, _const_spec((1, d)), _const_spec((d, 2 * D_FF)), _const_spec((3, D_FF)),
              _const_spec((D_FF, d)), _const_spec((1, d))]
    return pl.pallas_call(
        functools.partial(_ffn_kernel, has_attn=attn is not None, final=final),
        grid=(bsz, seq // t),
        in_specs=specs,
        out_specs=pl.BlockSpec((1, t, d), lambda b, i: (b, i, 0)),
        out_shape=jax.ShapeDtypeStruct(x.shape, F32),
        scratch_shapes=[pltpu.VMEM((t + HALO, D_FF), F32), pltpu.VMEM((HALO, D_FF), F32)],
        compiler_params=_params(("arbitrary", "arbitrary")),
        name="conv_ffn_attnproj" if attn is not None else "conv_ffn",
    )(*args)


KV_ROWS = NSA_KV * NSA_DH


def _nsa_proj_kernel(x_ref, sh_ref, sc_ref, ng_ref, wa_ref, wq_ref, wg_ref, wv_ref, hot_ref,
                     kc_ref, vc_ref, ks_ref, kw_ref, qt_ref, gt_ref, vst_ref, vwt_ref):
    hm = _norm_mod(x_ref[0], ng_ref[...], sc_ref[0], sh_ref[0]).astype(BF16)
    kv = jnp.dot(hm, wa_ref[...], preferred_element_type=F32).astype(BF16)
    for g in range(NSA_KV):
        part = lambda j: kv[:, j * KV_ROWS + g * NSA_DH:j * KV_ROWS + (g + 1) * NSA_DH]
        kc_ref[0, g] = part(0)
        vc_ref[0, g] = part(1)
        ks_ref[0, g] = jnp.concatenate([part(2), hot_ref[...]], axis=1)
        kw_ref[0, g] = part(3)
    qt = lax.dot_general(wq_ref[...], hm, _NT, preferred_element_type=F32)
    qt_ref[0, 0] = (qt * (NSA_DH ** -0.5 * LOG2E)).astype(BF16)
    gt_ref[0] = _sigmoid(lax.dot_general(wg_ref[...], hm, _NT, preferred_element_type=F32))
    vt = lax.dot_general(wv_ref[...], hm, _NT, preferred_element_type=F32).astype(BF16)
    vst_ref[0, 0] = vt[:KV_ROWS]
    vwt_ref[0, 0] = vt[KV_ROWS:]


def _nsa_proj(x, sh, sc, norm_g, w_in):
    bsz, seq, d = x.shape
    t = TOK_TILE
    nt = seq // t
    nq, ng = NSA_HEADS * NSA_DH, 3 * NSA_HEADS
    o = nq + ng
    w = w_in.astype(BF16)
    col = lambda j: w[:, o + j * KV_ROWS:o + (j + 1) * KV_ROWS]
    w_a = jnp.concatenate([col(0), col(1), col(2), col(4)], axis=1)
    w_q = w[:, :nq].T
    w_g = w[:, nq:o].T
    w_v = jnp.concatenate([col(3), col(5)], axis=1).T
    hot = (np.arange(t)[:, None] // SEL_BLOCK == np.arange(KEY_LANES - NSA_DH)[None, :]).astype(np.float32)
    vec = pl.BlockSpec((1, 1, d), lambda b, i: (b, 0, 0))
    tiled = lambda r: pl.BlockSpec((1, 1, r, t), lambda b, i: (b, i, 0, 0))
    grouped = lambda n: pl.BlockSpec((1, NSA_KV, t, n), lambda b, i: (b, 0, i, 0))
    rows = lambda n: jax.ShapeDtypeStruct((bsz, NSA_KV, seq, n), BF16)
    return pl.pallas_call(
        _nsa_proj_kernel,
        grid=(bsz, nt),
        in_specs=[pl.BlockSpec((1, t, d), lambda b, i: (b, i, 0)), vec, vec, _const_spec((1, d)),
                  _const_spec((d, 4 * KV_ROWS)), _const_spec((nq, d)), _const_spec((ng, d)),
                  _const_spec((2 * KV_ROWS, d)), _const_spec(hot.shape)],
        out_specs=[grouped(NSA_DH), grouped(NSA_DH), grouped(KEY_LANES), grouped(NSA_DH), tiled(nq),
                   pl.BlockSpec((1, ng, t), lambda b, i: (b, 0, i)), tiled(KV_ROWS), tiled(KV_ROWS)],
        out_shape=[rows(NSA_DH), rows(NSA_DH), rows(KEY_LANES), rows(NSA_DH),
                   jax.ShapeDtypeStruct((bsz, nt, nq, t), BF16),
                   jax.ShapeDtypeStruct((bsz, ng, seq), F32),
                   jax.ShapeDtypeStruct((bsz, nt, KV_ROWS, t), BF16),
                   jax.ShapeDtypeStruct((bsz, nt, KV_ROWS, t), BF16)],
        compiler_params=_params(("arbitrary", "arbitrary")),
        name="nsa_in_proj",
    )(x, sh, sc, norm_g.reshape(1, d), w_a, w_q, w_g, w_v, jnp.asarray(hot, BF16))


def _compress_kernel(hk_ref, hv_ref, pk_ref, pv_ref, w1k_ref, w2k_ref, w1v_ref, w2vt_ref, kc_ref, vct_ref):
    half = CMP_STRIDE * NSA_DH

    def hidden(h_ref, pos_ref, w1_ref):
        h = h_ref[0, 0]
        n = h.shape[0]
        top = jnp.dot(h, w1_ref[:half, :], preferred_element_type=F32)
        bot = jnp.dot(h, w1_ref[half:, :], preferred_element_type=F32)
        pos = jnp.dot(jnp.broadcast_to(pos_ref[...], (8, 2 * half)).astype(BF16), w1_ref[...],
                      preferred_element_type=F32)[0:1]
        return _silu(top + pltpu.roll(bot, n - 1, 0) + pos).astype(BF16)

    kc_ref[0, 0] = jnp.dot(hidden(hk_ref, pk_ref, w1k_ref), w2k_ref[...],
                           preferred_element_type=F32).astype(BF16)
    vct_ref[0, 0] = lax.dot_general(w2vt_ref[...], hidden(hv_ref, pv_ref, w1v_ref), _NT,
                                    preferred_element_type=F32).astype(BF16)


def _compress(hk, hv, pos_k, pos_v, w1_k, w2_k, w1_v, w2_v):
    bsz, g, n, width = hk.shape
    blk = pl.BlockSpec((1, 1, n, width), lambda b, j: (b, j, 0, 0))
    return pl.pallas_call(
        _compress_kernel,
        grid=(bsz, g),
        in_specs=[blk, blk, _const_spec((1, width * 2)), _const_spec((1, width * 2)),
                  _const_spec((2 * width, CMP_HIDDEN)), _const_spec((CMP_HIDDEN, NSA_DH)),
                  _const_spec((2 * width, CMP_HIDDEN)), _const_spec((NSA_DH, CMP_HIDDEN))],
        out_specs=[pl.BlockSpec((1, 1, n, NSA_DH), lambda b, j: (b, j, 0, 0)),
                   pl.BlockSpec((1, 1, NSA_DH, n), lambda b, j: (b, j, 0, 0))],
        out_shape=[jax.ShapeDtypeStruct((bsz, g, n, NSA_DH), BF16),
                   jax.ShapeDtypeStruct((bsz, g, NSA_DH, n), BF16)],
        compiler_params=_params(("arbitrary", "arbitrary")),
        name="nsa_compress",
    )(hk, hv, pos_k.reshape(1, -1), pos_v.reshape(1, -1), w1_k.astype(BF16), w2_k.astype(BF16),
      w1_v.astype(BF16), w2_v.T.astype(BF16))


def _bucket_upper_bounds():
    n = np.arange(4 * REL_MAX_DIST, dtype=np.float64)
    exact = REL_BUCKETS // 2
    large = exact + (np.log(np.maximum(n, exact) / exact) / math.log(REL_MAX_DIST / exact)
                     * (REL_BUCKETS - exact)).astype(np.int64)
    bucket = np.where(n < exact, n.astype(np.int64), np.minimum(large, REL_BUCKETS - 1))
    return [int(np.max(np.nonzero(bucket <= j)[0])) for j in range(REL_BUCKETS - 1)]


def _bias_of_dist(dist, rb_ref, heads, uppers):
    vals = [jnp.full(dist.shape, rb_ref[REL_BUCKETS - 1, h], F32) for h in heads]
    for j in range(REL_BUCKETS - 2, -1, -1):
        m = dist <= uppers[j]
        vals = [jnp.where(m, rb_ref[j, h], v) for h, v in zip(heads, vals)]
    return vals


def _bias_strip(dist_min, dist_max, dist, rb_ref, heads, uppers, store, limit=None):
    is_const = (dist_min > uppers[-1]) | (dist_max < 0)
    if limit is not None:
        is_const = is_const | (dist_min >= limit)

    @pl.when(is_const)
    def _():
        store([jnp.full(dist.shape, rb_ref[REL_BUCKETS - 1, h], F32) for h in heads])

    @pl.when(jnp.logical_not(is_const))
    def _():
        store(_bias_of_dist(dist, rb_ref, heads, uppers))


def _bias_near_kernel(rb_ref, o_ref, *, uppers):
    d = pl.program_id(0)
    t = o_ref.shape[2]
    heads = list(range(NSA_HEADS))
    is_win = d >= SEL_TABLES
    delta = jnp.where(is_win, d - SEL_TABLES, d)
    limit = jnp.where(is_win, WINDOW, (SEL_TABLES + 1) * t)

    def strip(i, carry):
        r0 = pl.multiple_of(i * 8, 8)
        key = r0 + lax.broadcasted_iota(jnp.int32, (8, t), 0)
        tok = lax.broadcasted_iota(jnp.int32, (8, t), 1)
        dist = delta * t + tok - key
        valid = (dist >= 0) & (dist < limit)

        def store(vals):
            for h in heads:
                o_ref[h, 0, pl.ds(r0, 8), :] = jnp.where(valid, vals[h], NEG)

        _bias_strip(delta * t - (r0 + 7), delta * t + (t - 1) - r0, dist, rb_ref, heads, uppers, store, limit)
        return carry

    lax.fori_loop(0, t // 8, strip, 0)


def _bias_cmp_kernel(rb_ref, o_ref, *, uppers):
    tb = pl.program_id(0)
    n, t = o_ref.shape[1], o_ref.shape[2]
    heads = list(range(NSA_HEADS))

    def strip(i, carry):
        r0 = pl.multiple_of(i * 8, 8)
        blk = r0 + lax.broadcasted_iota(jnp.int32, (8, t), 0)
        tok = tb * t + lax.broadcasted_iota(jnp.int32, (8, t), 1)
        dist = tok - (blk * CMP_STRIDE + CMP_LEN - 1)

        def store(vals):
            for h in heads:
                o_ref[h, pl.ds(r0, 8), :] = jnp.where(dist >= 0, vals[h], NEG)

        last = CMP_LEN - 1
        _bias_strip(tb * t - ((r0 + 7) * CMP_STRIDE + last), tb * t + (t - 1) - (r0 * CMP_STRIDE + last), dist,
                    rb_ref, heads, uppers, store)
        return carry

    lax.fori_loop(0, n // 8, strip, 0)


def _bias_tables(rel_bias, seq):
    t = TOK_TILE
    n_cmp = seq // CMP_STRIDE
    uppers = _bucket_upper_bounds()
    smem = pl.BlockSpec(memory_space=pltpu.SMEM)
    near = pl.pallas_call(
        functools.partial(_bias_near_kernel, uppers=uppers),
        grid=(SEL_TABLES + WIN_TILES,),
        in_specs=[smem],
        out_specs=pl.BlockSpec((NSA_HEADS, 1, t, t), lambda d: (0, d, 0, 0)),
        out_shape=jax.ShapeDtypeStruct((NSA_HEADS, SEL_TABLES + WIN_TILES, t, t), F32),
        compiler_params=_params(("arbitrary",)),
        name="relbias_near",
    )(rel_bias)
    cmp_b = pl.pallas_call(
        functools.partial(_bias_cmp_kernel, uppers=uppers),
        grid=(seq // t,),
        in_specs=[smem],
        out_specs=pl.BlockSpec((NSA_HEADS, n_cmp, t), lambda i: (0, 0, i)),
        out_shape=jax.ShapeDtypeStruct((NSA_HEADS, n_cmp, seq), F32),
        compiler_params=_params(("arbitrary",)),
        name="relbias_cmp",
    )(rel_bias)
    return near, cmp_b


def _cmp_to_sel_t(seq):
    n_cmp, n_sel = seq // CMP_STRIDE, seq // SEL_BLOCK
    c_start = np.arange(n_cmp)[:, None] * CMP_STRIDE
    s_start = np.arange(n_sel)[None, :] * SEL_BLOCK
    inside = np.clip(np.minimum(c_start + CMP_LEN, s_start + SEL_BLOCK) - np.maximum(c_start, s_start), 0, None)
    return (inside / CMP_LEN).T.astype(np.float32)


def _cmp_topk_kernel(qt_ref, kc_ref, vct_ref, bias_ref, c2s_ref, oc_ref, sel_ref, score_scr, cnt_scr):
    tb = pl.program_id(1)
    n, t = bias_ref.shape[1], bias_ref.shape[2]
    n_sel = c2s_ref.shape[0]
    nb = qt_ref.shape[0]
    tok = tb * t + lax.broadcasted_iota(jnp.int32, (1, t), 1)
    any_visible = tok >= CMP_LEN - 1
    ones = (lax.broadcasted_iota(jnp.int32, (SEL_ROWS, n), 0) == 0).astype(BF16)
    head_rows = lambda i: slice(i * NSA_DH, (i + 1) * NSA_DH)
    weights = {}
    for bb in range(nb):
        kc = kc_ref[bb, 0]
        for i in range(NSA_HPG):
            s = jnp.dot(kc, qt_ref[bb, 0, head_rows(i), :], preferred_element_type=F32) + bias_ref[i]
            m = jnp.max(_fold(s, jnp.max), axis=0, keepdims=True)
            weights[bb, i] = jnp.exp2(s - m).astype(BF16)
    imps = []
    for bb in range(nb):
        lhs = jnp.concatenate([vct_ref[bb, 0], ones, c2s_ref[...]], axis=0)
        imp = jnp.zeros((n_sel, t), F32)
        for i in range(NSA_HPG):
            r = jnp.dot(lhs, weights[bb, i], preferred_element_type=F32)
            scale = jnp.where(any_visible, 1.0 / r[NSA_DH:NSA_DH + 1], 0.0)
            oc_ref[bb, 0, head_rows(i), :] = (r[:NSA_DH] * scale).astype(BF16)
            imp = imp + r[NSA_DH + SEL_ROWS:] * scale
        imps.append(imp)

    w = nb * t
    blk = lax.broadcasted_iota(jnp.int32, (n_sel, t), 0)
    cur = (tb * t + lax.broadcasted_iota(jnp.int32, (n_sel, t), 1)) // SEL_BLOCK
    forced = (blk == 0) | (blk == cur) | (blk == cur - 1)
    for bb in range(nb):
        score_scr[:, bb * t:(bb + 1) * t] = jnp.where(blk > cur, -jnp.inf, jnp.where(forced, jnp.inf, imps[bb]))
    groups = n_sel // 8
    per_tile = t // SEL_BLOCK
    sub = lax.broadcasted_iota(jnp.int32, (8, w), 0)
    cnt_scr[...] = jnp.zeros_like(cnt_scr)
    for j in range(groups):
        @pl.when(8 * j < (tb + 1) * per_tile)
        def _():
            src = [jnp.broadcast_to(score_scr[sp:sp + 1, :], (8, w)) for sp in range(8 * j, 8 * j + 8)]
            for v in range(groups):
                part = score_scr[8 * v:8 * v + 8, :]
                cnt = cnt_scr[8 * v:8 * v + 8, :]
                for sp, r in zip(range(8 * j, 8 * j + 8), src):
                    if j < v:
                        beats = r >= part
                    elif j > v:
                        beats = r > part
                    else:
                        beats = (r > part) | ((r == part) & (sub > sp - 8 * v))
                    cnt = cnt + jnp.where(beats, 1.0, 0.0)
                cnt_scr[8 * v:8 * v + 8, :] = cnt
    pad = jnp.zeros((SEL_ROWS - per_tile, t), F32)
    for v in range(groups):
        mask = jnp.where(cnt_scr[8 * v:8 * v + 8, :] < min(SEL_TOPN, n_sel), 0.0, NEG)
        for bb in range(nb):
            for r0 in range(0, 8, per_tile):
                kt = (8 * v + r0) // per_tile
                rows = mask[r0:r0 + per_tile, bb * t:(bb + 1) * t]
                sel_ref[bb, 0, 0, kt] = jnp.concatenate([rows, pad], axis=0).astype(BF16)


def _cmp_topk(q_t, k_cmp, v_cmp_t, bias_cmp, seq):
    bsz, nt, nq, t = q_t.shape
    n_cmp, n_sel = seq // CMP_STRIDE, seq // SEL_BLOCK
    grp = NSA_HPG * NSA_DH
    nb = CMP_BATCH if bsz % CMP_BATCH == 0 else 1
    c2s = jnp.asarray(_cmp_to_sel_t(seq), BF16)
    return pl.pallas_call(
        _cmp_topk_kernel,
        grid=(NSA_KV, nt, bsz // nb),
        in_specs=[pl.BlockSpec((nb, 1, grp, t), lambda g, i, b: (b, i, g, 0)),
                  pl.BlockSpec((nb, 1, n_cmp, NSA_DH), lambda g, i, b: (b, g, 0, 0)),
                  pl.BlockSpec((nb, 1, NSA_DH, n_cmp), lambda g, i, b: (b, g, 0, 0)),
                  pl.BlockSpec((NSA_HPG, n_cmp, t), lambda g, i, b: (g, 0, i)),
                  _const_spec((n_sel, n_cmp))],
        out_specs=[pl.BlockSpec((nb, 1, grp, t), lambda g, i, b: (b, i, g, 0)),
                   pl.BlockSpec((nb, 1, 1, nt, SEL_ROWS, t), lambda g, i, b: (b, g, i, 0, 0, 0))],
        out_shape=[jax.ShapeDtypeStruct((bsz, nt, nq, t), BF16),
                   jax.ShapeDtypeStruct((bsz, NSA_KV, nt, nt, SEL_ROWS, t), BF16)],
        scratch_shapes=[pltpu.VMEM((n_sel, nb * t), F32), pltpu.VMEM((n_sel, nb * t), F32)],
        compiler_params=_params(("arbitrary", "arbitrary", "arbitrary")),
        name="nsa_cmp_topk",
    )(q_t, k_cmp, v_cmp_t, bias_cmp, c2s)


def _sel_win_kernel(qa_ref, qb_ref, ks_ref, vst_ref, kw_ref, vwt_ref, sela_ref, selb_ref, tab_ref, ga_ref, gb_ref,
                    oca_ref, ocb_ref, olo_ref, ohi_ref):
    h0 = pl.program_id(0) * SW_HEADS
    i = pl.program_id(2)
    nt, t = vst_ref.shape[1], vst_ref.shape[3]
    half = nt // 2
    tb = (i, nt - 1 - i)
    head_rows = lambda hh: slice(hh * NSA_DH, (hh + 1) * NSA_DH)
    q = [(qa_ref[0, 0, head_rows(hh), :], qb_ref[0, 0, head_rows(hh), :]) for hh in range(SW_HEADS)]
    sel = (sela_ref, selb_ref)

    def tile_rows(kt):
        return pl.ds(pl.multiple_of(kt * t, t), t)

    def pick(on_a, xa, xb):
        if isinstance(on_a, bool):
            return xa if on_a else xb
        return jnp.where(on_a, xa, xb)

    work = []
    for p in range(nt + 1):
        on_a = True if p == 0 else (False if p >= half else p <= i)
        kt = pick(on_a, p, jnp.maximum(p - i - 1, 0))
        work.append((on_a, kt, True, nt - p if on_a is False else pick(on_a, tb[0] - kt, nt - p)))
    for side in range(2):
        for delta in range(WIN_TILES):
            keep = True if (side == 1 and half >= WIN_TILES) else tb[side] >= delta
            work.append((side == 0, jnp.maximum(tb[side] - delta, 0), keep, delta))
    n_selpos = nt + 1

    ones = (lax.broadcasted_iota(jnp.int32, (SEL_ROWS, t), 0) == 0).astype(BF16)
    q_pad = jnp.zeros((KEY_LANES - NSA_DH - SEL_ROWS, t), BF16)
    keys3 = [(br, side, hh) for br in range(2) for side in range(2) for hh in range(SW_HEADS)]
    run_max = {k: jnp.full((1, t), NEG, F32) for k in keys3}
    acc = {k: jnp.zeros((NSA_DH + SEL_ROWS, t), F32) for k in keys3}

    def merge(k, m_tile, part):
        m_new = jnp.maximum(run_max[k], m_tile)
        acc[k] = acc[k] * jnp.exp2(run_max[k] - m_new) + part * jnp.exp2(m_tile - m_new)
        run_max[k] = m_new

    def scores(p, hh):
        on_a, kt, keep, delta = work[p]
        qp = pick(on_a, q[hh][0], q[hh][1])
        if p >= n_selpos:
            s = jnp.dot(kw_ref[0, 0, tile_rows(kt), :], qp, preferred_element_type=F32)
            s = s + tab_ref[hh, SEL_TABLES + delta]
        else:
            rows = pick(on_a, sel[0][0, 0, 0, kt], sel[1][0, 0, 0, kt])
            q_aug = jnp.concatenate([qp, rows, q_pad], axis=0)
            s = jnp.dot(ks_ref[0, 0, tile_rows(kt), :], q_aug, preferred_element_type=F32)
            if isinstance(delta, int) and delta >= SEL_TABLES - 1:
                m_tile = jnp.max(_fold(s, jnp.max), axis=0, keepdims=True)
                return m_tile + tab_ref[hh, SEL_TABLES - 1, 0:1, :], jnp.exp2(s - m_tile).astype(BF16)
            s = s + tab_ref[hh, jnp.minimum(delta, SEL_TABLES - 1)]
        m_tile = jnp.max(_fold(s, jnp.max), axis=0, keepdims=True)
        return m_tile, jnp.exp2(s - m_tile).astype(BF16)

    def values(p, hh, m_tile, pr):
        on_a, kt, keep, delta = work[p]
        br = int(p >= n_selpos)
        vt_ref = vwt_ref if br else vst_ref
        part = jnp.dot(jnp.concatenate([vt_ref[0, kt], ones], axis=0), pr, preferred_element_type=F32)
        if not isinstance(keep, bool):
            m_tile = jnp.where(keep, m_tile, NEG)
        if isinstance(on_a, bool):
            merge((br, int(not on_a), hh), m_tile, part)
        else:
            merge((br, 0, hh), jnp.where(on_a, m_tile, NEG), part)
            merge((br, 1, hh), jnp.where(on_a, NEG, m_tile), part)

    items = [(p, hh) for p in range(len(work)) for hh in range(SW_HEADS)]
    pending = {}
    for step in range(len(items) + MXU_SKEW):
        if step < len(items):
            pending[step] = scores(*items[step])
        if step >= MXU_SKEW:
            values(*items[step - MXU_SKEW], *pending.pop(step - MXU_SKEW))

    for side, (g_ref, oc_ref, o_ref) in enumerate(((ga_ref, oca_ref, olo_ref), (gb_ref, ocb_ref, ohi_ref))):
        for hh in range(SW_HEADS):
            gate = lambda br: g_ref[0, pl.ds(br * NSA_HEADS + h0 + hh, 1), :]
            o_s = acc[(0, side, hh)][:NSA_DH] / acc[(0, side, hh)][NSA_DH:NSA_DH + 1]
            o_w = acc[(1, side, hh)][:NSA_DH] / acc[(1, side, hh)][NSA_DH:NSA_DH + 1]
            o_c = oc_ref[0, 0, head_rows(hh), :].astype(F32)
            o_ref[0, 0, head_rows(hh), :] = (gate(0) * o_c + gate(1) * o_s + gate(2) * o_w).astype(BF16)


def _sel_win(q_t, ks, vs_t, kw, vw_t, sel, tables, gates_t, oc_t):
    bsz, nt, nq, t = q_t.shape
    seq = nt * t
    last = nt - 1
    per_group = NSA_HPG // SW_HEADS
    head_a = pl.BlockSpec((1, 1, SW_HEADS * NSA_DH, t), lambda h, b, i: (b, i, h, 0))
    head_b = pl.BlockSpec((1, 1, SW_HEADS * NSA_DH, t), lambda h, b, i: (b, last - i, h, 0))
    keys = pl.BlockSpec((1, 1, seq, NSA_DH), lambda h, b, i: (b, h // per_group, 0, 0))
    keys_sel = pl.BlockSpec((1, 1, seq, KEY_LANES), lambda h, b, i: (b, h // per_group, 0, 0))
    vals = pl.BlockSpec((1, nt, NSA_DH, t), lambda h, b, i: (b, 0, h // per_group, 0))
    sel_a = pl.BlockSpec((1, 1, 1, nt, SEL_ROWS, t), lambda h, b, i: (b, h // per_group, i, 0, 0, 0))
    sel_b = pl.BlockSpec((1, 1, 1, nt, SEL_ROWS, t), lambda h, b, i: (b, h // per_group, last - i, 0, 0, 0))
    gate_a = pl.BlockSpec((1, 3 * NSA_HEADS, t), lambda h, b, i: (b, 0, i))
    gate_b = pl.BlockSpec((1, 3 * NSA_HEADS, t), lambda h, b, i: (b, 0, last - i))
    out = jax.ShapeDtypeStruct((bsz, nt // 2, nq, t), BF16)
    return pl.pallas_call(
        _sel_win_kernel,
        grid=(NSA_HEADS // SW_HEADS, bsz, nt // 2),
        in_specs=[head_a, head_b, keys_sel, vals, keys, vals, sel_a, sel_b,
                  pl.BlockSpec((SW_HEADS, SEL_TABLES + WIN_TILES, t, t), lambda h, b, i: (h, 0, 0, 0)),
                  gate_a, gate_b, head_a, head_b],
        out_specs=[head_a, head_a],
        out_shape=[out, out],
        compiler_params=_params(("arbitrary", "arbitrary", "arbitrary")),
        name="nsa_sel_win",
    )(q_t, q_t, ks, vs_t, kw, vw_t, sel, sel, tables, gates_t, gates_t, oc_t, oc_t)


def kernel(x, c, mod_w, mod_b, norm_mix_g, norm_ffn_g, ab_w_in, ab_w_out, hgrn_lb_logits, hgrn_onorm_g, sconv_w, nsa_w_in, nsa_w_out, nsa_cmp_pos_k, nsa_cmp_pos_v, nsa_cmp_w1_k, nsa_cmp_w2_k, nsa_cmp_w1_v, nsa_cmp_w2_v, rel_bias, ffn_w_up, ffn_conv_w, ffn_w_down, final_norm_g):
    bsz, seq, d = x.shape
    assert d == D_MODEL and seq % TOK_TILE == 0
    lower = jnp.cumsum(jax.nn.softmax(hgrn_lb_logits.astype(F32), axis=0), axis=0)
    mod = _modulation(c, mod_w, mod_b)
    parts = lambda l: [mod[l, :, j * d:(j + 1) * d].reshape(bsz, 1, d) for j in range(6)]

    sh1, sc1, g1, sh2, sc2, g2 = parts(0)
    x = _mixer0(x, sh1, sc1, g1, norm_mix_g[0], ab_w_in[0], ab_w_out[0], lower[0], hgrn_onorm_g[0], sconv_w[0])
    x = _conv_ffn(x, sh2, sc2, g2, norm_ffn_g[0], ffn_w_up[0], ffn_conv_w[0], ffn_w_down[0], final_norm_g)

    sh1, sc1, g1, sh2, sc2, g2 = parts(1)
    kc, vc, ks, kw, q_t, gates_t, vs_t, vw_t = _nsa_proj(x, sh1, sc1, norm_mix_g[1], nsa_w_in[0])
    to_blocks = lambda a: a.reshape(bsz, NSA_KV, seq // CMP_STRIDE, CMP_STRIDE * NSA_DH)
    k_cmp, v_cmp_t = _compress(to_blocks(kc), to_blocks(vc), nsa_cmp_pos_k[0], nsa_cmp_pos_v[0],
                               nsa_cmp_w1_k[0], nsa_cmp_w2_k[0], nsa_cmp_w1_v[0], nsa_cmp_w2_v[0])
    near, bias_cmp = _bias_tables(rel_bias.astype(F32) * LOG2E, seq)
    oc_t, sel = _cmp_topk(q_t, k_cmp, v_cmp_t, bias_cmp, seq)
    o_lo, o_hi = _sel_win(q_t, ks, vs_t, kw, vw_t, sel, near, gates_t, oc_t)
    return _conv_ffn(x, sh2, sc2, g2, norm_ffn_g[1], ffn_w_up[1], ffn_conv_w[1], ffn_w_down[1], final_norm_g,
                     attn=(o_lo, o_hi, nsa_w_out[0], g1), final=True)
```

```python
import functools
import math

import jax
import jax.numpy as jnp
import numpy as np
from jax import lax
from jax.experimental import pallas as pl
from jax.experimental.pallas import tpu as pltpu

F32 = jnp.float32
BF16 = jnp.bfloat16

EPS = 1e-6
D_MODEL = 1024
D_FF = 2816

HG_HEADS = 4
HG_DK = 128
HG_WIDTH = HG_HEADS * HG_DK
SC_WIDTH = D_MODEL - HG_WIDTH
AB_IN = 4 * HG_WIDTH + 3 * SC_WIDTH
HG_CHUNK = 64
HG_LEVELS = 6
HG_MXU_LEVELS = ()

NSA_HEADS = 16
NSA_KV = 4
NSA_HPG = NSA_HEADS // NSA_KV
NSA_DH = D_MODEL // NSA_HEADS
CMP_LEN = 32
CMP_STRIDE = 16
CMP_HIDDEN = 256
SEL_BLOCK = 64
SEL_TOPN = 16
WINDOW = 512
REL_BUCKETS = 32
REL_MAX_DIST = 1024

TOK_TILE = 256
NEAR_TILES = 5
SEL_TABLES = NEAR_TILES + 1
WIN_TILES = WINDOW // TOK_TILE + 1
NEG = -1e30
LOG2E = math.log2(math.e)
KEY_LANES = 128
SEL_ROWS = 16
CMP_BATCH = 2
SW_HEADS = 4
MXU_SKEW = 5
VMEM_LIMIT = 56 * 1024 * 1024
HALO = 8

_NT = (((1,), (1,)), ((), ()))
_TN = (((0,), (0,)), ((), ()))


def _sigmoid(x):
    return 1.0 / (1.0 + jnp.exp(-x))


def _silu(x):
    return x * _sigmoid(x)


def _norm_mod(x, g, sc, sh):
    ms = jnp.mean(x * x, axis=-1, keepdims=True)
    return (x * lax.rsqrt(ms + EPS) * g) * (1.0 + sc) + sh


def _fold(x, op):
    return op(x.reshape(x.shape[0] // 8, 8, x.shape[1]), axis=0)


def _const_spec(shape):
    n = len(shape)
    return pl.BlockSpec(shape, lambda *_: (0,) * n, pipeline_mode=pl.Buffered(1))


def _params(sem):
    return pltpu.CompilerParams(dimension_semantics=sem, vmem_limit_bytes=VMEM_LIMIT)


def _mod_kernel(c_ref, w_ref, b_ref, o_ref):
    c = c_ref[...]
    o_ref[0] = jnp.dot(_silu(c), w_ref[0], preferred_element_type=F32,
                       precision=lax.Precision.HIGHEST) + b_ref[0]


def _modulation(c, mod_w, mod_b):
    depth, d, n = mod_w.shape
    bsz = c.shape[0]
    tn = 1024
    return pl.pallas_call(
        _mod_kernel,
        grid=(depth, n // tn),
        in_specs=[pl.BlockSpec((bsz, d), lambda l, j: (0, 0)),
                  pl.BlockSpec((1, d, tn), lambda l, j: (l, 0, j)),
                  pl.BlockSpec((1, 1, tn), lambda l, j: (l, 0, j))],
        out_specs=pl.BlockSpec((1, bsz, tn), lambda l, j: (l, 0, j)),
        out_shape=jax.ShapeDtypeStruct((depth, bsz, n), F32),
        compiler_params=_params(("arbitrary", "arbitrary")),
        name="adaln_mod",
    )(c, mod_w, mod_b.reshape(depth, 1, n))


def _hgrn_decay_matrix():
    t = np.arange(HG_CHUNK)
    tril = (t[None, :] <= t[:, None]).astype(np.float32)
    mats = [tril]
    for lvl in HG_MXU_LEVELS:
        m = 1 << (lvl - 1)
        mid = (t // (2 * m)) * (2 * m) + m - 1
        mats.append(tril - tril[mid])
    return np.concatenate(mats, axis=0)


def _mixer0_kernel(x_ref, sh_ref, sc_ref, gt_ref, ng_ref, win_ref, wout_ref, lb_ref, og_ref, cw_ref,
                   pm_ref, o_ref, proj_ref, cat_ref, st_ref, ubuf_ref, d_scr):
    tile = x_ref.shape[1]
    c64 = HG_CHUNK

    @pl.when(pl.program_id(1) == 0)
    def _():
        st_ref[...] = jnp.zeros_like(st_ref)
        ubuf_ref[0:HALO, :] = jnp.zeros((HALO, SC_WIDTH), F32)

    x = x_ref[0]
    hm = _norm_mod(x, ng_ref[...], sc_ref[0], sh_ref[0]).astype(BF16)
    proj_ref[...] = jnp.dot(hm, win_ref[...], preferred_element_type=F32)

    row = lax.broadcasted_iota(jnp.int32, (c64, HG_DK), 0)
    r64 = lax.broadcasted_iota(jnp.int32, (c64, c64), 0)
    c64i = lax.broadcasted_iota(jnp.int32, (c64, c64), 1)
    second = [None] + [(row & (1 << (lvl - 1))) != 0 for lvl in range(1, HG_LEVELS + 1)]
    same = [None] + [(r64 >> lvl) == (c64i >> lvl) for lvl in range(1, HG_LEVELS + 1)]
    eye = r64 == c64i
    pm = pm_ref[...]
    og = og_ref[...]

    units = [(c, h) for c in range(tile // c64) for h in range(HG_HEADS)]

    def blk(c, h, j):
        return pl.ds(c * c64, c64), pl.ds(j * HG_WIDTH + h * HG_DK, HG_DK)

    for u, (c, h) in enumerate(units):
        lb = lb_ref[:, h * HG_DK:(h + 1) * HG_DK]
        fg = lb + (1.0 - lb) * _sigmoid(proj_ref[blk(c, h, 1)])
        proj_ref[blk(c, h, 1)] = 1.0 - fg
        lg = jnp.log(fg)
        lg_hi = lg.astype(BF16)
        lg_lo = (lg - lg_hi.astype(F32)).astype(BF16)
        dall = jnp.dot(pm, jnp.concatenate([lg_hi, lg_lo], axis=1), preferred_element_type=F32)
        d_scr[u] = dall[:, :HG_DK] + dall[:, HG_DK:]

    intra, q_in, k_out, decay = [], [], [], []
    for u, (c, h) in enumerate(units):
        q = proj_ref[blk(c, h, 0)]
        k = proj_ref[blk(c, h, 1)]
        b = d_scr[u, 0:c64, :]
        a = jnp.zeros((c64, c64), F32)
        for lvl in range(1, HG_LEVELS + 1):
            m = 1 << (lvl - 1)
            if lvl == 1:
                qt = jnp.where(second[lvl], q * (1.0 - k), 0.0).astype(BF16)
                kt = jnp.where(second[lvl], 0.0, k).astype(BF16)
            else:
                if lvl in HG_MXU_LEVELS:
                    at = (1 + HG_MXU_LEVELS.index(lvl)) * c64
                    dlt = d_scr[u, at:at + c64, :]
                else:
                    mids = [jnp.broadcast_to(d_scr[u, r0 + m - 1:r0 + m, :], (2 * m, HG_DK))
                            for r0 in range(0, c64, 2 * m)]
                    dlt = b - (mids[0] if len(mids) == 1 else jnp.concatenate(mids, axis=0))
                e = jnp.exp(-jnp.abs(dlt))
                qt = jnp.where(second[lvl], q * e, 0.0).astype(BF16)
                kt = jnp.where(second[lvl], 0.0, k * e).astype(BF16)
            al = lax.dot_general(qt, kt, _NT, preferred_element_type=F32)
            a = a + (al if lvl == HG_LEVELS else jnp.where(same[lvl], al, 0.0))
        intra.append(jnp.where(eye, jnp.sum(q * k, axis=-1, keepdims=True), a).astype(BF16))
        b_last = d_scr[u, c64 - 1:c64, :]
        q_in.append((q * jnp.exp(b)).astype(BF16))
        k_out.append((k * jnp.exp(b_last - b)).astype(BF16))
        decay.append(jnp.exp(b_last))

    state = [st_ref[h] for h in range(HG_HEADS)]
    for u, (c, h) in enumerate(units):
        vb = proj_ref[blk(c, h, 2)].astype(BF16)
        o = jnp.dot(intra[u], vb, preferred_element_type=F32)
        o = o + lax.dot_general(q_in[u], state[h].astype(BF16), _NT, preferred_element_type=F32)
        state[h] = state[h] * decay[u] + lax.dot_general(vb, k_out[u], _TN, preferred_element_type=F32)
        on = o * lax.rsqrt(jnp.mean(o * o, axis=-1, keepdims=True) + EPS) * og
        rows, _ = blk(c, h, 0)
        cat_ref[rows, h * HG_DK:(h + 1) * HG_DK] = (on * _silu(proj_ref[blk(c, h, 3)])).astype(BF16)
    for h in range(HG_HEADS):
        st_ref[h] = state[h]

    off = 4 * HG_WIDTH
    u = proj_ref[:, off + SC_WIDTH:off + 2 * SC_WIDTH] * proj_ref[:, off + 2 * SC_WIDTH:off + 3 * SC_WIDTH]
    ubuf_ref[HALO:HALO + tile, :] = u
    cw = cw_ref[...]
    conv = (ubuf_ref[HALO - 2:HALO - 2 + tile, :] * cw[0:1] + ubuf_ref[HALO - 1:HALO - 1 + tile, :] * cw[1:2]
            + u * cw[2:3])
    ubuf_ref[0:HALO, :] = u[tile - HALO:tile, :]
    cat_ref[:, HG_WIDTH:] = (proj_ref[:, off:off + SC_WIDTH] * conv).astype(BF16)

    y = jnp.dot(cat_ref[...], wout_ref[...], preferred_element_type=F32)
    o_ref[0] = x + gt_ref[0] * y


def _mixer0(x, sh, sc, gt, norm_g, w_in, w_out, lower, onorm_g, sconv_w):
    bsz, seq, d = x.shape
    t = TOK_TILE
    pm = jnp.asarray(_hgrn_decay_matrix(), BF16)
    vec = pl.BlockSpec((1, 1, d), lambda b, i: (b, 0, 0))
    return pl.pallas_call(
        _mixer0_kernel,
        grid=(bsz, seq // t),
        in_specs=[pl.BlockSpec((1, t, d), lambda b, i: (b, i, 0)), vec, vec, vec,
                  _const_spec((1, d)), _const_spec((d, AB_IN)), _const_spec((d, d)),
                  _const_spec((1, HG_WIDTH)), _const_spec((1, HG_DK)), _const_spec((3, SC_WIDTH)),
                  _const_spec(pm.shape)],
        out_specs=pl.BlockSpec((1, t, d), lambda b, i: (b, i, 0)),
        out_shape=jax.ShapeDtypeStruct(x.shape, F32),
        scratch_shapes=[pltpu.VMEM((t, AB_IN), F32), pltpu.VMEM((t, d), BF16),
                        pltpu.VMEM((HG_HEADS, HG_DK, HG_DK), F32), pltpu.VMEM((t + HALO, SC_WIDTH), F32),
                        pltpu.VMEM((t // HG_CHUNK * HG_HEADS, pm.shape[0], HG_DK), F32)],
        compiler_params=_params(("arbitrary", "arbitrary")),
        name="mixer_hgrn_sconv",
    )(x, sh, sc, gt, norm_g.reshape(1, d), w_in.astype(BF16), w_out.astype(BF16),
      lower.reshape(1, HG_WIDTH), onorm_g.reshape(1, HG_DK), sconv_w, pm)


MXU_DIM = 256
FF_CHUNKS = ((0, 5 * MXU_DIM), (5 * MXU_DIM, D_FF))


def _ffn_kernel(*refs, has_attn, final):
    if has_attn:
        (x_ref, olo_ref, ohi_ref, wo_ref, g1_ref, sh_ref, sc_ref, g2_ref, ng_ref, wup_ref, cw_ref, wdn_ref, fg_ref,
         o_ref, gbuf_ref, carry_ref) = refs
    else:
        (x_ref, sh_ref, sc_ref, g2_ref, ng_ref, wup_ref, cw_ref, wdn_ref, fg_ref,
         o_ref, gbuf_ref, carry_ref) = refs
    tile = x_ref.shape[1]

    @pl.when(pl.program_id(1) == 0)
    def _():
        carry_ref[...] = jnp.zeros_like(carry_ref)

    x = x_ref[0]
    if has_attn:
        o_t = jnp.where(pl.program_id(1) < pl.num_programs(1) // 2, olo_ref[0, 0], ohi_ref[0, 0])
        x = x + g1_ref[0] * lax.dot_general(o_t, wo_ref[...], _TN, preferred_element_type=F32)
    hf = _norm_mod(x, ng_ref[...], sc_ref[0], sh_ref[0]).astype(BF16)
    gbuf_ref[0:HALO, :] = carry_ref[...]
    gates, ups = [], []
    for c0, c1 in FF_CHUNKS:
        gate = jnp.dot(hf, wup_ref[:, c0:c1], preferred_element_type=F32)
        ups.append(jnp.dot(hf, wup_ref[:, D_FF + c0:D_FF + c1], preferred_element_type=F32))
        gbuf_ref[HALO:HALO + tile, c0:c1] = gate
        carry_ref[:, c0:c1] = gate[tile - HALO:tile, :]
        gates.append(gate)
    acc = jnp.zeros((tile, D_MODEL), F32)
    for j, (c0, c1) in enumerate(FF_CHUNKS):
        cols = slice(c0, c1)
        cw = cw_ref[:, cols]
        conv = (gbuf_ref[HALO - 2:HALO - 2 + tile, cols] * cw[0:1] + gbuf_ref[HALO - 1:HALO - 1 + tile, cols] * cw[1:2]
                + gates[j] * cw[2:3])
        act = (_silu(conv) * ups[j]).astype(BF16)
        acc = acc + jnp.dot(act, wdn_ref[cols, :], preferred_element_type=F32)
    out = x + g2_ref[0] * acc
    if final:
        out = out * lax.rsqrt(jnp.mean(out * out, axis=-1, keepdims=True) + EPS) * fg_ref[...]
    o_ref[0] = out


def _conv_ffn(x, sh, sc, g2, norm_g, w_up, conv_w, w_down, final_g, attn=None, final=False):
    bsz, seq, d = x.shape
    t = TOK_TILE
    vec = pl.BlockSpec((1, 1, d), lambda b, i: (b, 0, 0))
    args, specs = [x], [pl.BlockSpec((1, t, d), lambda b, i: (b, i, 0))]
    if attn is not None:
        o_lo, o_hi, w_o, g1 = attn
        top = o_lo.shape[1] - 1
        last = seq // t - 1
        args += [o_lo, o_hi, w_o.astype(BF16), g1]
        specs += [pl.BlockSpec((1, 1, d, t), lambda b, i: (b, jnp.minimum(i, top), 0, 0)),
                  pl.BlockSpec((1, 1, d, t), lambda b, i: (b, jnp.minimum(last - i, top), 0, 0)),
                  _const_spec((d, d)), vec]
    args += [sh, sc, g2, norm_g.reshape(1, d), w_up.astype(BF16), conv_w, w_down.astype(BF16),
             final_g.reshape(1, d)]
    specs += [vec, vec, vec, _const_spec((1, d)), _const_spec((d, 2 * D_FF)), _const_spec((3, D_FF)),
              _const_spec((D_FF, d)), _const_spec((1, d))]
    return pl.pallas_call(
        functools.partial(_ffn_kernel, has_attn=attn is not None, final=final),
        grid=(bsz, seq // t),
        in_specs=specs,
        out_specs=pl.BlockSpec((1, t, d), lambda b, i: (b, i, 0)),
        out_shape=jax.ShapeDtypeStruct(x.shape, F32),
        scratch_shapes=[pltpu.VMEM((t + HALO, D_FF), F32), pltpu.VMEM((HALO, D_FF), F32)],
        compiler_params=_params(("arbitrary", "arbitrary")),
        name="conv_ffn_attnproj" if attn is not None else "conv_ffn",
    )(*args)


KV_ROWS = NSA_KV * NSA_DH


def _nsa_proj_kernel(x_ref, sh_ref, sc_ref, ng_ref, wa_ref, wq_ref, wg_ref, wv_ref, hot_ref, perm_ref,
                     kc_ref, vc_ref, ks_ref, kw_ref, qt_ref, gt_ref, vst_ref, vwt_ref):
    hm = _norm_mod(x_ref[0], ng_ref[...], sc_ref[0], sh_ref[0]).astype(BF16)
    kv = jnp.dot(hm, wa_ref[...], preferred_element_type=F32).astype(BF16)
    cmp_in = jnp.dot(perm_ref[...], kv[:, :2 * KV_ROWS], preferred_element_type=F32).astype(BF16)
    n_grp = perm_ref.shape[0] // CMP_STRIDE
    for g in range(NSA_KV):
        part = lambda j: kv[:, j * KV_ROWS + g * NSA_DH:j * KV_ROWS + (g + 1) * NSA_DH]
        for p in range(CMP_STRIDE):
            rows = slice(p * n_grp, (p + 1) * n_grp)
            kc_ref[0, g, p] = cmp_in[rows, g * NSA_DH:(g + 1) * NSA_DH]
            vc_ref[0, g, p] = cmp_in[rows, KV_ROWS + g * NSA_DH:KV_ROWS + (g + 1) * NSA_DH]
        ks_ref[0, g] = jnp.concatenate([part(2), hot_ref[...]], axis=1)
        kw_ref[0, g] = part(3)
    qt = lax.dot_general(wq_ref[...], hm, _NT, preferred_element_type=F32)
    qt_ref[0, 0] = (qt * (NSA_DH ** -0.5 * LOG2E)).astype(BF16)
    gt_ref[0] = _sigmoid(lax.dot_general(wg_ref[...], hm, _NT, preferred_element_type=F32))
    vt = lax.dot_general(wv_ref[...], hm, _NT, preferred_element_type=F32).astype(BF16)
    vst_ref[0, 0] = vt[:KV_ROWS]
    vwt_ref[0, 0] = vt[KV_ROWS:]


def _nsa_proj(x, sh, sc, norm_g, w_in):
    bsz, seq, d = x.shape
    t = TOK_TILE
    nt = seq // t
    nq, ng = NSA_HEADS * NSA_DH, 3 * NSA_HEADS
    o = nq + ng
    w = w_in.astype(BF16)
    col = lambda j: w[:, o + j * KV_ROWS:o + (j + 1) * KV_ROWS]
    w_a = jnp.concatenate([col(0), col(1), col(2), col(4)], axis=1)
    w_q = w[:, :nq].T
    w_g = w[:, nq:o].T
    w_v = jnp.concatenate([col(3), col(5)], axis=1).T
    hot = (np.arange(t)[:, None] // SEL_BLOCK == np.arange(KEY_LANES - NSA_DH)[None, :]).astype(np.float32)
    n_grp = t // CMP_STRIDE
    r = np.arange(t)
    perm = (r[None, :] == ((r % n_grp) * CMP_STRIDE + r // n_grp)[:, None]).astype(np.float32)
    vec = pl.BlockSpec((1, 1, d), lambda b, i: (b, 0, 0))
    tiled = lambda r: pl.BlockSpec((1, 1, r, t), lambda b, i: (b, i, 0, 0))
    grouped = lambda n: pl.BlockSpec((1, NSA_KV, t, n), lambda b, i: (b, 0, i, 0))
    rows = lambda n: jax.ShapeDtypeStruct((bsz, NSA_KV, seq, n), BF16)
    by_offset = pl.BlockSpec((1, NSA_KV, CMP_STRIDE, n_grp, NSA_DH), lambda b, i: (b, 0, 0, i, 0))
    offset_major = jax.ShapeDtypeStruct((bsz, NSA_KV, CMP_STRIDE, seq // CMP_STRIDE, NSA_DH), BF16)
    return pl.pallas_call(
        _nsa_proj_kernel,
        grid=(bsz, nt),
        in_specs=[pl.BlockSpec((1, t, d), lambda b, i: (b, i, 0)), vec, vec, _const_spec((1, d)),
                  _const_spec((d, 4 * KV_ROWS)), _const_spec((nq, d)), _const_spec((ng, d)),
                  _const_spec((2 * KV_ROWS, d)), _const_spec(hot.shape), _const_spec(perm.shape)],
        out_specs=[by_offset, by_offset, grouped(KEY_LANES), grouped(NSA_DH), tiled(nq),
                   pl.BlockSpec((1, ng, t), lambda b, i: (b, 0, i)), tiled(KV_ROWS), tiled(KV_ROWS)],
        out_shape=[offset_major, offset_major, rows(KEY_LANES), rows(NSA_DH),
                   jax.ShapeDtypeStruct((bsz, nt, nq, t), BF16),
                   jax.ShapeDtypeStruct((bsz, ng, seq), F32),
                   jax.ShapeDtypeStruct((bsz, nt, KV_ROWS, t), BF16),
                   jax.ShapeDtypeStruct((bsz, nt, KV_ROWS, t), BF16)],
        compiler_params=_params(("arbitrary", "arbitrary")),
        name="nsa_in_proj",
    )(x, sh, sc, norm_g.reshape(1, d), w_a, w_q, w_g, w_v, jnp.asarray(hot, BF16), jnp.asarray(perm, BF16))


def _compress_kernel(hk_ref, hv_ref, pk_ref, pv_ref, w1k_ref, w2k_ref, w1v_ref, w2vt_ref, kc_ref, vct_ref):
    half = CMP_STRIDE * NSA_DH

    def hidden(h_ref, pos_ref, w1_ref):
        n = h_ref.shape[3]
        top = jnp.zeros((n, CMP_HIDDEN), F32)
        bot = jnp.zeros((n, CMP_HIDDEN), F32)
        per = MXU_DIM // NSA_DH
        for p0 in range(0, CMP_STRIDE, per):
            h = jnp.concatenate([h_ref[0, 0, p] for p in range(p0, p0 + per)], axis=1)
            rows = slice(p0 * NSA_DH, (p0 + per) * NSA_DH)
            top = top + jnp.dot(h, w1_ref[rows, :], preferred_element_type=F32)
            bot = bot + jnp.dot(h, w1_ref[half + p0 * NSA_DH:half + (p0 + per) * NSA_DH, :],
                                preferred_element_type=F32)
        pos = jnp.dot(jnp.broadcast_to(pos_ref[...], (8, 2 * half)).astype(BF16), w1_ref[...],
                      preferred_element_type=F32)[0:1]
        return _silu(top + pltpu.roll(bot, n - 1, 0) + pos).astype(BF16)

    kc_ref[0, 0] = jnp.dot(hidden(hk_ref, pk_ref, w1k_ref), w2k_ref[...],
                           preferred_element_type=F32).astype(BF16)
    vct_ref[0, 0] = lax.dot_general(w2vt_ref[...], hidden(hv_ref, pv_ref, w1v_ref), _NT,
                                    preferred_element_type=F32).astype(BF16)


def _compress(hk, hv, pos_k, pos_v, w1_k, w2_k, w1_v, w2_v):
    bsz, g, stride, n, dh = hk.shape
    width = stride * dh
    blk = pl.BlockSpec((1, 1, stride, n, dh), lambda b, j: (b, j, 0, 0, 0))
    return pl.pallas_call(
        _compress_kernel,
        grid=(bsz, g),
        in_specs=[blk, blk, _const_spec((1, width * 2)), _const_spec((1, width * 2)),
                  _const_spec((2 * width, CMP_HIDDEN)), _const_spec((CMP_HIDDEN, NSA_DH)),
                  _const_spec((2 * width, CMP_HIDDEN)), _const_spec((NSA_DH, CMP_HIDDEN))],
        out_specs=[pl.BlockSpec((1, 1, n, NSA_DH), lambda b, j: (b, j, 0, 0)),
                   pl.BlockSpec((1, 1, NSA_DH, n), lambda b, j: (b, j, 0, 0))],
        out_shape=[jax.ShapeDtypeStruct((bsz, g, n, NSA_DH), BF16),
                   jax.ShapeDtypeStruct((bsz, g, NSA_DH, n), BF16)],
        compiler_params=_params(("arbitrary", "arbitrary")),
        name="nsa_compress",
    )(hk, hv, pos_k.reshape(1, -1), pos_v.reshape(1, -1), w1_k.astype(BF16), w2_k.astype(BF16),
      w1_v.astype(BF16), w2_v.T.astype(BF16))


def _bucket_upper_bounds():
    n = np.arange(4 * REL_MAX_DIST, dtype=np.float64)
    exact = REL_BUCKETS // 2
    large = exact + (np.log(np.maximum(n, exact) / exact) / math.log(REL_MAX_DIST / exact)
                     * (REL_BUCKETS - exact)).astype(np.int64)
    bucket = np.where(n < exact, n.astype(np.int64), np.minimum(large, REL_BUCKETS - 1))
    return [int(np.max(np.nonzero(bucket <= j)[0])) for j in range(REL_BUCKETS - 1)]


def _bias_of_dist(dist, rb_ref, heads, uppers):
    vals = [jnp.full(dist.shape, rb_ref[REL_BUCKETS - 1, h], F32) for h in heads]
    for j in range(REL_BUCKETS - 2, -1, -1):
        m = dist <= uppers[j]
        vals = [jnp.where(m, rb_ref[j, h], v) for h, v in zip(heads, vals)]
    return vals


def _bias_strip(dist_min, dist_max, dist, rb_ref, heads, uppers, store, limit=None):
    is_const = (dist_min > uppers[-1]) | (dist_max < 0)
    if limit is not None:
        is_const = is_const | (dist_min >= limit)

    @pl.when(is_const)
    def _():
        store([jnp.full(dist.shape, rb_ref[REL_BUCKETS - 1, h], F32) for h in heads])

    @pl.when(jnp.logical_not(is_const))
    def _():
        store(_bias_of_dist(dist, rb_ref, heads, uppers))


def _bias_near_kernel(rb_ref, o_ref, *, uppers):
    d = pl.program_id(0)
    t = o_ref.shape[2]
    heads = list(range(NSA_HEADS))
    is_win = d >= SEL_TABLES
    delta = jnp.where(is_win, d - SEL_TABLES, d)
    limit = jnp.where(is_win, WINDOW, (SEL_TABLES + 1) * t)

    def strip(i, carry):
        r0 = pl.multiple_of(i * 8, 8)
        key = r0 + lax.broadcasted_iota(jnp.int32, (8, t), 0)
        tok = lax.broadcasted_iota(jnp.int32, (8, t), 1)
        dist = delta * t + tok - key
        valid = (dist >= 0) & (dist < limit)

        def store(vals):
            for h in heads:
                o_ref[h, 0, pl.ds(r0, 8), :] = jnp.where(valid, vals[h], NEG)

        _bias_strip(delta * t - (r0 + 7), delta * t + (t - 1) - r0, dist, rb_ref, heads, uppers, store, limit)
        return carry

    lax.fori_loop(0, t // 8, strip, 0)


def _bias_cmp_kernel(rb_ref, o_ref, *, uppers):
    tb = pl.program_id(0)
    n, t = o_ref.shape[1], o_ref.shape[2]
    heads = list(range(NSA_HEADS))

    def strip(i, carry):
        r0 = pl.multiple_of(i * 8, 8)
        blk = r0 + lax.broadcasted_iota(jnp.int32, (8, t), 0)
        tok = tb * t + lax.broadcasted_iota(jnp.int32, (8, t), 1)
        dist = tok - (blk * CMP_STRIDE + CMP_LEN - 1)

        def store(vals):
            for h in heads:
                o_ref[h, pl.ds(r0, 8), :] = jnp.where(dist >= 0, vals[h], NEG)

        last = CMP_LEN - 1
        _bias_strip(tb * t - ((r0 + 7) * CMP_STRIDE + last), tb * t + (t - 1) - (r0 * CMP_STRIDE + last), dist,
                    rb_ref, heads, uppers, store)
        return carry

    lax.fori_loop(0, n // 8, strip, 0)


def _bias_tables(rel_bias, seq):
    t = TOK_TILE
    n_cmp = seq // CMP_STRIDE
    uppers = _bucket_upper_bounds()
    smem = pl.BlockSpec(memory_space=pltpu.SMEM)
    near = pl.pallas_call(
        functools.partial(_bias_near_kernel, uppers=uppers),
        grid=(SEL_TABLES + WIN_TILES,),
        in_specs=[smem],
        out_specs=pl.BlockSpec((NSA_HEADS, 1, t, t), lambda d: (0, d, 0, 0)),
        out_shape=jax.ShapeDtypeStruct((NSA_HEADS, SEL_TABLES + WIN_TILES, t, t), F32),
        compiler_params=_params(("arbitrary",)),
        name="relbias_near",
    )(rel_bias)
    cmp_b = pl.pallas_call(
        functools.partial(_bias_cmp_kernel, uppers=uppers),
        grid=(seq // t,),
        in_specs=[smem],
        out_specs=pl.BlockSpec((NSA_HEADS, n_cmp, t), lambda i: (0, 0, i)),
        out_shape=jax.ShapeDtypeStruct((NSA_HEADS, n_cmp, seq), F32),
        compiler_params=_params(("arbitrary",)),
        name="relbias_cmp",
    )(rel_bias)
    return near, cmp_b


def _cmp_to_sel_t(seq):
    n_cmp, n_sel = seq // CMP_STRIDE, seq // SEL_BLOCK
    c_start = np.arange(n_cmp)[:, None] * CMP_STRIDE
    s_start = np.arange(n_sel)[None, :] * SEL_BLOCK
    inside = np.clip(np.minimum(c_start + CMP_LEN, s_start + SEL_BLOCK) - np.maximum(c_start, s_start), 0, None)
    return (inside / CMP_LEN).T.astype(np.float32)


def _cmp_topk_kernel(qt_ref, kc_ref, vct_ref, bias_ref, c2s_ref, oc_ref, sel_ref, score_scr, cnt_scr):
    tb = pl.program_id(1)
    n, t = bias_ref.shape[1], bias_ref.shape[2]
    n_sel = c2s_ref.shape[0]
    nb = qt_ref.shape[0]
    tok = tb * t + lax.broadcasted_iota(jnp.int32, (1, t), 1)
    any_visible = tok >= CMP_LEN - 1
    ones = (lax.broadcasted_iota(jnp.int32, (SEL_ROWS, n), 0) == 0).astype(BF16)
    head_rows = lambda i: slice(i * NSA_DH, (i + 1) * NSA_DH)
    weights = {}
    for bb in range(nb):
        kc = kc_ref[bb, 0]
        for i in range(NSA_HPG):
            s = jnp.dot(kc, qt_ref[bb, 0, head_rows(i), :], preferred_element_type=F32) + bias_ref[i]
            m = jnp.max(_fold(s, jnp.max), axis=0, keepdims=True)
            weights[bb, i] = jnp.exp2(s - m).astype(BF16)
    imps = []
    for bb in range(nb):
        lhs = jnp.concatenate([vct_ref[bb, 0], ones, c2s_ref[...]], axis=0)
        imp = jnp.zeros((n_sel, t), F32)
        for i in range(NSA_HPG):
            r = jnp.dot(lhs, weights[bb, i], preferred_element_type=F32)
            scale = jnp.where(any_visible, 1.0 / r[NSA_DH:NSA_DH + 1], 0.0)
            oc_ref[bb, 0, head_rows(i), :] = (r[:NSA_DH] * scale).astype(BF16)
            imp = imp + r[NSA_DH + SEL_ROWS:] * scale
        imps.append(imp)

    w = nb * t
    blk = lax.broadcasted_iota(jnp.int32, (n_sel, t), 0)
    cur = (tb * t + lax.broadcasted_iota(jnp.int32, (n_sel, t), 1)) // SEL_BLOCK
    forced = (blk == 0) | (blk == cur) | (blk == cur - 1)
    for bb in range(nb):
        score_scr[:, bb * t:(bb + 1) * t] = jnp.where(blk > cur, -jnp.inf, jnp.where(forced, jnp.inf, imps[bb]))
    groups = n_sel // 8
    per_tile = t // SEL_BLOCK
    sub = lax.broadcasted_iota(jnp.int32, (8, w), 0)
    cnt_scr[...] = jnp.zeros_like(cnt_scr)
    for j in range(groups):
        @pl.when(8 * j < (tb + 1) * per_tile)
        def _():
            src = [jnp.broadcast_to(score_scr[sp:sp + 1, :], (8, w)) for sp in range(8 * j, 8 * j + 8)]
            for v in range(groups):
                part = score_scr[8 * v:8 * v + 8, :]
                cnt = cnt_scr[8 * v:8 * v + 8, :]
                for sp, r in zip(range(8 * j, 8 * j + 8), src):
                    if j < v:
                        beats = r >= part
                    elif j > v:
                        beats = r > part
                    else:
                        beats = (r > part) | ((r == part) & (sub > sp - 8 * v))
                    cnt = cnt + jnp.where(beats, 1.0, 0.0)
                cnt_scr[8 * v:8 * v + 8, :] = cnt
    pad = jnp.zeros((SEL_ROWS - per_tile, t), F32)
    for v in range(groups):
        mask = jnp.where(cnt_scr[8 * v:8 * v + 8, :] < min(SEL_TOPN, n_sel), 0.0, NEG)
        for bb in range(nb):
            for r0 in range(0, 8, per_tile):
                kt = (8 * v + r0) // per_tile
                rows = mask[r0:r0 + per_tile, bb * t:(bb + 1) * t]
                sel_ref[bb, 0, 0, kt] = jnp.concatenate([rows, pad], axis=0).astype(BF16)


def _cmp_topk(q_t, k_cmp, v_cmp_t, bias_cmp, seq):
    bsz, nt, nq, t = q_t.shape
    n_cmp, n_sel = seq // CMP_STRIDE, seq // SEL_BLOCK
    grp = NSA_HPG * NSA_DH
    nb = CMP_BATCH if bsz % CMP_BATCH == 0 else 1
    c2s = jnp.asarray(_cmp_to_sel_t(seq), BF16)
    return pl.pallas_call(
        _cmp_topk_kernel,
        grid=(NSA_KV, nt, bsz // nb),
        in_specs=[pl.BlockSpec((nb, 1, grp, t), lambda g, i, b: (b, i, g, 0)),
                  pl.BlockSpec((nb, 1, n_cmp, NSA_DH), lambda g, i, b: (b, g, 0, 0)),
                  pl.BlockSpec((nb, 1, NSA_DH, n_cmp), lambda g, i, b: (b, g, 0, 0)),
                  pl.BlockSpec((NSA_HPG, n_cmp, t), lambda g, i, b: (g, 0, i)),
                  _const_spec((n_sel, n_cmp))],
        out_specs=[pl.BlockSpec((nb, 1, grp, t), lambda g, i, b: (b, i, g, 0)),
                   pl.BlockSpec((nb, 1, 1, nt, SEL_ROWS, t), lambda g, i, b: (b, g, i, 0, 0, 0))],
        out_shape=[jax.ShapeDtypeStruct((bsz, nt, nq, t), BF16),
                   jax.ShapeDtypeStruct((bsz, NSA_KV, nt, nt, SEL_ROWS, t), BF16)],
        scratch_shapes=[pltpu.VMEM((n_sel, nb * t), F32), pltpu.VMEM((n_sel, nb * t), F32)],
        compiler_params=_params(("arbitrary", "arbitrary", "arbitrary")),
        name="nsa_cmp_topk",
    )(q_t, k_cmp, v_cmp_t, bias_cmp, c2s)


def _sel_win_kernel(qa_ref, qb_ref, ks_ref, vst_ref, kw_ref, vwt_ref, sela_ref, selb_ref, tab_ref, ga_ref, gb_ref,
                    oca_ref, ocb_ref, olo_ref, ohi_ref):
    h0 = pl.program_id(0) * SW_HEADS
    i = pl.program_id(2)
    nt, t = vst_ref.shape[1], vst_ref.shape[3]
    half = nt // 2
    tb = (i, nt - 1 - i)
    head_rows = lambda hh: slice(hh * NSA_DH, (hh + 1) * NSA_DH)
    q = [(qa_ref[0, 0, head_rows(hh), :], qb_ref[0, 0, head_rows(hh), :]) for hh in range(SW_HEADS)]
    sel = (sela_ref, selb_ref)

    def tile_rows(kt):
        return pl.ds(pl.multiple_of(kt * t, t), t)

    def pick(on_a, xa, xb):
        if isinstance(on_a, bool):
            return xa if on_a else xb
        return jnp.where(on_a, xa, xb)

    work = []
    for p in range(nt + 1):
        on_a = True if p == 0 else (False if p >= half else p <= i)
        kt = pick(on_a, p, jnp.maximum(p - i - 1, 0))
        work.append((on_a, kt, True, nt - p if on_a is False else pick(on_a, tb[0] - kt, nt - p)))
    for side in range(2):
        for delta in range(WIN_TILES):
            keep = True if (side == 1 and half >= WIN_TILES) else tb[side] >= delta
            work.append((side == 0, jnp.maximum(tb[side] - delta, 0), keep, delta))
    n_selpos = nt + 1

    ones = (lax.broadcasted_iota(jnp.int32, (SEL_ROWS, t), 0) == 0).astype(BF16)
    q_pad = jnp.zeros((KEY_LANES - NSA_DH - SEL_ROWS, t), BF16)
    keys3 = [(br, side, hh) for br in range(2) for side in range(2) for hh in range(SW_HEADS)]
    run_max = {k: jnp.full((1, t), NEG, F32) for k in keys3}
    acc = {k: jnp.zeros((NSA_DH + SEL_ROWS, t), F32) for k in keys3}

    def merge(k, m_tile, part):
        m_new = jnp.maximum(run_max[k], m_tile)
        acc[k] = acc[k] * jnp.exp2(run_max[k] - m_new) + part * jnp.exp2(m_tile - m_new)
        run_max[k] = m_new

    def scores(p, hh):
        on_a, kt, keep, delta = work[p]
        qp = pick(on_a, q[hh][0], q[hh][1])
        if p >= n_selpos:
            s = jnp.dot(kw_ref[0, 0, tile_rows(kt), :], qp, preferred_element_type=F32)
            s = s + tab_ref[hh, SEL_TABLES + delta]
        else:
            rows = pick(on_a, sel[0][0, 0, 0, kt], sel[1][0, 0, 0, kt])
            q_aug = jnp.concatenate([qp, rows, q_pad], axis=0)
            s = jnp.dot(ks_ref[0, 0, tile_rows(kt), :], q_aug, preferred_element_type=F32)
            if isinstance(delta, int) and delta >= SEL_TABLES - 1:
                m_tile = jnp.max(_fold(s, jnp.max), axis=0, keepdims=True)
                return m_tile + tab_ref[hh, SEL_TABLES - 1, 0:1, :], jnp.exp2(s - m_tile).astype(BF16)
            s = s + tab_ref[hh, jnp.minimum(delta, SEL_TABLES - 1)]
        m_tile = jnp.max(_fold(s, jnp.max), axis=0, keepdims=True)
        return m_tile, jnp.exp2(s - m_tile).astype(BF16)

    def values(p, hh, m_tile, pr):
        on_a, kt, keep, delta = work[p]
        br = int(p >= n_selpos)
        vt_ref = vwt_ref if br else vst_ref
        part = jnp.dot(jnp.concatenate([vt_ref[0, kt], ones], axis=0), pr, preferred_element_type=F32)
        if not isinstance(keep, bool):
            m_tile = jnp.where(keep, m_tile, NEG)
        if isinstance(on_a, bool):
            merge((br, int(not on_a), hh), m_tile, part)
        else:
            merge((br, 0, hh), jnp.where(on_a, m_tile, NEG), part)
            merge((br, 1, hh), jnp.where(on_a, NEG, m_tile), part)

    items = [(p, hh) for p in range(len(work)) for hh in range(SW_HEADS)]
    pending = {}
    for step in range(len(items) + MXU_SKEW):
        if step < len(items):
            pending[step] = scores(*items[step])
        if step >= MXU_SKEW:
            values(*items[step - MXU_SKEW], *pending.pop(step - MXU_SKEW))

    for side, (g_ref, oc_ref, o_ref) in enumerate(((ga_ref, oca_ref, olo_ref), (gb_ref, ocb_ref, ohi_ref))):
        for hh in range(SW_HEADS):
            gate = lambda br: g_ref[0, pl.ds(br * NSA_HEADS + h0 + hh, 1), :]
            o_s = acc[(0, side, hh)][:NSA_DH] / acc[(0, side, hh)][NSA_DH:NSA_DH + 1]
            o_w = acc[(1, side, hh)][:NSA_DH] / acc[(1, side, hh)][NSA_DH:NSA_DH + 1]
            o_c = oc_ref[0, 0, head_rows(hh), :].astype(F32)
            o_ref[0, 0, head_rows(hh), :] = (gate(0) * o_c + gate(1) * o_s + gate(2) * o_w).astype(BF16)


def _sel_win(q_t, ks, vs_t, kw, vw_t, sel, tables, gates_t, oc_t):
    bsz, nt, nq, t = q_t.shape
    seq = nt * t
    last = nt - 1
    per_group = NSA_HPG // SW_HEADS
    head_a = pl.BlockSpec((1, 1, SW_HEADS * NSA_DH, t), lambda h, b, i: (b, i, h, 0))
    head_b = pl.BlockSpec((1, 1, SW_HEADS * NSA_DH, t), lambda h, b, i: (b, last - i, h, 0))
    keys = pl.BlockSpec((1, 1, seq, NSA_DH), lambda h, b, i: (b, h // per_group, 0, 0))
    keys_sel = pl.BlockSpec((1, 1, seq, KEY_LANES), lambda h, b, i: (b, h // per_group, 0, 0))
    vals = pl.BlockSpec((1, nt, NSA_DH, t), lambda h, b, i: (b, 0, h // per_group, 0))
    sel_a = pl.BlockSpec((1, 1, 1, nt, SEL_ROWS, t), lambda h, b, i: (b, h // per_group, i, 0, 0, 0))
    sel_b = pl.BlockSpec((1, 1, 1, nt, SEL_ROWS, t), lambda h, b, i: (b, h // per_group, last - i, 0, 0, 0))
    gate_a = pl.BlockSpec((1, 3 * NSA_HEADS, t), lambda h, b, i: (b, 0, i))
    gate_b = pl.BlockSpec((1, 3 * NSA_HEADS, t), lambda h, b, i: (b, 0, last - i))
    out = jax.ShapeDtypeStruct((bsz, nt // 2, nq, t), BF16)
    return pl.pallas_call(
        _sel_win_kernel,
        grid=(NSA_HEADS // SW_HEADS, bsz, nt // 2),
        in_specs=[head_a, head_b, keys_sel, vals, keys, vals, sel_a, sel_b,
                  pl.BlockSpec((SW_HEADS, SEL_TABLES + WIN_TILES, t, t), lambda h, b, i: (h, 0, 0, 0)),
                  gate_a, gate_b, head_a, head_b],
        out_specs=[head_a, head_a],
        out_shape=[out, out],
        compiler_params=_params(("arbitrary", "arbitrary", "arbitrary")),
        name="nsa_sel_win",
    )(q_t, q_t, ks, vs_t, kw, vw_t, sel, sel, tables, gates_t, gates_t, oc_t, oc_t)


def kernel(x, c, mod_w, mod_b, norm_mix_g, norm_ffn_g, ab_w_in, ab_w_out, hgrn_lb_logits, hgrn_onorm_g, sconv_w, nsa_w_in, nsa_w_out, nsa_cmp_pos_k, nsa_cmp_pos_v, nsa_cmp_w1_k, nsa_cmp_w2_k, nsa_cmp_w1_v, nsa_cmp_w2_v, rel_bias, ffn_w_up, ffn_conv_w, ffn_w_down, final_norm_g):
    bsz, seq, d = x.shape
    assert d == D_MODEL and seq % TOK_TILE == 0
    lower = jnp.cumsum(jax.nn.softmax(hgrn_lb_logits.astype(F32), axis=0), axis=0)
    mod = _modulation(c, mod_w, mod_b)
    parts = lambda l: [mod[l, :, j * d:(j + 1) * d].reshape(bsz, 1, d) for j in range(6)]

    sh1, sc1, g1, sh2, sc2, g2 = parts(0)
    x = _mixer0(x, sh1, sc1, g1, norm_mix_g[0], ab_w_in[0], ab_w_out[0], lower[0], hgrn_onorm_g[0], sconv_w[0])
    x = _conv_ffn(x, sh2, sc2, g2, norm_ffn_g[0], ffn_w_up[0], ffn_conv_w[0], ffn_w_down[0], final_norm_g)

    sh1, sc1, g1, sh2, sc2, g2 = parts(1)
    kc, vc, ks, kw, q_t, gates_t, vs_t, vw_t = _nsa_proj(x, sh1, sc1, norm_mix_g[1], nsa_w_in[0])
    k_cmp, v_cmp_t = _compress(kc, vc, nsa_cmp_pos_k[0], nsa_cmp_pos_v[0],
                               nsa_cmp_w1_k[0], nsa_cmp_w2_k[0], nsa_cmp_w1_v[0], nsa_cmp_w2_v[0])
    near, bias_cmp = _bias_tables(rel_bias.astype(F32) * LOG2E, seq)
    oc_t, sel = _cmp_topk(q_t, k_cmp, v_cmp_t, bias_cmp, seq)
    o_lo, o_hi = _sel_win(q_t, ks, vs_t, kw, vw_t, sel, near, gates_t, oc_t)
    return _conv_ffn(x, sh2, sc2, g2, norm_ffn_g[1], ffn_w_up[1], ffn_conv_w[1], ffn_w_down[1], final_norm_g,
                     attn=(o_lo, o_hi, nsa_w_out[0], g1), final=True)
```

```python
import functools
import math

import jax
import jax.numpy as jnp
import numpy as np
from jax import lax
from jax.experimental import pallas as pl
from jax.experimental.pallas import tpu as pltpu

F32 = jnp.float32
BF16 = jnp.bfloat16

EPS = 1e-6
D_MODEL = 1024
D_FF = 2816

HG_HEADS = 4
HG_DK = 128
HG_WIDTH = HG_HEADS * HG_DK
SC_WIDTH = D_MODEL - HG_WIDTH
AB_IN = 4 * HG_WIDTH + 3 * SC_WIDTH
HG_CHUNK = 64
HG_LEVELS = 6
HG_MXU_LEVELS = ()

NSA_HEADS = 16
NSA_KV = 4
NSA_HPG = NSA_HEADS // NSA_KV
NSA_DH = D_MODEL // NSA_HEADS
CMP_LEN = 32
CMP_STRIDE = 16
CMP_HIDDEN = 256
SEL_BLOCK = 64
SEL_TOPN = 16
WINDOW = 512
REL_BUCKETS = 32
REL_MAX_DIST = 1024

TOK_TILE = 256
NEAR_TILES = 5
SEL_TABLES = NEAR_TILES + 1
WIN_TILES = WINDOW // TOK_TILE + 1
NEG = -1e30
LOG2E = math.log2(math.e)
KEY_LANES = 128
SEL_ROWS = 16
CMP_BATCH = 2
SW_HEADS = 4
MXU_SKEW = 5
VMEM_LIMIT = 56 * 1024 * 1024
HALO = 8

_NT = (((1,), (1,)), ((), ()))
_TN = (((0,), (0,)), ((), ()))


def _sigmoid(x):
    return 1.0 / (1.0 + jnp.exp(-x))


def _silu(x):
    return x * _sigmoid(x)


def _norm_mod(x, g, sc, sh):
    ms = jnp.mean(x * x, axis=-1, keepdims=True)
    return (x * lax.rsqrt(ms + EPS) * g) * (1.0 + sc) + sh


def _fold(x, op):
    return op(x.reshape(x.shape[0] // 8, 8, x.shape[1]), axis=0)


def _const_spec(shape):
    n = len(shape)
    return pl.BlockSpec(shape, lambda *_: (0,) * n, pipeline_mode=pl.Buffered(1))


def _params(sem):
    return pltpu.CompilerParams(dimension_semantics=sem, vmem_limit_bytes=VMEM_LIMIT)


def _mod_kernel(c_ref, w_ref, b_ref, o_ref):
    c = c_ref[...]
    o_ref[0] = jnp.dot(_silu(c), w_ref[0], preferred_element_type=F32,
                       precision=lax.Precision.HIGHEST) + b_ref[0]


def _modulation(c, mod_w, mod_b):
    depth, d, n = mod_w.shape
    bsz = c.shape[0]
    tn = 1024
    return pl.pallas_call(
        _mod_kernel,
        grid=(depth, n // tn),
        in_specs=[pl.BlockSpec((bsz, d), lambda l, j: (0, 0)),
                  pl.BlockSpec((1, d, tn), lambda l, j: (l, 0, j)),
                  pl.BlockSpec((1, 1, tn), lambda l, j: (l, 0, j))],
        out_specs=pl.BlockSpec((1, bsz, tn), lambda l, j: (l, 0, j)),
        out_shape=jax.ShapeDtypeStruct((depth, bsz, n), F32),
        compiler_params=_params(("arbitrary", "arbitrary")),
        name="adaln_mod",
    )(c, mod_w, mod_b.reshape(depth, 1, n))


def _hgrn_decay_matrix():
    t = np.arange(HG_CHUNK)
    tril = (t[None, :] <= t[:, None]).astype(np.float32)
    mats = [tril]
    for lvl in HG_MXU_LEVELS:
        m = 1 << (lvl - 1)
        mid = (t // (2 * m)) * (2 * m) + m - 1
        mats.append(tril - tril[mid])
    return np.concatenate(mats, axis=0)


def _mixer0_kernel(x_ref, sh_ref, sc_ref, gt_ref, ng_ref, win_ref, wout_ref, lb_ref, og_ref, cw_ref,
                   pm_ref, o_ref, proj_ref, cat_ref, st_ref, ubuf_ref, d_scr):
    tile = x_ref.shape[1]
    c64 = HG_CHUNK

    @pl.when(pl.program_id(1) == 0)
    def _():
        st_ref[...] = jnp.zeros_like(st_ref)
        ubuf_ref[0:HALO, :] = jnp.zeros((HALO, SC_WIDTH), F32)

    x = x_ref[0]
    hm = _norm_mod(x, ng_ref[...], sc_ref[0], sh_ref[0]).astype(BF16)
    n_hg = 4 * HG_WIDTH
    proj_ref[:, :n_hg] = jnp.dot(hm, win_ref[:, :n_hg], preferred_element_type=F32)

    row = lax.broadcasted_iota(jnp.int32, (c64, HG_DK), 0)
    r64 = lax.broadcasted_iota(jnp.int32, (c64, c64), 0)
    c64i = lax.broadcasted_iota(jnp.int32, (c64, c64), 1)
    second = [None] + [(row & (1 << (lvl - 1))) != 0 for lvl in range(1, HG_LEVELS + 1)]
    same = [None] + [(r64 >> lvl) == (c64i >> lvl) for lvl in range(1, HG_LEVELS + 1)]
    eye = r64 == c64i
    pm = pm_ref[...]
    og = og_ref[...]

    units = [(c, h) for c in range(tile // c64) for h in range(HG_HEADS)]

    def blk(c, h, j):
        return pl.ds(c * c64, c64), pl.ds(j * HG_WIDTH + h * HG_DK, HG_DK)

    for u, (c, h) in enumerate(units):
        lb = lb_ref[:, h * HG_DK:(h + 1) * HG_DK]
        fg = lb + (1.0 - lb) * _sigmoid(proj_ref[blk(c, h, 1)])
        proj_ref[blk(c, h, 1)] = 1.0 - fg
        lg = jnp.log(fg)
        lg_hi = lg.astype(BF16)
        lg_lo = (lg - lg_hi.astype(F32)).astype(BF16)
        dall = jnp.dot(pm, jnp.concatenate([lg_hi, lg_lo], axis=1), preferred_element_type=F32)
        d_scr[u] = dall[:, :HG_DK] + dall[:, HG_DK:]

    proj_ref[:, n_hg:] = jnp.dot(hm, win_ref[:, n_hg:], preferred_element_type=F32)

    intra, q_in, k_out, decay = [], [], [], []
    for u, (c, h) in enumerate(units):
        q = proj_ref[blk(c, h, 0)]
        k = proj_ref[blk(c, h, 1)]
        b = d_scr[u, 0:c64, :]
        a = jnp.zeros((c64, c64), F32)
        for lvl in range(1, HG_LEVELS + 1):
            m = 1 << (lvl - 1)
            if lvl == 1:
                qt = jnp.where(second[lvl], q * (1.0 - k), 0.0).astype(BF16)
                kt = jnp.where(second[lvl], 0.0, k).astype(BF16)
            else:
                if lvl in HG_MXU_LEVELS:
                    at = (1 + HG_MXU_LEVELS.index(lvl)) * c64
                    dlt = d_scr[u, at:at + c64, :]
                else:
                    mids = [jnp.broadcast_to(d_scr[u, r0 + m - 1:r0 + m, :], (2 * m, HG_DK))
                            for r0 in range(0, c64, 2 * m)]
                    dlt = b - (mids[0] if len(mids) == 1 else jnp.concatenate(mids, axis=0))
                e = jnp.exp(-jnp.abs(dlt))
                qt = jnp.where(second[lvl], q * e, 0.0).astype(BF16)
                kt = jnp.where(second[lvl], 0.0, k * e).astype(BF16)
            al = lax.dot_general(qt, kt, _NT, preferred_element_type=F32)
            a = a + (al if lvl == HG_LEVELS else jnp.where(same[lvl], al, 0.0))
        intra.append(jnp.where(eye, jnp.sum(q * k, axis=-1, keepdims=True), a).astype(BF16))
        b_last = d_scr[u, c64 - 1:c64, :]
        q_in.append((q * jnp.exp(b)).astype(BF16))
        k_out.append((k * jnp.exp(b_last - b)).astype(BF16))
        decay.append(jnp.exp(b_last))

    state = [st_ref[h] for h in range(HG_HEADS)]
    for u, (c, h) in enumerate(units):
        vb = proj_ref[blk(c, h, 2)].astype(BF16)
        o = jnp.dot(intra[u], vb, preferred_element_type=F32)
        o = o + lax.dot_general(q_in[u], state[h].astype(BF16), _NT, preferred_element_type=F32)
        state[h] = state[h] * decay[u] + lax.dot_general(vb, k_out[u], _TN, preferred_element_type=F32)
        on = o * lax.rsqrt(jnp.mean(o * o, axis=-1, keepdims=True) + EPS) * og
        rows, _ = blk(c, h, 0)
        cat_ref[rows, h * HG_DK:(h + 1) * HG_DK] = (on * _silu(proj_ref[blk(c, h, 3)])).astype(BF16)
    for h in range(HG_HEADS):
        st_ref[h] = state[h]

    off = 4 * HG_WIDTH
    u = proj_ref[:, off + SC_WIDTH:off + 2 * SC_WIDTH] * proj_ref[:, off + 2 * SC_WIDTH:off + 3 * SC_WIDTH]
    ubuf_ref[HALO:HALO + tile, :] = u
    cw = cw_ref[...]
    conv = (ubuf_ref[HALO - 2:HALO - 2 + tile, :] * cw[0:1] + ubuf_ref[HALO - 1:HALO - 1 + tile, :] * cw[1:2]
            + u * cw[2:3])
    ubuf_ref[0:HALO, :] = u[tile - HALO:tile, :]
    cat_ref[:, HG_WIDTH:] = (proj_ref[:, off:off + SC_WIDTH] * conv).astype(BF16)

    y = jnp.dot(cat_ref[...], wout_ref[...], preferred_element_type=F32)
    o_ref[0] = x + gt_ref[0] * y


def _mixer0(x, sh, sc, gt, norm_g, w_in, w_out, lower, onorm_g, sconv_w):
    bsz, seq, d = x.shape
    t = TOK_TILE
    pm = jnp.asarray(_hgrn_decay_matrix(), BF16)
    vec = pl.BlockSpec((1, 1, d), lambda b, i: (b, 0, 0))
    return pl.pallas_call(
        _mixer0_kernel,
        grid=(bsz, seq // t),
        in_specs=[pl.BlockSpec((1, t, d), lambda b, i: (b, i, 0)), vec, vec, vec,
                  _const_spec((1, d)), _const_spec((d, AB_IN)), _const_spec((d, d)),
                  _const_spec((1, HG_WIDTH)), _const_spec((1, HG_DK)), _const_spec((3, SC_WIDTH)),
                  _const_spec(pm.shape)],
        out_specs=pl.BlockSpec((1, t, d), lambda b, i: (b, i, 0)),
        out_shape=jax.ShapeDtypeStruct(x.shape, F32),
        scratch_shapes=[pltpu.VMEM((t, AB_IN), F32), pltpu.VMEM((t, d), BF16),
                        pltpu.VMEM((HG_HEADS, HG_DK, HG_DK), F32), pltpu.VMEM((t + HALO, SC_WIDTH), F32),
                        pltpu.VMEM((t // HG_CHUNK * HG_HEADS, pm.shape[0], HG_DK), F32)],
        compiler_params=_params(("arbitrary", "arbitrary")),
        name="mixer_hgrn_sconv",
    )(x, sh, sc, gt, norm_g.reshape(1, d), w_in.astype(BF16), w_out.astype(BF16),
      lower.reshape(1, HG_WIDTH), onorm_g.reshape(1, HG_DK), sconv_w, pm)


MXU_DIM = 256
FF_CHUNKS = ((0, 5 * MXU_DIM), (5 * MXU_DIM, D_FF))


def _ffn_kernel(*refs, has_attn, final):
    if has_attn:
        (x_ref, olo_ref, ohi_ref, wo_ref, g1_ref, sh_ref, sc_ref, g2_ref, ng_ref, wup_ref, cw_ref, wdn_ref, fg_ref,
         o_ref, gbuf_ref, carry_ref) = refs
    else:
        (x_ref, sh_ref, sc_ref, g2_ref, ng_ref, wup_ref, cw_ref, wdn_ref, fg_ref,
         o_ref, gbuf_ref, carry_ref) = refs
    tile = x_ref.shape[1]

    @pl.when(pl.program_id(1) == 0)
    def _():
        carry_ref[...] = jnp.zeros_like(carry_ref)

    x = x_ref[0]
    if has_attn:
        o_t = jnp.where(pl.program_id(1) < pl.num_programs(1) // 2, olo_ref[0, 0], ohi_ref[0, 0])
        x = x + g1_ref[0] * lax.dot_general(o_t, wo_ref[...], _TN, preferred_element_type=F32)
    hf = _norm_mod(x, ng_ref[...], sc_ref[0], sh_ref[0]).astype(BF16)
    gbuf_ref[0:HALO, :] = carry_ref[...]
    gates, ups = [], []
    for c0, c1 in FF_CHUNKS:
        gate = jnp.dot(hf, wup_ref[:, c0:c1], preferred_element_type=F32)
        ups.append(jnp.dot(hf, wup_ref[:, D_FF + c0:D_FF + c1], preferred_element_type=F32))
        gbuf_ref[HALO:HALO + tile, c0:c1] = gate
        carry_ref[:, c0:c1] = gate[tile - HALO:tile, :]
        gates.append(gate)
    acc = jnp.zeros((tile, D_MODEL), F32)
    for j, (c0, c1) in enumerate(FF_CHUNKS):
        cols = slice(c0, c1)
        cw = cw_ref[:, cols]
        conv = (gbuf_ref[HALO - 2:HALO - 2 + tile, cols] * cw[0:1] + gbuf_ref[HALO - 1:HALO - 1 + tile, cols] * cw[1:2]
                + gates[j] * cw[2:3])
        act = (_silu(conv) * ups[j]).astype(BF16)
        acc = acc + jnp.dot(act, wdn_ref[cols, :], preferred_element_type=F32)
    out = x + g2_ref[0] * acc
    if final:
        out = out * lax.rsqrt(jnp.mean(out * out, axis=-1, keepdims=True) + EPS) * fg_ref[...]
    o_ref[0] = out


def _conv_ffn(x, sh, sc, g2, norm_g, w_up, conv_w, w_down, final_g, attn=None, final=False):
    bsz, seq, d = x.shape
    t = TOK_TILE
    vec = pl.BlockSpec((1, 1, d), lambda b, i: (b, 0, 0))
    args, specs = [x], [pl.BlockSpec((1, t, d), lambda b, i: (b, i, 0))]
    if attn is not None:
        o_lo, o_hi, w_o, g1 = attn
        top = o_lo.shape[1] - 1
        last = seq // t - 1
        args += [o_lo, o_hi, w_o.astype(BF16), g1]
        specs += [pl.BlockSpec((1, 1, d, t), lambda b, i: (b, jnp.minimum(i, top), 0, 0)),
                  pl.BlockSpec((1, 1, d, t), lambda b, i: (b, jnp.minimum(last - i, top), 0, 0)),
                  _const_spec((d, d)), vec]
    args += [sh, sc, g2, norm_g.reshape(1, d), w_up.astype(BF16), conv_w, w_down.astype(BF16),
             final_g.reshape(1, d)]
    specs += [vec, vec, vec, _const_spec((1, d)), _const_spec((d, 2 * D_FF)), _const_spec((3, D_FF)),
              _const_spec((D_FF, d)), _const_spec((1, d))]
    return pl.pallas_call(
        functools.partial(_ffn_kernel, has_attn=attn is not None, final=final),
        grid=(bsz, seq // t),
        in_specs=specs,
        out_specs=pl.BlockSpec((1, t, d), lambda b, i: (b, i, 0)),
        out_shape=jax.ShapeDtypeStruct(x.shape, F32),
        scratch_shapes=[pltpu.VMEM((t + HALO, D_FF), F32), pltpu.VMEM((HALO, D_FF), F32)],
        compiler_params=_params(("arbitrary", "arbitrary")),
        name="conv_ffn_attnproj" if attn is not None else "conv_ffn",
    )(*args)


KV_ROWS = NSA_KV * NSA_DH


def _nsa_proj_kernel(x_ref, sh_ref, sc_ref, ng_ref, wa_ref, wq_ref, wg_ref, wv_ref, hot_ref, perm_ref,
                     kc_ref, vc_ref, ks_ref, kw_ref, qt_ref, gt_ref, vst_ref, vwt_ref):
    hm = _norm_mod(x_ref[0], ng_ref[...], sc_ref[0], sh_ref[0]).astype(BF16)
    kv = jnp.dot(hm, wa_ref[...], preferred_element_type=F32).astype(BF16)
    cmp_in = jnp.dot(perm_ref[...], kv[:, :2 * KV_ROWS], preferred_element_type=F32).astype(BF16)
    n_grp = perm_ref.shape[0] // CMP_STRIDE
    for g in range(NSA_KV):
        part = lambda j: kv[:, j * KV_ROWS + g * NSA_DH:j * KV_ROWS + (g + 1) * NSA_DH]
        for p in range(CMP_STRIDE):
            rows = slice(p * n_grp, (p + 1) * n_grp)
            kc_ref[0, g, p] = cmp_in[rows, g * NSA_DH:(g + 1) * NSA_DH]
            vc_ref[0, g, p] = cmp_in[rows, KV_ROWS + g * NSA_DH:KV_ROWS + (g + 1) * NSA_DH]
        ks_ref[0, g] = jnp.concatenate([part(2), hot_ref[...]], axis=1)
        kw_ref[0, g] = part(3)
    qt = lax.dot_general(wq_ref[...], hm, _NT, preferred_element_type=F32)
    qt_ref[0, 0] = (qt * (NSA_DH ** -0.5 * LOG2E)).astype(BF16)
    gt_ref[0] = _sigmoid(lax.dot_general(wg_ref[...], hm, _NT, preferred_element_type=F32))
    vt = lax.dot_general(wv_ref[...], hm, _NT, preferred_element_type=F32).astype(BF16)
    vst_ref[0, 0] = vt[:KV_ROWS]
    vwt_ref[0, 0] = vt[KV_ROWS:]


def _nsa_proj(x, sh, sc, norm_g, w_in):
    bsz, seq, d = x.shape
    t = TOK_TILE
    nt = seq // t
    nq, ng = NSA_HEADS * NSA_DH, 3 * NSA_HEADS
    o = nq + ng
    w = w_in.astype(BF16)
    col = lambda j: w[:, o + j * KV_ROWS:o + (j + 1) * KV_ROWS]
    w_a = jnp.concatenate([col(0), col(1), col(2), col(4)], axis=1)
    w_q = w[:, :nq].T
    w_g = w[:, nq:o].T
    w_v = jnp.concatenate([col(3), col(5)], axis=1).T
    hot = (np.arange(t)[:, None] // SEL_BLOCK == np.arange(KEY_LANES - NSA_DH)[None, :]).astype(np.float32)
    n_grp = t // CMP_STRIDE
    r = np.arange(t)
    perm = (r[None, :] == ((r % n_grp) * CMP_STRIDE + r // n_grp)[:, None]).astype(np.float32)
    vec = pl.BlockSpec((1, 1, d), lambda b, i: (b, 0, 0))
    tiled = lambda r: pl.BlockSpec((1, 1, r, t), lambda b, i: (b, i, 0, 0))
    grouped = lambda n: pl.BlockSpec((1, NSA_KV, t, n), lambda b, i: (b, 0, i, 0))
    rows = lambda n: jax.ShapeDtypeStruct((bsz, NSA_KV, seq, n), BF16)
    by_offset = pl.BlockSpec((1, NSA_KV, CMP_STRIDE, n_grp, NSA_DH), lambda b, i: (b, 0, 0, i, 0))
    offset_major = jax.ShapeDtypeStruct((bsz, NSA_KV, CMP_STRIDE, seq // CMP_STRIDE, NSA_DH), BF16)
    return pl.pallas_call(
        _nsa_proj_kernel,
        grid=(bsz, nt),
        in_specs=[pl.BlockSpec((1, t, d), lambda b, i: (b, i, 0)), vec, vec, _const_spec((1, d)),
                  _const_spec((d, 4 * KV_ROWS)), _const_spec((nq, d)), _const_spec((ng, d)),
                  _const_spec((2 * KV_ROWS, d)), _const_spec(hot.shape), _const_spec(perm.shape)],
        out_specs=[by_offset, by_offset, grouped(KEY_LANES), grouped(NSA_DH), tiled(nq),
                   pl.BlockSpec((1, ng, t), lambda b, i: (b, 0, i)), tiled(KV_ROWS), tiled(KV_ROWS)],
        out_shape=[offset_major, offset_major, rows(KEY_LANES), rows(NSA_DH),
                   jax.ShapeDtypeStruct((bsz, nt, nq, t), BF16),
                   jax.ShapeDtypeStruct((bsz, ng, seq), F32),
                   jax.ShapeDtypeStruct((bsz, nt, KV_ROWS, t), BF16),
                   jax.ShapeDtypeStruct((bsz, nt, KV_ROWS, t), BF16)],
        compiler_params=_params(("arbitrary", "arbitrary")),
        name="nsa_in_proj",
    )(x, sh, sc, norm_g.reshape(1, d), w_a, w_q, w_g, w_v, jnp.asarray(hot, BF16), jnp.asarray(perm, BF16))


def _compress_kernel(hk_ref, hv_ref, pk_ref, pv_ref, w1k_ref, w2k_ref, w1v_ref, w2vt_ref, kc_ref, vct_ref):
    half = CMP_STRIDE * NSA_DH

    def hidden(h_ref, pos_ref, w1_ref):
        n = h_ref.shape[3]
        top = jnp.zeros((n, CMP_HIDDEN), F32)
        bot = jnp.zeros((n, CMP_HIDDEN), F32)
        per = MXU_DIM // NSA_DH
        for p0 in range(0, CMP_STRIDE, per):
            h = jnp.concatenate([h_ref[0, 0, p] for p in range(p0, p0 + per)], axis=1)
            rows = slice(p0 * NSA_DH, (p0 + per) * NSA_DH)
            top = top + jnp.dot(h, w1_ref[rows, :], preferred_element_type=F32)
            bot = bot + jnp.dot(h, w1_ref[half + p0 * NSA_DH:half + (p0 + per) * NSA_DH, :],
                                preferred_element_type=F32)
        pos = jnp.dot(jnp.broadcast_to(pos_ref[...], (8, 2 * half)).astype(BF16), w1_ref[...],
                      preferred_element_type=F32)[0:1]
        return _silu(top + pltpu.roll(bot, n - 1, 0) + pos).astype(BF16)

    kc_ref[0, 0] = jnp.dot(hidden(hk_ref, pk_ref, w1k_ref), w2k_ref[...],
                           preferred_element_type=F32).astype(BF16)
    vct_ref[0, 0] = lax.dot_general(w2vt_ref[...], hidden(hv_ref, pv_ref, w1v_ref), _NT,
                                    preferred_element_type=F32).astype(BF16)


def _compress(hk, hv, pos_k, pos_v, w1_k, w2_k, w1_v, w2_v):
    bsz, g, stride, n, dh = hk.shape
    width = stride * dh
    blk = pl.BlockSpec((1, 1, stride, n, dh), lambda b, j: (b, j, 0, 0, 0))
    return pl.pallas_call(
        _compress_kernel,
        grid=(bsz, g),
        in_specs=[blk, blk, _const_spec((1, width * 2)), _const_spec((1, width * 2)),
                  _const_spec((2 * width, CMP_HIDDEN)), _const_spec((CMP_HIDDEN, NSA_DH)),
                  _const_spec((2 * width, CMP_HIDDEN)), _const_spec((NSA_DH, CMP_HIDDEN))],
        out_specs=[pl.BlockSpec((1, 1, n, NSA_DH), lambda b, j: (b, j, 0, 0)),
                   pl.BlockSpec((1, 1, NSA_DH, n), lambda b, j: (b, j, 0, 0))],
        out_shape=[jax.ShapeDtypeStruct((bsz, g, n, NSA_DH), BF16),
                   jax.ShapeDtypeStruct((bsz, g, NSA_DH, n), BF16)],
        compiler_params=_params(("arbitrary", "arbitrary")),
        name="nsa_compress",
    )(hk, hv, pos_k.reshape(1, -1), pos_v.reshape(1, -1), w1_k.astype(BF16), w2_k.astype(BF16),
      w1_v.astype(BF16), w2_v.T.astype(BF16))


def _bucket_upper_bounds():
    n = np.arange(4 * REL_MAX_DIST, dtype=np.float64)
    exact = REL_BUCKETS // 2
    large = exact + (np.log(np.maximum(n, exact) / exact) / math.log(REL_MAX_DIST / exact)
                     * (REL_BUCKETS - exact)).astype(np.int64)
    bucket = np.where(n < exact, n.astype(np.int64), np.minimum(large, REL_BUCKETS - 1))
    return [int(np.max(np.nonzero(bucket <= j)[0])) for j in range(REL_BUCKETS - 1)]


def _bias_of_dist(dist, rb_ref, heads, uppers):
    vals = [jnp.full(dist.shape, rb_ref[REL_BUCKETS - 1, h], F32) for h in heads]
    for j in range(REL_BUCKETS - 2, -1, -1):
        m = dist <= uppers[j]
        vals = [jnp.where(m, rb_ref[j, h], v) for h, v in zip(heads, vals)]
    return vals


def _bias_strip(dist_min, dist_max, dist, rb_ref, heads, uppers, store, limit=None):
    is_const = (dist_min > uppers[-1]) | (dist_max < 0)
    if limit is not None:
        is_const = is_const | (dist_min >= limit)

    @pl.when(is_const)
    def _():
        store([jnp.full(dist.shape, rb_ref[REL_BUCKETS - 1, h], F32) for h in heads])

    @pl.when(jnp.logical_not(is_const))
    def _():
        store(_bias_of_dist(dist, rb_ref, heads, uppers))


def _bias_near_kernel(rb_ref, o_ref, *, uppers):
    d = pl.program_id(0)
    t = o_ref.shape[2]
    heads = list(range(NSA_HEADS))
    is_win = d >= SEL_TABLES
    delta = jnp.where(is_win, d - SEL_TABLES, d)
    limit = jnp.where(is_win, WINDOW, (SEL_TABLES + 1) * t)

    def strip(i, carry):
        r0 = pl.multiple_of(i * 8, 8)
        key = r0 + lax.broadcasted_iota(jnp.int32, (8, t), 0)
        tok = lax.broadcasted_iota(jnp.int32, (8, t), 1)
        dist = delta * t + tok - key
        valid = (dist >= 0) & (dist < limit)

        def store(vals):
            for h in heads:
                o_ref[h, 0, pl.ds(r0, 8), :] = jnp.where(valid, vals[h], NEG)

        _bias_strip(delta * t - (r0 + 7), delta * t + (t - 1) - r0, dist, rb_ref, heads, uppers, store, limit)
        return carry

    lax.fori_loop(0, t // 8, strip, 0)


def _bias_cmp_kernel(rb_ref, o_ref, *, uppers):
    tb = pl.program_id(0)
    n, t = o_ref.shape[1], o_ref.shape[2]
    heads = list(range(NSA_HEADS))

    def strip(i, carry):
        r0 = pl.multiple_of(i * 8, 8)
        blk = r0 + lax.broadcasted_iota(jnp.int32, (8, t), 0)
        tok = tb * t + lax.broadcasted_iota(jnp.int32, (8, t), 1)
        dist = tok - (blk * CMP_STRIDE + CMP_LEN - 1)

        def store(vals):
            for h in heads:
                o_ref[h, pl.ds(r0, 8), :] = jnp.where(dist >= 0, vals[h], NEG)

        last = CMP_LEN - 1
        _bias_strip(tb * t - ((r0 + 7) * CMP_STRIDE + last), tb * t + (t - 1) - (r0 * CMP_STRIDE + last), dist,
                    rb_ref, heads, uppers, store)
        return carry

    lax.fori_loop(0, n // 8, strip, 0)


def _bias_tables(rel_bias, seq):
    t = TOK_TILE
    n_cmp = seq // CMP_STRIDE
    uppers = _bucket_upper_bounds()
    smem = pl.BlockSpec(memory_space=pltpu.SMEM)
    near = pl.pallas_call(
        functools.partial(_bias_near_kernel, uppers=uppers),
        grid=(SEL_TABLES + WIN_TILES,),
        in_specs=[smem],
        out_specs=pl.BlockSpec((NSA_HEADS, 1, t, t), lambda d: (0, d, 0, 0)),
        out_shape=jax.ShapeDtypeStruct((NSA_HEADS, SEL_TABLES + WIN_TILES, t, t), F32),
        compiler_params=_params(("arbitrary",)),
        name="relbias_near",
    )(rel_bias)
    cmp_b = pl.pallas_call(
        functools.partial(_bias_cmp_kernel, uppers=uppers),
        grid=(seq // t,),
        in_specs=[smem],
        out_specs=pl.BlockSpec((NSA_HEADS, n_cmp, t), lambda i: (0, 0, i)),
        out_shape=jax.ShapeDtypeStruct((NSA_HEADS, n_cmp, seq), F32),
        compiler_params=_params(("arbitrary",)),
        name="relbias_cmp",
    )(rel_bias)
    return near, cmp_b


def _cmp_to_sel_t(seq):
    n_cmp, n_sel = seq // CMP_STRIDE, seq // SEL_BLOCK
    c_start = np.arange(n_cmp)[:, None] * CMP_STRIDE
    s_start = np.arange(n_sel)[None, :] * SEL_BLOCK
    inside = np.clip(np.minimum(c_start + CMP_LEN, s_start + SEL_BLOCK) - np.maximum(c_start, s_start), 0, None)
    return (inside / CMP_LEN).T.astype(np.float32)


def _cmp_topk_kernel(qt_ref, kc_ref, vct_ref, bias_ref, c2s_ref, oc_ref, sel_ref, score_scr, cnt_scr):
    tb = pl.program_id(1)
    n, t = bias_ref.shape[1], bias_ref.shape[2]
    n_sel = c2s_ref.shape[0]
    nb = qt_ref.shape[0]
    tok = tb * t + lax.broadcasted_iota(jnp.int32, (1, t), 1)
    any_visible = tok >= CMP_LEN - 1
    ones = (lax.broadcasted_iota(jnp.int32, (SEL_ROWS, n), 0) == 0).astype(BF16)
    head_rows = lambda i: slice(i * NSA_DH, (i + 1) * NSA_DH)
    weights = {}
    for bb in range(nb):
        kc = kc_ref[bb, 0]
        for i in range(NSA_HPG):
            s = jnp.dot(kc, qt_ref[bb, 0, head_rows(i), :], preferred_element_type=F32) + bias_ref[i]
            m = jnp.max(_fold(s, jnp.max), axis=0, keepdims=True)
            weights[bb, i] = jnp.exp2(s - m).astype(BF16)
    imps = []
    for bb in range(nb):
        lhs = jnp.concatenate([vct_ref[bb, 0], ones, c2s_ref[...]], axis=0)
        imp = jnp.zeros((n_sel, t), F32)
        for i in range(NSA_HPG):
            r = jnp.dot(lhs, weights[bb, i], preferred_element_type=F32)
            scale = jnp.where(any_visible, 1.0 / r[NSA_DH:NSA_DH + 1], 0.0)
            oc_ref[bb, 0, head_rows(i), :] = (r[:NSA_DH] * scale).astype(BF16)
            imp = imp + r[NSA_DH + SEL_ROWS:] * scale
        imps.append(imp)

    w = nb * t
    blk = lax.broadcasted_iota(jnp.int32, (n_sel, t), 0)
    cur = (tb * t + lax.broadcasted_iota(jnp.int32, (n_sel, t), 1)) // SEL_BLOCK
    forced = (blk == 0) | (blk == cur) | (blk == cur - 1)
    for bb in range(nb):
        score_scr[:, bb * t:(bb + 1) * t] = jnp.where(blk > cur, -jnp.inf, jnp.where(forced, jnp.inf, imps[bb]))
    groups = n_sel // 8
    per_tile = t // SEL_BLOCK
    sub = lax.broadcasted_iota(jnp.int32, (8, w), 0)
    cnt_scr[...] = jnp.zeros_like(cnt_scr)
    for j in range(groups):
        @pl.when(8 * j < (tb + 1) * per_tile)
        def _():
            src = [jnp.broadcast_to(score_scr[sp:sp + 1, :], (8, w)) for sp in range(8 * j, 8 * j + 8)]
            for v in range(groups):
                part = score_scr[8 * v:8 * v + 8, :]
                cnt = cnt_scr[8 * v:8 * v + 8, :]
                for sp, r in zip(range(8 * j, 8 * j + 8), src):
                    if j < v:
                        beats = r >= part
                    elif j > v:
                        beats = r > part
                    else:
                        beats = (r > part) | ((r == part) & (sub > sp - 8 * v))
                    cnt = cnt + jnp.where(beats, 1.0, 0.0)
                cnt_scr[8 * v:8 * v + 8, :] = cnt
    pad = jnp.zeros((SEL_ROWS - per_tile, t), F32)
    for v in range(groups):
        mask = jnp.where(cnt_scr[8 * v:8 * v + 8, :] < min(SEL_TOPN, n_sel), 0.0, NEG)
        for bb in range(nb):
            for r0 in range(0, 8, per_tile):
                kt = (8 * v + r0) // per_tile
                rows = mask[r0:r0 + per_tile, bb * t:(bb + 1) * t]
                sel_ref[bb, 0, 0, kt] = jnp.concatenate([rows, pad], axis=0).astype(BF16)


def _cmp_topk(q_t, k_cmp, v_cmp_t, bias_cmp, seq):
    bsz, nt, nq, t = q_t.shape
    n_cmp, n_sel = seq // CMP_STRIDE, seq // SEL_BLOCK
    grp = NSA_HPG * NSA_DH
    nb = CMP_BATCH if bsz % CMP_BATCH == 0 else 1
    c2s = jnp.asarray(_cmp_to_sel_t(seq), BF16)
    return pl.pallas_call(
        _cmp_topk_kernel,
        grid=(NSA_KV, nt, bsz // nb),
        in_specs=[pl.BlockSpec((nb, 1, grp, t), lambda g, i, b: (b, i, g, 0)),
                  pl.BlockSpec((nb, 1, n_cmp, NSA_DH), lambda g, i, b: (b, g, 0, 0)),
                  pl.BlockSpec((nb, 1, NSA_DH, n_cmp), lambda g, i, b: (b, g, 0, 0)),
                  pl.BlockSpec((NSA_HPG, n_cmp, t), lambda g, i, b: (g, 0, i)),
                  _const_spec((n_sel, n_cmp))],
        out_specs=[pl.BlockSpec((nb, 1, grp, t), lambda g, i, b: (b, i, g, 0)),
                   pl.BlockSpec((nb, 1, 1, nt, SEL_ROWS, t), lambda g, i, b: (b, g, i, 0, 0, 0))],
        out_shape=[jax.ShapeDtypeStruct((bsz, nt, nq, t), BF16),
                   jax.ShapeDtypeStruct((bsz, NSA_KV, nt, nt, SEL_ROWS, t), BF16)],
        scratch_shapes=[pltpu.VMEM((n_sel, nb * t), F32), pltpu.VMEM((n_sel, nb * t), F32)],
        compiler_params=_params(("arbitrary", "arbitrary", "arbitrary")),
        name="nsa_cmp_topk",
    )(q_t, k_cmp, v_cmp_t, bias_cmp, c2s)


def _sel_win_kernel(qa_ref, qb_ref, ks_ref, vst_ref, kw_ref, vwt_ref, sela_ref, selb_ref, tab_ref, ga_ref, gb_ref,
                    oca_ref, ocb_ref, olo_ref, ohi_ref, part_scr):
    h0 = pl.program_id(0) * SW_HEADS
    i = pl.program_id(2)
    nt, t = vst_ref.shape[1], vst_ref.shape[3]
    half = nt // 2
    tb = (i, nt - 1 - i)
    head_rows = lambda hh: slice(hh * NSA_DH, (hh + 1) * NSA_DH)
    q = [(qa_ref[0, 0, head_rows(hh), :], qb_ref[0, 0, head_rows(hh), :]) for hh in range(SW_HEADS)]
    sel = (sela_ref, selb_ref)

    def tile_rows(kt):
        return pl.ds(pl.multiple_of(kt * t, t), t)

    def pick(on_a, xa, xb):
        if isinstance(on_a, bool):
            return xa if on_a else xb
        return jnp.where(on_a, xa, xb)

    work = []
    for p in range(nt + 1):
        on_a = True if p == 0 else (False if p >= half else p <= i)
        kt = pick(on_a, p, jnp.maximum(p - i - 1, 0))
        work.append((on_a, kt, True, nt - p if on_a is False else pick(on_a, tb[0] - kt, nt - p)))
    for side in range(2):
        for delta in range(WIN_TILES):
            keep = True if (side == 1 and half >= WIN_TILES) else tb[side] >= delta
            work.append((side == 0, jnp.maximum(tb[side] - delta, 0), keep, delta))
    n_selpos = nt + 1

    ones = (lax.broadcasted_iota(jnp.int32, (SEL_ROWS, t), 0) == 0).astype(BF16)
    q_pad = jnp.zeros((KEY_LANES - NSA_DH - SEL_ROWS, t), BF16)
    tile_max = {}

    def scores(p, hh):
        on_a, kt, keep, delta = work[p]
        qp = pick(on_a, q[hh][0], q[hh][1])
        if p >= n_selpos:
            s = jnp.dot(kw_ref[0, 0, tile_rows(kt), :], qp, preferred_element_type=F32)
            s = s + tab_ref[hh, SEL_TABLES + delta]
        else:
            rows = pick(on_a, sel[0][0, 0, 0, kt], sel[1][0, 0, 0, kt])
            q_aug = jnp.concatenate([qp, rows, q_pad], axis=0)
            s = jnp.dot(ks_ref[0, 0, tile_rows(kt), :], q_aug, preferred_element_type=F32)
            if isinstance(delta, int) and delta >= SEL_TABLES - 1:
                m_tile = jnp.max(_fold(s, jnp.max), axis=0, keepdims=True)
                return m_tile + tab_ref[hh, SEL_TABLES - 1, 0:1, :], jnp.exp2(s - m_tile).astype(BF16)
            s = s + tab_ref[hh, jnp.minimum(delta, SEL_TABLES - 1)]
        m_tile = jnp.max(_fold(s, jnp.max), axis=0, keepdims=True)
        return m_tile, jnp.exp2(s - m_tile).astype(BF16)

    def values(p, hh, m_tile, pr):
        on_a, kt, keep, delta = work[p]
        br = int(p >= n_selpos)
        vt_ref = vwt_ref if br else vst_ref
        part_scr[hh, p] = jnp.dot(jnp.concatenate([vt_ref[0, kt], ones], axis=0), pr,
                                  preferred_element_type=F32)
        if not isinstance(keep, bool):
            m_tile = jnp.where(keep, m_tile, NEG)
        if isinstance(on_a, bool):
            tile_max[p, hh] = (m_tile, None) if on_a else (None, m_tile)
        else:
            tile_max[p, hh] = (jnp.where(on_a, m_tile, NEG), jnp.where(on_a, NEG, m_tile))

    def finish(hh):
        for side, (g_ref, oc_ref, o_ref) in enumerate(((ga_ref, oca_ref, olo_ref), (gb_ref, ocb_ref, ohi_ref))):
            outs = []
            for ps in (range(n_selpos), range(n_selpos, len(work))):
                mine = [p for p in ps if tile_max[p, hh][side] is not None]
                top = functools.reduce(jnp.maximum, [tile_max[p, hh][side] for p in mine])
                acc = jnp.zeros((NSA_DH + SEL_ROWS, t), F32)
                for p in mine:
                    acc = acc + part_scr[hh, p] * jnp.exp2(tile_max[p, hh][side] - top)
                outs.append(acc[:NSA_DH] / acc[NSA_DH:NSA_DH + 1])
            gate = lambda br: g_ref[0, pl.ds(br * NSA_HEADS + h0 + hh, 1), :]
            o_c = oc_ref[0, 0, head_rows(hh), :].astype(F32)
            o_ref[0, 0, head_rows(hh), :] = (gate(0) * o_c + gate(1) * outs[0] + gate(2) * outs[1]).astype(BF16)

    items = [(p, hh) for hh in range(SW_HEADS) for p in range(len(work))]
    pending = {}
    for step in range(len(items) + MXU_SKEW):
        if step < len(items):
            pending[step] = scores(*items[step])
        if step >= MXU_SKEW:
            p, hh = items[step - MXU_SKEW]
            values(p, hh, *pending.pop(step - MXU_SKEW))
            if p == len(work) - 1:
                finish(hh)


def _sel_win(q_t, ks, vs_t, kw, vw_t, sel, tables, gates_t, oc_t):
    bsz, nt, nq, t = q_t.shape
    seq = nt * t
    last = nt - 1
    per_group = NSA_HPG // SW_HEADS
    head_a = pl.BlockSpec((1, 1, SW_HEADS * NSA_DH, t), lambda h, b, i: (b, i, h, 0))
    head_b = pl.BlockSpec((1, 1, SW_HEADS * NSA_DH, t), lambda h, b, i: (b, last - i, h, 0))
    keys = pl.BlockSpec((1, 1, seq, NSA_DH), lambda h, b, i: (b, h // per_group, 0, 0))
    keys_sel = pl.BlockSpec((1, 1, seq, KEY_LANES), lambda h, b, i: (b, h // per_group, 0, 0))
    vals = pl.BlockSpec((1, nt, NSA_DH, t), lambda h, b, i: (b, 0, h // per_group, 0))
    sel_a = pl.BlockSpec((1, 1, 1, nt, SEL_ROWS, t), lambda h, b, i: (b, h // per_group, i, 0, 0, 0))
    sel_b = pl.BlockSpec((1, 1, 1, nt, SEL_ROWS, t), lambda h, b, i: (b, h // per_group, last - i, 0, 0, 0))
    gate_a = pl.BlockSpec((1, 3 * NSA_HEADS, t), lambda h, b, i: (b, 0, i))
    gate_b = pl.BlockSpec((1, 3 * NSA_HEADS, t), lambda h, b, i: (b, 0, last - i))
    out = jax.ShapeDtypeStruct((bsz, nt // 2, nq, t), BF16)
    return pl.pallas_call(
        _sel_win_kernel,
        grid=(NSA_HEADS // SW_HEADS, bsz, nt // 2),
        in_specs=[head_a, head_b, keys_sel, vals, keys, vals, sel_a, sel_b,
                  pl.BlockSpec((SW_HEADS, SEL_TABLES + WIN_TILES, t, t), lambda h, b, i: (h, 0, 0, 0)),
                  gate_a, gate_b, head_a, head_b],
        out_specs=[head_a, head_a],
        out_shape=[out, out],
        scratch_shapes=[pltpu.VMEM((SW_HEADS, nt + 1 + 2 * WIN_TILES, NSA_DH + SEL_ROWS, t), F32)],
        compiler_params=_params(("arbitrary", "arbitrary", "arbitrary")),
        name="nsa_sel_win",
    )(q_t, q_t, ks, vs_t, kw, vw_t, sel, sel, tables, gates_t, gates_t, oc_t, oc_t)


def kernel(x, c, mod_w, mod_b, norm_mix_g, norm_ffn_g, ab_w_in, ab_w_out, hgrn_lb_logits, hgrn_onorm_g, sconv_w, nsa_w_in, nsa_w_out, nsa_cmp_pos_k, nsa_cmp_pos_v, nsa_cmp_w1_k, nsa_cmp_w2_k, nsa_cmp_w1_v, nsa_cmp_w2_v, rel_bias, ffn_w_up, ffn_conv_w, ffn_w_down, final_norm_g):
    bsz, seq, d = x.shape
    assert d == D_MODEL and seq % TOK_TILE == 0
    lower = jnp.cumsum(jax.nn.softmax(hgrn_lb_logits.astype(F32), axis=0), axis=0)
    mod = _modulation(c, mod_w, mod_b)
    parts = lambda l: [mod[l, :, j * d:(j + 1) * d].reshape(bsz, 1, d) for j in range(6)]

    sh1, sc1, g1, sh2, sc2, g2 = parts(0)
    x = _mixer0(x, sh1, sc1, g1, norm_mix_g[0], ab_w_in[0], ab_w_out[0], lower[0], hgrn_onorm_g[0], sconv_w[0])
    x = _conv_ffn(x, sh2, sc2, g2, norm_ffn_g[0], ffn_w_up[0], ffn_conv_w[0], ffn_w_down[0], final_norm_g)

    sh1, sc1, g1, sh2, sc2, g2 = parts(1)
    kc, vc, ks, kw, q_t, gates_t, vs_t, vw_t = _nsa_proj(x, sh1, sc1, norm_mix_g[1], nsa_w_in[0])
    k_cmp, v_cmp_t = _compress(kc, vc, nsa_cmp_pos_k[0], nsa_cmp_pos_v[0],
                               nsa_cmp_w1_k[0], nsa_cmp_w2_k[0], nsa_cmp_w1_v[0], nsa_cmp_w2_v[0])
    near, bias_cmp = _bias_tables(rel_bias.astype(F32) * LOG2E, seq)
    oc_t, sel = _cmp_topk(q_t, k_cmp, v_cmp_t, bias_cmp, seq)
    o_lo, o_hi = _sel_win(q_t, ks, vs_t, kw, vw_t, sel, near, gates_t, oc_t)
    return _conv_ffn(x, sh2, sc2, g2, norm_ffn_g[1], ffn_w_up[1], ffn_conv_w[1], ffn_w_down[1], final_norm_g,
                     attn=(o_lo, o_hi, nsa_w_out[0], g1), final=True)
```

```python
import functools
import math

import jax
import jax.numpy as jnp
import numpy as np
from jax import lax
from jax.experimental import pallas as pl
from jax.experimental.pallas import tpu as pltpu

F32 = jnp.float32
BF16 = jnp.bfloat16

EPS = 1e-6
D_MODEL = 1024
D_FF = 2816

HG_HEADS = 4
HG_DK = 128
HG_WIDTH = HG_HEADS * HG_DK
SC_WIDTH = D_MODEL - HG_WIDTH
AB_IN = 4 * HG_WIDTH + 3 * SC_WIDTH
HG_CHUNK = 64
HG_LEVELS = 6
HG_MXU_LEVELS = ()

NSA_HEADS = 16
NSA_KV = 4
NSA_HPG = NSA_HEADS // NSA_KV
NSA_DH = D_MODEL // NSA_HEADS
CMP_LEN = 32
CMP_STRIDE = 16
CMP_HIDDEN = 256
SEL_BLOCK = 64
SEL_TOPN = 16
WINDOW = 512
REL_BUCKETS = 32
REL_MAX_DIST = 1024

TOK_TILE = 256
NEAR_TILES = 5
SEL_TABLES = NEAR_TILES + 1
WIN_TILES = WINDOW // TOK_TILE + 1
NEG = -1e30
LOG2E = math.log2(math.e)
KEY_LANES = 128
SEL_ROWS = 16
MIX_BATCH = 2
CMP_BATCH = 4
SW_HEADS = 4
MXU_SKEW = 5
VMEM_LIMIT = 56 * 1024 * 1024
HALO = 8

_NT = (((1,), (1,)), ((), ()))
_TN = (((0,), (0,)), ((), ()))


def _sigmoid(x):
    return 1.0 / (1.0 + jnp.exp(-x))


def _silu(x):
    return x * _sigmoid(x)


def _norm_mod(x, g, sc, sh):
    ms = jnp.mean(x * x, axis=-1, keepdims=True)
    return (x * lax.rsqrt(ms + EPS) * g) * (1.0 + sc) + sh


def _fold(x, op):
    return op(x.reshape(x.shape[0] // 8, 8, x.shape[1]), axis=0)


def _const_spec(shape):
    n = len(shape)
    return pl.BlockSpec(shape, lambda *_: (0,) * n, pipeline_mode=pl.Buffered(1))


def _params(sem):
    return pltpu.CompilerParams(dimension_semantics=sem, vmem_limit_bytes=VMEM_LIMIT)


def _mod_kernel(c_ref, w_ref, b_ref, o_ref):
    c = c_ref[...]
    o_ref[0] = jnp.dot(_silu(c), w_ref[0], preferred_element_type=F32,
                       precision=lax.Precision.HIGHEST) + b_ref[0]


def _modulation(c, mod_w, mod_b):
    depth, d, n = mod_w.shape
    bsz = c.shape[0]
    tn = 1024
    return pl.pallas_call(
        _mod_kernel,
        grid=(depth, n // tn),
        in_specs=[pl.BlockSpec((bsz, d), lambda l, j: (0, 0)),
                  pl.BlockSpec((1, d, tn), lambda l, j: (l, 0, j)),
                  pl.BlockSpec((1, 1, tn), lambda l, j: (l, 0, j))],
        out_specs=pl.BlockSpec((1, bsz, tn), lambda l, j: (l, 0, j)),
        out_shape=jax.ShapeDtypeStruct((depth, bsz, n), F32),
        compiler_params=_params(("arbitrary", "arbitrary")),
        name="adaln_mod",
    )(c, mod_w, mod_b.reshape(depth, 1, n))


def _hgrn_decay_matrix():
    t = np.arange(HG_CHUNK)
    tril = (t[None, :] <= t[:, None]).astype(np.float32)
    mats = [tril]
    for lvl in HG_MXU_LEVELS:
        m = 1 << (lvl - 1)
        mid = (t // (2 * m)) * (2 * m) + m - 1
        mats.append(tril - tril[mid])
    return np.concatenate(mats, axis=0)


def _mixer0_kernel(x_ref, sh_ref, sc_ref, gt_ref, ng_ref, win_ref, wout_ref, lb_ref, og_ref, cw_ref,
                   pm_ref, o_ref, proj_ref, cat_ref, st_ref, ubuf_ref, d_scr):
    nb, tile = x_ref.shape[0], x_ref.shape[1]
    c64 = HG_CHUNK
    n_hg = 4 * HG_WIDTH

    @pl.when(pl.program_id(1) == 0)
    def _():
        st_ref[...] = jnp.zeros_like(st_ref)
        ubuf_ref[:, 0:HALO, :] = jnp.zeros((nb, HALO, SC_WIDTH), F32)

    row = lax.broadcasted_iota(jnp.int32, (c64, HG_DK), 0)
    r64 = lax.broadcasted_iota(jnp.int32, (c64, c64), 0)
    c64i = lax.broadcasted_iota(jnp.int32, (c64, c64), 1)
    second = [None] + [(row & (1 << (lvl - 1))) != 0 for lvl in range(1, HG_LEVELS + 1)]
    same = [None] + [(r64 >> lvl) == (c64i >> lvl) for lvl in range(1, HG_LEVELS + 1)]
    eye = r64 == c64i
    pm = pm_ref[...]
    og = og_ref[...]
    units = [(c, h) for c in range(tile // c64) for h in range(HG_HEADS)]
    hm, factors = {}, {}

    def blk(s, c, h, j):
        return s, pl.ds(c * c64, c64), pl.ds(j * HG_WIDTH + h * HG_DK, HG_DK)

    def project_heads(s):
        hm[s] = _norm_mod(x_ref[s], ng_ref[...], sc_ref[s], sh_ref[s]).astype(BF16)
        proj_ref[s, :, :n_hg] = jnp.dot(hm[s], win_ref[:, :n_hg], preferred_element_type=F32)

    def project_conv(s):
        proj_ref[s, :, n_hg:] = jnp.dot(hm[s], win_ref[:, n_hg:], preferred_element_type=F32)

    def decay_sums(s):
        for u, (c, h) in enumerate(units):
            lb = lb_ref[:, h * HG_DK:(h + 1) * HG_DK]
            fg = lb + (1.0 - lb) * _sigmoid(proj_ref[blk(s, c, h, 1)])
            proj_ref[blk(s, c, h, 1)] = 1.0 - fg
            lg = jnp.log(fg)
            lg_hi = lg.astype(BF16)
            lg_lo = (lg - lg_hi.astype(F32)).astype(BF16)
            dall = jnp.dot(pm, jnp.concatenate([lg_hi, lg_lo], axis=1), preferred_element_type=F32)
            d_scr[s, u] = dall[:, :HG_DK] + dall[:, HG_DK:]

    def level_products(s):
        intra, q_in, k_out, decay = [], [], [], []
        for u, (c, h) in enumerate(units):
            q = proj_ref[blk(s, c, h, 0)]
            k = proj_ref[blk(s, c, h, 1)]
            b = d_scr[s, u, 0:c64, :]
            a = jnp.zeros((c64, c64), F32)
            for lvl in range(1, HG_LEVELS + 1):
                m = 1 << (lvl - 1)
                if lvl == 1:
                    qt = jnp.where(second[lvl], q * (1.0 - k), 0.0).astype(BF16)
                    kt = jnp.where(second[lvl], 0.0, k).astype(BF16)
                else:
                    if lvl in HG_MXU_LEVELS:
                        at = (1 + HG_MXU_LEVELS.index(lvl)) * c64
                        dlt = d_scr[s, u, at:at + c64, :]
                    else:
                        mids = [jnp.broadcast_to(d_scr[s, u, r0 + m - 1:r0 + m, :], (2 * m, HG_DK))
                                for r0 in range(0, c64, 2 * m)]
                        dlt = b - (mids[0] if len(mids) == 1 else jnp.concatenate(mids, axis=0))
                    e = jnp.exp(-jnp.abs(dlt))
                    qt = jnp.where(second[lvl], q * e, 0.0).astype(BF16)
                    kt = jnp.where(second[lvl], 0.0, k * e).astype(BF16)
                al = lax.dot_general(qt, kt, _NT, preferred_element_type=F32)
                a = a + (al if lvl == HG_LEVELS else jnp.where(same[lvl], al, 0.0))
            intra.append(jnp.where(eye, jnp.sum(q * k, axis=-1, keepdims=True), a).astype(BF16))
            b_last = d_scr[s, u, c64 - 1:c64, :]
            q_in.append((q * jnp.exp(b)).astype(BF16))
            k_out.append((k * jnp.exp(b_last - b)).astype(BF16))
            decay.append(jnp.exp(b_last))
        factors[s] = (intra, q_in, k_out, decay)

    def recurrence(s):
        intra, q_in, k_out, decay = factors[s]
        state = [st_ref[s, h] for h in range(HG_HEADS)]
        for u, (c, h) in enumerate(units):
            vb = proj_ref[blk(s, c, h, 2)].astype(BF16)
            o = jnp.dot(intra[u], vb, preferred_element_type=F32)
            o = o + lax.dot_general(q_in[u], state[h].astype(BF16), _NT, preferred_element_type=F32)
            state[h] = state[h] * decay[u] + lax.dot_general(vb, k_out[u], _TN, preferred_element_type=F32)
            on = o * lax.rsqrt(jnp.mean(o * o, axis=-1, keepdims=True) + EPS) * og
            gate = _silu(proj_ref[blk(s, c, h, 3)])
            cat_ref[s, pl.ds(c * c64, c64), h * HG_DK:(h + 1) * HG_DK] = (on * gate).astype(BF16)
        for h in range(HG_HEADS):
            st_ref[s, h] = state[h]

    def short_conv_and_out(s):
        off = n_hg
        u = (proj_ref[s, :, off + SC_WIDTH:off + 2 * SC_WIDTH]
             * proj_ref[s, :, off + 2 * SC_WIDTH:off + 3 * SC_WIDTH])
        ubuf_ref[s, HALO:HALO + tile, :] = u
        cw = cw_ref[...]
        conv = (ubuf_ref[s, HALO - 2:HALO - 2 + tile, :] * cw[0:1]
                + ubuf_ref[s, HALO - 1:HALO - 1 + tile, :] * cw[1:2] + u * cw[2:3])
        ubuf_ref[s, 0:HALO, :] = u[tile - HALO:tile, :]
        cat_ref[s, :, HG_WIDTH:] = (proj_ref[s, :, off:off + SC_WIDTH] * conv).astype(BF16)
        y = jnp.dot(cat_ref[s], wout_ref[...], preferred_element_type=F32)
        o_ref[s] = x_ref[s] + gt_ref[s] * y

    for s in range(nb):
        project_heads(s)
    decay_sums(0)
    project_conv(0)
    for s in range(1, nb):
        decay_sums(s)
        level_products(s - 1)
        project_conv(s)
    level_products(nb - 1)
    for s in range(nb):
        recurrence(s)
        short_conv_and_out(s)


def _mixer0(x, sh, sc, gt, norm_g, w_in, w_out, lower, onorm_g, sconv_w):
    bsz, seq, d = x.shape
    t = TOK_TILE
    nb = MIX_BATCH if bsz % MIX_BATCH == 0 else 1
    pm = jnp.asarray(_hgrn_decay_matrix(), BF16)
    vec = pl.BlockSpec((nb, 1, d), lambda b, i: (b, 0, 0))
    return pl.pallas_call(
        _mixer0_kernel,
        grid=(bsz // nb, seq // t),
        in_specs=[pl.BlockSpec((nb, t, d), lambda b, i: (b, i, 0)), vec, vec, vec,
                  _const_spec((1, d)), _const_spec((d, AB_IN)), _const_spec((d, d)),
                  _const_spec((1, HG_WIDTH)), _const_spec((1, HG_DK)), _const_spec((3, SC_WIDTH)),
                  _const_spec(pm.shape)],
        out_specs=pl.BlockSpec((nb, t, d), lambda b, i: (b, i, 0)),
        out_shape=jax.ShapeDtypeStruct(x.shape, F32),
        scratch_shapes=[pltpu.VMEM((nb, t, AB_IN), F32), pltpu.VMEM((nb, t, d), BF16),
                        pltpu.VMEM((nb, HG_HEADS, HG_DK, HG_DK), F32),
                        pltpu.VMEM((nb, t + HALO, SC_WIDTH), F32),
                        pltpu.VMEM((nb, t // HG_CHUNK * HG_HEADS, pm.shape[0], HG_DK), F32)],
        compiler_params=_params(("arbitrary", "arbitrary")),
        name="mixer_hgrn_sconv",
    )(x, sh, sc, gt, norm_g.reshape(1, d), w_in.astype(BF16), w_out.astype(BF16),
      lower.reshape(1, HG_WIDTH), onorm_g.reshape(1, HG_DK), sconv_w, pm)


MXU_DIM = 256
FF_CHUNKS = ((0, 5 * MXU_DIM), (5 * MXU_DIM, D_FF))


def _ffn_kernel(*refs, has_attn, final):
    if has_attn:
        (x_ref, olo_ref, ohi_ref, wo_ref, g1_ref, sh_ref, sc_ref, g2_ref, ng_ref, wup_ref, cw_ref, wdn_ref, fg_ref,
         o_ref, gbuf_ref, carry_ref) = refs
    else:
        (x_ref, sh_ref, sc_ref, g2_ref, ng_ref, wup_ref, cw_ref, wdn_ref, fg_ref,
         o_ref, gbuf_ref, carry_ref) = refs
    tile = x_ref.shape[1]

    @pl.when(pl.program_id(1) == 0)
    def _():
        carry_ref[...] = jnp.zeros_like(carry_ref)

    x = x_ref[0]
    if has_attn:
        o_t = jnp.where(pl.program_id(1) < pl.num_programs(1) // 2, olo_ref[0, 0], ohi_ref[0, 0])
        x = x + g1_ref[0] * lax.dot_general(o_t, wo_ref[...], _TN, preferred_element_type=F32)
    hf = _norm_mod(x, ng_ref[...], sc_ref[0], sh_ref[0]).astype(BF16)
    gbuf_ref[0:HALO, :] = carry_ref[...]
    gates, ups = [], []
    for c0, c1 in FF_CHUNKS:
        gate = jnp.dot(hf, wup_ref[:, c0:c1], preferred_element_type=F32)
        ups.append(jnp.dot(hf, wup_ref[:, D_FF + c0:D_FF + c1], preferred_element_type=F32))
        gbuf_ref[HALO:HALO + tile, c0:c1] = gate
        carry_ref[:, c0:c1] = gate[tile - HALO:tile, :]
        gates.append(gate)
    acc = jnp.zeros((tile, D_MODEL), F32)
    for j, (c0, c1) in enumerate(FF_CHUNKS):
        cols = slice(c0, c1)
        cw = cw_ref[:, cols]
        conv = (gbuf_ref[HALO - 2:HALO - 2 + tile, cols] * cw[0:1] + gbuf_ref[HALO - 1:HALO - 1 + tile, cols] * cw[1:2]
                + gates[j] * cw[2:3])
        act = (_silu(conv) * ups[j]).astype(BF16)
        acc = acc + jnp.dot(act, wdn_ref[cols, :], preferred_element_type=F32)
    out = x + g2_ref[0] * acc
    if final:
        out = out * lax.rsqrt(jnp.mean(out * out, axis=-1, keepdims=True) + EPS) * fg_ref[...]
    o_ref[0] = out


def _conv_ffn(x, sh, sc, g2, norm_g, w_up, conv_w, w_down, final_g, attn=None, final=False):
    bsz, seq, d = x.shape
    t = TOK_TILE
    vec = pl.BlockSpec((1, 1, d), lambda b, i: (b, 0, 0))
    args, specs = [x], [pl.BlockSpec((1, t, d), lambda b, i: (b, i, 0))]
    if attn is not None:
        o_lo, o_hi, w_o, g1 = attn
        top = o_lo.shape[1] - 1
        last = seq // t - 1
        args += [o_lo, o_hi, w_o.astype(BF16), g1]
        specs += [pl.BlockSpec((1, 1, d, t), lambda b, i: (b, jnp.minimum(i, top), 0, 0)),
                  pl.BlockSpec((1, 1, d, t), lambda b, i: (b, jnp.minimum(last - i, top), 0, 0)),
                  _const_spec((d, d)), vec]
    args += [sh, sc, g2, norm_g.reshape(1, d), w_up.astype(BF16), conv_w, w_down.astype(BF16),
             final_g.reshape(1, d)]
    specs += [vec, vec, vec, _const_spec((1, d)), _const_spec((d, 2 * D_FF)), _const_spec((3, D_FF)),
              _const_spec((D_FF, d)), _const_spec((1, d))]
    return pl.pallas_call(
        functools.partial(_ffn_kernel, has_attn=attn is not None, final=final),
        grid=(bsz, seq // t),
        in_specs=specs,
        out_specs=pl.BlockSpec((1, t, d), lambda b, i: (b, i, 0)),
        out_shape=jax.ShapeDtypeStruct(x.shape, F32),
        scratch_shapes=[pltpu.VMEM((t + HALO, D_FF), F32), pltpu.VMEM((HALO, D_FF), F32)],
        compiler_params=_params(("arbitrary", "arbitrary")),
        name="conv_ffn_attnproj" if attn is not None else "conv_ffn",
    )(*args)


KV_ROWS = NSA_KV * NSA_DH


def _nsa_proj_kernel(x_ref, sh_ref, sc_ref, ng_ref, wa_ref, wq_ref, wg_ref, wv_ref, hot_ref, perm_ref,
                     kc_ref, vc_ref, ks_ref, kw_ref, qt_ref, gt_ref, vst_ref, vwt_ref):
    hm = _norm_mod(x_ref[0], ng_ref[...], sc_ref[0], sh_ref[0]).astype(BF16)
    kv = jnp.dot(hm, wa_ref[...], preferred_element_type=F32).astype(BF16)
    cmp_in = jnp.dot(perm_ref[...], kv[:, :2 * KV_ROWS], preferred_element_type=F32).astype(BF16)
    n_grp = perm_ref.shape[0] // CMP_STRIDE
    for g in range(NSA_KV):
        part = lambda j: kv[:, j * KV_ROWS + g * NSA_DH:j * KV_ROWS + (g + 1) * NSA_DH]
        for p in range(CMP_STRIDE):
            rows = slice(p * n_grp, (p + 1) * n_grp)
            kc_ref[0, g, p] = cmp_in[rows, g * NSA_DH:(g + 1) * NSA_DH]
            vc_ref[0, g, p] = cmp_in[rows, KV_ROWS + g * NSA_DH:KV_ROWS + (g + 1) * NSA_DH]
        ks_ref[0, g] = jnp.concatenate([part(2), hot_ref[...]], axis=1)
        kw_ref[0, g] = part(3)
    qt = lax.dot_general(wq_ref[...], hm, _NT, preferred_element_type=F32)
    qt_ref[0, 0] = (qt * (NSA_DH ** -0.5 * LOG2E)).astype(BF16)
    gt_ref[0] = _sigmoid(lax.dot_general(wg_ref[...], hm, _NT, preferred_element_type=F32))
    vt = lax.dot_general(wv_ref[...], hm, _NT, preferred_element_type=F32).astype(BF16)
    vst_ref[0, 0] = vt[:KV_ROWS]
    vwt_ref[0, 0] = vt[KV_ROWS:]


def _nsa_proj(x, sh, sc, norm_g, w_in):
    bsz, seq, d = x.shape
    t = TOK_TILE
    nt = seq // t
    nq, ng = NSA_HEADS * NSA_DH, 3 * NSA_HEADS
    o = nq + ng
    w = w_in.astype(BF16)
    col = lambda j: w[:, o + j * KV_ROWS:o + (j + 1) * KV_ROWS]
    w_a = jnp.concatenate([col(0), col(1), col(2), col(4)], axis=1)
    w_q = w[:, :nq].T
    w_g = w[:, nq:o].T
    w_v = jnp.concatenate([col(3), col(5)], axis=1).T
    hot = (np.arange(t)[:, None] // SEL_BLOCK == np.arange(KEY_LANES - NSA_DH)[None, :]).astype(np.float32)
    n_grp = t // CMP_STRIDE
    r = np.arange(t)
    perm = (r[None, :] == ((r % n_grp) * CMP_STRIDE + r // n_grp)[:, None]).astype(np.float32)
    vec = pl.BlockSpec((1, 1, d), lambda b, i: (b, 0, 0))
    tiled = lambda r: pl.BlockSpec((1, 1, r, t), lambda b, i: (b, i, 0, 0))
    grouped = lambda n: pl.BlockSpec((1, NSA_KV, t, n), lambda b, i: (b, 0, i, 0))
    rows = lambda n: jax.ShapeDtypeStruct((bsz, NSA_KV, seq, n), BF16)
    by_offset = pl.BlockSpec((1, NSA_KV, CMP_STRIDE, n_grp, NSA_DH), lambda b, i: (b, 0, 0, i, 0))
    offset_major = jax.ShapeDtypeStruct((bsz, NSA_KV, CMP_STRIDE, seq // CMP_STRIDE, NSA_DH), BF16)
    return pl.pallas_call(
        _nsa_proj_kernel,
        grid=(bsz, nt),
        in_specs=[pl.BlockSpec((1, t, d), lambda b, i: (b, i, 0)), vec, vec, _const_spec((1, d)),
                  _const_spec((d, 4 * KV_ROWS)), _const_spec((nq, d)), _const_spec((ng, d)),
                  _const_spec((2 * KV_ROWS, d)), _const_spec(hot.shape), _const_spec(perm.shape)],
        out_specs=[by_offset, by_offset, grouped(KEY_LANES), grouped(NSA_DH), tiled(nq),
                   pl.BlockSpec((1, ng, t), lambda b, i: (b, 0, i)), tiled(KV_ROWS), tiled(KV_ROWS)],
        out_shape=[offset_major, offset_major, rows(KEY_LANES), rows(NSA_DH),
                   jax.ShapeDtypeStruct((bsz, nt, nq, t), BF16),
                   jax.ShapeDtypeStruct((bsz, ng, seq), F32),
                   jax.ShapeDtypeStruct((bsz, nt, KV_ROWS, t), BF16),
                   jax.ShapeDtypeStruct((bsz, nt, KV_ROWS, t), BF16)],
        compiler_params=_params(("arbitrary", "arbitrary")),
        name="nsa_in_proj",
    )(x, sh, sc, norm_g.reshape(1, d), w_a, w_q, w_g, w_v, jnp.asarray(hot, BF16), jnp.asarray(perm, BF16))


def _compress_kernel(hk_ref, hv_ref, pk_ref, pv_ref, w1k_ref, w2k_ref, w1v_ref, w2vt_ref, kc_ref, vct_ref):
    half = CMP_STRIDE * NSA_DH

    def hidden(h_ref, pos_ref, w1_ref):
        n = h_ref.shape[3]
        top = jnp.zeros((n, CMP_HIDDEN), F32)
        bot = jnp.zeros((n, CMP_HIDDEN), F32)
        per = MXU_DIM // NSA_DH
        for p0 in range(0, CMP_STRIDE, per):
            h = jnp.concatenate([h_ref[0, 0, p] for p in range(p0, p0 + per)], axis=1)
            rows = slice(p0 * NSA_DH, (p0 + per) * NSA_DH)
            top = top + jnp.dot(h, w1_ref[rows, :], preferred_element_type=F32)
            bot = bot + jnp.dot(h, w1_ref[half + p0 * NSA_DH:half + (p0 + per) * NSA_DH, :],
                                preferred_element_type=F32)
        pos = jnp.dot(jnp.broadcast_to(pos_ref[...], (8, 2 * half)).astype(BF16), w1_ref[...],
                      preferred_element_type=F32)[0:1]
        return _silu(top + pltpu.roll(bot, n - 1, 0) + pos).astype(BF16)

    kc_ref[0, 0] = jnp.dot(hidden(hk_ref, pk_ref, w1k_ref), w2k_ref[...],
                           preferred_element_type=F32).astype(BF16)
    vct_ref[0, 0] = lax.dot_general(w2vt_ref[...], hidden(hv_ref, pv_ref, w1v_ref), _NT,
                                    preferred_element_type=F32).astype(BF16)


def _compress(hk, hv, pos_k, pos_v, w1_k, w2_k, w1_v, w2_v):
    bsz, g, stride, n, dh = hk.shape
    width = stride * dh
    blk = pl.BlockSpec((1, 1, stride, n, dh), lambda b, j: (b, j, 0, 0, 0))
    return pl.pallas_call(
        _compress_kernel,
        grid=(bsz, g),
        in_specs=[blk, blk, _const_spec((1, width * 2)), _const_spec((1, width * 2)),
                  _const_spec((2 * width, CMP_HIDDEN)), _const_spec((CMP_HIDDEN, NSA_DH)),
                  _const_spec((2 * width, CMP_HIDDEN)), _const_spec((NSA_DH, CMP_HIDDEN))],
        out_specs=[pl.BlockSpec((1, 1, n, NSA_DH), lambda b, j: (b, j, 0, 0)),
                   pl.BlockSpec((1, 1, NSA_DH, n), lambda b, j: (b, j, 0, 0))],
        out_shape=[jax.ShapeDtypeStruct((bsz, g, n, NSA_DH), BF16),
                   jax.ShapeDtypeStruct((bsz, g, NSA_DH, n), BF16)],
        compiler_params=_params(("arbitrary", "arbitrary")),
        name="nsa_compress",
    )(hk, hv, pos_k.reshape(1, -1), pos_v.reshape(1, -1), w1_k.astype(BF16), w2_k.astype(BF16),
      w1_v.astype(BF16), w2_v.T.astype(BF16))


def _bucket_upper_bounds():
    n = np.arange(4 * REL_MAX_DIST, dtype=np.float64)
    exact = REL_BUCKETS // 2
    large = exact + (np.log(np.maximum(n, exact) / exact) / math.log(REL_MAX_DIST / exact)
                     * (REL_BUCKETS - exact)).astype(np.int64)
    bucket = np.where(n < exact, n.astype(np.int64), np.minimum(large, REL_BUCKETS - 1))
    return [int(np.max(np.nonzero(bucket <= j)[0])) for j in range(REL_BUCKETS - 1)]


def _bias_of_dist(dist, rb_ref, heads, uppers):
    vals = [jnp.full(dist.shape, rb_ref[REL_BUCKETS - 1, h], F32) for h in heads]
    for j in range(REL_BUCKETS - 2, -1, -1):
        m = dist <= uppers[j]
        vals = [jnp.where(m, rb_ref[j, h], v) for h, v in zip(heads, vals)]
    return vals


def _bias_near_kernel(rb_ref, o_ref, line_scr, *, uppers):
    d = pl.program_id(0)
    t = o_ref.shape[2]
    heads = list(range(NSA_HEADS))

    @pl.when(d == 0)
    def _():
        line = ((lax.broadcasted_iota(jnp.int32, (8, t), 0) - 1) * t + lax.broadcasted_iota(jnp.int32, (8, t), 1))
        for h, v in zip(heads, _bias_of_dist(line, rb_ref, heads, uppers)):
            line_scr[h] = v

    is_win = d >= SEL_TABLES
    delta = jnp.where(is_win, d - SEL_TABLES, d)
    limit = jnp.where(is_win, WINDOW, (SEL_TABLES + 1) * t)
    dist = delta * t + lax.broadcasted_iota(jnp.int32, (t, t), 1) - lax.broadcasted_iota(jnp.int32, (t, t), 0)
    valid = (dist >= 0) & (dist < limit)
    for h in heads:
        span = jnp.concatenate([line_scr[h, pl.ds(delta, 1), :], line_scr[h, pl.ds(delta + 1, 1), :]], axis=1)
        rolled = pltpu.roll(jnp.broadcast_to(span, (t, 2 * t)), 0, 1, stride=1, stride_axis=0)
        o_ref[h, 0] = jnp.where(valid, rolled[:, t:], NEG)


CMP_BAND = 72


def _bias_cmp_kernel(rb_ref, o_ref, band_scr, *, uppers):
    tb = pl.program_id(0)
    n, t = o_ref.shape[1], o_ref.shape[2]
    heads = list(range(NSA_HEADS))
    per_tile = t // CMP_STRIDE
    first = CMP_STRIDE * (CMP_BAND - per_tile) - (CMP_LEN - 1)
    assert first + CMP_STRIDE > uppers[-1] and first + (t - 1) - CMP_STRIDE * CMP_BAND < 0

    @pl.when(tb == 0)
    def _():
        for r0 in range(0, CMP_BAND, 8):
            row = r0 + lax.broadcasted_iota(jnp.int32, (8, t), 0)
            dist = first + lax.broadcasted_iota(jnp.int32, (8, t), 1) - CMP_STRIDE * row
            for h, v in zip(heads, _bias_of_dist(dist, rb_ref, heads, uppers)):
                band_scr[h, r0:r0 + 8, :] = jnp.where(dist >= 0, v, NEG)

    def strip(i, carry):
        r0 = pl.multiple_of(i * 8, 8)
        rel = r0 - (tb * per_tile - (CMP_BAND - per_tile))
        at = pl.multiple_of(jnp.clip(rel, 0, CMP_BAND - 8), 8)
        for h in heads:
            inside = band_scr[h, pl.ds(at, 8), :]
            far = jnp.full((8, t), rb_ref[REL_BUCKETS - 1, h], F32)
            o_ref[h, pl.ds(r0, 8), :] = jnp.where(rel < 0, far, jnp.where(rel >= CMP_BAND, NEG, inside))
        return carry

    lax.fori_loop(0, n // 8, strip, 0)


def _bias_tables(rel_bias, seq):
    t = TOK_TILE
    n_cmp = seq // CMP_STRIDE
    uppers = _bucket_upper_bounds()
    smem = pl.BlockSpec(memory_space=pltpu.SMEM)
    near = pl.pallas_call(
        functools.partial(_bias_near_kernel, uppers=uppers),
        grid=(SEL_TABLES + WIN_TILES,),
        in_specs=[smem],
        out_specs=pl.BlockSpec((NSA_HEADS, 1, t, t), lambda d: (0, d, 0, 0)),
        out_shape=jax.ShapeDtypeStruct((NSA_HEADS, SEL_TABLES + WIN_TILES, t, t), F32),
        scratch_shapes=[pltpu.VMEM((NSA_HEADS, 8, t), F32)],
        compiler_params=_params(("arbitrary",)),
        name="relbias_near",
    )(rel_bias)
    cmp_b = pl.pallas_call(
        functools.partial(_bias_cmp_kernel, uppers=uppers),
        grid=(seq // t,),
        in_specs=[smem],
        out_specs=pl.BlockSpec((NSA_HEADS, n_cmp, t), lambda i: (0, 0, i)),
        out_shape=jax.ShapeDtypeStruct((NSA_HEADS, n_cmp, seq), F32),
        scratch_shapes=[pltpu.VMEM((NSA_HEADS, CMP_BAND, t), F32)],
        compiler_params=_params(("arbitrary",)),
        name="relbias_cmp",
    )(rel_bias)
    return near, cmp_b


def _cmp_to_sel_t(seq):
    n_cmp, n_sel = seq // CMP_STRIDE, seq // SEL_BLOCK
    c_start = np.arange(n_cmp)[:, None] * CMP_STRIDE
    s_start = np.arange(n_sel)[None, :] * SEL_BLOCK
    inside = np.clip(np.minimum(c_start + CMP_LEN, s_start + SEL_BLOCK) - np.maximum(c_start, s_start), 0, None)
    return (inside / CMP_LEN).T.astype(np.float32)


def _cmp_topk_kernel(qt_ref, kc_ref, vct_ref, bias_ref, c2s_ref, oc_ref, sel_ref, score_scr, cnt_scr):
    tb = pl.program_id(1)
    n, t = bias_ref.shape[1], bias_ref.shape[2]
    n_sel = c2s_ref.shape[0]
    nb = qt_ref.shape[0]
    tok = tb * t + lax.broadcasted_iota(jnp.int32, (1, t), 1)
    any_visible = tok >= CMP_LEN - 1
    ones = (lax.broadcasted_iota(jnp.int32, (SEL_ROWS, n), 0) == 0).astype(BF16)
    head_rows = lambda i: slice(i * NSA_DH, (i + 1) * NSA_DH)
    weights = {}
    for bb in range(nb):
        kc = kc_ref[bb, 0]
        for i in range(NSA_HPG):
            s = jnp.dot(kc, qt_ref[bb, 0, head_rows(i), :], preferred_element_type=F32) + bias_ref[i]
            m = jnp.max(_fold(s, jnp.max), axis=0, keepdims=True)
            weights[bb, i] = jnp.exp2(s - m).astype(BF16)
    imps = []
    for bb in range(nb):
        lhs = jnp.concatenate([vct_ref[bb, 0], ones, c2s_ref[...]], axis=0)
        imp = jnp.zeros((n_sel, t), F32)
        for i in range(NSA_HPG):
            r = jnp.dot(lhs, weights[bb, i], preferred_element_type=F32)
            scale = jnp.where(any_visible, 1.0 / r[NSA_DH:NSA_DH + 1], 0.0)
            oc_ref[bb, 0, head_rows(i), :] = (r[:NSA_DH] * scale).astype(BF16)
            imp = imp + r[NSA_DH + SEL_ROWS:] * scale
        imps.append(imp)

    w = nb * t
    blk = lax.broadcasted_iota(jnp.int32, (n_sel, t), 0)
    cur = (tb * t + lax.broadcasted_iota(jnp.int32, (n_sel, t), 1)) // SEL_BLOCK
    forced = (blk == 0) | (blk == cur) | (blk == cur - 1)
    for bb in range(nb):
        score_scr[:, bb * t:(bb + 1) * t] = jnp.where(blk > cur, -jnp.inf, jnp.where(forced, jnp.inf, imps[bb]))
    groups = n_sel // 8
    per_tile = t // SEL_BLOCK
    sub = lax.broadcasted_iota(jnp.int32, (8, w), 0)
    cnt_scr[...] = jnp.zeros_like(cnt_scr)
    for j in range(groups):
        @pl.when(8 * j < (tb + 1) * per_tile)
        def _():
            src = [jnp.broadcast_to(score_scr[sp:sp + 1, :], (8, w)) for sp in range(8 * j, 8 * j + 8)]
            for v in range(groups):
                part = score_scr[8 * v:8 * v + 8, :]
                cnt = cnt_scr[8 * v:8 * v + 8, :]
                for sp, r in zip(range(8 * j, 8 * j + 8), src):
                    if j < v:
                        beats = r >= part
                    elif j > v:
                        beats = r > part
                    else:
                        beats = (r > part) | ((r == part) & (sub > sp - 8 * v))
                    cnt = cnt + jnp.where(beats, 1.0, 0.0)
                cnt_scr[8 * v:8 * v + 8, :] = cnt
    pad = jnp.zeros((SEL_ROWS - per_tile, t), F32)
    for v in range(groups):
        mask = jnp.where(cnt_scr[8 * v:8 * v + 8, :] < min(SEL_TOPN, n_sel), 0.0, NEG)
        for bb in range(nb):
            for r0 in range(0, 8, per_tile):
                kt = (8 * v + r0) // per_tile
                rows = mask[r0:r0 + per_tile, bb * t:(bb + 1) * t]
                sel_ref[bb, 0, 0, kt] = jnp.concatenate([rows, pad], axis=0).astype(BF16)


def _cmp_topk(q_t, k_cmp, v_cmp_t, bias_cmp, seq):
    bsz, nt, nq, t = q_t.shape
    n_cmp, n_sel = seq // CMP_STRIDE, seq // SEL_BLOCK
    grp = NSA_HPG * NSA_DH
    nb = CMP_BATCH if bsz % CMP_BATCH == 0 else 1
    c2s = jnp.asarray(_cmp_to_sel_t(seq), BF16)
    return pl.pallas_call(
        _cmp_topk_kernel,
        grid=(NSA_KV, nt, bsz // nb),
        in_specs=[pl.BlockSpec((nb, 1, grp, t), lambda g, i, b: (b, i, g, 0)),
                  pl.BlockSpec((nb, 1, n_cmp, NSA_DH), lambda g, i, b: (b, g, 0, 0)),
                  pl.BlockSpec((nb, 1, NSA_DH, n_cmp), lambda g, i, b: (b, g, 0, 0)),
                  pl.BlockSpec((NSA_HPG, n_cmp, t), lambda g, i, b: (g, 0, i)),
                  _const_spec((n_sel, n_cmp))],
        out_specs=[pl.BlockSpec((nb, 1, grp, t), lambda g, i, b: (b, i, g, 0)),
                   pl.BlockSpec((nb, 1, 1, nt, SEL_ROWS, t), lambda g, i, b: (b, g, i, 0, 0, 0))],
        out_shape=[jax.ShapeDtypeStruct((bsz, nt, nq, t), BF16),
                   jax.ShapeDtypeStruct((bsz, NSA_KV, nt, nt, SEL_ROWS, t), BF16)],
        scratch_shapes=[pltpu.VMEM((n_sel, nb * t), F32), pltpu.VMEM((n_sel, nb * t), F32)],
        compiler_params=_params(("arbitrary", "arbitrary", "arbitrary")),
        name="nsa_cmp_topk",
    )(q_t, k_cmp, v_cmp_t, bias_cmp, c2s)


def _sel_win_kernel(qa_ref, qb_ref, ks_ref, vst_ref, kw_ref, vwt_ref, sela_ref, selb_ref, tab_ref, ga_ref, gb_ref,
                    oca_ref, ocb_ref, olo_ref, ohi_ref, part_scr):
    h0 = pl.program_id(0) * SW_HEADS
    i = pl.program_id(2)
    nt, t = vst_ref.shape[1], vst_ref.shape[3]
    half = nt // 2
    tb = (i, nt - 1 - i)
    head_rows = lambda hh: slice(hh * NSA_DH, (hh + 1) * NSA_DH)
    q = [(qa_ref[0, 0, head_rows(hh), :], qb_ref[0, 0, head_rows(hh), :]) for hh in range(SW_HEADS)]
    sel = (sela_ref, selb_ref)

    def tile_rows(kt):
        return pl.ds(pl.multiple_of(kt * t, t), t)

    def pick(on_a, xa, xb):
        if isinstance(on_a, bool):
            return xa if on_a else xb
        return jnp.where(on_a, xa, xb)

    work = []
    for p in range(nt + 1):
        on_a = True if p == 0 else (False if p >= half else p <= i)
        kt = pick(on_a, p, jnp.maximum(p - i - 1, 0))
        work.append((on_a, kt, True, nt - p if on_a is False else pick(on_a, tb[0] - kt, nt - p)))
    for side in range(2):
        for delta in range(WIN_TILES):
            keep = True if (side == 1 and half >= WIN_TILES) else tb[side] >= delta
            work.append((side == 0, jnp.maximum(tb[side] - delta, 0), keep, delta))
    n_selpos = nt + 1

    ones = (lax.broadcasted_iota(jnp.int32, (SEL_ROWS, t), 0) == 0).astype(BF16)
    q_pad = jnp.zeros((KEY_LANES - NSA_DH - SEL_ROWS, t), BF16)
    tile_max = {}

    def scores(p, hh):
        on_a, kt, keep, delta = work[p]
        qp = pick(on_a, q[hh][0], q[hh][1])
        if p >= n_selpos:
            s = jnp.dot(kw_ref[0, 0, tile_rows(kt), :], qp, preferred_element_type=F32)
            s = s + tab_ref[hh, SEL_TABLES + delta]
        else:
            rows = pick(on_a, sel[0][0, 0, 0, kt], sel[1][0, 0, 0, kt])
            q_aug = jnp.concatenate([qp, rows, q_pad], axis=0)
            s = jnp.dot(ks_ref[0, 0, tile_rows(kt), :], q_aug, preferred_element_type=F32)
            if isinstance(delta, int) and delta >= SEL_TABLES - 1:
                m_tile = jnp.max(_fold(s, jnp.max), axis=0, keepdims=True)
                return m_tile + tab_ref[hh, SEL_TABLES - 1, 0:1, :], jnp.exp2(s - m_tile).astype(BF16)
            s = s + tab_ref[hh, jnp.minimum(delta, SEL_TABLES - 1)]
        m_tile = jnp.max(_fold(s, jnp.max), axis=0, keepdims=True)
        return m_tile, jnp.exp2(s - m_tile).astype(BF16)

    def values(p, hh, m_tile, pr):
        on_a, kt, keep, delta = work[p]
        br = int(p >= n_selpos)
        vt_ref = vwt_ref if br else vst_ref
        part_scr[hh, p] = jnp.dot(jnp.concatenate([vt_ref[0, kt], ones], axis=0), pr,
                                  preferred_element_type=F32)
        if not isinstance(keep, bool):
            m_tile = jnp.where(keep, m_tile, NEG)
        if isinstance(on_a, bool):
            tile_max[p, hh] = (m_tile, None) if on_a else (None, m_tile)
        else:
            tile_max[p, hh] = (jnp.where(on_a, m_tile, NEG), jnp.where(on_a, NEG, m_tile))

    def finish(hh):
        for side, (g_ref, oc_ref, o_ref) in enumerate(((ga_ref, oca_ref, olo_ref), (gb_ref, ocb_ref, ohi_ref))):
            outs = []
            for ps in (range(n_selpos), range(n_selpos, len(work))):
                mine = [p for p in ps if tile_max[p, hh][side] is not None]
                top = functools.reduce(jnp.maximum, [tile_max[p, hh][side] for p in mine])
                acc = jnp.zeros((NSA_DH + SEL_ROWS, t), F32)
                for p in mine:
                    acc = acc + part_scr[hh, p] * jnp.exp2(tile_max[p, hh][side] - top)
                outs.append(acc[:NSA_DH] / acc[NSA_DH:NSA_DH + 1])
            gate = lambda br: g_ref[0, pl.ds(br * NSA_HEADS + h0 + hh, 1), :]
            o_c = oc_ref[0, 0, head_rows(hh), :].astype(F32)
            o_ref[0, 0, head_rows(hh), :] = (gate(0) * o_c + gate(1) * outs[0] + gate(2) * outs[1]).astype(BF16)

    items = [(p, hh) for hh in range(SW_HEADS) for p in range(len(work))]
    pending = {}
    for step in range(len(items) + MXU_SKEW):
        if step < len(items):
            pending[step] = scores(*items[step])
        if step >= MXU_SKEW:
            p, hh = items[step - MXU_SKEW]
            values(p, hh, *pending.pop(step - MXU_SKEW))
            if p == len(work) - 1:
                finish(hh)


def _sel_win(q_t, ks, vs_t, kw, vw_t, sel, tables, gates_t, oc_t):
    bsz, nt, nq, t = q_t.shape
    seq = nt * t
    last = nt - 1
    per_group = NSA_HPG // SW_HEADS
    head_a = pl.BlockSpec((1, 1, SW_HEADS * NSA_DH, t), lambda h, b, i: (b, i, h, 0))
    head_b = pl.BlockSpec((1, 1, SW_HEADS * NSA_DH, t), lambda h, b, i: (b, last - i, h, 0))
    keys = pl.BlockSpec((1, 1, seq, NSA_DH), lambda h, b, i: (b, h // per_group, 0, 0))
    keys_sel = pl.BlockSpec((1, 1, seq, KEY_LANES), lambda h, b, i: (b, h // per_group, 0, 0))
    vals = pl.BlockSpec((1, nt, NSA_DH, t), lambda h, b, i: (b, 0, h // per_group, 0))
    sel_a = pl.BlockSpec((1, 1, 1, nt, SEL_ROWS, t), lambda h, b, i: (b, h // per_group, i, 0, 0, 0))
    sel_b = pl.BlockSpec((1, 1, 1, nt, SEL_ROWS, t), lambda h, b, i: (b, h // per_group, last - i, 0, 0, 0))
    gate_a = pl.BlockSpec((1, 3 * NSA_HEADS, t), lambda h, b, i: (b, 0, i))
    gate_b = pl.BlockSpec((1, 3 * NSA_HEADS, t), lambda h, b, i: (b, 0, last - i))
    out = jax.ShapeDtypeStruct((bsz, nt // 2, nq, t), BF16)
    return pl.pallas_call(
        _sel_win_kernel,
        grid=(NSA_HEADS // SW_HEADS, bsz, nt // 2),
        in_specs=[head_a, head_b, keys_sel, vals, keys, vals, sel_a, sel_b,
                  pl.BlockSpec((SW_HEADS, SEL_TABLES + WIN_TILES, t, t), lambda h, b, i: (h, 0, 0, 0)),
                  gate_a, gate_b, head_a, head_b],
        out_specs=[head_a, head_a],
        out_shape=[out, out],
        scratch_shapes=[pltpu.VMEM((SW_HEADS, nt + 1 + 2 * WIN_TILES, NSA_DH + SEL_ROWS, t), F32)],
        compiler_params=_params(("arbitrary", "arbitrary", "arbitrary")),
        name="nsa_sel_win",
    )(q_t, q_t, ks, vs_t, kw, vw_t, sel, sel, tables, gates_t, gates_t, oc_t, oc_t)


def kernel(x, c, mod_w, mod_b, norm_mix_g, norm_ffn_g, ab_w_in, ab_w_out, hgrn_lb_logits, hgrn_onorm_g, sconv_w, nsa_w_in, nsa_w_out, nsa_cmp_pos_k, nsa_cmp_pos_v, nsa_cmp_w1_k, nsa_cmp_w2_k, nsa_cmp_w1_v, nsa_cmp_w2_v, rel_bias, ffn_w_up, ffn_conv_w, ffn_w_down, final_norm_g):
    bsz, seq, d = x.shape
    assert d == D_MODEL and seq % TOK_TILE == 0
    lower = jnp.cumsum(jax.nn.softmax(hgrn_lb_logits.astype(F32), axis=0), axis=0)
    mod = _modulation(c, mod_w, mod_b)
    parts = lambda l: [mod[l, :, j * d:(j + 1) * d].reshape(bsz, 1, d) for j in range(6)]

    sh1, sc1, g1, sh2, sc2, g2 = parts(0)
    x = _mixer0(x, sh1, sc1, g1, norm_mix_g[0], ab_w_in[0], ab_w_out[0], lower[0], hgrn_onorm_g[0], sconv_w[0])
    x = _conv_ffn(x, sh2, sc2, g2, norm_ffn_g[0], ffn_w_up[0], ffn_conv_w[0], ffn_w_down[0], final_norm_g)

    sh1, sc1, g1, sh2, sc2, g2 = parts(1)
    kc, vc, ks, kw, q_t, gates_t, vs_t, vw_t = _nsa_proj(x, sh1, sc1, norm_mix_g[1], nsa_w_in[0])
    k_cmp, v_cmp_t = _compress(kc, vc, nsa_cmp_pos_k[0], nsa_cmp_pos_v[0],
                               nsa_cmp_w1_k[0], nsa_cmp_w2_k[0], nsa_cmp_w1_v[0], nsa_cmp_w2_v[0])
    near, bias_cmp = _bias_tables(rel_bias.astype(F32) * LOG2E, seq)
    oc_t, sel = _cmp_topk(q_t, k_cmp, v_cmp_t, bias_cmp, seq)
    o_lo, o_hi = _sel_win(q_t, ks, vs_t, kw, vw_t, sel, near, gates_t, oc_t)
    return _conv_ffn(x, sh2, sc2, g2, norm_ffn_g[1], ffn_w_up[1], ffn_conv_w[1], ffn_w_down[1], final_norm_g,
                     attn=(o_lo, o_hi, nsa_w_out[0], g1), final=True)
```

```python
import functools
import math

import jax
import jax.numpy as jnp
import numpy as np
from jax import lax
from jax.experimental import pallas as pl
from jax.experimental.pallas import tpu as pltpu

F32 = jnp.float32
BF16 = jnp.bfloat16

EPS = 1e-6
D_MODEL = 1024
D_FF = 2816

HG_HEADS = 4
HG_DK = 128
HG_WIDTH = HG_HEADS * HG_DK
SC_WIDTH = D_MODEL - HG_WIDTH
AB_IN = 4 * HG_WIDTH + 3 * SC_WIDTH
HG_CHUNK = 64
HG_LEVELS = 6

NSA_HEADS = 16
NSA_KV = 4
NSA_HPG = NSA_HEADS // NSA_KV
NSA_DH = D_MODEL // NSA_HEADS
CMP_LEN = 32
CMP_STRIDE = 16
CMP_HIDDEN = 256
SEL_BLOCK = 64
SEL_TOPN = 16
WINDOW = 512
REL_BUCKETS = 32
REL_MAX_DIST = 1024

MXU_DIM = 256
VMEM_LIMIT = 56 * 1024 * 1024
TOK_TILE = 256
NEAR_TILES = 5
SEL_TABLES = NEAR_TILES + 1
WIN_TILES = WINDOW // TOK_TILE + 1
NEG = -1e30
LOG2E = math.log2(math.e)
KEY_LANES = 128
SEL_ROWS = 16
MIX_BATCH = 2
CMP_BATCH = 4
SW_HEADS = 4
MXU_SKEW = 5
HALO = 8

_NT = (((1,), (1,)), ((), ()))
_TN = (((0,), (0,)), ((), ()))


def _sigmoid(x):
    return 1.0 / (1.0 + jnp.exp(-x))


def _silu(x):
    return x * _sigmoid(x)


def _norm_mod(x, g, sc, sh):
    ms = jnp.mean(x * x, axis=-1, keepdims=True)
    return (x * lax.rsqrt(ms + EPS) * g) * (1.0 + sc) + sh


def _fold(x, op):
    return op(x.reshape(x.shape[0] // 8, 8, x.shape[1]), axis=0)


def _const_spec(shape):
    n = len(shape)
    return pl.BlockSpec(shape, lambda *_: (0,) * n, pipeline_mode=pl.Buffered(1))


def _params(sem):
    return pltpu.CompilerParams(dimension_semantics=sem, vmem_limit_bytes=VMEM_LIMIT)


def _mod_kernel(c_ref, w_ref, b_ref, o_ref):
    c = c_ref[...]
    o_ref[0] = jnp.dot(_silu(c), w_ref[0], preferred_element_type=F32,
                       precision=lax.Precision.HIGHEST) + b_ref[0]


def _modulation(c, mod_w, mod_b):
    depth, d, n = mod_w.shape
    bsz = c.shape[0]
    tn = 1024
    return pl.pallas_call(
        _mod_kernel,
        grid=(depth, n // tn),
        in_specs=[pl.BlockSpec((bsz, d), lambda l, j: (0, 0)),
                  pl.BlockSpec((1, d, tn), lambda l, j: (l, 0, j)),
                  pl.BlockSpec((1, 1, tn), lambda l, j: (l, 0, j))],
        out_specs=pl.BlockSpec((1, bsz, tn), lambda l, j: (l, 0, j)),
        out_shape=jax.ShapeDtypeStruct((depth, bsz, n), F32),
        compiler_params=_params(("arbitrary", "arbitrary")),
        name="adaln_mod",
    )(c, mod_w, mod_b.reshape(depth, 1, n))


def _hgrn_cumsum_matrix():
    t = np.arange(HG_CHUNK)
    return (t[None, :] <= t[:, None]).astype(np.float32)


def _mixer0_kernel(x_ref, sh_ref, sc_ref, gt_ref, ng_ref, win_ref, wout_ref, lb_ref, og_ref, cw_ref,
                   pm_ref, o_ref, proj_ref, cat_ref, st_ref, ubuf_ref, d_scr):
    nb, tile = x_ref.shape[0], x_ref.shape[1]
    c64 = HG_CHUNK
    n_hg = 4 * HG_WIDTH

    @pl.when(pl.program_id(1) == 0)
    def _():
        st_ref[...] = jnp.zeros_like(st_ref)
        ubuf_ref[:, 0:HALO, :] = jnp.zeros((nb, HALO, SC_WIDTH), F32)

    row = lax.broadcasted_iota(jnp.int32, (c64, HG_DK), 0)
    r64 = lax.broadcasted_iota(jnp.int32, (c64, c64), 0)
    c64i = lax.broadcasted_iota(jnp.int32, (c64, c64), 1)
    second = [None] + [(row & (1 << (lvl - 1))) != 0 for lvl in range(1, HG_LEVELS + 1)]
    same = [None] + [(r64 >> lvl) == (c64i >> lvl) for lvl in range(1, HG_LEVELS + 1)]
    eye = r64 == c64i
    pm = pm_ref[...]
    og = og_ref[...]
    units = [(c, h) for c in range(tile // c64) for h in range(HG_HEADS)]
    hm, factors = {}, {}

    def blk(s, c, h, j):
        return s, pl.ds(c * c64, c64), pl.ds(j * HG_WIDTH + h * HG_DK, HG_DK)

    def project_heads(s):
        hm[s] = _norm_mod(x_ref[s], ng_ref[...], sc_ref[s], sh_ref[s]).astype(BF16)
        proj_ref[s, :, :n_hg] = jnp.dot(hm[s], win_ref[:, :n_hg], preferred_element_type=F32)

    def project_conv(s):
        proj_ref[s, :, n_hg:] = jnp.dot(hm[s], win_ref[:, n_hg:], preferred_element_type=F32)

    def decay_sums(s):
        for u, (c, h) in enumerate(units):
            lb = lb_ref[:, h * HG_DK:(h + 1) * HG_DK]
            fg = lb + (1.0 - lb) * _sigmoid(proj_ref[blk(s, c, h, 1)])
            proj_ref[blk(s, c, h, 1)] = 1.0 - fg
            lg = jnp.log(fg)
            lg_hi = lg.astype(BF16)
            lg_lo = (lg - lg_hi.astype(F32)).astype(BF16)
            dall = jnp.dot(pm, jnp.concatenate([lg_hi, lg_lo], axis=1), preferred_element_type=F32)
            d_scr[s, u] = dall[:, :HG_DK] + dall[:, HG_DK:]

    def level_products(s):
        intra, q_in, k_out, decay = [], [], [], []
        for u, (c, h) in enumerate(units):
            q = proj_ref[blk(s, c, h, 0)]
            k = proj_ref[blk(s, c, h, 1)]
            b = d_scr[s, u, 0:c64, :]
            a = jnp.zeros((c64, c64), F32)
            for lvl in range(1, HG_LEVELS + 1):
                m = 1 << (lvl - 1)
                if lvl == 1:
                    qt = jnp.where(second[lvl], q * (1.0 - k), 0.0).astype(BF16)
                    kt = jnp.where(second[lvl], 0.0, k).astype(BF16)
                else:
                    mids = [jnp.broadcast_to(d_scr[s, u, r0 + m - 1:r0 + m, :], (2 * m, HG_DK))
                            for r0 in range(0, c64, 2 * m)]
                    dlt = b - (mids[0] if len(mids) == 1 else jnp.concatenate(mids, axis=0))
                    e = jnp.exp(-jnp.abs(dlt))
                    qt = jnp.where(second[lvl], q * e, 0.0).astype(BF16)
                    kt = jnp.where(second[lvl], 0.0, k * e).astype(BF16)
                al = lax.dot_general(qt, kt, _NT, preferred_element_type=F32)
                a = a + (al if lvl == HG_LEVELS else jnp.where(same[lvl], al, 0.0))
            intra.append(jnp.where(eye, jnp.sum(q * k, axis=-1, keepdims=True), a).astype(BF16))
            b_last = d_scr[s, u, c64 - 1:c64, :]
            q_in.append((q * jnp.exp(b)).astype(BF16))
            k_out.append((k * jnp.exp(b_last - b)).astype(BF16))
            decay.append(jnp.exp(b_last))
        factors[s] = (intra, q_in, k_out, decay)

    def recurrence(s):
        intra, q_in, k_out, decay = factors[s]
        state = [st_ref[s, h] for h in range(HG_HEADS)]
        for u, (c, h) in enumerate(units):
            vb = proj_ref[blk(s, c, h, 2)].astype(BF16)
            o = jnp.dot(intra[u], vb, preferred_element_type=F32)
            o = o + lax.dot_general(q_in[u], state[h].astype(BF16), _NT, preferred_element_type=F32)
            state[h] = state[h] * decay[u] + lax.dot_general(vb, k_out[u], _TN, preferred_element_type=F32)
            on = o * lax.rsqrt(jnp.mean(o * o, axis=-1, keepdims=True) + EPS) * og
            gate = _silu(proj_ref[blk(s, c, h, 3)])
            cat_ref[s, pl.ds(c * c64, c64), h * HG_DK:(h + 1) * HG_DK] = (on * gate).astype(BF16)
        for h in range(HG_HEADS):
            st_ref[s, h] = state[h]

    def short_conv_and_out(s):
        off = n_hg
        u = (proj_ref[s, :, off + SC_WIDTH:off + 2 * SC_WIDTH]
             * proj_ref[s, :, off + 2 * SC_WIDTH:off + 3 * SC_WIDTH])
        ubuf_ref[s, HALO:HALO + tile, :] = u
        cw = cw_ref[...]
        conv = (ubuf_ref[s, HALO - 2:HALO - 2 + tile, :] * cw[0:1]
                + ubuf_ref[s, HALO - 1:HALO - 1 + tile, :] * cw[1:2] + u * cw[2:3])
        ubuf_ref[s, 0:HALO, :] = u[tile - HALO:tile, :]
        cat_ref[s, :, HG_WIDTH:] = (proj_ref[s, :, off:off + SC_WIDTH] * conv).astype(BF16)
        y = jnp.dot(cat_ref[s], wout_ref[...], preferred_element_type=F32)
        o_ref[s] = x_ref[s] + gt_ref[s] * y

    for s in range(nb):
        project_heads(s)
    decay_sums(0)
    project_conv(0)
    for s in range(1, nb):
        decay_sums(s)
        level_products(s - 1)
        project_conv(s)
    level_products(nb - 1)
    for s in range(nb):
        recurrence(s)
        short_conv_and_out(s)


def _mixer0(x, sh, sc, gt, norm_g, w_in, w_out, lower, onorm_g, sconv_w):
    bsz, seq, d = x.shape
    t = TOK_TILE
    nb = MIX_BATCH if bsz % MIX_BATCH == 0 else 1
    pm = jnp.asarray(_hgrn_cumsum_matrix(), BF16)
    vec = pl.BlockSpec((nb, 1, d), lambda b, i: (b, 0, 0))
    return pl.pallas_call(
        _mixer0_kernel,
        grid=(bsz // nb, seq // t),
        in_specs=[pl.BlockSpec((nb, t, d), lambda b, i: (b, i, 0)), vec, vec, vec,
                  _const_spec((1, d)), _const_spec((d, AB_IN)), _const_spec((d, d)),
                  _const_spec((1, HG_WIDTH)), _const_spec((1, HG_DK)), _const_spec((3, SC_WIDTH)),
                  _const_spec(pm.shape)],
        out_specs=pl.BlockSpec((nb, t, d), lambda b, i: (b, i, 0)),
        out_shape=jax.ShapeDtypeStruct(x.shape, F32),
        scratch_shapes=[pltpu.VMEM((nb, t, AB_IN), F32), pltpu.VMEM((nb, t, d), BF16),
                        pltpu.VMEM((nb, HG_HEADS, HG_DK, HG_DK), F32),
                        pltpu.VMEM((nb, t + HALO, SC_WIDTH), F32),
                        pltpu.VMEM((nb, t // HG_CHUNK * HG_HEADS, HG_CHUNK, HG_DK), F32)],
        compiler_params=_params(("arbitrary", "arbitrary")),
        name="mixer_hgrn_sconv",
    )(x, sh, sc, gt, norm_g.reshape(1, d), w_in.astype(BF16), w_out.astype(BF16),
      lower.reshape(1, HG_WIDTH), onorm_g.reshape(1, HG_DK), sconv_w, pm)


FF_CHUNKS = ((0, 5 * MXU_DIM), (5 * MXU_DIM, D_FF))


def _ffn_kernel(*refs, has_attn, final):
    if has_attn:
        (x_ref, olo_ref, ohi_ref, wo_ref, g1_ref, sh_ref, sc_ref, g2_ref, ng_ref, wup_ref, cw_ref, wdn_ref, fg_ref,
         o_ref, gbuf_ref, carry_ref) = refs
    else:
        (x_ref, sh_ref, sc_ref, g2_ref, ng_ref, wup_ref, cw_ref, wdn_ref, fg_ref,
         o_ref, gbuf_ref, carry_ref) = refs
    tile = x_ref.shape[1]

    @pl.when(pl.program_id(1) == 0)
    def _():
        carry_ref[...] = jnp.zeros_like(carry_ref)

    x = x_ref[0]
    if has_attn:
        o_t = jnp.where(pl.program_id(1) < pl.num_programs(1) // 2, olo_ref[0, 0], ohi_ref[0, 0])
        x = x + g1_ref[0] * lax.dot_general(o_t, wo_ref[...], _TN, preferred_element_type=F32)
    hf = _norm_mod(x, ng_ref[...], sc_ref[0], sh_ref[0]).astype(BF16)
    gbuf_ref[0:HALO, :] = carry_ref[...]
    gates, ups = [], []
    for c0, c1 in FF_CHUNKS:
        gate = jnp.dot(hf, wup_ref[:, c0:c1], preferred_element_type=F32)
        ups.append(jnp.dot(hf, wup_ref[:, D_FF + c0:D_FF + c1], preferred_element_type=F32))
        gbuf_ref[HALO:HALO + tile, c0:c1] = gate
        carry_ref[:, c0:c1] = gate[tile - HALO:tile, :]
        gates.append(gate)
    acc = jnp.zeros((tile, D_MODEL), F32)
    for j, (c0, c1) in enumerate(FF_CHUNKS):
        cols = slice(c0, c1)
        cw = cw_ref[:, cols]
        conv = (gbuf_ref[HALO - 2:HALO - 2 + tile, cols] * cw[0:1] + gbuf_ref[HALO - 1:HALO - 1 + tile, cols] * cw[1:2]
                + gates[j] * cw[2:3])
        act = (_silu(conv) * ups[j]).astype(BF16)
        acc = acc + jnp.dot(act, wdn_ref[cols, :], preferred_element_type=F32)
    out = x + g2_ref[0] * acc
    if final:
        out = out * lax.rsqrt(jnp.mean(out * out, axis=-1, keepdims=True) + EPS) * fg_ref[...]
    o_ref[0] = out


def _conv_ffn(x, sh, sc, g2, norm_g, w_up, conv_w, w_down, final_g, attn=None, final=False):
    bsz, seq, d = x.shape
    t = TOK_TILE
    vec = pl.BlockSpec((1, 1, d), lambda b, i: (b, 0, 0))
    args, specs = [x], [pl.BlockSpec((1, t, d), lambda b, i: (b, i, 0))]
    if attn is not None:
        o_lo, o_hi, w_o, g1 = attn
        top = o_lo.shape[1] - 1
        last = seq // t - 1
        args += [o_lo, o_hi, w_o.astype(BF16), g1]
        specs += [pl.BlockSpec((1, 1, d, t), lambda b, i: (b, jnp.minimum(i, top), 0, 0)),
                  pl.BlockSpec((1, 1, d, t), lambda b, i: (b, jnp.minimum(last - i, top), 0, 0)),
                  _const_spec((d, d)), vec]
    args += [sh, sc, g2, norm_g.reshape(1, d), w_up.astype(BF16), conv_w, w_down.astype(BF16),
             final_g.reshape(1, d)]
    specs += [vec, vec, vec, _const_spec((1, d)), _const_spec((d, 2 * D_FF)), _const_spec((3, D_FF)),
              _const_spec((D_FF, d)), _const_spec((1, d))]
    return pl.pallas_call(
        functools.partial(_ffn_kernel, has_attn=attn is not None, final=final),
        grid=(bsz, seq // t),
        in_specs=specs,
        out_specs=pl.BlockSpec((1, t, d), lambda b, i: (b, i, 0)),
        out_shape=jax.ShapeDtypeStruct(x.shape, F32),
        scratch_shapes=[pltpu.VMEM((t + HALO, D_FF), F32), pltpu.VMEM((HALO, D_FF), F32)],
        compiler_params=_params(("arbitrary", "arbitrary")),
        name="conv_ffn_attnproj" if attn is not None else "conv_ffn",
    )(*args)


KV_ROWS = NSA_KV * NSA_DH


def _nsa_proj_kernel(x_ref, sh_ref, sc_ref, ng_ref, wa_ref, wq_ref, wg_ref, wv_ref, hot_ref, perm_ref,
                     kc_ref, vc_ref, ks_ref, kw_ref, qt_ref, gt_ref, vst_ref, vwt_ref):
    hm = _norm_mod(x_ref[0], ng_ref[...], sc_ref[0], sh_ref[0]).astype(BF16)
    kv = jnp.dot(hm, wa_ref[...], preferred_element_type=F32).astype(BF16)
    cmp_in = jnp.dot(perm_ref[...], kv[:, :2 * KV_ROWS], preferred_element_type=F32).astype(BF16)
    n_grp = perm_ref.shape[0] // CMP_STRIDE
    for g in range(NSA_KV):
        part = lambda j: kv[:, j * KV_ROWS + g * NSA_DH:j * KV_ROWS + (g + 1) * NSA_DH]
        for p in range(CMP_STRIDE):
            rows = slice(p * n_grp, (p + 1) * n_grp)
            kc_ref[0, g, p] = cmp_in[rows, g * NSA_DH:(g + 1) * NSA_DH]
            vc_ref[0, g, p] = cmp_in[rows, KV_ROWS + g * NSA_DH:KV_ROWS + (g + 1) * NSA_DH]
        ks_ref[0, g] = jnp.concatenate([part(2), hot_ref[...]], axis=1)
        kw_ref[0, g] = part(3)
    qt = lax.dot_general(wq_ref[...], hm, _NT, preferred_element_type=F32)
    qt_ref[0, 0] = (qt * (NSA_DH ** -0.5 * LOG2E)).astype(BF16)
    gt_ref[0] = _sigmoid(lax.dot_general(wg_ref[...], hm, _NT, preferred_element_type=F32))
    vt = lax.dot_general(wv_ref[...], hm, _NT, preferred_element_type=F32).astype(BF16)
    vst_ref[0, 0] = vt[:KV_ROWS]
    vwt_ref[0, 0] = vt[KV_ROWS:]


def _nsa_proj(x, sh, sc, norm_g, w_in):
    bsz, seq, d = x.shape
    t = TOK_TILE
    nt = seq // t
    nq, ng = NSA_HEADS * NSA_DH, 3 * NSA_HEADS
    o = nq + ng
    w = w_in.astype(BF16)
    col = lambda j: w[:, o + j * KV_ROWS:o + (j + 1) * KV_ROWS]
    w_a = jnp.concatenate([col(0), col(1), col(2), col(4)], axis=1)
    w_q = w[:, :nq].T
    w_g = w[:, nq:o].T
    w_v = jnp.concatenate([col(3), col(5)], axis=1).T
    hot = (np.arange(t)[:, None] // SEL_BLOCK == np.arange(KEY_LANES - NSA_DH)[None, :]).astype(np.float32)
    n_grp = t // CMP_STRIDE
    r = np.arange(t)
    perm = (r[None, :] == ((r % n_grp) * CMP_STRIDE + r // n_grp)[:, None]).astype(np.float32)
    vec = pl.BlockSpec((1, 1, d), lambda b, i: (b, 0, 0))
    tiled = lambda r: pl.BlockSpec((1, 1, r, t), lambda b, i: (b, i, 0, 0))
    grouped = lambda n: pl.BlockSpec((1, NSA_KV, t, n), lambda b, i: (b, 0, i, 0))
    rows = lambda n: jax.ShapeDtypeStruct((bsz, NSA_KV, seq, n), BF16)
    by_offset = pl.BlockSpec((1, NSA_KV, CMP_STRIDE, n_grp, NSA_DH), lambda b, i: (b, 0, 0, i, 0))
    offset_major = jax.ShapeDtypeStruct((bsz, NSA_KV, CMP_STRIDE, seq // CMP_STRIDE, NSA_DH), BF16)
    return pl.pallas_call(
        _nsa_proj_kernel,
        grid=(bsz, nt),
        in_specs=[pl.BlockSpec((1, t, d), lambda b, i: (b, i, 0)), vec, vec, _const_spec((1, d)),
                  _const_spec((d, 4 * KV_ROWS)), _const_spec((nq, d)), _const_spec((ng, d)),
                  _const_spec((2 * KV_ROWS, d)), _const_spec(hot.shape), _const_spec(perm.shape)],
        out_specs=[by_offset, by_offset, grouped(KEY_LANES), grouped(NSA_DH), tiled(nq),
                   pl.BlockSpec((1, ng, t), lambda b, i: (b, 0, i)), tiled(KV_ROWS), tiled(KV_ROWS)],
        out_shape=[offset_major, offset_major, rows(KEY_LANES), rows(NSA_DH),
                   jax.ShapeDtypeStruct((bsz, nt, nq, t), BF16),
                   jax.ShapeDtypeStruct((bsz, ng, seq), F32),
                   jax.ShapeDtypeStruct((bsz, nt, KV_ROWS, t), BF16),
                   jax.ShapeDtypeStruct((bsz, nt, KV_ROWS, t), BF16)],
        compiler_params=_params(("arbitrary", "arbitrary")),
        name="nsa_in_proj",
    )(x, sh, sc, norm_g.reshape(1, d), w_a, w_q, w_g, w_v, jnp.asarray(hot, BF16), jnp.asarray(perm, BF16))


def _compress_kernel(hk_ref, hv_ref, pk_ref, pv_ref, w1k_ref, w2k_ref, w1v_ref, w2vt_ref, kc_ref, vct_ref):
    half = CMP_STRIDE * NSA_DH

    def hidden(h_ref, pos_ref, w1_ref):
        n = h_ref.shape[3]
        top = jnp.zeros((n, CMP_HIDDEN), F32)
        bot = jnp.zeros((n, CMP_HIDDEN), F32)
        per = MXU_DIM // NSA_DH
        for p0 in range(0, CMP_STRIDE, per):
            h = jnp.concatenate([h_ref[0, 0, p] for p in range(p0, p0 + per)], axis=1)
            rows = slice(p0 * NSA_DH, (p0 + per) * NSA_DH)
            top = top + jnp.dot(h, w1_ref[rows, :], preferred_element_type=F32)
            bot = bot + jnp.dot(h, w1_ref[half + p0 * NSA_DH:half + (p0 + per) * NSA_DH, :],
                                preferred_element_type=F32)
        pos = jnp.dot(jnp.broadcast_to(pos_ref[...], (8, 2 * half)).astype(BF16), w1_ref[...],
                      preferred_element_type=F32)[0:1]
        return _silu(top + pltpu.roll(bot, n - 1, 0) + pos).astype(BF16)

    kc_ref[0, 0] = jnp.dot(hidden(hk_ref, pk_ref, w1k_ref), w2k_ref[...],
                           preferred_element_type=F32).astype(BF16)
    vct_ref[0, 0] = lax.dot_general(w2vt_ref[...], hidden(hv_ref, pv_ref, w1v_ref), _NT,
                                    preferred_element_type=F32).astype(BF16)


def _compress(hk, hv, pos_k, pos_v, w1_k, w2_k, w1_v, w2_v):
    bsz, g, stride, n, dh = hk.shape
    width = stride * dh
    blk = pl.BlockSpec((1, 1, stride, n, dh), lambda b, j: (b, j, 0, 0, 0))
    return pl.pallas_call(
        _compress_kernel,
        grid=(bsz, g),
        in_specs=[blk, blk, _const_spec((1, width * 2)), _const_spec((1, width * 2)),
                  _const_spec((2 * width, CMP_HIDDEN)), _const_spec((CMP_HIDDEN, NSA_DH)),
                  _const_spec((2 * width, CMP_HIDDEN)), _const_spec((NSA_DH, CMP_HIDDEN))],
        out_specs=[pl.BlockSpec((1, 1, n, NSA_DH), lambda b, j: (b, j, 0, 0)),
                   pl.BlockSpec((1, 1, NSA_DH, n), lambda b, j: (b, j, 0, 0))],
        out_shape=[jax.ShapeDtypeStruct((bsz, g, n, NSA_DH), BF16),
                   jax.ShapeDtypeStruct((bsz, g, NSA_DH, n), BF16)],
        compiler_params=_params(("arbitrary", "arbitrary")),
        name="nsa_compress",
    )(hk, hv, pos_k.reshape(1, -1), pos_v.reshape(1, -1), w1_k.astype(BF16), w2_k.astype(BF16),
      w1_v.astype(BF16), w2_v.T.astype(BF16))


def _bucket_upper_bounds():
    n = np.arange(4 * REL_MAX_DIST, dtype=np.float64)
    exact = REL_BUCKETS // 2
    large = exact + (np.log(np.maximum(n, exact) / exact) / math.log(REL_MAX_DIST / exact)
                     * (REL_BUCKETS - exact)).astype(np.int64)
    bucket = np.where(n < exact, n.astype(np.int64), np.minimum(large, REL_BUCKETS - 1))
    return [int(np.max(np.nonzero(bucket <= j)[0])) for j in range(REL_BUCKETS - 1)]


def _bias_of_dist(dist, rb_ref, heads, uppers):
    vals = [jnp.full(dist.shape, rb_ref[REL_BUCKETS - 1, h], F32) for h in heads]
    for j in range(REL_BUCKETS - 2, -1, -1):
        m = dist <= uppers[j]
        vals = [jnp.where(m, rb_ref[j, h], v) for h, v in zip(heads, vals)]
    return vals


def _bias_near_kernel(rb_ref, o_ref, line_scr, *, uppers):
    d = pl.program_id(0)
    t = o_ref.shape[2]
    heads = list(range(NSA_HEADS))

    @pl.when(d == 0)
    def _():
        line = ((lax.broadcasted_iota(jnp.int32, (8, t), 0) - 1) * t + lax.broadcasted_iota(jnp.int32, (8, t), 1))
        for h, v in zip(heads, _bias_of_dist(line, rb_ref, heads, uppers)):
            line_scr[h] = v

    is_win = d >= SEL_TABLES
    delta = jnp.where(is_win, d - SEL_TABLES, d)
    limit = jnp.where(is_win, WINDOW, (SEL_TABLES + 1) * t)
    dist = delta * t + lax.broadcasted_iota(jnp.int32, (t, t), 1) - lax.broadcasted_iota(jnp.int32, (t, t), 0)
    valid = (dist >= 0) & (dist < limit)
    for h in heads:
        span = jnp.concatenate([line_scr[h, pl.ds(delta, 1), :], line_scr[h, pl.ds(delta + 1, 1), :]], axis=1)
        rolled = pltpu.roll(jnp.broadcast_to(span, (t, 2 * t)), 0, 1, stride=1, stride_axis=0)
        o_ref[h, 0] = jnp.where(valid, rolled[:, t:], NEG)


CMP_BAND = 72


def _bias_cmp_kernel(rb_ref, o_ref, band_scr, *, uppers):
    tb = pl.program_id(0)
    n, t = o_ref.shape[1], o_ref.shape[2]
    heads = list(range(NSA_HEADS))
    per_tile = t // CMP_STRIDE
    first = CMP_STRIDE * (CMP_BAND - per_tile) - (CMP_LEN - 1)
    assert first + CMP_STRIDE > uppers[-1] and first + (t - 1) - CMP_STRIDE * CMP_BAND < 0

    @pl.when(tb == 0)
    def _():
        for r0 in range(0, CMP_BAND, 8):
            row = r0 + lax.broadcasted_iota(jnp.int32, (8, t), 0)
            dist = first + lax.broadcasted_iota(jnp.int32, (8, t), 1) - CMP_STRIDE * row
            for h, v in zip(heads, _bias_of_dist(dist, rb_ref, heads, uppers)):
                band_scr[h, r0:r0 + 8, :] = jnp.where(dist >= 0, v, NEG)

    def strip(i, carry):
        r0 = pl.multiple_of(i * 8, 8)
        rel = r0 - (tb * per_tile - (CMP_BAND - per_tile))
        at = pl.multiple_of(jnp.clip(rel, 0, CMP_BAND - 8), 8)
        for h in heads:
            inside = band_scr[h, pl.ds(at, 8), :]
            far = jnp.full((8, t), rb_ref[REL_BUCKETS - 1, h], F32)
            o_ref[h, pl.ds(r0, 8), :] = jnp.where(rel < 0, far, jnp.where(rel >= CMP_BAND, NEG, inside))
        return carry

    lax.fori_loop(0, n // 8, strip, 0)


def _bias_tables(rel_bias, seq):
    t = TOK_TILE
    n_cmp = seq // CMP_STRIDE
    uppers = _bucket_upper_bounds()
    assert NEAR_TILES * t - (t - 1) > uppers[-1]
    smem = pl.BlockSpec(memory_space=pltpu.SMEM)
    near = pl.pallas_call(
        functools.partial(_bias_near_kernel, uppers=uppers),
        grid=(SEL_TABLES + WIN_TILES,),
        in_specs=[smem],
        out_specs=pl.BlockSpec((NSA_HEADS, 1, t, t), lambda d: (0, d, 0, 0)),
        out_shape=jax.ShapeDtypeStruct((NSA_HEADS, SEL_TABLES + WIN_TILES, t, t), F32),
        scratch_shapes=[pltpu.VMEM((NSA_HEADS, 8, t), F32)],
        compiler_params=_params(("arbitrary",)),
        name="relbias_near",
    )(rel_bias)
    cmp_b = pl.pallas_call(
        functools.partial(_bias_cmp_kernel, uppers=uppers),
        grid=(seq // t,),
        in_specs=[smem],
        out_specs=pl.BlockSpec((NSA_HEADS, n_cmp, t), lambda i: (0, 0, i)),
        out_shape=jax.ShapeDtypeStruct((NSA_HEADS, n_cmp, seq), F32),
        scratch_shapes=[pltpu.VMEM((NSA_HEADS, CMP_BAND, t), F32)],
        compiler_params=_params(("arbitrary",)),
        name="relbias_cmp",
    )(rel_bias)
    return near, cmp_b


def _cmp_to_sel_t(seq):
    n_cmp, n_sel = seq // CMP_STRIDE, seq // SEL_BLOCK
    c_start = np.arange(n_cmp)[:, None] * CMP_STRIDE
    s_start = np.arange(n_sel)[None, :] * SEL_BLOCK
    inside = np.clip(np.minimum(c_start + CMP_LEN, s_start + SEL_BLOCK) - np.maximum(c_start, s_start), 0, None)
    return (inside / CMP_LEN).T.astype(np.float32)


def _cmp_topk_kernel(qt_ref, kc_ref, vct_ref, bias_ref, c2s_ref, oc_ref, sel_ref, score_scr, cnt_scr):
    tb = pl.program_id(1)
    n, t = bias_ref.shape[1], bias_ref.shape[2]
    n_sel = c2s_ref.shape[0]
    nb = qt_ref.shape[0]
    tok = tb * t + lax.broadcasted_iota(jnp.int32, (1, t), 1)
    any_visible = tok >= CMP_LEN - 1
    ones = (lax.broadcasted_iota(jnp.int32, (SEL_ROWS, n), 0) == 0).astype(BF16)
    head_rows = lambda i: slice(i * NSA_DH, (i + 1) * NSA_DH)
    weights = {}
    for bb in range(nb):
        kc = kc_ref[bb, 0]
        for i in range(NSA_HPG):
            s = jnp.dot(kc, qt_ref[bb, 0, head_rows(i), :], preferred_element_type=F32) + bias_ref[i]
            m = jnp.max(_fold(s, jnp.max), axis=0, keepdims=True)
            weights[bb, i] = jnp.exp2(s - m).astype(BF16)
    imps = []
    for bb in range(nb):
        lhs = jnp.concatenate([vct_ref[bb, 0], ones, c2s_ref[...]], axis=0)
        imp = jnp.zeros((n_sel, t), F32)
        for i in range(NSA_HPG):
            r = jnp.dot(lhs, weights[bb, i], preferred_element_type=F32)
            scale = jnp.where(any_visible, 1.0 / r[NSA_DH:NSA_DH + 1], 0.0)
            oc_ref[bb, 0, head_rows(i), :] = (r[:NSA_DH] * scale).astype(BF16)
            imp = imp + r[NSA_DH + SEL_ROWS:] * scale
        imps.append(imp)

    w = nb * t
    blk = lax.broadcasted_iota(jnp.int32, (n_sel, t), 0)
    cur = (tb * t + lax.broadcasted_iota(jnp.int32, (n_sel, t), 1)) // SEL_BLOCK
    forced = (blk == 0) | (blk == cur) | (blk == cur - 1)
    for bb in range(nb):
        score_scr[:, bb * t:(bb + 1) * t] = jnp.where(blk > cur, -jnp.inf, jnp.where(forced, jnp.inf, imps[bb]))
    groups = n_sel // 8
    per_tile = t // SEL_BLOCK
    sub = lax.broadcasted_iota(jnp.int32, (8, w), 0)
    cnt_scr[...] = jnp.zeros_like(cnt_scr)
    for j in range(groups):
        @pl.when(8 * j < (tb + 1) * per_tile)
        def _():
            src = [jnp.broadcast_to(score_scr[sp:sp + 1, :], (8, w)) for sp in range(8 * j, 8 * j + 8)]
            for v in range(groups):
                part = score_scr[8 * v:8 * v + 8, :]
                cnt = cnt_scr[8 * v:8 * v + 8, :]
                for sp, r in zip(range(8 * j, 8 * j + 8), src):
                    if j < v:
                        beats = r >= part
                    elif j > v:
                        beats = r > part
                    else:
                        beats = (r > part) | ((r == part) & (sub > sp - 8 * v))
                    cnt = cnt + jnp.where(beats, 1.0, 0.0)
                cnt_scr[8 * v:8 * v + 8, :] = cnt
    pad = jnp.zeros((SEL_ROWS - per_tile, t), F32)
    for v in range(groups):
        mask = jnp.where(cnt_scr[8 * v:8 * v + 8, :] < min(SEL_TOPN, n_sel), 0.0, NEG)
        for bb in range(nb):
            for r0 in range(0, 8, per_tile):
                kt = (8 * v + r0) // per_tile
                rows = mask[r0:r0 + per_tile, bb * t:(bb + 1) * t]
                sel_ref[bb, 0, 0, kt] = jnp.concatenate([rows, pad], axis=0).astype(BF16)


def _cmp_topk(q_t, k_cmp, v_cmp_t, bias_cmp, seq):
    bsz, nt, nq, t = q_t.shape
    n_cmp, n_sel = seq // CMP_STRIDE, seq // SEL_BLOCK
    grp = NSA_HPG * NSA_DH
    nb = CMP_BATCH if bsz % CMP_BATCH == 0 else 1
    c2s = jnp.asarray(_cmp_to_sel_t(seq), BF16)
    return pl.pallas_call(
        _cmp_topk_kernel,
        grid=(NSA_KV, nt, bsz // nb),
        in_specs=[pl.BlockSpec((nb, 1, grp, t), lambda g, i, b: (b, i, g, 0)),
                  pl.BlockSpec((nb, 1, n_cmp, NSA_DH), lambda g, i, b: (b, g, 0, 0)),
                  pl.BlockSpec((nb, 1, NSA_DH, n_cmp), lambda g, i, b: (b, g, 0, 0)),
                  pl.BlockSpec((NSA_HPG, n_cmp, t), lambda g, i, b: (g, 0, i)),
                  _const_spec((n_sel, n_cmp))],
        out_specs=[pl.BlockSpec((nb, 1, grp, t), lambda g, i, b: (b, i, g, 0)),
                   pl.BlockSpec((nb, 1, 1, nt, SEL_ROWS, t), lambda g, i, b: (b, g, i, 0, 0, 0))],
        out_shape=[jax.ShapeDtypeStruct((bsz, nt, nq, t), BF16),
                   jax.ShapeDtypeStruct((bsz, NSA_KV, nt, nt, SEL_ROWS, t), BF16)],
        scratch_shapes=[pltpu.VMEM((n_sel, nb * t), F32), pltpu.VMEM((n_sel, nb * t), F32)],
        compiler_params=_params(("arbitrary", "arbitrary", "arbitrary")),
        name="nsa_cmp_topk",
    )(q_t, k_cmp, v_cmp_t, bias_cmp, c2s)


def _sel_win_kernel(qa_ref, qb_ref, ks_ref, vst_ref, kw_ref, vwt_ref, sela_ref, selb_ref, tab_ref, ga_ref, gb_ref,
                    oca_ref, ocb_ref, olo_ref, ohi_ref, part_scr):
    h0 = pl.program_id(0) * SW_HEADS
    i = pl.program_id(2)
    nt, t = vst_ref.shape[1], vst_ref.shape[3]
    half = nt // 2
    tb = (i, nt - 1 - i)
    head_rows = lambda hh: slice(hh * NSA_DH, (hh + 1) * NSA_DH)
    q = [(qa_ref[0, 0, head_rows(hh), :], qb_ref[0, 0, head_rows(hh), :]) for hh in range(SW_HEADS)]
    sel = (sela_ref, selb_ref)

    def tile_rows(kt):
        return pl.ds(pl.multiple_of(kt * t, t), t)

    def pick(on_a, xa, xb):
        if isinstance(on_a, bool):
            return xa if on_a else xb
        return jnp.where(on_a, xa, xb)

    work = []
    for p in range(nt + 1):
        on_a = True if p == 0 else (False if p >= half else p <= i)
        kt = pick(on_a, p, jnp.maximum(p - i - 1, 0))
        work.append((on_a, kt, True, nt - p if on_a is False else pick(on_a, tb[0] - kt, nt - p)))
    for side in range(2):
        for delta in range(WIN_TILES):
            keep = True if (side == 1 and half >= WIN_TILES) else tb[side] >= delta
            work.append((side == 0, jnp.maximum(tb[side] - delta, 0), keep, delta))
    n_selpos = nt + 1

    ones = (lax.broadcasted_iota(jnp.int32, (SEL_ROWS, t), 0) == 0).astype(BF16)
    q_pad = jnp.zeros((KEY_LANES - NSA_DH - SEL_ROWS, t), BF16)
    tile_max = {}

    def scores(p, hh):
        on_a, kt, keep, delta = work[p]
        qp = pick(on_a, q[hh][0], q[hh][1])
        if p >= n_selpos:
            s = jnp.dot(kw_ref[0, 0, tile_rows(kt), :], qp, preferred_element_type=F32)
            s = s + tab_ref[hh, SEL_TABLES + delta]
        else:
            rows = pick(on_a, sel[0][0, 0, 0, kt], sel[1][0, 0, 0, kt])
            q_aug = jnp.concatenate([qp, rows, q_pad], axis=0)
            s = jnp.dot(ks_ref[0, 0, tile_rows(kt), :], q_aug, preferred_element_type=F32)
            if isinstance(delta, int) and delta >= SEL_TABLES - 1:
                m_tile = jnp.max(_fold(s, jnp.max), axis=0, keepdims=True)
                return m_tile + tab_ref[hh, SEL_TABLES - 1, 0:1, :], jnp.exp2(s - m_tile).astype(BF16)
            s = s + tab_ref[hh, jnp.minimum(delta, SEL_TABLES - 1)]
        m_tile = jnp.max(_fold(s, jnp.max), axis=0, keepdims=True)
        return m_tile, jnp.exp2(s - m_tile).astype(BF16)

    def values(p, hh, m_tile, pr):
        on_a, kt, keep, delta = work[p]
        br = int(p >= n_selpos)
        vt_ref = vwt_ref if br else vst_ref
        part_scr[hh, p] = jnp.dot(jnp.concatenate([vt_ref[0, kt], ones], axis=0), pr,
                                  preferred_element_type=F32)
        if not isinstance(keep, bool):
            m_tile = jnp.where(keep, m_tile, NEG)
        if isinstance(on_a, bool):
            tile_max[p, hh] = (m_tile, None) if on_a else (None, m_tile)
        else:
            tile_max[p, hh] = (jnp.where(on_a, m_tile, NEG), jnp.where(on_a, NEG, m_tile))

    def finish(hh):
        for side, (g_ref, oc_ref, o_ref) in enumerate(((ga_ref, oca_ref, olo_ref), (gb_ref, ocb_ref, ohi_ref))):
            outs = []
            for ps in (range(n_selpos), range(n_selpos, len(work))):
                mine = [p for p in ps if tile_max[p, hh][side] is not None]
                top = functools.reduce(jnp.maximum, [tile_max[p, hh][side] for p in mine])
                acc = jnp.zeros((NSA_DH + SEL_ROWS, t), F32)
                for p in mine:
                    acc = acc + part_scr[hh, p] * jnp.exp2(tile_max[p, hh][side] - top)
                outs.append(acc[:NSA_DH] / acc[NSA_DH:NSA_DH + 1])
            gate = lambda br: g_ref[0, pl.ds(br * NSA_HEADS + h0 + hh, 1), :]
            o_c = oc_ref[0, 0, head_rows(hh), :].astype(F32)
            o_ref[0, 0, head_rows(hh), :] = (gate(0) * o_c + gate(1) * outs[0] + gate(2) * outs[1]).astype(BF16)

    items = [(p, hh) for hh in range(SW_HEADS) for p in range(len(work))]
    pending = {}
    for step in range(len(items) + MXU_SKEW):
        if step < len(items):
            pending[step] = scores(*items[step])
        if step >= MXU_SKEW:
            p, hh = items[step - MXU_SKEW]
            values(p, hh, *pending.pop(step - MXU_SKEW))
            if p == len(work) - 1:
                finish(hh)


def _sel_win(q_t, ks, vs_t, kw, vw_t, sel, tables, gates_t, oc_t):
    bsz, nt, nq, t = q_t.shape
    seq = nt * t
    last = nt - 1
    per_group = NSA_HPG // SW_HEADS
    head_a = pl.BlockSpec((1, 1, SW_HEADS * NSA_DH, t), lambda h, b, i: (b, i, h, 0))
    head_b = pl.BlockSpec((1, 1, SW_HEADS * NSA_DH, t), lambda h, b, i: (b, last - i, h, 0))
    keys = pl.BlockSpec((1, 1, seq, NSA_DH), lambda h, b, i: (b, h // per_group, 0, 0))
    keys_sel = pl.BlockSpec((1, 1, seq, KEY_LANES), lambda h, b, i: (b, h // per_group, 0, 0))
    vals = pl.BlockSpec((1, nt, NSA_DH, t), lambda h, b, i: (b, 0, h // per_group, 0))
    sel_a = pl.BlockSpec((1, 1, 1, nt, SEL_ROWS, t), lambda h, b, i: (b, h // per_group, i, 0, 0, 0))
    sel_b = pl.BlockSpec((1, 1, 1, nt, SEL_ROWS, t), lambda h, b, i: (b, h // per_group, last - i, 0, 0, 0))
    gate_a = pl.BlockSpec((1, 3 * NSA_HEADS, t), lambda h, b, i: (b, 0, i))
    gate_b = pl.BlockSpec((1, 3 * NSA_HEADS, t), lambda h, b, i: (b, 0, last - i))
    out = jax.ShapeDtypeStruct((bsz, nt // 2, nq, t), BF16)
    return pl.pallas_call(
        _sel_win_kernel,
        grid=(NSA_HEADS // SW_HEADS, bsz, nt // 2),
        in_specs=[head_a, head_b, keys_sel, vals, keys, vals, sel_a, sel_b,
                  pl.BlockSpec((SW_HEADS, SEL_TABLES + WIN_TILES, t, t), lambda h, b, i: (h, 0, 0, 0)),
                  gate_a, gate_b, head_a, head_b],
        out_specs=[head_a, head_a],
        out_shape=[out, out],
        scratch_shapes=[pltpu.VMEM((SW_HEADS, nt + 1 + 2 * WIN_TILES, NSA_DH + SEL_ROWS, t), F32)],
        compiler_params=_params(("arbitrary", "arbitrary", "arbitrary")),
        name="nsa_sel_win",
    )(q_t, q_t, ks, vs_t, kw, vw_t, sel, sel, tables, gates_t, gates_t, oc_t, oc_t)


def kernel(x, c, mod_w, mod_b, norm_mix_g, norm_ffn_g, ab_w_in, ab_w_out, hgrn_lb_logits, hgrn_onorm_g, sconv_w, nsa_w_in, nsa_w_out, nsa_cmp_pos_k, nsa_cmp_pos_v, nsa_cmp_w1_k, nsa_cmp_w2_k, nsa_cmp_w1_v, nsa_cmp_w2_v, rel_bias, ffn_w_up, ffn_conv_w, ffn_w_down, final_norm_g):
    bsz, seq, d = x.shape
    assert d == D_MODEL and seq % TOK_TILE == 0
    lower = jnp.cumsum(jax.nn.softmax(hgrn_lb_logits.astype(F32), axis=0), axis=0)
    mod = _modulation(c, mod_w, mod_b)
    parts = lambda l: [mod[l, :, j * d:(j + 1) * d].reshape(bsz, 1, d) for j in range(6)]

    sh1, sc1, g1, sh2, sc2, g2 = parts(0)
    x = _mixer0(x, sh1, sc1, g1, norm_mix_g[0], ab_w_in[0], ab_w_out[0], lower[0], hgrn_onorm_g[0], sconv_w[0])
    x = _conv_ffn(x, sh2, sc2, g2, norm_ffn_g[0], ffn_w_up[0], ffn_conv_w[0], ffn_w_down[0], final_norm_g)

    sh1, sc1, g1, sh2, sc2, g2 = parts(1)
    kc, vc, ks, kw, q_t, gates_t, vs_t, vw_t = _nsa_proj(x, sh1, sc1, norm_mix_g[1], nsa_w_in[0])
    k_cmp, v_cmp_t = _compress(kc, vc, nsa_cmp_pos_k[0], nsa_cmp_pos_v[0],
                               nsa_cmp_w1_k[0], nsa_cmp_w2_k[0], nsa_cmp_w1_v[0], nsa_cmp_w2_v[0])
    near, bias_cmp = _bias_tables(rel_bias.astype(F32) * LOG2E, seq)
    oc_t, sel = _cmp_topk(q_t, k_cmp, v_cmp_t, bias_cmp, seq)
    o_lo, o_hi = _sel_win(q_t, ks, vs_t, kw, vw_t, sel, near, gates_t, oc_t)
    return _conv_ffn(x, sh2, sc2, g2, norm_ffn_g[1], ffn_w_up[1], ffn_conv_w[1], ffn_w_down[1], final_norm_g,
                     attn=(o_lo, o_hi, nsa_w_out[0], g1), final=True)
```

```python
import functools
import math

import jax
import jax.numpy as jnp
import numpy as np
from jax import lax
from jax.experimental import pallas as pl
from jax.experimental.pallas import tpu as pltpu

F32 = jnp.float32
BF16 = jnp.bfloat16

EPS = 1e-6
D_MODEL = 1024
D_FF = 2816

HG_HEADS = 4
HG_DK = 128
HG_WIDTH = HG_HEADS * HG_DK
SC_WIDTH = D_MODEL - HG_WIDTH
AB_IN = 4 * HG_WIDTH + 3 * SC_WIDTH
HG_CHUNK = 64
HG_LEVELS = 6

NSA_HEADS = 16
NSA_KV = 4
NSA_HPG = NSA_HEADS // NSA_KV
NSA_DH = D_MODEL // NSA_HEADS
CMP_LEN = 32
CMP_STRIDE = 16
CMP_HIDDEN = 256
SEL_BLOCK = 64
SEL_TOPN = 16
WINDOW = 512
REL_BUCKETS = 32
REL_MAX_DIST = 1024

MXU_DIM = 256
VMEM_LIMIT = 56 * 1024 * 1024
TOK_TILE = 256
FFN_TILE = 512
NEAR_TILES = 5
SEL_TABLES = NEAR_TILES + 1
WIN_TILES = WINDOW // TOK_TILE + 1
NEG = -1e30
LOG2E = math.log2(math.e)
KEY_LANES = 128
SEL_ROWS = 16
MIX_BATCH = 2
CMP_BATCH = 8
SW_HEADS = 4
MXU_SKEW = 5
HALO = 8

_NT = (((1,), (1,)), ((), ()))
_TN = (((0,), (0,)), ((), ()))


def _sigmoid(x):
    return 1.0 / (1.0 + jnp.exp(-x))


def _silu(x):
    return x * _sigmoid(x)


def _norm_mod(x, g, sc, sh):
    ms = jnp.mean(x * x, axis=-1, keepdims=True)
    return (x * lax.rsqrt(ms + EPS) * g) * (1.0 + sc) + sh


def _fold(x, op):
    return op(x.reshape(x.shape[0] // 8, 8, x.shape[1]), axis=0)


def _const_spec(shape):
    n = len(shape)
    return pl.BlockSpec(shape, lambda *_: (0,) * n, pipeline_mode=pl.Buffered(1))


def _params(sem):
    return pltpu.CompilerParams(dimension_semantics=sem, vmem_limit_bytes=VMEM_LIMIT)


def _mod_kernel(c_ref, w_ref, b_ref, o_ref):
    c = c_ref[...]
    o_ref[0] = jnp.dot(_silu(c), w_ref[0], preferred_element_type=F32,
                       precision=lax.Precision.HIGHEST) + b_ref[0]


def _modulation(c, mod_w, mod_b):
    depth, d, n = mod_w.shape
    bsz = c.shape[0]
    tn = 1024
    return pl.pallas_call(
        _mod_kernel,
        grid=(depth, n // tn),
        in_specs=[pl.BlockSpec((bsz, d), lambda l, j: (0, 0)),
                  pl.BlockSpec((1, d, tn), lambda l, j: (l, 0, j)),
                  pl.BlockSpec((1, 1, tn), lambda l, j: (l, 0, j))],
        out_specs=pl.BlockSpec((1, bsz, tn), lambda l, j: (l, 0, j)),
        out_shape=jax.ShapeDtypeStruct((depth, bsz, n), F32),
        compiler_params=_params(("arbitrary", "arbitrary")),
        name="adaln_mod",
    )(c, mod_w, mod_b.reshape(depth, 1, n))


def _hgrn_cumsum_matrix():
    t = np.arange(HG_CHUNK)
    return (t[None, :] <= t[:, None]).astype(np.float32)


def _mixer0_kernel(x_ref, sh_ref, sc_ref, gt_ref, ng_ref, win_ref, wout_ref, lb_ref, og_ref, cw_ref,
                   pm_ref, o_ref, proj_ref, cat_ref, st_ref, ubuf_ref, d_scr):
    nb, tile = x_ref.shape[0], x_ref.shape[1]
    c64 = HG_CHUNK
    n_hg = 4 * HG_WIDTH

    @pl.when(pl.program_id(1) == 0)
    def _():
        st_ref[...] = jnp.zeros_like(st_ref)
        ubuf_ref[:, 0:HALO, :] = jnp.zeros((nb, HALO, SC_WIDTH), F32)

    row = lax.broadcasted_iota(jnp.int32, (c64, HG_DK), 0)
    r64 = lax.broadcasted_iota(jnp.int32, (c64, c64), 0)
    c64i = lax.broadcasted_iota(jnp.int32, (c64, c64), 1)
    second = [None] + [(row & (1 << (lvl - 1))) != 0 for lvl in range(1, HG_LEVELS + 1)]
    same = [None] + [(r64 >> lvl) == (c64i >> lvl) for lvl in range(1, HG_LEVELS + 1)]
    eye = r64 == c64i
    pm = pm_ref[...]
    og = og_ref[...]
    units = [(c, h) for c in range(tile // c64) for h in range(HG_HEADS)]
    hm, factors = {}, {}

    def blk(s, c, h, j):
        return s, pl.ds(c * c64, c64), pl.ds(j * HG_WIDTH + h * HG_DK, HG_DK)

    def project_heads(s):
        hm[s] = _norm_mod(x_ref[s], ng_ref[...], sc_ref[s], sh_ref[s]).astype(BF16)
        proj_ref[s, :, :n_hg] = jnp.dot(hm[s], win_ref[:, :n_hg], preferred_element_type=F32)

    def project_conv(s):
        proj_ref[s, :, n_hg:] = jnp.dot(hm[s], win_ref[:, n_hg:], preferred_element_type=F32)

    def decay_sums(s):
        for u, (c, h) in enumerate(units):
            lb = lb_ref[:, h * HG_DK:(h + 1) * HG_DK]
            fg = lb + (1.0 - lb) * _sigmoid(proj_ref[blk(s, c, h, 1)])
            proj_ref[blk(s, c, h, 1)] = 1.0 - fg
            lg = jnp.log(fg)
            lg_hi = lg.astype(BF16)
            lg_lo = (lg - lg_hi.astype(F32)).astype(BF16)
            dall = jnp.dot(pm, jnp.concatenate([lg_hi, lg_lo], axis=1), preferred_element_type=F32)
            d_scr[s, u] = dall[:, :HG_DK] + dall[:, HG_DK:]

    def level_products(s):
        intra, q_in, k_out, decay = [], [], [], []
        for u, (c, h) in enumerate(units):
            q = proj_ref[blk(s, c, h, 0)]
            k = proj_ref[blk(s, c, h, 1)]
            b = d_scr[s, u, 0:c64, :]
            a = jnp.zeros((c64, c64), F32)
            for lvl in range(1, HG_LEVELS + 1):
                m = 1 << (lvl - 1)
                if lvl == 1:
                    qt = jnp.where(second[lvl], q * (1.0 - k), 0.0).astype(BF16)
                    kt = jnp.where(second[lvl], 0.0, k).astype(BF16)
                else:
                    mids = [jnp.broadcast_to(d_scr[s, u, r0 + m - 1:r0 + m, :], (2 * m, HG_DK))
                            for r0 in range(0, c64, 2 * m)]
                    dlt = b - (mids[0] if len(mids) == 1 else jnp.concatenate(mids, axis=0))
                    e = jnp.exp(-jnp.abs(dlt))
                    qt = jnp.where(second[lvl], q * e, 0.0).astype(BF16)
                    kt = jnp.where(second[lvl], 0.0, k * e).astype(BF16)
                al = lax.dot_general(qt, kt, _NT, preferred_element_type=F32)
                a = a + (al if lvl == HG_LEVELS else jnp.where(same[lvl], al, 0.0))
            intra.append(jnp.where(eye, jnp.sum(q * k, axis=-1, keepdims=True), a).astype(BF16))
            b_last = d_scr[s, u, c64 - 1:c64, :]
            q_in.append((q * jnp.exp(b)).astype(BF16))
            k_out.append((k * jnp.exp(b_last - b)).astype(BF16))
            decay.append(jnp.exp(b_last))
        factors[s] = (intra, q_in, k_out, decay)

    def recurrence(s):
        intra, q_in, k_out, decay = factors[s]
        state = [st_ref[s, h] for h in range(HG_HEADS)]
        for u, (c, h) in enumerate(units):
            vb = proj_ref[blk(s, c, h, 2)].astype(BF16)
            o = jnp.dot(intra[u], vb, preferred_element_type=F32)
            o = o + lax.dot_general(q_in[u], state[h].astype(BF16), _NT, preferred_element_type=F32)
            state[h] = state[h] * decay[u] + lax.dot_general(vb, k_out[u], _TN, preferred_element_type=F32)
            on = o * lax.rsqrt(jnp.mean(o * o, axis=-1, keepdims=True) + EPS) * og
            gate = _silu(proj_ref[blk(s, c, h, 3)])
            cat_ref[s, pl.ds(c * c64, c64), h * HG_DK:(h + 1) * HG_DK] = (on * gate).astype(BF16)
        for h in range(HG_HEADS):
            st_ref[s, h] = state[h]

    def short_conv_and_out(s):
        off = n_hg
        u = (proj_ref[s, :, off + SC_WIDTH:off + 2 * SC_WIDTH]
             * proj_ref[s, :, off + 2 * SC_WIDTH:off + 3 * SC_WIDTH])
        ubuf_ref[s, HALO:HALO + tile, :] = u
        cw = cw_ref[...]
        conv = (ubuf_ref[s, HALO - 2:HALO - 2 + tile, :] * cw[0:1]
                + ubuf_ref[s, HALO - 1:HALO - 1 + tile, :] * cw[1:2] + u * cw[2:3])
        ubuf_ref[s, 0:HALO, :] = u[tile - HALO:tile, :]
        cat_ref[s, :, HG_WIDTH:] = (proj_ref[s, :, off:off + SC_WIDTH] * conv).astype(BF16)
        y = jnp.dot(cat_ref[s], wout_ref[...], preferred_element_type=F32)
        o_ref[s] = x_ref[s] + gt_ref[s] * y

    for s in range(nb):
        project_heads(s)
    decay_sums(0)
    project_conv(0)
    for s in range(1, nb):
        decay_sums(s)
        level_products(s - 1)
        project_conv(s)
    level_products(nb - 1)
    for s in range(nb):
        recurrence(s)
        short_conv_and_out(s)


def _mixer0(x, sh, sc, gt, norm_g, w_in, w_out, lower, onorm_g, sconv_w):
    bsz, seq, d = x.shape
    t = TOK_TILE
    nb = MIX_BATCH if bsz % MIX_BATCH == 0 else 1
    pm = jnp.asarray(_hgrn_cumsum_matrix(), BF16)
    vec = pl.BlockSpec((nb, 1, d), lambda b, i: (b, 0, 0))
    return pl.pallas_call(
        _mixer0_kernel,
        grid=(bsz // nb, seq // t),
        in_specs=[pl.BlockSpec((nb, t, d), lambda b, i: (b, i, 0)), vec, vec, vec,
                  _const_spec((1, d)), _const_spec((d, AB_IN)), _const_spec((d, d)),
                  _const_spec((1, HG_WIDTH)), _const_spec((1, HG_DK)), _const_spec((3, SC_WIDTH)),
                  _const_spec(pm.shape)],
        out_specs=pl.BlockSpec((nb, t, d), lambda b, i: (b, i, 0)),
        out_shape=jax.ShapeDtypeStruct(x.shape, F32),
        scratch_shapes=[pltpu.VMEM((nb, t, AB_IN), F32), pltpu.VMEM((nb, t, d), BF16),
                        pltpu.VMEM((nb, HG_HEADS, HG_DK, HG_DK), F32),
                        pltpu.VMEM((nb, t + HALO, SC_WIDTH), F32),
                        pltpu.VMEM((nb, t // HG_CHUNK * HG_HEADS, HG_CHUNK, HG_DK), F32)],
        compiler_params=_params(("arbitrary", "arbitrary")),
        name="mixer_hgrn_sconv",
    )(x, sh, sc, gt, norm_g.reshape(1, d), w_in.astype(BF16), w_out.astype(BF16),
      lower.reshape(1, HG_WIDTH), onorm_g.reshape(1, HG_DK), sconv_w, pm)


FF_CHUNKS = ((0, 5 * MXU_DIM), (5 * MXU_DIM, D_FF))


def _ffn_kernel(*refs, has_attn, final):
    if has_attn:
        (x_ref, olo_ref, ohi_ref, wo_ref, g1_ref, sh_ref, sc_ref, g2_ref, ng_ref, wup_ref, cw_ref, wdn_ref, fg_ref,
         o_ref, gbuf_ref, carry_ref) = refs
    else:
        (x_ref, sh_ref, sc_ref, g2_ref, ng_ref, wup_ref, cw_ref, wdn_ref, fg_ref,
         o_ref, gbuf_ref, carry_ref) = refs
    tile = x_ref.shape[1]

    @pl.when(pl.program_id(1) == 0)
    def _():
        carry_ref[...] = jnp.zeros_like(carry_ref)

    x = x_ref[0]
    if has_attn:
        o_t = jnp.where(pl.program_id(1) < pl.num_programs(1) // 2, olo_ref[0, 0], ohi_ref[0, 0])
        x = x + g1_ref[0] * lax.dot_general(o_t, wo_ref[...], _TN, preferred_element_type=F32)
    hf = _norm_mod(x, ng_ref[...], sc_ref[0], sh_ref[0]).astype(BF16)
    gbuf_ref[0:HALO, :] = carry_ref[...]
    gates, ups = [], []
    for c0, c1 in FF_CHUNKS:
        gate = jnp.dot(hf, wup_ref[:, c0:c1], preferred_element_type=F32)
        ups.append(jnp.dot(hf, wup_ref[:, D_FF + c0:D_FF + c1], preferred_element_type=F32))
        gbuf_ref[HALO:HALO + tile, c0:c1] = gate
        carry_ref[:, c0:c1] = gate[tile - HALO:tile, :]
        gates.append(gate)
    acc = jnp.zeros((tile, D_MODEL), F32)
    for j, (c0, c1) in enumerate(FF_CHUNKS):
        cols = slice(c0, c1)
        cw = cw_ref[:, cols]
        conv = (gbuf_ref[HALO - 2:HALO - 2 + tile, cols] * cw[0:1] + gbuf_ref[HALO - 1:HALO - 1 + tile, cols] * cw[1:2]
                + gates[j] * cw[2:3])
        act = (_silu(conv) * ups[j]).astype(BF16)
        acc = acc + jnp.dot(act, wdn_ref[cols, :], preferred_element_type=F32)
    out = x + g2_ref[0] * acc
    if final:
        out = out * lax.rsqrt(jnp.mean(out * out, axis=-1, keepdims=True) + EPS) * fg_ref[...]
    o_ref[0] = out


def _conv_ffn(x, sh, sc, g2, norm_g, w_up, conv_w, w_down, final_g, attn=None, final=False):
    bsz, seq, d = x.shape
    t = TOK_TILE if attn is not None else FFN_TILE
    vec = pl.BlockSpec((1, 1, d), lambda b, i: (b, 0, 0))
    args, specs = [x], [pl.BlockSpec((1, t, d), lambda b, i: (b, i, 0))]
    if attn is not None:
        o_lo, o_hi, w_o, g1 = attn
        top = o_lo.shape[1] - 1
        last = seq // t - 1
        args += [o_lo, o_hi, w_o.astype(BF16), g1]
        specs += [pl.BlockSpec((1, 1, d, t), lambda b, i: (b, jnp.minimum(i, top), 0, 0)),
                  pl.BlockSpec((1, 1, d, t), lambda b, i: (b, jnp.minimum(last - i, top), 0, 0)),
                  _const_spec((d, d)), vec]
    args += [sh, sc, g2, norm_g.reshape(1, d), w_up.astype(BF16), conv_w, w_down.astype(BF16),
             final_g.reshape(1, d)]
    specs += [vec, vec, vec, _const_spec((1, d)), _const_spec((d, 2 * D_FF)), _const_spec((3, D_FF)),
              _const_spec((D_FF, d)), _const_spec((1, d))]
    return pl.pallas_call(
        functools.partial(_ffn_kernel, has_attn=attn is not None, final=final),
        grid=(bsz, seq // t),
        in_specs=specs,
        out_specs=pl.BlockSpec((1, t, d), lambda b, i: (b, i, 0)),
        out_shape=jax.ShapeDtypeStruct(x.shape, F32),
        scratch_shapes=[pltpu.VMEM((t + HALO, D_FF), F32), pltpu.VMEM((HALO, D_FF), F32)],
        compiler_params=_params(("arbitrary", "arbitrary")),
        name="conv_ffn_attnproj" if attn is not None else "conv_ffn",
    )(*args)


KV_ROWS = NSA_KV * NSA_DH


def _nsa_proj_kernel(x_ref, sh_ref, sc_ref, ng_ref, wa_ref, wq_ref, wg_ref, wv_ref, hot_ref, perm_ref,
                     kc_ref, vc_ref, ks_ref, kw_ref, qt_ref, gt_ref, vst_ref, vwt_ref):
    hm = _norm_mod(x_ref[0], ng_ref[...], sc_ref[0], sh_ref[0]).astype(BF16)
    kv = jnp.dot(hm, wa_ref[...], preferred_element_type=F32).astype(BF16)
    cmp_in = jnp.dot(perm_ref[...], kv[:, :2 * KV_ROWS], preferred_element_type=F32).astype(BF16)
    n_grp = perm_ref.shape[0] // CMP_STRIDE
    for g in range(NSA_KV):
        part = lambda j: kv[:, j * KV_ROWS + g * NSA_DH:j * KV_ROWS + (g + 1) * NSA_DH]
        for p in range(CMP_STRIDE):
            rows = slice(p * n_grp, (p + 1) * n_grp)
            kc_ref[0, g, p] = cmp_in[rows, g * NSA_DH:(g + 1) * NSA_DH]
            vc_ref[0, g, p] = cmp_in[rows, KV_ROWS + g * NSA_DH:KV_ROWS + (g + 1) * NSA_DH]
        ks_ref[0, g] = jnp.concatenate([part(2), hot_ref[...]], axis=1)
        kw_ref[0, g] = part(3)
    qt = lax.dot_general(wq_ref[...], hm, _NT, preferred_element_type=F32)
    qt_ref[0, 0] = (qt * (NSA_DH ** -0.5 * LOG2E)).astype(BF16)
    gt_ref[0] = _sigmoid(lax.dot_general(wg_ref[...], hm, _NT, preferred_element_type=F32))
    vt = lax.dot_general(wv_ref[...], hm, _NT, preferred_element_type=F32).astype(BF16)
    vst_ref[0, 0] = vt[:KV_ROWS]
    vwt_ref[0, 0] = vt[KV_ROWS:]


def _nsa_proj(x, sh, sc, norm_g, w_in):
    bsz, seq, d = x.shape
    t = TOK_TILE
    nt = seq // t
    nq, ng = NSA_HEADS * NSA_DH, 3 * NSA_HEADS
    o = nq + ng
    w = w_in.astype(BF16)
    col = lambda j: w[:, o + j * KV_ROWS:o + (j + 1) * KV_ROWS]
    w_a = jnp.concatenate([col(0), col(1), col(2), col(4)], axis=1)
    w_q = w[:, :nq].T
    w_g = w[:, nq:o].T
    w_v = jnp.concatenate([col(3), col(5)], axis=1).T
    hot = (np.arange(t)[:, None] // SEL_BLOCK == np.arange(KEY_LANES - NSA_DH)[None, :]).astype(np.float32)
    n_grp = t // CMP_STRIDE
    r = np.arange(t)
    perm = (r[None, :] == ((r % n_grp) * CMP_STRIDE + r // n_grp)[:, None]).astype(np.float32)
    vec = pl.BlockSpec((1, 1, d), lambda b, i: (b, 0, 0))
    tiled = lambda r: pl.BlockSpec((1, 1, r, t), lambda b, i: (b, i, 0, 0))
    grouped = lambda n: pl.BlockSpec((1, NSA_KV, t, n), lambda b, i: (b, 0, i, 0))
    rows = lambda n: jax.ShapeDtypeStruct((bsz, NSA_KV, seq, n), BF16)
    by_offset = pl.BlockSpec((1, NSA_KV, CMP_STRIDE, n_grp, NSA_DH), lambda b, i: (b, 0, 0, i, 0))
    offset_major = jax.ShapeDtypeStruct((bsz, NSA_KV, CMP_STRIDE, seq // CMP_STRIDE, NSA_DH), BF16)
    return pl.pallas_call(
        _nsa_proj_kernel,
        grid=(bsz, nt),
        in_specs=[pl.BlockSpec((1, t, d), lambda b, i: (b, i, 0)), vec, vec, _const_spec((1, d)),
                  _const_spec((d, 4 * KV_ROWS)), _const_spec((nq, d)), _const_spec((ng, d)),
                  _const_spec((2 * KV_ROWS, d)), _const_spec(hot.shape), _const_spec(perm.shape)],
        out_specs=[by_offset, by_offset, grouped(KEY_LANES), grouped(NSA_DH), tiled(nq),
                   pl.BlockSpec((1, ng, t), lambda b, i: (b, 0, i)), tiled(KV_ROWS), tiled(KV_ROWS)],
        out_shape=[offset_major, offset_major, rows(KEY_LANES), rows(NSA_DH),
                   jax.ShapeDtypeStruct((bsz, nt, nq, t), BF16),
                   jax.ShapeDtypeStruct((bsz, ng, seq), F32),
                   jax.ShapeDtypeStruct((bsz, nt, KV_ROWS, t), BF16),
                   jax.ShapeDtypeStruct((bsz, nt, KV_ROWS, t), BF16)],
        compiler_params=_params(("arbitrary", "arbitrary")),
        name="nsa_in_proj",
    )(x, sh, sc, norm_g.reshape(1, d), w_a, w_q, w_g, w_v, jnp.asarray(hot, BF16), jnp.asarray(perm, BF16))


def _compress_kernel(hk_ref, hv_ref, pk_ref, pv_ref, w1k_ref, w2k_ref, w1v_ref, w2vt_ref, kc_ref, vct_ref):
    half = CMP_STRIDE * NSA_DH

    def hidden(h_ref, pos_ref, w1_ref):
        n = h_ref.shape[3]
        top = jnp.zeros((n, CMP_HIDDEN), F32)
        bot = jnp.zeros((n, CMP_HIDDEN), F32)
        per = MXU_DIM // NSA_DH
        for p0 in range(0, CMP_STRIDE, per):
            h = jnp.concatenate([h_ref[0, 0, p] for p in range(p0, p0 + per)], axis=1)
            rows = slice(p0 * NSA_DH, (p0 + per) * NSA_DH)
            top = top + jnp.dot(h, w1_ref[rows, :], preferred_element_type=F32)
            bot = bot + jnp.dot(h, w1_ref[half + p0 * NSA_DH:half + (p0 + per) * NSA_DH, :],
                                preferred_element_type=F32)
        pos = jnp.dot(jnp.broadcast_to(pos_ref[...], (8, 2 * half)).astype(BF16), w1_ref[...],
                      preferred_element_type=F32)[0:1]
        return _silu(top + pltpu.roll(bot, n - 1, 0) + pos).astype(BF16)

    kc_ref[0, 0] = jnp.dot(hidden(hk_ref, pk_ref, w1k_ref), w2k_ref[...],
                           preferred_element_type=F32).astype(BF16)
    vct_ref[0, 0] = lax.dot_general(w2vt_ref[...], hidden(hv_ref, pv_ref, w1v_ref), _NT,
                                    preferred_element_type=F32).astype(BF16)


def _compress(hk, hv, pos_k, pos_v, w1_k, w2_k, w1_v, w2_v):
    bsz, g, stride, n, dh = hk.shape
    width = stride * dh
    blk = pl.BlockSpec((1, 1, stride, n, dh), lambda b, j: (b, j, 0, 0, 0))
    return pl.pallas_call(
        _compress_kernel,
        grid=(bsz, g),
        in_specs=[blk, blk, _const_spec((1, width * 2)), _const_spec((1, width * 2)),
                  _const_spec((2 * width, CMP_HIDDEN)), _const_spec((CMP_HIDDEN, NSA_DH)),
                  _const_spec((2 * width, CMP_HIDDEN)), _const_spec((NSA_DH, CMP_HIDDEN))],
        out_specs=[pl.BlockSpec((1, 1, n, NSA_DH), lambda b, j: (b, j, 0, 0)),
                   pl.BlockSpec((1, 1, NSA_DH, n), lambda b, j: (b, j, 0, 0))],
        out_shape=[jax.ShapeDtypeStruct((bsz, g, n, NSA_DH), BF16),
                   jax.ShapeDtypeStruct((bsz, g, NSA_DH, n), BF16)],
        compiler_params=_params(("arbitrary", "arbitrary")),
        name="nsa_compress",
    )(hk, hv, pos_k.reshape(1, -1), pos_v.reshape(1, -1), w1_k.astype(BF16), w2_k.astype(BF16),
      w1_v.astype(BF16), w2_v.T.astype(BF16))


def _bucket_upper_bounds():
    n = np.arange(4 * REL_MAX_DIST, dtype=np.float64)
    exact = REL_BUCKETS // 2
    large = exact + (np.log(np.maximum(n, exact) / exact) / math.log(REL_MAX_DIST / exact)
                     * (REL_BUCKETS - exact)).astype(np.int64)
    bucket = np.where(n < exact, n.astype(np.int64), np.minimum(large, REL_BUCKETS - 1))
    return [int(np.max(np.nonzero(bucket <= j)[0])) for j in range(REL_BUCKETS - 1)]


def _bias_of_dist(dist, rb_ref, heads, uppers):
    vals = [jnp.full(dist.shape, rb_ref[REL_BUCKETS - 1, h], F32) for h in heads]
    for j in range(REL_BUCKETS - 2, -1, -1):
        m = dist <= uppers[j]
        vals = [jnp.where(m, rb_ref[j, h], v) for h, v in zip(heads, vals)]
    return vals


def _bias_near_kernel(rb_ref, o_ref, line_scr, *, uppers):
    d = pl.program_id(0)
    t = o_ref.shape[2]
    heads = list(range(NSA_HEADS))

    @pl.when(d == 0)
    def _():
        line = ((lax.broadcasted_iota(jnp.int32, (8, t), 0) - 1) * t + lax.broadcasted_iota(jnp.int32, (8, t), 1))
        for h, v in zip(heads, _bias_of_dist(line, rb_ref, heads, uppers)):
            line_scr[h] = v

    is_win = d >= SEL_TABLES
    delta = jnp.where(is_win, d - SEL_TABLES, d)
    limit = jnp.where(is_win, WINDOW, (SEL_TABLES + 1) * t)
    dist = delta * t + lax.broadcasted_iota(jnp.int32, (t, t), 1) - lax.broadcasted_iota(jnp.int32, (t, t), 0)
    valid = (dist >= 0) & (dist < limit)
    for h in heads:
        span = jnp.concatenate([line_scr[h, pl.ds(delta, 1), :], line_scr[h, pl.ds(delta + 1, 1), :]], axis=1)
        rolled = pltpu.roll(jnp.broadcast_to(span, (t, 2 * t)), 0, 1, stride=1, stride_axis=0)
        o_ref[h, 0] = jnp.where(valid, rolled[:, t:], NEG)


CMP_BAND = 72


def _bias_cmp_kernel(rb_ref, o_ref, band_scr, *, uppers):
    tb = pl.program_id(0)
    n, t = o_ref.shape[1], o_ref.shape[2]
    heads = list(range(NSA_HEADS))
    per_tile = t // CMP_STRIDE
    first = CMP_STRIDE * (CMP_BAND - per_tile) - (CMP_LEN - 1)
    assert first + CMP_STRIDE > uppers[-1] and first + (t - 1) - CMP_STRIDE * CMP_BAND < 0

    @pl.when(tb == 0)
    def _():
        for r0 in range(0, CMP_BAND, 8):
            row = r0 + lax.broadcasted_iota(jnp.int32, (8, t), 0)
            dist = first + lax.broadcasted_iota(jnp.int32, (8, t), 1) - CMP_STRIDE * row
            for h, v in zip(heads, _bias_of_dist(dist, rb_ref, heads, uppers)):
                band_scr[h, r0:r0 + 8, :] = jnp.where(dist >= 0, v, NEG)

    def strip(i, carry):
        r0 = pl.multiple_of(i * 8, 8)
        rel = r0 - (tb * per_tile - (CMP_BAND - per_tile))
        at = pl.multiple_of(jnp.clip(rel, 0, CMP_BAND - 8), 8)
        for h in heads:
            inside = band_scr[h, pl.ds(at, 8), :]
            far = jnp.full((8, t), rb_ref[REL_BUCKETS - 1, h], F32)
            o_ref[h, pl.ds(r0, 8), :] = jnp.where(rel < 0, far, jnp.where(rel >= CMP_BAND, NEG, inside))
        return carry

    lax.fori_loop(0, n // 8, strip, 0)


def _bias_tables(rel_bias, seq):
    t = TOK_TILE
    n_cmp = seq // CMP_STRIDE
    uppers = _bucket_upper_bounds()
    assert NEAR_TILES * t - (t - 1) > uppers[-1]
    smem = pl.BlockSpec(memory_space=pltpu.SMEM)
    near = pl.pallas_call(
        functools.partial(_bias_near_kernel, uppers=uppers),
        grid=(SEL_TABLES + WIN_TILES,),
        in_specs=[smem],
        out_specs=pl.BlockSpec((NSA_HEADS, 1, t, t), lambda d: (0, d, 0, 0)),
        out_shape=jax.ShapeDtypeStruct((NSA_HEADS, SEL_TABLES + WIN_TILES, t, t), F32),
        scratch_shapes=[pltpu.VMEM((NSA_HEADS, 8, t), F32)],
        compiler_params=_params(("arbitrary",)),
        name="relbias_near",
    )(rel_bias)
    cmp_b = pl.pallas_call(
        functools.partial(_bias_cmp_kernel, uppers=uppers),
        grid=(seq // t,),
        in_specs=[smem],
        out_specs=pl.BlockSpec((NSA_HEADS, n_cmp, t), lambda i: (0, 0, i)),
        out_shape=jax.ShapeDtypeStruct((NSA_HEADS, n_cmp, seq), F32),
        scratch_shapes=[pltpu.VMEM((NSA_HEADS, CMP_BAND, t), F32)],
        compiler_params=_params(("arbitrary",)),
        name="relbias_cmp",
    )(rel_bias)
    return near, cmp_b


def _cmp_to_sel_t(seq):
    n_cmp, n_sel = seq // CMP_STRIDE, seq // SEL_BLOCK
    c_start = np.arange(n_cmp)[:, None] * CMP_STRIDE
    s_start = np.arange(n_sel)[None, :] * SEL_BLOCK
    inside = np.clip(np.minimum(c_start + CMP_LEN, s_start + SEL_BLOCK) - np.maximum(c_start, s_start), 0, None)
    return (inside / CMP_LEN).T.astype(np.float32)


def _cmp_topk_kernel(qt_ref, kc_ref, vct_ref, bias_ref, c2s_ref, oc_ref, sel_ref, score_scr, cnt_scr):
    tb = pl.program_id(1)
    n, t = bias_ref.shape[1], bias_ref.shape[2]
    n_sel = c2s_ref.shape[0]
    nb = qt_ref.shape[0]
    tok = tb * t + lax.broadcasted_iota(jnp.int32, (1, t), 1)
    any_visible = tok >= CMP_LEN - 1
    ones = (lax.broadcasted_iota(jnp.int32, (SEL_ROWS, n), 0) == 0).astype(BF16)
    head_rows = lambda i: slice(i * NSA_DH, (i + 1) * NSA_DH)
    weights = {}
    for bb in range(nb):
        kc = kc_ref[bb, 0]
        for i in range(NSA_HPG):
            s = jnp.dot(kc, qt_ref[bb, 0, head_rows(i), :], preferred_element_type=F32) + bias_ref[i]
            m = jnp.max(_fold(s, jnp.max), axis=0, keepdims=True)
            weights[bb, i] = jnp.exp2(s - m).astype(BF16)
    imps = []
    for bb in range(nb):
        lhs = jnp.concatenate([vct_ref[bb, 0], ones, c2s_ref[...]], axis=0)
        imp = jnp.zeros((n_sel, t), F32)
        for i in range(NSA_HPG):
            r = jnp.dot(lhs, weights[bb, i], preferred_element_type=F32)
            scale = jnp.where(any_visible, 1.0 / r[NSA_DH:NSA_DH + 1], 0.0)
            oc_ref[bb, 0, head_rows(i), :] = (r[:NSA_DH] * scale).astype(BF16)
            imp = imp + r[NSA_DH + SEL_ROWS:] * scale
        imps.append(imp)

    w = nb * t
    blk = lax.broadcasted_iota(jnp.int32, (n_sel, t), 0)
    cur = (tb * t + lax.broadcasted_iota(jnp.int32, (n_sel, t), 1)) // SEL_BLOCK
    forced = (blk == 0) | (blk == cur) | (blk == cur - 1)
    for bb in range(nb):
        score_scr[:, bb * t:(bb + 1) * t] = jnp.where(blk > cur, -jnp.inf, jnp.where(forced, jnp.inf, imps[bb]))
    groups = n_sel // 8
    per_tile = t // SEL_BLOCK
    sub = lax.broadcasted_iota(jnp.int32, (8, w), 0)
    cnt_scr[...] = jnp.zeros_like(cnt_scr)
    for j in range(groups):
        @pl.when(8 * j < (tb + 1) * per_tile)
        def _():
            src = [jnp.broadcast_to(score_scr[sp:sp + 1, :], (8, w)) for sp in range(8 * j, 8 * j + 8)]
            for v in range(groups):
                part = score_scr[8 * v:8 * v + 8, :]
                cnt = cnt_scr[8 * v:8 * v + 8, :]
                for sp, r in zip(range(8 * j, 8 * j + 8), src):
                    if j < v:
                        beats = r >= part
                    elif j > v:
                        beats = r > part
                    else:
                        beats = (r > part) | ((r == part) & (sub > sp - 8 * v))
                    cnt = cnt + jnp.where(beats, 1.0, 0.0)
                cnt_scr[8 * v:8 * v + 8, :] = cnt
    pad = jnp.zeros((SEL_ROWS - per_tile, t), F32)
    for v in range(groups):
        mask = jnp.where(cnt_scr[8 * v:8 * v + 8, :] < min(SEL_TOPN, n_sel), 0.0, NEG)
        for bb in range(nb):
            for r0 in range(0, 8, per_tile):
                kt = (8 * v + r0) // per_tile
                rows = mask[r0:r0 + per_tile, bb * t:(bb + 1) * t]
                sel_ref[bb, 0, 0, kt] = jnp.concatenate([rows, pad], axis=0).astype(BF16)


def _cmp_topk(q_t, k_cmp, v_cmp_t, bias_cmp, seq):
    bsz, nt, nq, t = q_t.shape
    n_cmp, n_sel = seq // CMP_STRIDE, seq // SEL_BLOCK
    grp = NSA_HPG * NSA_DH
    nb = CMP_BATCH if bsz % CMP_BATCH == 0 else 1
    c2s = jnp.asarray(_cmp_to_sel_t(seq), BF16)
    return pl.pallas_call(
        _cmp_topk_kernel,
        grid=(NSA_KV, nt, bsz // nb),
        in_specs=[pl.BlockSpec((nb, 1, grp, t), lambda g, i, b: (b, i, g, 0)),
                  pl.BlockSpec((nb, 1, n_cmp, NSA_DH), lambda g, i, b: (b, g, 0, 0)),
                  pl.BlockSpec((nb, 1, NSA_DH, n_cmp), lambda g, i, b: (b, g, 0, 0)),
                  pl.BlockSpec((NSA_HPG, n_cmp, t), lambda g, i, b: (g, 0, i)),
                  _const_spec((n_sel, n_cmp))],
        out_specs=[pl.BlockSpec((nb, 1, grp, t), lambda g, i, b: (b, i, g, 0)),
                   pl.BlockSpec((nb, 1, 1, nt, SEL_ROWS, t), lambda g, i, b: (b, g, i, 0, 0, 0))],
        out_shape=[jax.ShapeDtypeStruct((bsz, nt, nq, t), BF16),
                   jax.ShapeDtypeStruct((bsz, NSA_KV, nt, nt, SEL_ROWS, t), BF16)],
        scratch_shapes=[pltpu.VMEM((n_sel, nb * t), F32), pltpu.VMEM((n_sel, nb * t), F32)],
        compiler_params=_params(("arbitrary", "arbitrary", "arbitrary")),
        name="nsa_cmp_topk",
    )(q_t, k_cmp, v_cmp_t, bias_cmp, c2s)


def _sel_win_kernel(qa_ref, qb_ref, ks_ref, vst_ref, kw_ref, vwt_ref, sela_ref, selb_ref, tab_ref, ga_ref, gb_ref,
                    oca_ref, ocb_ref, olo_ref, ohi_ref, part_scr):
    h0 = pl.program_id(0) * SW_HEADS
    i = pl.program_id(2)
    nt, t = vst_ref.shape[1], vst_ref.shape[3]
    half = nt // 2
    tb = (i, nt - 1 - i)
    head_rows = lambda hh: slice(hh * NSA_DH, (hh + 1) * NSA_DH)
    q = [(qa_ref[0, 0, head_rows(hh), :], qb_ref[0, 0, head_rows(hh), :]) for hh in range(SW_HEADS)]
    sel = (sela_ref, selb_ref)

    def tile_rows(kt):
        return pl.ds(pl.multiple_of(kt * t, t), t)

    def pick(on_a, xa, xb):
        if isinstance(on_a, bool):
            return xa if on_a else xb
        return jnp.where(on_a, xa, xb)

    work = []
    for p in range(nt + 1):
        on_a = True if p == 0 else (False if p >= half else p <= i)
        kt = pick(on_a, p, jnp.maximum(p - i - 1, 0))
        work.append((on_a, kt, True, nt - p if on_a is False else pick(on_a, tb[0] - kt, nt - p)))
    for side in range(2):
        for delta in range(WIN_TILES):
            keep = True if (side == 1 and half >= WIN_TILES) else tb[side] >= delta
            work.append((side == 0, jnp.maximum(tb[side] - delta, 0), keep, delta))
    n_selpos = nt + 1

    ones = (lax.broadcasted_iota(jnp.int32, (SEL_ROWS, t), 0) == 0).astype(BF16)
    q_pad = jnp.zeros((KEY_LANES - NSA_DH - SEL_ROWS, t), BF16)
    tile_max = {}

    def scores(p, hh):
        on_a, kt, keep, delta = work[p]
        qp = pick(on_a, q[hh][0], q[hh][1])
        if p >= n_selpos:
            s = jnp.dot(kw_ref[0, 0, tile_rows(kt), :], qp, preferred_element_type=F32)
            s = s + tab_ref[hh, SEL_TABLES + delta]
        else:
            rows = pick(on_a, sel[0][0, 0, 0, kt], sel[1][0, 0, 0, kt])
            q_aug = jnp.concatenate([qp, rows, q_pad], axis=0)
            s = jnp.dot(ks_ref[0, 0, tile_rows(kt), :], q_aug, preferred_element_type=F32)
            if isinstance(delta, int) and delta >= SEL_TABLES - 1:
                m_tile = jnp.max(_fold(s, jnp.max), axis=0, keepdims=True)
                return m_tile + tab_ref[hh, SEL_TABLES - 1, 0:1, :], jnp.exp2(s - m_tile).astype(BF16)
            s = s + tab_ref[hh, jnp.minimum(delta, SEL_TABLES - 1)]
        m_tile = jnp.max(_fold(s, jnp.max), axis=0, keepdims=True)
        return m_tile, jnp.exp2(s - m_tile).astype(BF16)

    def values(p, hh, m_tile, pr):
        on_a, kt, keep, delta = work[p]
        br = int(p >= n_selpos)
        vt_ref = vwt_ref if br else vst_ref
        part_scr[hh, p] = jnp.dot(jnp.concatenate([vt_ref[0, kt], ones], axis=0), pr,
                                  preferred_element_type=F32)
        if not isinstance(keep, bool):
            m_tile = jnp.where(keep, m_tile, NEG)
        if isinstance(on_a, bool):
            tile_max[p, hh] = (m_tile, None) if on_a else (None, m_tile)
        else:
            tile_max[p, hh] = (jnp.where(on_a, m_tile, NEG), jnp.where(on_a, NEG, m_tile))

    def finish(hh):
        for side, (g_ref, oc_ref, o_ref) in enumerate(((ga_ref, oca_ref, olo_ref), (gb_ref, ocb_ref, ohi_ref))):
            outs = []
            for ps in (range(n_selpos), range(n_selpos, len(work))):
                mine = [p for p in ps if tile_max[p, hh][side] is not None]
                top = functools.reduce(jnp.maximum, [tile_max[p, hh][side] for p in mine])
                acc = jnp.zeros((NSA_DH + SEL_ROWS, t), F32)
                for p in mine:
                    acc = acc + part_scr[hh, p] * jnp.exp2(tile_max[p, hh][side] - top)
                outs.append(acc[:NSA_DH] / acc[NSA_DH:NSA_DH + 1])
            gate = lambda br: g_ref[0, pl.ds(br * NSA_HEADS + h0 + hh, 1), :]
            o_c = oc_ref[0, 0, head_rows(hh), :].astype(F32)
            o_ref[0, 0, head_rows(hh), :] = (gate(0) * o_c + gate(1) * outs[0] + gate(2) * outs[1]).astype(BF16)

    items = [(p, hh) for hh in range(SW_HEADS) for p in range(len(work))]
    pending = {}
    for step in range(len(items) + MXU_SKEW):
        if step < len(items):
            pending[step] = scores(*items[step])
        if step >= MXU_SKEW:
            p, hh = items[step - MXU_SKEW]
            values(p, hh, *pending.pop(step - MXU_SKEW))
            if p == len(work) - 1:
                finish(hh)


def _sel_win(q_t, ks, vs_t, kw, vw_t, sel, tables, gates_t, oc_t):
    bsz, nt, nq, t = q_t.shape
    seq = nt * t
    last = nt - 1
    per_group = NSA_HPG // SW_HEADS
    head_a = pl.BlockSpec((1, 1, SW_HEADS * NSA_DH, t), lambda h, b, i: (b, i, h, 0))
    head_b = pl.BlockSpec((1, 1, SW_HEADS * NSA_DH, t), lambda h, b, i: (b, last - i, h, 0))
    keys = pl.BlockSpec((1, 1, seq, NSA_DH), lambda h, b, i: (b, h // per_group, 0, 0))
    keys_sel = pl.BlockSpec((1, 1, seq, KEY_LANES), lambda h, b, i: (b, h // per_group, 0, 0))
    vals = pl.BlockSpec((1, nt, NSA_DH, t), lambda h, b, i: (b, 0, h // per_group, 0))
    sel_a = pl.BlockSpec((1, 1, 1, nt, SEL_ROWS, t), lambda h, b, i: (b, h // per_group, i, 0, 0, 0))
    sel_b = pl.BlockSpec((1, 1, 1, nt, SEL_ROWS, t), lambda h, b, i: (b, h // per_group, last - i, 0, 0, 0))
    gate_a = pl.BlockSpec((1, 3 * NSA_HEADS, t), lambda h, b, i: (b, 0, i))
    gate_b = pl.BlockSpec((1, 3 * NSA_HEADS, t), lambda h, b, i: (b, 0, last - i))
    out = jax.ShapeDtypeStruct((bsz, nt // 2, nq, t), BF16)
    return pl.pallas_call(
        _sel_win_kernel,
        grid=(NSA_HEADS // SW_HEADS, bsz, nt // 2),
        in_specs=[head_a, head_b, keys_sel, vals, keys, vals, sel_a, sel_b,
                  pl.BlockSpec((SW_HEADS, SEL_TABLES + WIN_TILES, t, t), lambda h, b, i: (h, 0, 0, 0)),
                  gate_a, gate_b, head_a, head_b],
        out_specs=[head_a, head_a],
        out_shape=[out, out],
        scratch_shapes=[pltpu.VMEM((SW_HEADS, nt + 1 + 2 * WIN_TILES, NSA_DH + SEL_ROWS, t), F32)],
        compiler_params=_params(("arbitrary", "arbitrary", "arbitrary")),
        name="nsa_sel_win",
    )(q_t, q_t, ks, vs_t, kw, vw_t, sel, sel, tables, gates_t, gates_t, oc_t, oc_t)


def kernel(x, c, mod_w, mod_b, norm_mix_g, norm_ffn_g, ab_w_in, ab_w_out, hgrn_lb_logits, hgrn_onorm_g, sconv_w, nsa_w_in, nsa_w_out, nsa_cmp_pos_k, nsa_cmp_pos_v, nsa_cmp_w1_k, nsa_cmp_w2_k, nsa_cmp_w1_v, nsa_cmp_w2_v, rel_bias, ffn_w_up, ffn_conv_w, ffn_w_down, final_norm_g):
    bsz, seq, d = x.shape
    assert d == D_MODEL and seq % TOK_TILE == 0
    lower = jnp.cumsum(jax.nn.softmax(hgrn_lb_logits.astype(F32), axis=0), axis=0)
    mod = _modulation(c, mod_w, mod_b)
    parts = lambda l: [mod[l, :, j * d:(j + 1) * d].reshape(bsz, 1, d) for j in range(6)]

    sh1, sc1, g1, sh2, sc2, g2 = parts(0)
    x = _mixer0(x, sh1, sc1, g1, norm_mix_g[0], ab_w_in[0], ab_w_out[0], lower[0], hgrn_onorm_g[0], sconv_w[0])
    x = _conv_ffn(x, sh2, sc2, g2, norm_ffn_g[0], ffn_w_up[0], ffn_conv_w[0], ffn_w_down[0], final_norm_g)

    sh1, sc1, g1, sh2, sc2, g2 = parts(1)
    kc, vc, ks, kw, q_t, gates_t, vs_t, vw_t = _nsa_proj(x, sh1, sc1, norm_mix_g[1], nsa_w_in[0])
    k_cmp, v_cmp_t = _compress(kc, vc, nsa_cmp_pos_k[0], nsa_cmp_pos_v[0],
                               nsa_cmp_w1_k[0], nsa_cmp_w2_k[0], nsa_cmp_w1_v[0], nsa_cmp_w2_v[0])
    near, bias_cmp = _bias_tables(rel_bias.astype(F32) * LOG2E, seq)
    oc_t, sel = _cmp_topk(q_t, k_cmp, v_cmp_t, bias_cmp, seq)
    o_lo, o_hi = _sel_win(q_t, ks, vs_t, kw, vw_t, sel, near, gates_t, oc_t)
    return _conv_ffn(x, sh2, sc2, g2, norm_ffn_g[1], ffn_w_up[1], ffn_conv_w[1], ffn_w_down[1], final_norm_g,
                     attn=(o_lo, o_hi, nsa_w_out[0], g1), final=True)
```

```python
import functools
import math

import jax
import jax.numpy as jnp
import numpy as np
from jax import lax
from jax.experimental import pallas as pl
from jax.experimental.pallas import tpu as pltpu

F32 = jnp.float32
BF16 = jnp.bfloat16

EPS = 1e-6
D_MODEL = 1024
D_FF = 2816

HG_HEADS = 4
HG_DK = 128
HG_WIDTH = HG_HEADS * HG_DK
SC_WIDTH = D_MODEL - HG_WIDTH
AB_IN = 4 * HG_WIDTH + 3 * SC_WIDTH
HG_CHUNK = 64
HG_LEVELS = 6

NSA_HEADS = 16
NSA_KV = 4
NSA_HPG = NSA_HEADS // NSA_KV
NSA_DH = D_MODEL // NSA_HEADS
CMP_LEN = 32
CMP_STRIDE = 16
CMP_HIDDEN = 256
SEL_BLOCK = 64
SEL_TOPN = 16
WINDOW = 512
REL_BUCKETS = 32
REL_MAX_DIST = 1024

MXU_DIM = 256
VMEM_LIMIT = 56 * 1024 * 1024
TOK_TILE = 256
FFN_TILE = 512
NEAR_TILES = 5
SEL_TABLES = NEAR_TILES + 1
WIN_TILES = WINDOW // TOK_TILE + 1
NEG = -1e30
LOG2E = math.log2(math.e)
KEY_LANES = 128
SEL_ROWS = 16
MIX_BATCH = 2
CMP_BATCH = 8
SW_HEADS = 4
LANE_SLAB = 128
MXU_SKEW = 5
HALO = 8

_NT = (((1,), (1,)), ((), ()))
_TN = (((0,), (0,)), ((), ()))


def _sigmoid(x):
    return 1.0 / (1.0 + jnp.exp(-x))


def _silu(x):
    return x * _sigmoid(x)


def _norm_mod(x, g, sc, sh):
    ms = jnp.mean(x * x, axis=-1, keepdims=True)
    return (x * lax.rsqrt(ms + EPS) * g) * (1.0 + sc) + sh


def _fold(x, op):
    return op(x.reshape(x.shape[0] // 8, 8, x.shape[1]), axis=0)


def _const_spec(shape):
    n = len(shape)
    return pl.BlockSpec(shape, lambda *_: (0,) * n, pipeline_mode=pl.Buffered(1))


def _params(sem):
    return pltpu.CompilerParams(dimension_semantics=sem, vmem_limit_bytes=VMEM_LIMIT)


def _mod_kernel(c_ref, w_ref, b_ref, o_ref):
    c = c_ref[...]
    o_ref[0] = jnp.dot(_silu(c), w_ref[0], preferred_element_type=F32,
                       precision=lax.Precision.HIGHEST) + b_ref[0]


def _modulation(c, mod_w, mod_b):
    depth, d, n = mod_w.shape
    bsz = c.shape[0]
    tn = 1024
    return pl.pallas_call(
        _mod_kernel,
        grid=(depth, n // tn),
        in_specs=[pl.BlockSpec((bsz, d), lambda l, j: (0, 0)),
                  pl.BlockSpec((1, d, tn), lambda l, j: (l, 0, j)),
                  pl.BlockSpec((1, 1, tn), lambda l, j: (l, 0, j))],
        out_specs=pl.BlockSpec((1, bsz, tn), lambda l, j: (l, 0, j)),
        out_shape=jax.ShapeDtypeStruct((depth, bsz, n), F32),
        compiler_params=_params(("arbitrary", "arbitrary")),
        name="adaln_mod",
    )(c, mod_w, mod_b.reshape(depth, 1, n))


def _hgrn_cumsum_matrix():
    t = np.arange(HG_CHUNK)
    return (t[None, :] <= t[:, None]).astype(np.float32)


def _mixer0_kernel(x_ref, sh_ref, sc_ref, gt_ref, ng_ref, win_ref, wout_ref, lb_ref, og_ref, cw_ref,
                   pm_ref, o_ref, proj_ref, cat_ref, st_ref, ubuf_ref, d_scr):
    nb, tile = x_ref.shape[0], x_ref.shape[1]
    c64 = HG_CHUNK
    n_hg = 4 * HG_WIDTH

    @pl.when(pl.program_id(1) == 0)
    def _():
        st_ref[...] = jnp.zeros_like(st_ref)
        ubuf_ref[:, 0:HALO, :] = jnp.zeros((nb, HALO, SC_WIDTH), F32)

    row = lax.broadcasted_iota(jnp.int32, (c64, HG_DK), 0)
    r64 = lax.broadcasted_iota(jnp.int32, (c64, c64), 0)
    c64i = lax.broadcasted_iota(jnp.int32, (c64, c64), 1)
    second = [None] + [(row & (1 << (lvl - 1))) != 0 for lvl in range(1, HG_LEVELS + 1)]
    same = [None] + [(r64 >> lvl) == (c64i >> lvl) for lvl in range(1, HG_LEVELS + 1)]
    eye = r64 == c64i
    pm = pm_ref[...]
    og = og_ref[...]
    units = [(c, h) for c in range(tile // c64) for h in range(HG_HEADS)]
    hm, factors = {}, {}

    def blk(s, c, h, j):
        return s, pl.ds(c * c64, c64), pl.ds(j * HG_WIDTH + h * HG_DK, HG_DK)

    def project_heads(s):
        hm[s] = _norm_mod(x_ref[s], ng_ref[...], sc_ref[s], sh_ref[s]).astype(BF16)
        proj_ref[s, :, :n_hg] = jnp.dot(hm[s], win_ref[:, :n_hg], preferred_element_type=F32)

    def project_conv(s):
        proj_ref[s, :, n_hg:] = jnp.dot(hm[s], win_ref[:, n_hg:], preferred_element_type=F32)

    def decay_sums(s):
        for u, (c, h) in enumerate(units):
            lb = lb_ref[:, h * HG_DK:(h + 1) * HG_DK]
            fg = lb + (1.0 - lb) * _sigmoid(proj_ref[blk(s, c, h, 1)])
            proj_ref[blk(s, c, h, 1)] = 1.0 - fg
            lg = jnp.log(fg)
            lg_hi = lg.astype(BF16)
            lg_lo = (lg - lg_hi.astype(F32)).astype(BF16)
            dall = jnp.dot(pm, jnp.concatenate([lg_hi, lg_lo], axis=1), preferred_element_type=F32)
            d_scr[s, u] = dall[:, :HG_DK] + dall[:, HG_DK:]

    def level_products(s):
        intra, q_in, k_out, decay = [], [], [], []
        for u, (c, h) in enumerate(units):
            q = proj_ref[blk(s, c, h, 0)]
            k = proj_ref[blk(s, c, h, 1)]
            b = d_scr[s, u, 0:c64, :]
            a = jnp.zeros((c64, c64), F32)
            for lvl in range(1, HG_LEVELS + 1):
                m = 1 << (lvl - 1)
                if lvl == 1:
                    qt = jnp.where(second[lvl], q * (1.0 - k), 0.0).astype(BF16)
                    kt = jnp.where(second[lvl], 0.0, k).astype(BF16)
                else:
                    mids = [jnp.broadcast_to(d_scr[s, u, r0 + m - 1:r0 + m, :], (2 * m, HG_DK))
                            for r0 in range(0, c64, 2 * m)]
                    dlt = b - (mids[0] if len(mids) == 1 else jnp.concatenate(mids, axis=0))
                    e = jnp.exp(-jnp.abs(dlt))
                    qt = jnp.where(second[lvl], q * e, 0.0).astype(BF16)
                    kt = jnp.where(second[lvl], 0.0, k * e).astype(BF16)
                al = lax.dot_general(qt, kt, _NT, preferred_element_type=F32)
                a = a + (al if lvl == HG_LEVELS else jnp.where(same[lvl], al, 0.0))
            intra.append(jnp.where(eye, jnp.sum(q * k, axis=-1, keepdims=True), a).astype(BF16))
            b_last = d_scr[s, u, c64 - 1:c64, :]
            q_in.append((q * jnp.exp(b)).astype(BF16))
            k_out.append((k * jnp.exp(b_last - b)).astype(BF16))
            decay.append(jnp.exp(b_last))
        factors[s] = (intra, q_in, k_out, decay)

    def recurrence(s):
        intra, q_in, k_out, decay = factors[s]
        state = [st_ref[s, h] for h in range(HG_HEADS)]
        for u, (c, h) in enumerate(units):
            vb = proj_ref[blk(s, c, h, 2)].astype(BF16)
            o = jnp.dot(intra[u], vb, preferred_element_type=F32)
            o = o + lax.dot_general(q_in[u], state[h].astype(BF16), _NT, preferred_element_type=F32)
            state[h] = state[h] * decay[u] + lax.dot_general(vb, k_out[u], _TN, preferred_element_type=F32)
            on = o * lax.rsqrt(jnp.mean(o * o, axis=-1, keepdims=True) + EPS) * og
            gate = _silu(proj_ref[blk(s, c, h, 3)])
            cat_ref[s, pl.ds(c * c64, c64), h * HG_DK:(h + 1) * HG_DK] = (on * gate).astype(BF16)
        for h in range(HG_HEADS):
            st_ref[s, h] = state[h]

    def short_conv_and_out(s):
        off = n_hg
        u = (proj_ref[s, :, off + SC_WIDTH:off + 2 * SC_WIDTH]
             * proj_ref[s, :, off + 2 * SC_WIDTH:off + 3 * SC_WIDTH])
        ubuf_ref[s, HALO:HALO + tile, :] = u
        cw = cw_ref[...]
        conv = (ubuf_ref[s, HALO - 2:HALO - 2 + tile, :] * cw[0:1]
                + ubuf_ref[s, HALO - 1:HALO - 1 + tile, :] * cw[1:2] + u * cw[2:3])
        ubuf_ref[s, 0:HALO, :] = u[tile - HALO:tile, :]
        cat_ref[s, :, HG_WIDTH:] = (proj_ref[s, :, off:off + SC_WIDTH] * conv).astype(BF16)
        y = jnp.dot(cat_ref[s], wout_ref[...], preferred_element_type=F32)
        o_ref[s] = x_ref[s] + gt_ref[s] * y

    for s in range(nb):
        project_heads(s)
    decay_sums(0)
    project_conv(0)
    for s in range(1, nb):
        decay_sums(s)
        level_products(s - 1)
        project_conv(s)
    level_products(nb - 1)
    for s in range(nb):
        recurrence(s)
        short_conv_and_out(s)


def _mixer0(x, sh, sc, gt, norm_g, w_in, w_out, lower, onorm_g, sconv_w):
    bsz, seq, d = x.shape
    t = TOK_TILE
    nb = MIX_BATCH if bsz % MIX_BATCH == 0 else 1
    pm = jnp.asarray(_hgrn_cumsum_matrix(), BF16)
    vec = pl.BlockSpec((nb, 1, d), lambda b, i: (b, 0, 0))
    return pl.pallas_call(
        _mixer0_kernel,
        grid=(bsz // nb, seq // t),
        in_specs=[pl.BlockSpec((nb, t, d), lambda b, i: (b, i, 0)), vec, vec, vec,
                  _const_spec((1, d)), _const_spec((d, AB_IN)), _const_spec((d, d)),
                  _const_spec((1, HG_WIDTH)), _const_spec((1, HG_DK)), _const_spec((3, SC_WIDTH)),
                  _const_spec(pm.shape)],
        out_specs=pl.BlockSpec((nb, t, d), lambda b, i: (b, i, 0)),
        out_shape=jax.ShapeDtypeStruct(x.shape, F32),
        scratch_shapes=[pltpu.VMEM((nb, t, AB_IN), F32), pltpu.VMEM((nb, t, d), BF16),
                        pltpu.VMEM((nb, HG_HEADS, HG_DK, HG_DK), F32),
                        pltpu.VMEM((nb, t + HALO, SC_WIDTH), F32),
                        pltpu.VMEM((nb, t // HG_CHUNK * HG_HEADS, HG_CHUNK, HG_DK), F32)],
        compiler_params=_params(("arbitrary", "arbitrary")),
        name="mixer_hgrn_sconv",
    )(x, sh, sc, gt, norm_g.reshape(1, d), w_in.astype(BF16), w_out.astype(BF16),
      lower.reshape(1, HG_WIDTH), onorm_g.reshape(1, HG_DK), sconv_w, pm)


FF_CHUNKS = ((0, 5 * MXU_DIM), (5 * MXU_DIM, D_FF))


def _ffn_kernel(*refs, has_attn, final):
    if has_attn:
        (x_ref, olo_ref, ohi_ref, wo_ref, g1_ref, sh_ref, sc_ref, g2_ref, ng_ref, wup_ref, cw_ref, wdn_ref, fg_ref,
         o_ref, gbuf_ref, carry_ref) = refs
    else:
        (x_ref, sh_ref, sc_ref, g2_ref, ng_ref, wup_ref, cw_ref, wdn_ref, fg_ref,
         o_ref, gbuf_ref, carry_ref) = refs
    tile = x_ref.shape[1]

    @pl.when(pl.program_id(1) == 0)
    def _():
        carry_ref[...] = jnp.zeros_like(carry_ref)

    x = x_ref[0]
    if has_attn:
        is_lo = pl.program_id(1) < pl.num_programs(1) // 2
        r = olo_ref.shape[1]
        y = [lax.dot_general(jnp.where(is_lo, olo_ref[0, k], ohi_ref[0, r - 1 - k]), wo_ref[...], _TN,
                             preferred_element_type=F32) for k in range(r)]
        x = x + g1_ref[0] * (y[0] if r == 1 else jnp.concatenate(y, axis=0))
    hf = _norm_mod(x, ng_ref[...], sc_ref[0], sh_ref[0]).astype(BF16)
    gbuf_ref[0:HALO, :] = carry_ref[...]
    gates, ups = [], []
    for c0, c1 in FF_CHUNKS:
        gate = jnp.dot(hf, wup_ref[:, c0:c1], preferred_element_type=F32)
        ups.append(jnp.dot(hf, wup_ref[:, D_FF + c0:D_FF + c1], preferred_element_type=F32))
        gbuf_ref[HALO:HALO + tile, c0:c1] = gate
        carry_ref[:, c0:c1] = gate[tile - HALO:tile, :]
        gates.append(gate)
    acc = jnp.zeros((tile, D_MODEL), F32)
    for j, (c0, c1) in enumerate(FF_CHUNKS):
        cols = slice(c0, c1)
        cw = cw_ref[:, cols]
        conv = (gbuf_ref[HALO - 2:HALO - 2 + tile, cols] * cw[0:1] + gbuf_ref[HALO - 1:HALO - 1 + tile, cols] * cw[1:2]
                + gates[j] * cw[2:3])
        act = (_silu(conv) * ups[j]).astype(BF16)
        acc = acc + jnp.dot(act, wdn_ref[cols, :], preferred_element_type=F32)
    out = x + g2_ref[0] * acc
    if final:
        out = out * lax.rsqrt(jnp.mean(out * out, axis=-1, keepdims=True) + EPS) * fg_ref[...]
    o_ref[0] = out


def _conv_ffn(x, sh, sc, g2, norm_g, w_up, conv_w, w_down, final_g, attn=None, final=False):
    bsz, seq, d = x.shape
    t = FFN_TILE
    vec = pl.BlockSpec((1, 1, d), lambda b, i: (b, 0, 0))
    args, specs = [x], [pl.BlockSpec((1, t, d), lambda b, i: (b, i, 0))]
    if attn is not None:
        o_lo, o_hi, w_o, g1 = attn
        r = t // o_lo.shape[3]
        assert o_lo.shape[1] % r == 0
        top = o_lo.shape[1] // r - 1
        last = seq // t - 1
        args += [o_lo, o_hi, w_o.astype(BF16), g1]
        specs += [pl.BlockSpec((1, r, d, o_lo.shape[3]), lambda b, i: (b, jnp.minimum(i, top), 0, 0)),
                  pl.BlockSpec((1, r, d, o_lo.shape[3]), lambda b, i: (b, jnp.minimum(last - i, top), 0, 0)),
                  _const_spec((d, d)), vec]
    args += [sh, sc, g2, norm_g.reshape(1, d), w_up.astype(BF16), conv_w, w_down.astype(BF16),
             final_g.reshape(1, d)]
    specs += [vec, vec, vec, _const_spec((1, d)), _const_spec((d, 2 * D_FF)), _const_spec((3, D_FF)),
              _const_spec((D_FF, d)), _const_spec((1, d))]
    return pl.pallas_call(
        functools.partial(_ffn_kernel, has_attn=attn is not None, final=final),
        grid=(bsz, seq // t),
        in_specs=specs,
        out_specs=pl.BlockSpec((1, t, d), lambda b, i: (b, i, 0)),
        out_shape=jax.ShapeDtypeStruct(x.shape, F32),
        scratch_shapes=[pltpu.VMEM((t + HALO, D_FF), F32), pltpu.VMEM((HALO, D_FF), F32)],
        compiler_params=_params(("arbitrary", "arbitrary")),
        name="conv_ffn_attnproj" if attn is not None else "conv_ffn",
    )(*args)


KV_ROWS = NSA_KV * NSA_DH


def _nsa_proj_kernel(x_ref, sh_ref, sc_ref, ng_ref, wa_ref, wq_ref, wg_ref, wv_ref, hot_ref, perm_ref,
                     kc_ref, vc_ref, ks_ref, kw_ref, qt_ref, gt_ref, vst_ref, vwt_ref):
    hm = _norm_mod(x_ref[0], ng_ref[...], sc_ref[0], sh_ref[0]).astype(BF16)
    kv = jnp.dot(hm, wa_ref[...], preferred_element_type=F32).astype(BF16)
    cmp_in = jnp.dot(perm_ref[...], kv[:, :2 * KV_ROWS], preferred_element_type=F32).astype(BF16)
    n_grp = perm_ref.shape[0] // CMP_STRIDE
    for g in range(NSA_KV):
        part = lambda j: kv[:, j * KV_ROWS + g * NSA_DH:j * KV_ROWS + (g + 1) * NSA_DH]
        for p in range(CMP_STRIDE):
            rows = slice(p * n_grp, (p + 1) * n_grp)
            kc_ref[0, g, p] = cmp_in[rows, g * NSA_DH:(g + 1) * NSA_DH]
            vc_ref[0, g, p] = cmp_in[rows, KV_ROWS + g * NSA_DH:KV_ROWS + (g + 1) * NSA_DH]
        ks_ref[0, g] = jnp.concatenate([part(2), hot_ref[...]], axis=1)
        kw_ref[0, g] = part(3)
    qt = lax.dot_general(wq_ref[...], hm, _NT, preferred_element_type=F32)
    qt_ref[0, 0] = (qt * (NSA_DH ** -0.5 * LOG2E)).astype(BF16)
    gt_ref[0] = _sigmoid(lax.dot_general(wg_ref[...], hm, _NT, preferred_element_type=F32))
    vt = lax.dot_general(wv_ref[...], hm, _NT, preferred_element_type=F32).astype(BF16)
    vst_ref[0, 0] = vt[:KV_ROWS]
    vwt_ref[0, 0] = vt[KV_ROWS:]


def _nsa_proj(x, sh, sc, norm_g, w_in):
    bsz, seq, d = x.shape
    t = TOK_TILE
    nt = seq // t
    nq, ng = NSA_HEADS * NSA_DH, 3 * NSA_HEADS
    o = nq + ng
    w = w_in.astype(BF16)
    col = lambda j: w[:, o + j * KV_ROWS:o + (j + 1) * KV_ROWS]
    w_a = jnp.concatenate([col(0), col(1), col(2), col(4)], axis=1)
    w_q = w[:, :nq].T
    w_g = w[:, nq:o].T
    w_v = jnp.concatenate([col(3), col(5)], axis=1).T
    hot = (np.arange(t)[:, None] // SEL_BLOCK == np.arange(KEY_LANES - NSA_DH)[None, :]).astype(np.float32)
    n_grp = t // CMP_STRIDE
    r = np.arange(t)
    perm = (r[None, :] == ((r % n_grp) * CMP_STRIDE + r // n_grp)[:, None]).astype(np.float32)
    vec = pl.BlockSpec((1, 1, d), lambda b, i: (b, 0, 0))
    tiled = lambda r: pl.BlockSpec((1, 1, r, t), lambda b, i: (b, i, 0, 0))
    grouped = lambda n: pl.BlockSpec((1, NSA_KV, t, n), lambda b, i: (b, 0, i, 0))
    rows = lambda n: jax.ShapeDtypeStruct((bsz, NSA_KV, seq, n), BF16)
    by_offset = pl.BlockSpec((1, NSA_KV, CMP_STRIDE, n_grp, NSA_DH), lambda b, i: (b, 0, 0, i, 0))
    offset_major = jax.ShapeDtypeStruct((bsz, NSA_KV, CMP_STRIDE, seq // CMP_STRIDE, NSA_DH), BF16)
    return pl.pallas_call(
        _nsa_proj_kernel,
        grid=(bsz, nt),
        in_specs=[pl.BlockSpec((1, t, d), lambda b, i: (b, i, 0)), vec, vec, _const_spec((1, d)),
                  _const_spec((d, 4 * KV_ROWS)), _const_spec((nq, d)), _const_spec((ng, d)),
                  _const_spec((2 * KV_ROWS, d)), _const_spec(hot.shape), _const_spec(perm.shape)],
        out_specs=[by_offset, by_offset, grouped(KEY_LANES), grouped(NSA_DH), tiled(nq),
                   pl.BlockSpec((1, ng, t), lambda b, i: (b, 0, i)), tiled(KV_ROWS), tiled(KV_ROWS)],
        out_shape=[offset_major, offset_major, rows(KEY_LANES), rows(NSA_DH),
                   jax.ShapeDtypeStruct((bsz, nt, nq, t), BF16),
                   jax.ShapeDtypeStruct((bsz, ng, seq), F32),
                   jax.ShapeDtypeStruct((bsz, nt, KV_ROWS, t), BF16),
                   jax.ShapeDtypeStruct((bsz, nt, KV_ROWS, t), BF16)],
        compiler_params=_params(("arbitrary", "arbitrary")),
        name="nsa_in_proj",
    )(x, sh, sc, norm_g.reshape(1, d), w_a, w_q, w_g, w_v, jnp.asarray(hot, BF16), jnp.asarray(perm, BF16))


def _compress_kernel(hk_ref, hv_ref, pk_ref, pv_ref, w1k_ref, w2k_ref, w1v_ref, w2vt_ref, kc_ref, vct_ref):
    half = CMP_STRIDE * NSA_DH

    def hidden(h_ref, pos_ref, w1_ref):
        n = h_ref.shape[3]
        top = jnp.zeros((n, CMP_HIDDEN), F32)
        bot = jnp.zeros((n, CMP_HIDDEN), F32)
        per = MXU_DIM // NSA_DH
        for p0 in range(0, CMP_STRIDE, per):
            h = jnp.concatenate([h_ref[0, 0, p] for p in range(p0, p0 + per)], axis=1)
            rows = slice(p0 * NSA_DH, (p0 + per) * NSA_DH)
            top = top + jnp.dot(h, w1_ref[rows, :], preferred_element_type=F32)
            bot = bot + jnp.dot(h, w1_ref[half + p0 * NSA_DH:half + (p0 + per) * NSA_DH, :],
                                preferred_element_type=F32)
        pos = jnp.dot(jnp.broadcast_to(pos_ref[...], (8, 2 * half)).astype(BF16), w1_ref[...],
                      preferred_element_type=F32)[0:1]
        return _silu(top + pltpu.roll(bot, n - 1, 0) + pos).astype(BF16)

    kc_ref[0, 0] = jnp.dot(hidden(hk_ref, pk_ref, w1k_ref), w2k_ref[...],
                           preferred_element_type=F32).astype(BF16)
    vct_ref[0, 0] = lax.dot_general(w2vt_ref[...], hidden(hv_ref, pv_ref, w1v_ref), _NT,
                                    preferred_element_type=F32).astype(BF16)


def _compress(hk, hv, pos_k, pos_v, w1_k, w2_k, w1_v, w2_v):
    bsz, g, stride, n, dh = hk.shape
    width = stride * dh
    blk = pl.BlockSpec((1, 1, stride, n, dh), lambda b, j: (b, j, 0, 0, 0))
    return pl.pallas_call(
        _compress_kernel,
        grid=(bsz, g),
        in_specs=[blk, blk, _const_spec((1, width * 2)), _const_spec((1, width * 2)),
                  _const_spec((2 * width, CMP_HIDDEN)), _const_spec((CMP_HIDDEN, NSA_DH)),
                  _const_spec((2 * width, CMP_HIDDEN)), _const_spec((NSA_DH, CMP_HIDDEN))],
        out_specs=[pl.BlockSpec((1, 1, n, NSA_DH), lambda b, j: (b, j, 0, 0)),
                   pl.BlockSpec((1, 1, NSA_DH, n), lambda b, j: (b, j, 0, 0))],
        out_shape=[jax.ShapeDtypeStruct((bsz, g, n, NSA_DH), BF16),
                   jax.ShapeDtypeStruct((bsz, g, NSA_DH, n), BF16)],
        compiler_params=_params(("arbitrary", "arbitrary")),
        name="nsa_compress",
    )(hk, hv, pos_k.reshape(1, -1), pos_v.reshape(1, -1), w1_k.astype(BF16), w2_k.astype(BF16),
      w1_v.astype(BF16), w2_v.T.astype(BF16))


def _bucket_upper_bounds():
    n = np.arange(4 * REL_MAX_DIST, dtype=np.float64)
    exact = REL_BUCKETS // 2
    large = exact + (np.log(np.maximum(n, exact) / exact) / math.log(REL_MAX_DIST / exact)
                     * (REL_BUCKETS - exact)).astype(np.int64)
    bucket = np.where(n < exact, n.astype(np.int64), np.minimum(large, REL_BUCKETS - 1))
    return [int(np.max(np.nonzero(bucket <= j)[0])) for j in range(REL_BUCKETS - 1)]


def _bias_of_dist(dist, rb_ref, heads, uppers):
    vals = [jnp.full(dist.shape, rb_ref[REL_BUCKETS - 1, h], F32) for h in heads]
    for j in range(REL_BUCKETS - 2, -1, -1):
        m = dist <= uppers[j]
        vals = [jnp.where(m, rb_ref[j, h], v) for h, v in zip(heads, vals)]
    return vals


def _bias_near_kernel(rb_ref, o_ref, line_scr, *, uppers):
    d = pl.program_id(0)
    t = o_ref.shape[2]
    heads = list(range(NSA_HEADS))

    @pl.when(d == 0)
    def _():
        line = ((lax.broadcasted_iota(jnp.int32, (8, t), 0) - 1) * t + lax.broadcasted_iota(jnp.int32, (8, t), 1))
        for h, v in zip(heads, _bias_of_dist(line, rb_ref, heads, uppers)):
            line_scr[h] = v

    is_win = d >= SEL_TABLES
    delta = jnp.where(is_win, d - SEL_TABLES, d)
    limit = jnp.where(is_win, WINDOW, (SEL_TABLES + 1) * t)
    dist = delta * t + lax.broadcasted_iota(jnp.int32, (t, t), 1) - lax.broadcasted_iota(jnp.int32, (t, t), 0)
    valid = (dist >= 0) & (dist < limit)
    for h in heads:
        span = jnp.concatenate([line_scr[h, pl.ds(delta, 1), :], line_scr[h, pl.ds(delta + 1, 1), :]], axis=1)
        rolled = pltpu.roll(jnp.broadcast_to(span, (t, 2 * t)), 0, 1, stride=1, stride_axis=0)
        o_ref[h, 0] = jnp.where(valid, rolled[:, t:], NEG)


CMP_BAND = 72


def _bias_cmp_kernel(rb_ref, o_ref, band_scr, *, uppers):
    tb = pl.program_id(0)
    n, t = o_ref.shape[1], o_ref.shape[2]
    heads = list(range(NSA_HEADS))
    per_tile = t // CMP_STRIDE
    first = CMP_STRIDE * (CMP_BAND - per_tile) - (CMP_LEN - 1)
    assert first + CMP_STRIDE > uppers[-1] and first + (t - 1) - CMP_STRIDE * CMP_BAND < 0

    @pl.when(tb == 0)
    def _():
        for r0 in range(0, CMP_BAND, 8):
            row = r0 + lax.broadcasted_iota(jnp.int32, (8, t), 0)
            dist = first + lax.broadcasted_iota(jnp.int32, (8, t), 1) - CMP_STRIDE * row
            for h, v in zip(heads, _bias_of_dist(dist, rb_ref, heads, uppers)):
                band_scr[h, r0:r0 + 8, :] = jnp.where(dist >= 0, v, NEG)

    def strip(i, carry):
        r0 = pl.multiple_of(i * 8, 8)
        rel = r0 - (tb * per_tile - (CMP_BAND - per_tile))
        at = pl.multiple_of(jnp.clip(rel, 0, CMP_BAND - 8), 8)
        for h in heads:
            inside = band_scr[h, pl.ds(at, 8), :]
            far = jnp.full((8, t), rb_ref[REL_BUCKETS - 1, h], F32)
            o_ref[h, pl.ds(r0, 8), :] = jnp.where(rel < 0, far, jnp.where(rel >= CMP_BAND, NEG, inside))
        return carry

    lax.fori_loop(0, n // 8, strip, 0)


def _bias_tables(rel_bias, seq):
    t = TOK_TILE
    n_cmp = seq // CMP_STRIDE
    uppers = _bucket_upper_bounds()
    assert NEAR_TILES * t - (t - 1) > uppers[-1]
    smem = pl.BlockSpec(memory_space=pltpu.SMEM)
    near = pl.pallas_call(
        functools.partial(_bias_near_kernel, uppers=uppers),
        grid=(SEL_TABLES + WIN_TILES,),
        in_specs=[smem],
        out_specs=pl.BlockSpec((NSA_HEADS, 1, t, t), lambda d: (0, d, 0, 0)),
        out_shape=jax.ShapeDtypeStruct((NSA_HEADS, SEL_TABLES + WIN_TILES, t, t), F32),
        scratch_shapes=[pltpu.VMEM((NSA_HEADS, 8, t), F32)],
        compiler_params=_params(("arbitrary",)),
        name="relbias_near",
    )(rel_bias)
    cmp_b = pl.pallas_call(
        functools.partial(_bias_cmp_kernel, uppers=uppers),
        grid=(seq // t,),
        in_specs=[smem],
        out_specs=pl.BlockSpec((NSA_HEADS, n_cmp, t), lambda i: (0, 0, i)),
        out_shape=jax.ShapeDtypeStruct((NSA_HEADS, n_cmp, seq), F32),
        scratch_shapes=[pltpu.VMEM((NSA_HEADS, CMP_BAND, t), F32)],
        compiler_params=_params(("arbitrary",)),
        name="relbias_cmp",
    )(rel_bias)
    return near, cmp_b


def _cmp_to_sel_t(seq):
    n_cmp, n_sel = seq // CMP_STRIDE, seq // SEL_BLOCK
    c_start = np.arange(n_cmp)[:, None] * CMP_STRIDE
    s_start = np.arange(n_sel)[None, :] * SEL_BLOCK
    inside = np.clip(np.minimum(c_start + CMP_LEN, s_start + SEL_BLOCK) - np.maximum(c_start, s_start), 0, None)
    return (inside / CMP_LEN).T.astype(np.float32)


def _cmp_topk_kernel(qt_ref, kc_ref, vct_ref, bias_ref, c2s_ref, oc_ref, sel_ref, score_scr, cnt_scr):
    tb = pl.program_id(1)
    n, t = bias_ref.shape[1], bias_ref.shape[2]
    n_sel = c2s_ref.shape[0]
    nb = qt_ref.shape[0]
    tok = tb * t + lax.broadcasted_iota(jnp.int32, (1, t), 1)
    any_visible = tok >= CMP_LEN - 1
    ones = (lax.broadcasted_iota(jnp.int32, (SEL_ROWS, n), 0) == 0).astype(BF16)
    head_rows = lambda i: slice(i * NSA_DH, (i + 1) * NSA_DH)
    weights = {}
    for bb in range(nb):
        kc = kc_ref[bb, 0]
        for i in range(NSA_HPG):
            s = jnp.dot(kc, qt_ref[bb, 0, head_rows(i), :], preferred_element_type=F32) + bias_ref[i]
            m = jnp.max(_fold(s, jnp.max), axis=0, keepdims=True)
            weights[bb, i] = jnp.exp2(s - m).astype(BF16)
    imps = []
    for bb in range(nb):
        lhs = jnp.concatenate([vct_ref[bb, 0], ones, c2s_ref[...]], axis=0)
        imp = jnp.zeros((n_sel, t), F32)
        for i in range(NSA_HPG):
            r = jnp.dot(lhs, weights[bb, i], preferred_element_type=F32)
            scale = jnp.where(any_visible, 1.0 / r[NSA_DH:NSA_DH + 1], 0.0)
            oc_ref[bb, 0, head_rows(i), :] = (r[:NSA_DH] * scale).astype(BF16)
            imp = imp + r[NSA_DH + SEL_ROWS:] * scale
        imps.append(imp)

    w = nb * t
    blk = lax.broadcasted_iota(jnp.int32, (n_sel, t), 0)
    cur = (tb * t + lax.broadcasted_iota(jnp.int32, (n_sel, t), 1)) // SEL_BLOCK
    forced = (blk == 0) | (blk == cur) | (blk == cur - 1)
    for bb in range(nb):
        score_scr[:, bb * t:(bb + 1) * t] = jnp.where(blk > cur, -jnp.inf, jnp.where(forced, jnp.inf, imps[bb]))
    groups = n_sel // 8
    per_tile = t // SEL_BLOCK
    sub = lax.broadcasted_iota(jnp.int32, (8, w), 0)
    cnt_scr[...] = jnp.zeros_like(cnt_scr)
    for j in range(groups):
        @pl.when(8 * j < (tb + 1) * per_tile)
        def _():
            src = [jnp.broadcast_to(score_scr[sp:sp + 1, :], (8, w)) for sp in range(8 * j, 8 * j + 8)]
            for v in range(groups):
                part = score_scr[8 * v:8 * v + 8, :]
                cnt = cnt_scr[8 * v:8 * v + 8, :]
                for sp, r in zip(range(8 * j, 8 * j + 8), src):
                    if j < v:
                        beats = r >= part
                    elif j > v:
                        beats = r > part
                    else:
                        beats = (r > part) | ((r == part) & (sub > sp - 8 * v))
                    cnt = cnt + jnp.where(beats, 1.0, 0.0)
                cnt_scr[8 * v:8 * v + 8, :] = cnt
    pad = jnp.zeros((SEL_ROWS - per_tile, t), F32)
    for v in range(groups):
        mask = jnp.where(cnt_scr[8 * v:8 * v + 8, :] < min(SEL_TOPN, n_sel), 0.0, NEG)
        for bb in range(nb):
            for r0 in range(0, 8, per_tile):
                kt = (8 * v + r0) // per_tile
                rows = mask[r0:r0 + per_tile, bb * t:(bb + 1) * t]
                sel_ref[bb, 0, 0, kt] = jnp.concatenate([rows, pad], axis=0).astype(BF16)


def _cmp_topk(q_t, k_cmp, v_cmp_t, bias_cmp, seq):
    bsz, nt, nq, t = q_t.shape
    n_cmp, n_sel = seq // CMP_STRIDE, seq // SEL_BLOCK
    grp = NSA_HPG * NSA_DH
    nb = CMP_BATCH if bsz % CMP_BATCH == 0 else 1
    c2s = jnp.asarray(_cmp_to_sel_t(seq), BF16)
    return pl.pallas_call(
        _cmp_topk_kernel,
        grid=(NSA_KV, nt, bsz // nb),
        in_specs=[pl.BlockSpec((nb, 1, grp, t), lambda g, i, b: (b, i, g, 0)),
                  pl.BlockSpec((nb, 1, n_cmp, NSA_DH), lambda g, i, b: (b, g, 0, 0)),
                  pl.BlockSpec((nb, 1, NSA_DH, n_cmp), lambda g, i, b: (b, g, 0, 0)),
                  pl.BlockSpec((NSA_HPG, n_cmp, t), lambda g, i, b: (g, 0, i)),
                  _const_spec((n_sel, n_cmp))],
        out_specs=[pl.BlockSpec((nb, 1, grp, t), lambda g, i, b: (b, i, g, 0)),
                   pl.BlockSpec((nb, 1, 1, nt, SEL_ROWS, t), lambda g, i, b: (b, g, i, 0, 0, 0))],
        out_shape=[jax.ShapeDtypeStruct((bsz, nt, nq, t), BF16),
                   jax.ShapeDtypeStruct((bsz, NSA_KV, nt, nt, SEL_ROWS, t), BF16)],
        scratch_shapes=[pltpu.VMEM((n_sel, nb * t), F32), pltpu.VMEM((n_sel, nb * t), F32)],
        compiler_params=_params(("arbitrary", "arbitrary", "arbitrary")),
        name="nsa_cmp_topk",
    )(q_t, k_cmp, v_cmp_t, bias_cmp, c2s)


def _sel_win_kernel(qa_ref, qb_ref, ks_ref, vst_ref, kw_ref, vwt_ref, sela_ref, selb_ref, tab_ref, ga_ref, gb_ref,
                    oca_ref, ocb_ref, olo_ref, ohi_ref, part_scr):
    h0 = pl.program_id(0) * SW_HEADS
    i = pl.program_id(2)
    nt, t = vst_ref.shape[1], vst_ref.shape[3]
    half = nt // 2
    tb = (i, nt - 1 - i)
    head_rows = lambda hh: slice(hh * NSA_DH, (hh + 1) * NSA_DH)
    q = [(qa_ref[0, 0, head_rows(hh), :], qb_ref[0, 0, head_rows(hh), :]) for hh in range(SW_HEADS)]
    sel = (sela_ref, selb_ref)

    def tile_rows(kt):
        return pl.ds(pl.multiple_of(kt * t, t), t)

    def pick(on_a, xa, xb):
        if isinstance(on_a, bool):
            return xa if on_a else xb
        return jnp.where(on_a, xa, xb)

    work = []
    for p in range(nt + 1):
        on_a = True if p == 0 else (False if p >= half else p <= i)
        kt = pick(on_a, p, jnp.maximum(p - i - 1, 0))
        work.append((on_a, kt, True, nt - p if on_a is False else pick(on_a, tb[0] - kt, nt - p)))
    for side in range(2):
        for delta in range(WIN_TILES):
            keep = True if (side == 1 and half >= WIN_TILES) else tb[side] >= delta
            work.append((side == 0, jnp.maximum(tb[side] - delta, 0), keep, delta))
    n_selpos = nt + 1

    ones = (lax.broadcasted_iota(jnp.int32, (SEL_ROWS, t), 0) == 0).astype(BF16)
    q_pad = jnp.zeros((KEY_LANES - NSA_DH - SEL_ROWS, t), BF16)
    tile_max = {}

    def scores(p, hh):
        on_a, kt, keep, delta = work[p]
        qp = pick(on_a, q[hh][0], q[hh][1])
        if p >= n_selpos:
            s = jnp.dot(kw_ref[0, 0, tile_rows(kt), :], qp, preferred_element_type=F32)
        else:
            rows = pick(on_a, sel[0][0, 0, 0, kt], sel[1][0, 0, 0, kt])
            q_aug = jnp.concatenate([qp, rows, q_pad], axis=0)
            s = jnp.dot(ks_ref[0, 0, tile_rows(kt), :], q_aug, preferred_element_type=F32)
            if isinstance(delta, int) and delta >= SEL_TABLES - 1:
                m_tile, pr = weights(s, None)
                return m_tile + tab_ref[hh, SEL_TABLES - 1, 0:1, :], pr
            return weights(s, (hh, jnp.minimum(delta, SEL_TABLES - 1)))
        return weights(s, (hh, SEL_TABLES + delta))

    def weights(s, table):
        maxes, probs = [], []
        for c in range(0, t, LANE_SLAB):
            sl = s[:, c:c + LANE_SLAB]
            if table is not None:
                sl = sl + tab_ref[table[0], table[1], :, pl.ds(c, LANE_SLAB)]
            m = jnp.max(_fold(sl, jnp.max), axis=0, keepdims=True)
            maxes.append(m)
            probs.append(jnp.exp2(sl - m).astype(BF16))
        return jnp.concatenate(maxes, axis=1), jnp.concatenate(probs, axis=1)

    def values(p, hh, m_tile, pr):
        on_a, kt, keep, delta = work[p]
        br = int(p >= n_selpos)
        vt_ref = vwt_ref if br else vst_ref
        part_scr[hh, p] = jnp.dot(jnp.concatenate([vt_ref[0, kt], ones], axis=0), pr,
                                  preferred_element_type=F32)
        if not isinstance(keep, bool):
            m_tile = jnp.where(keep, m_tile, NEG)
        if isinstance(on_a, bool):
            tile_max[p, hh] = (m_tile, None) if on_a else (None, m_tile)
        else:
            tile_max[p, hh] = (jnp.where(on_a, m_tile, NEG), jnp.where(on_a, NEG, m_tile))

    def finish(hh):
        for side, (g_ref, oc_ref, o_ref) in enumerate(((ga_ref, oca_ref, olo_ref), (gb_ref, ocb_ref, ohi_ref))):
            outs = []
            for ps in (range(n_selpos), range(n_selpos, len(work))):
                mine = [p for p in ps if tile_max[p, hh][side] is not None]
                top = functools.reduce(jnp.maximum, [tile_max[p, hh][side] for p in mine])
                acc = jnp.zeros((NSA_DH + SEL_ROWS, t), F32)
                for p in mine:
                    acc = acc + part_scr[hh, p] * jnp.exp2(tile_max[p, hh][side] - top)
                outs.append(acc[:NSA_DH] / acc[NSA_DH:NSA_DH + 1])
            gate = lambda br: g_ref[0, pl.ds(br * NSA_HEADS + h0 + hh, 1), :]
            o_c = oc_ref[0, 0, head_rows(hh), :].astype(F32)
            o_ref[0, 0, head_rows(hh), :] = (gate(0) * o_c + gate(1) * outs[0] + gate(2) * outs[1]).astype(BF16)

    items = [(p, hh) for hh in range(SW_HEADS) for p in range(len(work))]
    pending = {}
    for step in range(len(items) + MXU_SKEW):
        if step < len(items):
            pending[step] = scores(*items[step])
        if step >= MXU_SKEW:
            p, hh = items[step - MXU_SKEW]
            values(p, hh, *pending.pop(step - MXU_SKEW))
            if p == len(work) - 1:
                finish(hh)


def _sel_win(q_t, ks, vs_t, kw, vw_t, sel, tables, gates_t, oc_t):
    bsz, nt, nq, t = q_t.shape
    seq = nt * t
    last = nt - 1
    per_group = NSA_HPG // SW_HEADS
    head_a = pl.BlockSpec((1, 1, SW_HEADS * NSA_DH, t), lambda h, b, i: (b, i, h, 0))
    head_b = pl.BlockSpec((1, 1, SW_HEADS * NSA_DH, t), lambda h, b, i: (b, last - i, h, 0))
    keys = pl.BlockSpec((1, 1, seq, NSA_DH), lambda h, b, i: (b, h // per_group, 0, 0))
    keys_sel = pl.BlockSpec((1, 1, seq, KEY_LANES), lambda h, b, i: (b, h // per_group, 0, 0))
    vals = pl.BlockSpec((1, nt, NSA_DH, t), lambda h, b, i: (b, 0, h // per_group, 0))
    sel_a = pl.BlockSpec((1, 1, 1, nt, SEL_ROWS, t), lambda h, b, i: (b, h // per_group, i, 0, 0, 0))
    sel_b = pl.BlockSpec((1, 1, 1, nt, SEL_ROWS, t), lambda h, b, i: (b, h // per_group, last - i, 0, 0, 0))
    gate_a = pl.BlockSpec((1, 3 * NSA_HEADS, t), lambda h, b, i: (b, 0, i))
    gate_b = pl.BlockSpec((1, 3 * NSA_HEADS, t), lambda h, b, i: (b, 0, last - i))
    out = jax.ShapeDtypeStruct((bsz, nt // 2, nq, t), BF16)
    return pl.pallas_call(
        _sel_win_kernel,
        grid=(NSA_HEADS // SW_HEADS, bsz, nt // 2),
        in_specs=[head_a, head_b, keys_sel, vals, keys, vals, sel_a, sel_b,
                  pl.BlockSpec((SW_HEADS, SEL_TABLES + WIN_TILES, t, t), lambda h, b, i: (h, 0, 0, 0)),
                  gate_a, gate_b, head_a, head_b],
        out_specs=[head_a, head_a],
        out_shape=[out, out],
        scratch_shapes=[pltpu.VMEM((SW_HEADS, nt + 1 + 2 * WIN_TILES, NSA_DH + SEL_ROWS, t), F32)],
        compiler_params=_params(("arbitrary", "arbitrary", "arbitrary")),
        name="nsa_sel_win",
    )(q_t, q_t, ks, vs_t, kw, vw_t, sel, sel, tables, gates_t, gates_t, oc_t, oc_t)


def kernel(x, c, mod_w, mod_b, norm_mix_g, norm_ffn_g, ab_w_in, ab_w_out, hgrn_lb_logits, hgrn_onorm_g, sconv_w, nsa_w_in, nsa_w_out, nsa_cmp_pos_k, nsa_cmp_pos_v, nsa_cmp_w1_k, nsa_cmp_w2_k, nsa_cmp_w1_v, nsa_cmp_w2_v, rel_bias, ffn_w_up, ffn_conv_w, ffn_w_down, final_norm_g):
    bsz, seq, d = x.shape
    assert d == D_MODEL and seq % (2 * TOK_TILE) == 0 and seq % FFN_TILE == 0
    lower = jnp.cumsum(jax.nn.softmax(hgrn_lb_logits.astype(F32), axis=0), axis=0)
    mod = _modulation(c, mod_w, mod_b)
    parts = lambda l: [mod[l, :, j * d:(j + 1) * d].reshape(bsz, 1, d) for j in range(6)]

    sh1, sc1, g1, sh2, sc2, g2 = parts(0)
    x = _mixer0(x, sh1, sc1, g1, norm_mix_g[0], ab_w_in[0], ab_w_out[0], lower[0], hgrn_onorm_g[0], sconv_w[0])
    x = _conv_ffn(x, sh2, sc2, g2, norm_ffn_g[0], ffn_w_up[0], ffn_conv_w[0], ffn_w_down[0], final_norm_g)

    sh1, sc1, g1, sh2, sc2, g2 = parts(1)
    kc, vc, ks, kw, q_t, gates_t, vs_t, vw_t = _nsa_proj(x, sh1, sc1, norm_mix_g[1], nsa_w_in[0])
    k_cmp, v_cmp_t = _compress(kc, vc, nsa_cmp_pos_k[0], nsa_cmp_pos_v[0],
                               nsa_cmp_w1_k[0], nsa_cmp_w2_k[0], nsa_cmp_w1_v[0], nsa_cmp_w2_v[0])
    near, bias_cmp = _bias_tables(rel_bias.astype(F32) * LOG2E, seq)
    oc_t, sel = _cmp_topk(q_t, k_cmp, v_cmp_t, bias_cmp, seq)
    o_lo, o_hi = _sel_win(q_t, ks, vs_t, kw, vw_t, sel, near, gates_t, oc_t)
    return _conv_ffn(x, sh2, sc2, g2, norm_ffn_g[1], ffn_w_up[1], ffn_conv_w[1], ffn_w_down[1], final_norm_g,
                     attn=(o_lo, o_hi, nsa_w_out[0], g1), final=True)
```
